```python
import math
import jax
import jax.numpy as jnp
from jax import lax
import numpy as np

D_MODEL = 1024
BATCH = 8
SEQ = 2048
DEPTH = 2
DEC_BATCH = 4
DEC_SEQ = 8192
PAST_LEN = 128

GRID_W = 64
EPS = 1e-6
Q_BLOCK = 128
ROPE_THETA = 500000.0
AXIAL_THETA = 10000.0
NEG = -1e30

A_GROUPS = ((128, 1), (512, 4), (2048, 16))
A_HEADS = 4
A_HEAD_DIM = 96
A_ROT = A_HEAD_DIM // 4
A_QKV = len(A_GROUPS) * A_HEADS * A_HEAD_DIM
A_WIDTH = A_HEADS * A_HEAD_DIM

B_HEADS = 6
B_KV_HEADS = 2
B_HEAD_DIM = 64
B_WIDTH = B_HEADS * B_HEAD_DIM
B_KV = B_KV_HEADS * B_HEAD_DIM

C_HEADS = 4
C_HEAD_DIM = 64
C_WIDTH = C_HEADS * 2 * C_HEAD_DIM
C_ROT = C_HEAD_DIM // 4

IN_SIZES = (A_QKV, A_QKV, A_QKV, A_WIDTH,
            B_WIDTH, B_KV, B_KV, B_WIDTH,
            C_WIDTH, C_WIDTH, C_WIDTH, C_WIDTH)
D_IN = sum(IN_SIZES)
N_BRANCH = 3

kernel_name = 'hybrid_gated_dilated_gqa_diff_encoder'


def rms_norm(x, g):
    xf = x.astype(jnp.float32)
    y = xf * lax.rsqrt(jnp.mean(xf * xf, axis=-1, keepdims=True) + EPS)
    return (y * g.astype(jnp.float32)).astype(x.dtype)


def rope_angles(pos, dim, theta):
    inv = theta ** (-jnp.arange(0, dim, 2, dtype=jnp.float32) / dim)
    return pos.astype(jnp.float32)[:, None] * inv[None, :]


def rotate(x, ang):
    half = x.shape[-1] // 2
    xf = x.astype(jnp.float32)
    x1, x2 = xf[..., :half], xf[..., half:]
    cos = jnp.cos(ang)[None, :, None, :]
    sin = jnp.sin(ang)[None, :, None, :]
    return jnp.concatenate([x1 * cos - x2 * sin, x2 * cos + x1 * sin], axis=-1).astype(x.dtype)


def partial_rope(x, ang):
    n = 2 * ang.shape[-1]
    return jnp.concatenate([rotate(x[..., :n], ang), x[..., n:]], axis=-1)


def axial_rope(x, ang_row, ang_col):
    n = 2 * ang_row.shape[-1]
    return jnp.concatenate([rotate(x[..., :n], ang_row), rotate(x[..., n:2 * n], ang_col)], axis=-1)


def dilated_window_attention(q, k, v, window, dilation):
    Bn, S, H, Dh = q.shape
    R = window // (2 * dilation)
    L = S // dilation
    N = Bn * dilation

    def split(t):
        return t.reshape(Bn, L, dilation, H, Dh).transpose(0, 2, 1, 3, 4).reshape(N, L, H, Dh)

    qs, ks, vs = split(q), split(k), split(v)
    nb = -(-L // R)
    Lp = nb * R
    qs = jnp.pad(qs, ((0, 0), (0, Lp - L), (0, 0), (0, 0)))
    kp = jnp.pad(ks, ((0, 0), (R, Lp - L + R), (0, 0), (0, 0)))
    vp = jnp.pad(vs, ((0, 0), (R, Lp - L + R), (0, 0), (0, 0)))
    qb = qs.reshape(N, nb, R, H, Dh)

    def band(t):
        tb = t.reshape(N, nb + 2, R, H, Dh)
        return jnp.concatenate([tb[:, :-2], tb[:, 1:-1], tb[:, 2:]], axis=2)

    kb, vb = band(kp), band(vp)
    s = jnp.einsum('nbqhd,nbkhd->nbhqk', qb, kb, preferred_element_type=jnp.float32) * (Dh ** -0.5)
    qi = jnp.arange(nb)[:, None] * R + jnp.arange(R)[None, :]
    kj = jnp.arange(nb)[:, None] * R - R + jnp.arange(3 * R)[None, :]
    dist = kj[:, None, :] - qi[:, :, None]
    mask = (jnp.abs(dist) <= R) & (kj[:, None, :] >= 0) & (kj[:, None, :] < L)
    s = jnp.where(mask[None, :, None, :, :], s, NEG)
    m = jnp.max(s, axis=-1, keepdims=True)
    p = jnp.exp(s - m)
    l = jnp.sum(p, axis=-1)
    o = jnp.einsum('nbhqk,nbkhd->nbqhd', p.astype(v.dtype), vb, preferred_element_type=jnp.float32)
    o = o / l.transpose(0, 1, 3, 2)[..., None]
    lse = (m[..., 0] + jnp.log(l)).transpose(0, 1, 3, 2)
    o = o.reshape(N, Lp, H, Dh)[:, :L].reshape(Bn, dilation, L, H, Dh)
    o = o.transpose(0, 2, 1, 3, 4).reshape(Bn, S, H, Dh)
    lse = lse.reshape(N, Lp, H)[:, :L].reshape(Bn, dilation, L, H).transpose(0, 2, 1, 3).reshape(Bn, S, H)
    return o, lse


def dilated_mixture(q, k, v):
    outs, lses = [], []
    for g, (window, dil) in enumerate(A_GROUPS):
        sl = slice(g * A_HEADS, (g + 1) * A_HEADS)
        o, lse = dilated_window_attention(q[:, :, sl], k[:, :, sl], v[:, :, sl], window, dil)
        outs.append(o.astype(jnp.float32))
        lses.append(lse)
    w = jax.nn.softmax(jnp.stack(lses, axis=0), axis=0)
    o = jnp.sum(w[..., None] * jnp.stack(outs, axis=0), axis=0)
    return o.astype(q.dtype)


def gqa_attention(q, k, v):
    Bn, S, Hq, Dh = q.shape
    Hkv = k.shape[2]
    G = Hq // Hkv
    nq = S // Q_BLOCK
    qb = q.reshape(Bn, nq, Q_BLOCK, Hkv, G, Dh).transpose(1, 0, 2, 3, 4, 5)
    scale = Dh ** -0.5

    def block(qi):
        s = jnp.einsum('bqhgd,bkhd->bhgqk', qi, k, preferred_element_type=jnp.float32) * scale
        p = jax.nn.softmax(s, axis=-1).astype(v.dtype)
        return jnp.einsum('bhgqk,bkhd->bqhgd', p, v, preferred_element_type=jnp.float32).astype(v.dtype)

    o = lax.map(block, qb)
    return o.transpose(1, 0, 2, 3, 4, 5).reshape(Bn, S, Hq, Dh)


def diff_attention(q1, q2, k1, k2, v, lam):
    Bn, S, H, Dh = q1.shape
    nq = S // Q_BLOCK
    scale = Dh ** -0.5

    def blk(t):
        return t.reshape(Bn, nq, Q_BLOCK, H, Dh).transpose(1, 0, 2, 3, 4)

    def block(args):
        a, b = args
        s1 = jnp.einsum('bqhd,bkhd->bhqk', a, k1, preferred_element_type=jnp.float32) * scale
        s2 = jnp.einsum('bqhd,bkhd->bhqk', b, k2, preferred_element_type=jnp.float32) * scale
        w = jax.nn.softmax(s1, axis=-1) - lam * jax.nn.softmax(s2, axis=-1)
        return jnp.einsum('bhqk,bkhe->bqhe', w.astype(v.dtype), v, preferred_element_type=jnp.float32).astype(v.dtype)

    o = lax.map(block, (blk(q1), blk(q2)))
    return o.transpose(1, 0, 2, 3, 4).reshape(Bn, S, H, v.shape[-1])


def encoder_layer(x, c, ang_a, ang_row, ang_col, ang_c, lam_init,
                  norm_g, w_ada, b_ada, w_in, qn_a, kn_a, qn_b, kn_b, qn_c, kn_c,
                  lam_q1, lam_k1, lam_q2, lam_k2, subln_c, w_oa, w_ob, w_oc, w_bg, b_bg, w_out):
    Bn, S, _ = x.shape
    mod = jnp.einsum('bd,de->be', jax.nn.silu(c), w_ada) + b_ada
    shift, scale, gate = jnp.split(mod, 3, axis=-1)
    h = rms_norm(x, norm_g) * (1 + scale[:, None, :]) + shift[:, None, :]

    u = jnp.einsum('bsd,de->bse', h, w_in)
    offs = []
    acc = 0
    for n in IN_SIZES[:-1]:
        acc += n
        offs.append(acc)
    qa, ka, va, za, qb, kb, vb, zb, qc, kc, vc, zc = jnp.split(u, offs, axis=-1)

    n_a = len(A_GROUPS) * A_HEADS
    qa = partial_rope(rms_norm(qa.reshape(Bn, S, n_a, A_HEAD_DIM), qn_a), ang_a)
    ka = partial_rope(rms_norm(ka.reshape(Bn, S, n_a, A_HEAD_DIM), kn_a), ang_a)
    va = va.reshape(Bn, S, n_a, A_HEAD_DIM)
    ya = dilated_mixture(qa, ka, va).reshape(Bn, S, A_WIDTH)

    qb = axial_rope(rms_norm(qb.reshape(Bn, S, B_HEADS, B_HEAD_DIM), qn_b), ang_row, ang_col)
    kb = axial_rope(rms_norm(kb.reshape(Bn, S, B_KV_HEADS, B_HEAD_DIM), kn_b), ang_row, ang_col)
    yb = gqa_attention(qb, kb, vb.reshape(Bn, S, B_KV_HEADS, B_HEAD_DIM)).reshape(Bn, S, B_WIDTH)

    qc = partial_rope(rms_norm(qc.reshape(Bn, S, 2 * C_HEADS, C_HEAD_DIM), qn_c), ang_c)
    kc = partial_rope(rms_norm(kc.reshape(Bn, S, 2 * C_HEADS, C_HEAD_DIM), kn_c), ang_c)
    f32 = jnp.float32
    lam = (jnp.exp(jnp.sum(lam_q1.astype(f32) * lam_k1.astype(f32)))
           - jnp.exp(jnp.sum(lam_q2.astype(f32) * lam_k2.astype(f32))) + lam_init)
    oc = diff_attention(qc[:, :, 0::2], qc[:, :, 1::2], kc[:, :, 0::2], kc[:, :, 1::2],
                        vc.reshape(Bn, S, C_HEADS, 2 * C_HEAD_DIM), lam)
    yc = (rms_norm(oc, subln_c) * (1.0 - lam_init)).reshape(Bn, S, C_WIDTH)

    pa = jnp.einsum('bse,ed->bsd', ya * jax.nn.silu(za), w_oa)
    pb = jnp.einsum('bse,ed->bsd', yb * jax.nn.silu(zb), w_ob)
    pc = jnp.einsum('bse,ed->bsd', yc * jax.nn.silu(zc), w_oc)

    g = jax.nn.sigmoid(jnp.einsum('bsd,de->bse', h, w_bg) + b_bg)
    ga, gb, gc = jnp.split(g, N_BRANCH, axis=-1)
    merged = ga * pa + gb * pb + gc * pc
    out = jnp.einsum('bsd,de->bse', merged, w_out)
    return x + gate[:, None, :] * out


def trunk(x, c, layer_params):
    S = x.shape[1]
    rows = S // GRID_W
    pos = jnp.arange(S)
    row = jnp.repeat(jnp.arange(rows), GRID_W)
    col = jnp.tile(jnp.arange(GRID_W), rows)
    ang_a = rope_angles(pos, A_ROT, ROPE_THETA)
    ang_c = rope_angles(pos, C_ROT, ROPE_THETA)
    ang_row = rope_angles(row, B_HEAD_DIM // 2, AXIAL_THETA)
    ang_col = rope_angles(col, B_HEAD_DIM // 2, AXIAL_THETA)
    for l in range(DEPTH):
        lam_init = 0.8 - 0.6 * math.exp(-0.3 * l)
        x = encoder_layer(x, c, ang_a, ang_row, ang_col, ang_c, lam_init,
                          *[p[l] for p in layer_params])
    return x


def setup_inputs(seed: int = 0) -> dict:
    key = jax.random.key(seed)
    ks = jax.random.split(key, 26)
    f32 = jnp.float32

    def nrm(k, shape, s):
        return jax.random.normal(k, shape, f32) * s

    D = D_MODEL
    return {
        'x_prompt': nrm(ks[0], (BATCH, SEQ, D), 1.0),
        'x_sample': nrm(ks[1], (DEC_BATCH, DEC_SEQ, D), 1.0),
        'c_prompt': nrm(ks[2], (BATCH, D), 1.0),
        'c_sample': nrm(ks[3], (DEC_BATCH, D), 1.0),
        'norm_g': 1.0 + nrm(ks[4], (DEPTH, D), 0.05),
        'w_ada': nrm(ks[5], (DEPTH, D, 3 * D), D ** -0.5),
        'b_ada': nrm(ks[6], (DEPTH, 3 * D), 0.02),
        'w_in': nrm(ks[7], (DEPTH, D, D_IN), D ** -0.5),
        'qn_a': 1.0 + nrm(ks[8], (DEPTH, A_HEAD_DIM), 0.05),
        'kn_a': 1.0 + nrm(ks[9], (DEPTH, A_HEAD_DIM), 0.05),
        'qn_b': 1.0 + nrm(ks[10], (DEPTH, B_HEAD_DIM), 0.05),
        'kn_b': 1.0 + nrm(ks[11], (DEPTH, B_HEAD_DIM), 0.05),
        'qn_c': 1.0 + nrm(ks[12], (DEPTH, C_HEAD_DIM), 0.05),
        'kn_c': 1.0 + nrm(ks[13], (DEPTH, C_HEAD_DIM), 0.05),
        'lam_q1': nrm(ks[14], (DEPTH, C_HEAD_DIM), 0.1),
        'lam_k1': nrm(ks[15], (DEPTH, C_HEAD_DIM), 0.1),
        'lam_q2': nrm(ks[16], (DEPTH, C_HEAD_DIM), 0.1),
        'lam_k2': nrm(ks[17], (DEPTH, C_HEAD_DIM), 0.1),
        'subln_c': 1.0 + nrm(ks[18], (DEPTH, 2 * C_HEAD_DIM), 0.05),
        'w_oa': nrm(ks[19], (DEPTH, A_WIDTH, D), A_WIDTH ** -0.5),
        'w_ob': nrm(ks[20], (DEPTH, B_WIDTH, D), B_WIDTH ** -0.5),
        'w_oc': nrm(ks[21], (DEPTH, C_WIDTH, D), C_WIDTH ** -0.5),
        'w_bg': nrm(ks[22], (DEPTH, D, N_BRANCH * D), D ** -0.5),
        'b_bg': nrm(ks[23], (DEPTH, N_BRANCH * D), 0.02),
        'w_out': nrm(ks[24], (DEPTH, D, D), D ** -0.5),
    }


def reference(x_prompt, x_sample, c_prompt, c_sample, norm_g, w_ada, b_ada, w_in,
              qn_a, kn_a, qn_b, kn_b, qn_c, kn_c, lam_q1, lam_k1, lam_q2, lam_k2, subln_c,
              w_oa, w_ob, w_oc, w_bg, b_bg, w_out):
    layer_params = (norm_g, w_ada, b_ada, w_in, qn_a, kn_a, qn_b, kn_b, qn_c, kn_c,
                    lam_q1, lam_k1, lam_q2, lam_k2, subln_c, w_oa, w_ob, w_oc, w_bg, b_bg, w_out)
    y_prompt = trunk(x_prompt, c_prompt, layer_params)
    y_sample = trunk(x_sample, c_sample, layer_params)
    return (y_prompt, y_sample)
```

```python
import functools
import math

import jax
import jax.numpy as jnp
from jax import lax
from jax.experimental import pallas as pl
from jax.experimental.pallas import tpu as pltpu

D_MODEL = 1024
DEPTH = 2
GRID_W = 64
EPS = 1e-6
NEG = -1e30
ROPE_THETA = 500000.0
AXIAL_THETA = 10000.0

A_GROUPS = ((128, 1), (512, 4), (2048, 16))
A_HEADS = 4
A_HEAD_DIM = 96
A_ROT = A_HEAD_DIM // 4
A_BAND = 64
N_A = len(A_GROUPS) * A_HEADS

B_HEADS = 6
B_KV_HEADS = 2
B_HEAD_DIM = 64
C_HEADS = 4
C_HEAD_DIM = 64
C_ROT = C_HEAD_DIM // 4

A_QKV = N_A * A_HEAD_DIM
A_WIDTH = A_HEADS * A_HEAD_DIM
B_WIDTH = B_HEADS * B_HEAD_DIM
B_KV = B_KV_HEADS * B_HEAD_DIM
C_WIDTH = C_HEADS * 2 * C_HEAD_DIM
IN_SIZES = (A_QKV, A_QKV, A_QKV, A_WIDTH, B_WIDTH, B_KV, B_KV, B_WIDTH, C_WIDTH, C_WIDTH, C_WIDTH, C_WIDTH)

LANE = 128
HALF = LANE // 2
A_SLAB = A_HEADS * LANE
VMEM_LIMIT = 48 * 1024 * 1024

BF16 = jnp.bfloat16
F32 = jnp.float32


def _cparams(n_axes):
    return pltpu.CompilerParams(dimension_semantics=("arbitrary",) * n_axes, vmem_limit_bytes=VMEM_LIMIT)


def _dot(a, b):
    return jnp.dot(a, b, preferred_element_type=F32)


def _dot_nt(a, b):
    return lax.dot_general(a, b, (((1,), (1,)), ((), ())), preferred_element_type=F32)


def _silu(x):
    return x * jax.nn.sigmoid(x)


def _ada_kernel(c_ref, w_ref, b_ref, o_ref):
    c = c_ref[...]
    o_ref[...] = _dot(_silu(c).astype(BF16), w_ref[...]) + b_ref[...]


def _ada_mod(c_all, w_ada, b_ada):
    rows = c_all.shape[0]
    return pl.pallas_call(
        _ada_kernel,
        grid=(DEPTH, 3),
        in_specs=[
            pl.BlockSpec((rows, D_MODEL), lambda l, n: (0, 0)),
            pl.BlockSpec((None, D_MODEL, D_MODEL), lambda l, n: (l, 0, n)),
            pl.BlockSpec((None, 1, D_MODEL), lambda l, n: (l, 0, n)),
        ],
        out_specs=pl.BlockSpec((None, rows, D_MODEL), lambda l, n: (l, 0, n)),
        out_shape=jax.ShapeDtypeStruct((DEPTH, rows, 3 * D_MODEL), F32),
        compiler_params=_cparams(2),
        name="ada_mod",
    )(c_all, w_ada, b_ada.reshape(DEPTH, 1, 3 * D_MODEL))


def _modulated_norm(x, g, scale, shift):
    ms = jnp.mean(x * x, axis=-1, keepdims=True)
    h = x * lax.rsqrt(ms + EPS) * g
    return h * (1.0 + scale) + shift


def _head_norm_rope(u, gain, cos, sin_lo, sin_hi, shift, two_heads, head_dim):
    sq = u * u
    if two_heads:
        lane = lax.broadcasted_iota(jnp.int32, (1, LANE), 1)
        first = lane < HALF
        s_lo = jnp.sum(jnp.where(first, sq, 0.0), axis=-1, keepdims=True)
        s_all = jnp.sum(sq, axis=-1, keepdims=True)
        ms = jnp.where(first, s_lo, s_all - s_lo) * (1.0 / head_dim)
    else:
        ms = jnp.sum(sq, axis=-1, keepdims=True) * (1.0 / head_dim)
    y = u * lax.rsqrt(ms + EPS) * gain
    return y * cos + pltpu.roll(y, LANE - shift, 1) * sin_lo + pltpu.roll(y, shift, 1) * sin_hi


def _proj_kernel(x_ref, shift_ref, scale_ref, g_ref, w_ref, gain_ref, cos_ref, slo_ref, shi_ref, *out_refs,
                 outs, rot_shift, two_heads, head_dim, q_scale):
    h = _modulated_norm(x_ref[...], g_ref[...], scale_ref[...], shift_ref[...]).astype(BF16)
    cos, s_lo, s_hi = cos_ref[...], slo_ref[...], shi_ref[...]
    col = 0
    for o_ref, (width, kind) in zip(out_refs, outs):
        u = _dot(h, w_ref[:, col:col + width])
        if kind == "v":
            o_ref[...] = u.astype(BF16)
        else:
            for s in range(width // LANE):
                lo = s * LANE
                y = _head_norm_rope(u[:, lo:lo + LANE], gain_ref[:, col + lo:col + lo + LANE], cos, s_lo, s_hi,
                                    rot_shift, two_heads, head_dim)
                if kind == "q" and q_scale != 1.0:
                    y = y * q_scale
                o_ref[:, lo:lo + LANE] = y.astype(BF16)
        col += width


def _project(x, shift, scale, norm_g, w, gain, tables, outs, *, rot_shift, two_heads, head_dim, q_scale, tm, name):
    bn, s, _ = x.shape
    wcols = w.shape[1]
    kern = functools.partial(_proj_kernel, outs=outs, rot_shift=rot_shift, two_heads=two_heads,
                             head_dim=head_dim, q_scale=q_scale)
    vec = pl.BlockSpec((None, 1, D_MODEL), lambda b, i: (b, 0, 0))
    tab = pl.BlockSpec((tm, LANE), lambda b, i: (i, 0))
    return pl.pallas_call(
        kern,
        grid=(bn, s // tm),
        in_specs=[
            pl.BlockSpec((None, tm, D_MODEL), lambda b, i: (b, i, 0)),
            vec, vec,
            pl.BlockSpec((1, D_MODEL), lambda b, i: (0, 0)),
            pl.BlockSpec((D_MODEL, wcols), lambda b, i: (0, 0)),
            pl.BlockSpec((1, wcols), lambda b, i: (0, 0)),
            tab, tab, tab,
        ],
        out_specs=[pl.BlockSpec((None, tm, width), lambda b, i: (b, i, 0)) for width, _ in outs],
        out_shape=[jax.ShapeDtypeStruct((bn, s, width), BF16) for width, _ in outs],
        compiler_params=_cparams(2),
        name=name,
    )(x, shift, scale, norm_g, w, gain, *tables)


def _band_attn_kernel(q_ref, kp_ref, kc_ref, kn_ref, vp_ref, vc_ref, vn_ref, o_ref, *, tb, seq_len, scale):
    i = pl.program_id(2)
    nk = tb + 2 * A_BAND
    row = lax.broadcasted_iota(jnp.int32, (tb, nk), 0)
    colk = lax.broadcasted_iota(jnp.int32, (tb, nk), 1)
    dist = colk - A_BAND - row
    kpos = i * tb - A_BAND + colk
    valid = (jnp.abs(dist) <= A_BAND) & (kpos >= 0) & (kpos < seq_len)
    lane = lax.broadcasted_iota(jnp.int32, (1, LANE), 1)
    for j in range(A_HEADS):
        sl = slice(j * LANE, (j + 1) * LANE)
        q = q_ref[:, sl]
        k = jnp.concatenate([kp_ref[tb - A_BAND:, sl], kc_ref[:, sl], kn_ref[:A_BAND, sl]], axis=0)
        v = jnp.concatenate([vp_ref[tb - A_BAND:, sl], vc_ref[:, sl], vn_ref[:A_BAND, sl]], axis=0)
        s = jnp.where(valid, _dot_nt(q, k) * scale, NEG)
        m = jnp.max(s, axis=-1, keepdims=True)
        p = jnp.exp(s - m)
        l = jnp.sum(p, axis=-1, keepdims=True)
        o = _dot(p.astype(BF16), v) / l
        lse = m + jnp.log(l)
        o_ref[:, sl] = jnp.where(lane >= A_HEAD_DIM, lse, o)


def _band_attention(q, k, v, dilation):
    bn, s, _ = q.shape
    length = s // dilation
    tb = min(256, length)
    nb = length // tb
    view = lambda t: t.reshape(bn, length, dilation * A_SLAB)
    blk = lambda f: pl.BlockSpec((None, tb, A_SLAB), f)
    cur = blk(lambda b, r, i: (b, i, r))
    prv = blk(lambda b, r, i: (b, jnp.maximum(i - 1, 0), r))
    nxt = blk(lambda b, r, i: (b, jnp.minimum(i + 1, nb - 1), r))
    kern = functools.partial(_band_attn_kernel, tb=tb, seq_len=length, scale=A_HEAD_DIM ** -0.5)
    out = pl.pallas_call(
        kern,
        grid=(bn, dilation, nb),
        in_specs=[cur, prv, cur, nxt, prv, cur, nxt],
        out_specs=cur,
        out_shape=jax.ShapeDtypeStruct((bn, length, dilation * A_SLAB), F32),
        compiler_params=_cparams(3),
        name=f"band_attn_d{dilation}",
    )(view(q), view(k), view(k), view(k), view(v), view(v), view(v))
    return out.reshape(bn, s, A_SLAB)


def _stacked_flash(q_ref, k_ref, v_ref, m_sc, l_sc, acc_sc, *, tq, tk, seq_len):
    lane = lax.broadcasted_iota(jnp.int32, (1, LANE), 1)
    q = q_ref[...]
    zero = jnp.zeros_like(q)
    q2 = jnp.concatenate([jnp.where(lane < HALF, q, zero), jnp.where(lane >= HALF, q, zero)], axis=0)
    m_sc[...] = jnp.full(m_sc.shape, -jnp.inf, F32)
    l_sc[...] = jnp.zeros(l_sc.shape, F32)
    acc_sc[...] = jnp.zeros(acc_sc.shape, F32)

    def body(c, carry):
        start = pl.multiple_of(c * tk, tk)
        kc = k_ref[pl.ds(start, tk), :]
        vc = v_ref[pl.ds(start, tk), :]
        s = _dot_nt(q2, kc)
        m_prev = m_sc[...]
        m_next = jnp.maximum(m_prev, jnp.max(s, axis=-1, keepdims=True))
        alpha = jnp.exp(m_prev - m_next)
        p = jnp.exp(s - m_next[:, :1])
        l_sc[...] = alpha * l_sc[...] + jnp.sum(p, axis=-1, keepdims=True)
        acc_sc[...] = alpha * acc_sc[...] + _dot(p.astype(BF16), vc)
        m_sc[...] = m_next
        return carry

    lax.fori_loop(0, seq_len // tk, body, 0)


def _gqa_kernel(q_ref, k_ref, v_ref, o_ref, m_sc, l_sc, acc_sc, *, tq, tk, seq_len):
    _stacked_flash(q_ref, k_ref, v_ref, m_sc, l_sc, acc_sc, tq=tq, tk=tk, seq_len=seq_len)
    lane = lax.broadcasted_iota(jnp.int32, (1, LANE), 1)
    o = acc_sc[...] / l_sc[...]
    o_ref[...] = jnp.where(lane < HALF, o[:tq], o[tq:])


def _diff_kernel(q_ref, k_ref, v_ref, lq1_ref, lk1_ref, lq2_ref, lk2_ref, sub_ref, o_ref, m_sc, l_sc, acc_sc,
                 *, tq, tk, seq_len, lam_init):
    _stacked_flash(q_ref, k_ref, v_ref, m_sc, l_sc, acc_sc, tq=tq, tk=tk, seq_len=seq_len)
    lam = (jnp.exp(jnp.sum(lq1_ref[...] * lk1_ref[...], axis=-1, keepdims=True))
           - jnp.exp(jnp.sum(lq2_ref[...] * lk2_ref[...], axis=-1, keepdims=True)) + lam_init)
    o = acc_sc[...] / l_sc[...]
    oc = o[:tq] - lam * o[tq:]
    ms = jnp.mean(oc * oc, axis=-1, keepdims=True)
    o_ref[...] = oc * lax.rsqrt(ms + EPS) * sub_ref[...] * (1.0 - lam_init)


def _flash_scratch(tq):
    return [pltpu.VMEM((2 * tq, LANE), F32), pltpu.VMEM((2 * tq, LANE), F32), pltpu.VMEM((2 * tq, LANE), F32)]


def _flash_tiles(s):
    return min(256, s), min(512, s)


def _gqa_attention(q, k, v):
    bn, s, _ = q.shape
    tq, tk = _flash_tiles(s)
    kern = functools.partial(_gqa_kernel, tq=tq, tk=tk, seq_len=s)
    qo = pl.BlockSpec((None, tq, LANE), lambda b, p, i: (b, i, p))
    kv = pl.BlockSpec((None, s, LANE), lambda b, p, i: (b, 0, 0))
    return pl.pallas_call(
        kern,
        grid=(bn, B_HEADS // 2, s // tq),
        in_specs=[qo, kv, kv],
        out_specs=qo,
        out_shape=jax.ShapeDtypeStruct((bn, s, B_WIDTH), F32),
        scratch_shapes=_flash_scratch(tq),
        compiler_params=_cparams(3),
        name="gqa_attn",
    )(q, k, v)


def _diff_attention(q, k, v, lam_rows, subln, lam_init):
    bn, s, _ = q.shape
    tq, tk = _flash_tiles(s)
    kern = functools.partial(_diff_kernel, tq=tq, tk=tk, seq_len=s, lam_init=lam_init)
    qo = pl.BlockSpec((None, tq, LANE), lambda b, h, i: (b, i, h))
    kv = pl.BlockSpec((None, s, LANE), lambda b, h, i: (b, 0, h))
    row = lambda n: pl.BlockSpec((1, n), lambda b, h, i: (0, 0))
    return pl.pallas_call(
        kern,
        grid=(bn, C_HEADS, s // tq),
        in_specs=[qo, kv, kv, row(C_HEAD_DIM), row(C_HEAD_DIM), row(C_HEAD_DIM), row(C_HEAD_DIM), row(LANE)],
        out_specs=qo,
        out_shape=jax.ShapeDtypeStruct((bn, s, C_WIDTH), F32),
        scratch_shapes=_flash_scratch(tq),
        compiler_params=_cparams(3),
        name="diff_attn",
    )(q, k, v, *lam_rows, subln)


def _final_kernel(x_ref, shift_ref, scale_ref, gate_ref, g_ref, oa0_ref, oa1_ref, oa2_ref, yb_ref, yc_ref,
                  wz_ref, wbg_ref, bbg_ref, woa_ref, wob_ref, woc_ref, wout_ref, o_ref):
    x = x_ref[...]
    h = _modulated_norm(x, g_ref[...], scale_ref[...], shift_ref[...]).astype(BF16)
    lane = lax.broadcasted_iota(jnp.int32, (1, LANE), 1)

    ya = []
    for j in range(A_HEADS):
        sl = slice(j * LANE, (j + 1) * LANE)
        slabs = [r[:, sl] for r in (oa0_ref, oa1_ref, oa2_ref)]
        lses = [t[:, A_HEAD_DIM:A_HEAD_DIM + 1] for t in slabs]
        top = jnp.maximum(jnp.maximum(lses[0], lses[1]), lses[2])
        ws = [jnp.exp(t - top) for t in lses]
        mix = (ws[0] * slabs[0] + ws[1] * slabs[1] + ws[2] * slabs[2]) / (ws[0] + ws[1] + ws[2])
        ya.append(jnp.where(lane < A_HEAD_DIM, mix, 0.0))
    ya = jnp.concatenate(ya, axis=-1)

    def branch(y, z_lo, z_hi, wo_ref, g_lo):
        z = _dot(h, wz_ref[:, z_lo:z_hi])
        p = _dot((y * _silu(z)).astype(BF16), wo_ref[...])
        gate = jax.nn.sigmoid(_dot(h, wbg_ref[:, g_lo:g_lo + D_MODEL]) + bbg_ref[:, g_lo:g_lo + D_MODEL])
        return gate * p

    za_hi = A_SLAB
    zb_hi = za_hi + B_WIDTH
    zc_hi = zb_hi + C_WIDTH
    merged = (branch(ya, 0, za_hi, woa_ref, 0)
              + branch(yb_ref[...], za_hi, zb_hi, wob_ref, D_MODEL)
              + branch(yc_ref[...], zb_hi, zc_hi, woc_ref, 2 * D_MODEL))
    out = _dot(merged.astype(BF16), wout_ref[...])
    o_ref[...] = x + gate_ref[...] * out


def _final(x, shift, scale, gate, norm_g, oa, yb, yc, wz, wbg, bbg, woa, wob, woc, wout, *, tm):
    bn, s, _ = x.shape
    tok = lambda w: pl.BlockSpec((None, tm, w), lambda b, i: (b, i, 0))
    vec = pl.BlockSpec((None, 1, D_MODEL), lambda b, i: (b, 0, 0))
    full = lambda a: pl.BlockSpec(a.shape, lambda b, i: (0, 0))
    return pl.pallas_call(
        _final_kernel,
        grid=(bn, s // tm),
        in_specs=[tok(D_MODEL), vec, vec, vec, full(norm_g), tok(A_SLAB), tok(A_SLAB), tok(A_SLAB),
                  tok(B_WIDTH), tok(C_WIDTH), full(wz), full(wbg), full(bbg), full(woa), full(wob), full(woc),
                  full(wout)],
        out_specs=tok(D_MODEL),
        out_shape=jax.ShapeDtypeStruct((bn, s, D_MODEL), F32),
        compiler_params=_cparams(2),
        name="final",
    )(x, shift, scale, gate, norm_g, *oa, yb, yc, wz, wbg, bbg, woa, wob, woc, wout)


B_Q_ORDER = (0, 3, 1, 4, 2, 5)


def _pad_heads(w, n_heads):
    lead = w.shape[:-1]
    w = w.reshape(*lead, n_heads, A_HEAD_DIM)
    w = jnp.pad(w, [(0, 0)] * len(lead) + [(0, 0), (0, LANE - A_HEAD_DIM)])
    return w.reshape(*lead, n_heads * LANE)


def _permute_heads(w, order, dim):
    lead = w.shape[:-1]
    w = w.reshape(*lead, len(order), dim)
    return w[..., jnp.array(order), :].reshape(*lead, len(order) * dim)


def _prep_layer(w_in, qn_a, kn_a, qn_b, kn_b, qn_c, kn_c, w_oa, w_ob, w_oc):
    offs = [0]
    for n in IN_SIZES:
        offs.append(offs[-1] + n)
    qa, ka, va, za, qb, kb, vb, zb, qc, kc, vc, zc = [w_in[:, offs[i]:offs[i + 1]] for i in range(len(IN_SIZES))]

    w_a = jnp.concatenate([_pad_heads(qa, N_A), _pad_heads(ka, N_A), _pad_heads(va, N_A)], axis=-1).astype(BF16)
    pad_gain = lambda g: jnp.tile(jnp.pad(g, (0, LANE - A_HEAD_DIM)), N_A)
    gain_a = jnp.concatenate([pad_gain(qn_a), pad_gain(kn_a), jnp.zeros((N_A * LANE,), F32)])[None, :]

    w_b = jnp.concatenate([_permute_heads(qb, B_Q_ORDER, B_HEAD_DIM), kb, vb], axis=-1).astype(BF16)
    gain_b = jnp.concatenate([jnp.tile(qn_b, B_HEADS), jnp.tile(kn_b, B_KV_HEADS), jnp.zeros((B_KV,), F32)])[None, :]

    w_c = jnp.concatenate([qc, kc, vc], axis=-1).astype(BF16)
    gain_c = jnp.concatenate([jnp.tile(qn_c, 2 * C_HEADS), jnp.tile(kn_c, 2 * C_HEADS),
                              jnp.zeros((C_WIDTH,), F32)])[None, :]

    w_z = jnp.concatenate([_pad_heads(za, A_HEADS), _permute_heads(zb, B_Q_ORDER, B_HEAD_DIM), zc],
                          axis=-1).astype(BF16)
    w_oa_p = _pad_heads(w_oa.T, A_HEADS).T.astype(BF16)
    w_ob_p = _permute_heads(w_ob.T, B_Q_ORDER, B_HEAD_DIM).T.astype(BF16)
    return dict(w_a=w_a, gain_a=gain_a, w_b=w_b, gain_b=gain_b, w_c=w_c, gain_c=gain_c, w_z=w_z,
                w_oa=w_oa_p, w_ob=w_ob_p, w_oc=w_oc.astype(BF16))


def _rope_angles(pos, dim, theta):
    inv = theta ** (-jnp.arange(0, dim, 2, dtype=F32) / dim)
    return pos.astype(F32)[:, None] * inv[None, :]


def _rope_tables(segments, s):
    cos = jnp.ones((s, LANE), F32)
    s_lo = jnp.zeros((s, LANE), F32)
    s_hi = jnp.zeros((s, LANE), F32)
    for first, half, ang in segments:
        c, sn = jnp.cos(ang), jnp.sin(ang)
        cos = cos.at[:, first:first + half].set(c).at[:, first + half:first + 2 * half].set(c)
        s_lo = s_lo.at[:, first:first + half].set(-sn)
        s_hi = s_hi.at[:, first + half:first + 2 * half].set(sn)
    return cos, s_lo, s_hi


def _tables(s):
    pos = jnp.arange(s)
    ang_a = _rope_angles(pos, A_ROT, ROPE_THETA)
    ang_c = _rope_angles(pos, C_ROT, ROPE_THETA)
    ang_row = _rope_angles(pos // GRID_W, B_HEAD_DIM // 2, AXIAL_THETA)
    ang_col = _rope_angles(pos % GRID_W, B_HEAD_DIM // 2, AXIAL_THETA)
    quarter = B_HEAD_DIM // 4
    tab_a = _rope_tables([(0, A_ROT // 2, ang_a)], s)
    tab_b = _rope_tables([(base + off, quarter, ang) for base in (0, HALF)
                          for off, ang in ((0, ang_row), (2 * quarter, ang_col))], s)
    tab_c = _rope_tables([(base, C_ROT // 2, ang_c) for base in (0, HALF)], s)
    return tab_a, tab_b, tab_c


def _layer(x, mod, l, lp, tabs, norm_g, lam_rows, subln, w_bg, b_bg, w_out):
    bn, s, _ = x.shape
    shift, scale, gate = [m.reshape(bn, 1, D_MODEL) for m in jnp.split(mod, 3, axis=-1)]
    tm = min(256, s)
    tab_a, tab_b, tab_c = tabs
    lam_init = 0.8 - 0.6 * math.exp(-0.3 * l)

    a_outs = tuple((A_SLAB, kind) for kind in ("q", "k", "v") for _ in A_GROUPS)
    pa = _project(x, shift, scale, norm_g, lp["w_a"], lp["gain_a"], tab_a, a_outs, rot_shift=A_ROT // 2,
                  two_heads=False, head_dim=A_HEAD_DIM, q_scale=1.0, tm=tm, name="proj_a")
    n_g = len(A_GROUPS)
    oa = [_band_attention(pa[g], pa[n_g + g], pa[2 * n_g + g], dil) for g, (_, dil) in enumerate(A_GROUPS)]

    qb, kb, vb = _project(x, shift, scale, norm_g, lp["w_b"], lp["gain_b"], tab_b,
                          ((B_WIDTH, "q"), (B_KV, "k"), (B_KV, "v")), rot_shift=B_HEAD_DIM // 4,
                          two_heads=True, head_dim=B_HEAD_DIM, q_scale=B_HEAD_DIM ** -0.5, tm=tm, name="proj_b")
    yb = _gqa_attention(qb, kb, vb)

    qc, kc, vc = _project(x, shift, scale, norm_g, lp["w_c"], lp["gain_c"], tab_c,
                          ((C_WIDTH, "q"), (C_WIDTH, "k"), (C_WIDTH, "v")), rot_shift=C_ROT // 2,
                          two_heads=True, head_dim=C_HEAD_DIM, q_scale=C_HEAD_DIM ** -0.5, tm=tm, name="proj_c")
    yc = _diff_attention(qc, kc, vc, lam_rows, subln, lam_init)

    return _final(x, shift, scale, gate, norm_g, oa, yb, yc, lp["w_z"], w_bg, b_bg, lp["w_oa"], lp["w_ob"],
                  lp["w_oc"], w_out, tm=tm)


def kernel(x_prompt, x_sample, c_prompt, c_sample, norm_g, w_ada, b_ada, w_in, qn_a, kn_a, qn_b, kn_b, qn_c, kn_c,
           lam_q1, lam_k1, lam_q2, lam_k2, subln_c, w_oa, w_ob, w_oc, w_bg, b_bg, w_out):
    groups = ((x_prompt, c_prompt), (x_sample, c_sample))
    rows = [c.shape[0] for _, c in groups]
    pad = -sum(rows) % 8
    c_all = jnp.concatenate([c for _, c in groups] + [jnp.zeros((pad, D_MODEL), F32)], axis=0)
    mod_all = _ada_mod(c_all, w_ada.astype(BF16), b_ada)

    layers = [_prep_layer(w_in[l], qn_a[l], kn_a[l], qn_b[l], kn_b[l], qn_c[l], kn_c[l], w_oa[l], w_ob[l], w_oc[l])
              for l in range(DEPTH)]
    w_bg16, w_out16 = w_bg.astype(BF16), w_out.astype(BF16)

    outs = []
    row0 = 0
    for (x, c), n in zip(groups, rows):
        tabs = _tables(x.shape[1])
        for l in range(DEPTH):
            lam_rows = [p[l][None, :] for p in (lam_q1, lam_k1, lam_q2, lam_k2)]
            x = _layer(x, mod_all[l, row0:row0 + n], l, layers[l], tabs, norm_g[l][None, :], lam_rows,
                       subln_c[l][None, :], w_bg16[l], b_bg[l][None, :], w_out16[l])
        outs.append(x)
        row0 += n
    return tuple(outs)
```

```python
import functools
import math

import jax
import jax.numpy as jnp
from jax import lax
from jax.experimental import pallas as pl
from jax.experimental.pallas import tpu as pltpu

D_MODEL = 1024
DEPTH = 2
GRID_W = 64
EPS = 1e-6
NEG = -1e30
ROPE_THETA = 500000.0
AXIAL_THETA = 10000.0

A_GROUPS = ((128, 1), (512, 4), (2048, 16))
A_HEADS = 4
A_HEAD_DIM = 96
A_ROT = A_HEAD_DIM // 4
A_BAND = 64
N_A = len(A_GROUPS) * A_HEADS

B_HEADS = 6
B_KV_HEADS = 2
B_HEAD_DIM = 64
C_HEADS = 4
C_HEAD_DIM = 64
C_ROT = C_HEAD_DIM // 4

A_QKV = N_A * A_HEAD_DIM
A_WIDTH = A_HEADS * A_HEAD_DIM
B_WIDTH = B_HEADS * B_HEAD_DIM
B_KV = B_KV_HEADS * B_HEAD_DIM
C_WIDTH = C_HEADS * 2 * C_HEAD_DIM
IN_SIZES = (A_QKV, A_QKV, A_QKV, A_WIDTH, B_WIDTH, B_KV, B_KV, B_WIDTH, C_WIDTH, C_WIDTH, C_WIDTH, C_WIDTH)

LANE = 128
HALF = LANE // 2
A_SLAB = A_HEADS * LANE
BF16_SUBLANES = 16
V_ROWS = LANE + BF16_SUBLANES
FLASH_TK = 512
LOG2E = math.log2(math.e)
VMEM_LIMIT = 48 * 1024 * 1024

BF16 = jnp.bfloat16
F32 = jnp.float32


def _cparams(n_axes):
    return pltpu.CompilerParams(dimension_semantics=("arbitrary",) * n_axes, vmem_limit_bytes=VMEM_LIMIT)


def _dot(a, b):
    return jnp.dot(a, b, preferred_element_type=F32)


def _dot_nt(a, b):
    return lax.dot_general(a, b, (((1,), (1,)), ((), ())), preferred_element_type=F32)


def _silu(x):
    return x * jax.nn.sigmoid(x)


def _ada_kernel(c_ref, w_ref, b_ref, o_ref):
    c = c_ref[...]
    o_ref[...] = _dot(_silu(c).astype(BF16), w_ref[...]) + b_ref[...]


def _ada_mod(c_all, w_ada, b_ada):
    rows = c_all.shape[0]
    return pl.pallas_call(
        _ada_kernel,
        grid=(DEPTH, 3),
        in_specs=[
            pl.BlockSpec((rows, D_MODEL), lambda l, n: (0, 0)),
            pl.BlockSpec((None, D_MODEL, D_MODEL), lambda l, n: (l, 0, n)),
            pl.BlockSpec((None, 1, D_MODEL), lambda l, n: (l, 0, n)),
        ],
        out_specs=pl.BlockSpec((None, rows, D_MODEL), lambda l, n: (l, 0, n)),
        out_shape=jax.ShapeDtypeStruct((DEPTH, rows, 3 * D_MODEL), F32),
        compiler_params=_cparams(2),
        name="ada_mod",
    )(c_all, w_ada, b_ada.reshape(DEPTH, 1, 3 * D_MODEL))


def _modulated_norm(x, g, scale, shift):
    ms = jnp.mean(x * x, axis=-1, keepdims=True)
    h = x * lax.rsqrt(ms + EPS) * g
    return h * (1.0 + scale) + shift


def _head_norm_rope(u, gain, cos, sin_lo, sin_hi, shift, two_heads, head_dim):
    sq = u * u
    if two_heads:
        lane = lax.broadcasted_iota(jnp.int32, (1, LANE), 1)
        first = lane < HALF
        s_lo = jnp.sum(jnp.where(first, sq, 0.0), axis=-1, keepdims=True)
        s_all = jnp.sum(sq, axis=-1, keepdims=True)
        ms = jnp.where(first, s_lo, s_all - s_lo) * (1.0 / head_dim)
    else:
        ms = jnp.sum(sq, axis=-1, keepdims=True) * (1.0 / head_dim)
    y = u * lax.rsqrt(ms + EPS) * gain
    return y * cos + pltpu.roll(y, LANE - shift, 1) * sin_lo + pltpu.roll(y, shift, 1) * sin_hi


def _proj_kernel(x_ref, shift_ref, scale_ref, g_ref, w_ref, gain_ref, cos_ref, slo_ref, shi_ref, *out_refs,
                 outs, rot_shift, two_heads, head_dim, q_scale):
    h = _modulated_norm(x_ref[...], g_ref[...], scale_ref[...], shift_ref[...]).astype(BF16)
    cos, s_lo, s_hi = cos_ref[...], slo_ref[...], shi_ref[...]
    tm = x_ref.shape[0]
    col = 0
    for o_ref, (width, kind) in zip(out_refs, outs):
        u = _dot(h, w_ref[:, col:col + width])
        if kind == "v":
            o_ref[...] = u.astype(BF16)
        elif kind == "vT":
            for s in range(width // LANE):
                o_ref[s * V_ROWS:s * V_ROWS + LANE, :] = u[:, s * LANE:(s + 1) * LANE].T.astype(BF16)
                o_ref[s * V_ROWS + LANE:(s + 1) * V_ROWS, :] = jnp.ones((V_ROWS - LANE, tm), BF16)
        else:
            for s in range(width // LANE):
                lo = s * LANE
                y = _head_norm_rope(u[:, lo:lo + LANE], gain_ref[:, col + lo:col + lo + LANE], cos, s_lo, s_hi,
                                    rot_shift, two_heads, head_dim)
                if kind in ("q", "qT") and q_scale != 1.0:
                    y = y * q_scale
                if kind == "qT":
                    o_ref[lo:lo + LANE, :] = y.T.astype(BF16)
                else:
                    o_ref[:, lo:lo + LANE] = y.astype(BF16)
        col += width


def _proj_out(kind, width, bn, s, tm):
    if kind == "qT":
        return (bn, width, s), pl.BlockSpec((None, width, tm), lambda b, i: (b, 0, i))
    if kind == "vT":
        per = FLASH_TK // tm
        rows = width // LANE * V_ROWS
        return ((bn, s // FLASH_TK, rows, FLASH_TK),
                pl.BlockSpec((None, None, rows, tm), lambda b, i: (b, i // per, 0, i % per)))
    return (bn, s, width), pl.BlockSpec((None, tm, width), lambda b, i: (b, i, 0))


def _project(x, shift, scale, norm_g, w, gain, tables, outs, *, rot_shift, two_heads, head_dim, q_scale, tm, name):
    bn, s, _ = x.shape
    wcols = w.shape[1]
    kern = functools.partial(_proj_kernel, outs=outs, rot_shift=rot_shift, two_heads=two_heads,
                             head_dim=head_dim, q_scale=q_scale)
    vec = pl.BlockSpec((None, 1, D_MODEL), lambda b, i: (b, 0, 0))
    tab = pl.BlockSpec((tm, LANE), lambda b, i: (i, 0))
    shapes, specs = zip(*[_proj_out(kind, width, bn, s, tm) for width, kind in outs])
    return pl.pallas_call(
        kern,
        grid=(bn, s // tm),
        in_specs=[
            pl.BlockSpec((None, tm, D_MODEL), lambda b, i: (b, i, 0)),
            vec, vec,
            pl.BlockSpec((1, D_MODEL), lambda b, i: (0, 0)),
            pl.BlockSpec((D_MODEL, wcols), lambda b, i: (0, 0)),
            pl.BlockSpec((1, wcols), lambda b, i: (0, 0)),
            tab, tab, tab,
        ],
        out_specs=list(specs),
        out_shape=[jax.ShapeDtypeStruct(shape, BF16) for shape in shapes],
        compiler_params=_cparams(2),
        name=name,
    )(x, shift, scale, norm_g, w, gain, *tables)


def _band_attn_kernel(q_ref, kp_ref, kc_ref, kn_ref, vp_ref, vc_ref, vn_ref, o_ref, *, tb, seq_len, scale):
    i = pl.program_id(2)
    nk = tb + 2 * A_BAND
    row = lax.broadcasted_iota(jnp.int32, (tb, nk), 0)
    colk = lax.broadcasted_iota(jnp.int32, (tb, nk), 1)
    dist = colk - A_BAND - row
    kpos = i * tb - A_BAND + colk
    valid = (jnp.abs(dist) <= A_BAND) & (kpos >= 0) & (kpos < seq_len)
    lane = lax.broadcasted_iota(jnp.int32, (1, LANE), 1)
    for j in range(A_HEADS):
        sl = slice(j * LANE, (j + 1) * LANE)
        q = q_ref[:, sl]
        k = jnp.concatenate([kp_ref[tb - A_BAND:, sl], kc_ref[:, sl], kn_ref[:A_BAND, sl]], axis=0)
        v = jnp.concatenate([vp_ref[tb - A_BAND:, sl], vc_ref[:, sl], vn_ref[:A_BAND, sl]], axis=0)
        s = jnp.where(valid, _dot_nt(q, k) * scale, NEG)
        m = jnp.max(s, axis=-1, keepdims=True)
        p = jnp.exp(s - m)
        l = jnp.sum(p, axis=-1, keepdims=True)
        o = _dot(p.astype(BF16), v) / l
        lse = m + jnp.log(l)
        o_ref[:, sl] = jnp.where(lane >= A_HEAD_DIM, lse, o)


def _band_attention(q, k, v, dilation):
    bn, s, _ = q.shape
    length = s // dilation
    tb = min(256, length)
    nb = length // tb
    view = lambda t: t.reshape(bn, length, dilation * A_SLAB)
    blk = lambda f: pl.BlockSpec((None, tb, A_SLAB), f)
    cur = blk(lambda b, r, i: (b, i, r))
    prv = blk(lambda b, r, i: (b, jnp.maximum(i - 1, 0), r))
    nxt = blk(lambda b, r, i: (b, jnp.minimum(i + 1, nb - 1), r))
    kern = functools.partial(_band_attn_kernel, tb=tb, seq_len=length, scale=A_HEAD_DIM ** -0.5)
    out = pl.pallas_call(
        kern,
        grid=(bn, dilation, nb),
        in_specs=[cur, prv, cur, nxt, prv, cur, nxt],
        out_specs=cur,
        out_shape=jax.ShapeDtypeStruct((bn, length, dilation * A_SLAB), F32),
        compiler_params=_cparams(3),
        name=f"band_attn_d{dilation}",
    )(view(q), view(k), view(k), view(k), view(v), view(v), view(v))
    return out.reshape(bn, s, A_SLAB)


def _stacked_flash(qt_ref, k_ref, vt_ref, acc_sc, s_sc, *, tq, n_chunks):
    row = lax.broadcasted_iota(jnp.int32, (LANE, 1), 0)
    qt = qt_ref[...]
    zero = jnp.zeros_like(qt)
    q2t = jnp.concatenate([jnp.where(row < HALF, qt, zero), jnp.where(row >= HALF, qt, zero)], axis=1)
    acc_sc[...] = jnp.zeros(acc_sc.shape, F32)

    def scores(c, slot):
        start = pl.multiple_of(c * FLASH_TK, FLASH_TK)
        s_sc[slot] = _dot(k_ref[pl.ds(start, FLASH_TK), :], q2t)

    def consume(c, slot, m_prev):
        st = s_sc[slot]
        m_next = jnp.maximum(m_prev, jnp.max(st, axis=0, keepdims=True))
        pt = jnp.exp2(st - m_next).astype(BF16)
        acc_sc[...] = jnp.exp2(m_prev - m_next) * acc_sc[...] + _dot(vt_ref[c], pt)
        return m_next

    def pair(j, m):
        c = 2 * j
        scores(c + 1, 1)
        m = consume(c, 0, m)
        scores(c + 2, 0)
        return consume(c + 1, 1, m)

    scores(0, 0)
    m = lax.fori_loop(0, n_chunks // 2 - 1, pair, jnp.full((1, 2 * tq), -jnp.inf, F32))
    scores(n_chunks - 1, 1)
    m = consume(n_chunks - 2, 0, m)
    consume(n_chunks - 1, 1, m)
    acc = acc_sc[...]
    return acc[:LANE] / acc[LANE:LANE + 1]


def _gqa_kernel(qt_ref, k_ref, vt_ref, o_ref, acc_sc, s_sc, *, tq, n_chunks):
    o = _stacked_flash(qt_ref, k_ref, vt_ref, acc_sc, s_sc, tq=tq, n_chunks=n_chunks)
    row = lax.broadcasted_iota(jnp.int32, (LANE, 1), 0)
    o_ref[...] = jnp.where(row < HALF, o[:, :tq], o[:, tq:]).T


def _diff_kernel(qt_ref, k_ref, vt_ref, lq1_ref, lk1_ref, lq2_ref, lk2_ref, sub_ref, o_ref, acc_sc,
                 s_sc, *, tq, n_chunks, lam_init):
    o = _stacked_flash(qt_ref, k_ref, vt_ref, acc_sc, s_sc, tq=tq, n_chunks=n_chunks)
    lam = (jnp.exp(jnp.sum(lq1_ref[...] * lk1_ref[...], axis=-1, keepdims=True))
           - jnp.exp(jnp.sum(lq2_ref[...] * lk2_ref[...], axis=-1, keepdims=True)) + lam_init)
    oc = (o[:, :tq] - lam * o[:, tq:]).T
    ms = jnp.mean(oc * oc, axis=-1, keepdims=True)
    o_ref[...] = oc * lax.rsqrt(ms + EPS) * sub_ref[...] * (1.0 - lam_init)


def _flash_tq(s):
    return min(256, s)


def _flash_scratch(tq):
    return [pltpu.VMEM((V_ROWS, 2 * tq), F32), pltpu.VMEM((2, FLASH_TK, 2 * tq), F32)]


def _gqa_attention(qt, k, vt):
    bn, _, s = qt.shape
    tq, n_chunks = _flash_tq(s), s // FLASH_TK
    kern = functools.partial(_gqa_kernel, tq=tq, n_chunks=n_chunks)
    return pl.pallas_call(
        kern,
        grid=(bn, B_HEADS // 2, s // tq),
        in_specs=[pl.BlockSpec((None, LANE, tq), lambda b, p, i: (b, p, i)),
                  pl.BlockSpec((None, s, LANE), lambda b, p, i: (b, 0, 0)),
                  pl.BlockSpec((None, n_chunks, V_ROWS, FLASH_TK), lambda b, p, i: (b, 0, 0, 0))],
        out_specs=pl.BlockSpec((None, tq, LANE), lambda b, p, i: (b, i, p)),
        out_shape=jax.ShapeDtypeStruct((bn, s, B_WIDTH), F32),
        scratch_shapes=_flash_scratch(tq),
        compiler_params=_cparams(3),
        name="gqa_attn",
    )(qt, k, vt)


def _diff_attention(qt, k, vt, lam_rows, subln, lam_init):
    bn, _, s = qt.shape
    tq, n_chunks = _flash_tq(s), s // FLASH_TK
    kern = functools.partial(_diff_kernel, tq=tq, n_chunks=n_chunks, lam_init=lam_init)
    row = lambda n: pl.BlockSpec((1, n), lambda b, h, i: (0, 0))
    return pl.pallas_call(
        kern,
        grid=(bn, C_HEADS, s // tq),
        in_specs=[pl.BlockSpec((None, LANE, tq), lambda b, h, i: (b, h, i)),
                  pl.BlockSpec((None, s, LANE), lambda b, h, i: (b, 0, h)),
                  pl.BlockSpec((None, n_chunks, V_ROWS, FLASH_TK), lambda b, h, i: (b, 0, h, 0)),
                  row(C_HEAD_DIM), row(C_HEAD_DIM), row(C_HEAD_DIM), row(C_HEAD_DIM), row(LANE)],
        out_specs=pl.BlockSpec((None, tq, LANE), lambda b, h, i: (b, i, h)),
        out_shape=jax.ShapeDtypeStruct((bn, s, C_WIDTH), F32),
        scratch_shapes=_flash_scratch(tq),
        compiler_params=_cparams(3),
        name="diff_attn",
    )(qt, k, vt, *lam_rows, subln)


def _final_kernel(x_ref, shift_ref, scale_ref, gate_ref, g_ref, oa0_ref, oa1_ref, oa2_ref, yb_ref, yc_ref,
                  wz_ref, wbg_ref, bbg_ref, woa_ref, wob_ref, woc_ref, wout_ref, o_ref):
    x = x_ref[...]
    h = _modulated_norm(x, g_ref[...], scale_ref[...], shift_ref[...]).astype(BF16)
    lane = lax.broadcasted_iota(jnp.int32, (1, LANE), 1)

    ya = []
    for j in range(A_HEADS):
        sl = slice(j * LANE, (j + 1) * LANE)
        slabs = [r[:, sl] for r in (oa0_ref, oa1_ref, oa2_ref)]
        lses = [t[:, A_HEAD_DIM:A_HEAD_DIM + 1] for t in slabs]
        top = jnp.maximum(jnp.maximum(lses[0], lses[1]), lses[2])
        ws = [jnp.exp(t - top) for t in lses]
        mix = (ws[0] * slabs[0] + ws[1] * slabs[1] + ws[2] * slabs[2]) / (ws[0] + ws[1] + ws[2])
        ya.append(jnp.where(lane < A_HEAD_DIM, mix, 0.0))
    ya = jnp.concatenate(ya, axis=-1)

    def branch(y, z_lo, z_hi, wo_ref, g_lo):
        z = _dot(h, wz_ref[:, z_lo:z_hi])
        p = _dot((y * _silu(z)).astype(BF16), wo_ref[...])
        gate = jax.nn.sigmoid(_dot(h, wbg_ref[:, g_lo:g_lo + D_MODEL]) + bbg_ref[:, g_lo:g_lo + D_MODEL])
        return gate * p

    za_hi = A_SLAB
    zb_hi = za_hi + B_WIDTH
    zc_hi = zb_hi + C_WIDTH
    merged = (branch(ya, 0, za_hi, woa_ref, 0)
              + branch(yb_ref[...], za_hi, zb_hi, wob_ref, D_MODEL)
              + branch(yc_ref[...], zb_hi, zc_hi, woc_ref, 2 * D_MODEL))
    out = _dot(merged.astype(BF16), wout_ref[...])
    o_ref[...] = x + gate_ref[...] * out


def _final(x, shift, scale, gate, norm_g, oa, yb, yc, wz, wbg, bbg, woa, wob, woc, wout, *, tm):
    bn, s, _ = x.shape
    tok = lambda w: pl.BlockSpec((None, tm, w), lambda b, i: (b, i, 0))
    vec = pl.BlockSpec((None, 1, D_MODEL), lambda b, i: (b, 0, 0))
    full = lambda a: pl.BlockSpec(a.shape, lambda b, i: (0, 0))
    return pl.pallas_call(
        _final_kernel,
        grid=(bn, s // tm),
        in_specs=[tok(D_MODEL), vec, vec, vec, full(norm_g), tok(A_SLAB), tok(A_SLAB), tok(A_SLAB),
                  tok(B_WIDTH), tok(C_WIDTH), full(wz), full(wbg), full(bbg), full(woa), full(wob), full(woc),
                  full(wout)],
        out_specs=tok(D_MODEL),
        out_shape=jax.ShapeDtypeStruct((bn, s, D_MODEL), F32),
        compiler_params=_cparams(2),
        name="final",
    )(x, shift, scale, gate, norm_g, *oa, yb, yc, wz, wbg, bbg, woa, wob, woc, wout)


B_Q_ORDER = (0, 3, 1, 4, 2, 5)


def _pad_heads(w, n_heads):
    lead = w.shape[:-1]
    w = w.reshape(*lead, n_heads, A_HEAD_DIM)
    w = jnp.pad(w, [(0, 0)] * len(lead) + [(0, 0), (0, LANE - A_HEAD_DIM)])
    return w.reshape(*lead, n_heads * LANE)


def _permute_heads(w, order, dim):
    lead = w.shape[:-1]
    w = w.reshape(*lead, len(order), dim)
    return w[..., jnp.array(order), :].reshape(*lead, len(order) * dim)


def _prep_layer(w_in, qn_a, kn_a, qn_b, kn_b, qn_c, kn_c, w_oa, w_ob, w_oc):
    offs = [0]
    for n in IN_SIZES:
        offs.append(offs[-1] + n)
    qa, ka, va, za, qb, kb, vb, zb, qc, kc, vc, zc = [w_in[:, offs[i]:offs[i + 1]] for i in range(len(IN_SIZES))]

    w_a = jnp.concatenate([_pad_heads(qa, N_A), _pad_heads(ka, N_A), _pad_heads(va, N_A)], axis=-1).astype(BF16)
    pad_gain = lambda g: jnp.tile(jnp.pad(g, (0, LANE - A_HEAD_DIM)), N_A)
    gain_a = jnp.concatenate([pad_gain(qn_a), pad_gain(kn_a), jnp.zeros((N_A * LANE,), F32)])[None, :]

    w_b = jnp.concatenate([_permute_heads(qb, B_Q_ORDER, B_HEAD_DIM), kb, vb], axis=-1).astype(BF16)
    gain_b = jnp.concatenate([jnp.tile(qn_b, B_HEADS), jnp.tile(kn_b, B_KV_HEADS), jnp.zeros((B_KV,), F32)])[None, :]

    w_c = jnp.concatenate([qc, kc, vc], axis=-1).astype(BF16)
    gain_c = jnp.concatenate([jnp.tile(qn_c, 2 * C_HEADS), jnp.tile(kn_c, 2 * C_HEADS),
                              jnp.zeros((C_WIDTH,), F32)])[None, :]

    w_z = jnp.concatenate([_pad_heads(za, A_HEADS), _permute_heads(zb, B_Q_ORDER, B_HEAD_DIM), zc],
                          axis=-1).astype(BF16)
    w_oa_p = _pad_heads(w_oa.T, A_HEADS).T.astype(BF16)
    w_ob_p = _permute_heads(w_ob.T, B_Q_ORDER, B_HEAD_DIM).T.astype(BF16)
    return dict(w_a=w_a, gain_a=gain_a, w_b=w_b, gain_b=gain_b, w_c=w_c, gain_c=gain_c, w_z=w_z,
                w_oa=w_oa_p, w_ob=w_ob_p, w_oc=w_oc.astype(BF16))


def _rope_angles(pos, dim, theta):
    inv = theta ** (-jnp.arange(0, dim, 2, dtype=F32) / dim)
    return pos.astype(F32)[:, None] * inv[None, :]


def _rope_tables(segments, s):
    cos = jnp.ones((s, LANE), F32)
    s_lo = jnp.zeros((s, LANE), F32)
    s_hi = jnp.zeros((s, LANE), F32)
    for first, half, ang in segments:
        c, sn = jnp.cos(ang), jnp.sin(ang)
        cos = cos.at[:, first:first + half].set(c).at[:, first + half:first + 2 * half].set(c)
        s_lo = s_lo.at[:, first:first + half].set(-sn)
        s_hi = s_hi.at[:, first + half:first + 2 * half].set(sn)
    return cos, s_lo, s_hi


def _tables(s):
    pos = jnp.arange(s)
    ang_a = _rope_angles(pos, A_ROT, ROPE_THETA)
    ang_c = _rope_angles(pos, C_ROT, ROPE_THETA)
    ang_row = _rope_angles(pos // GRID_W, B_HEAD_DIM // 2, AXIAL_THETA)
    ang_col = _rope_angles(pos % GRID_W, B_HEAD_DIM // 2, AXIAL_THETA)
    quarter = B_HEAD_DIM // 4
    tab_a = _rope_tables([(0, A_ROT // 2, ang_a)], s)
    tab_b = _rope_tables([(base + off, quarter, ang) for base in (0, HALF)
                          for off, ang in ((0, ang_row), (2 * quarter, ang_col))], s)
    tab_c = _rope_tables([(base, C_ROT // 2, ang_c) for base in (0, HALF)], s)
    return tab_a, tab_b, tab_c


def _layer(x, mod, l, lp, tabs, norm_g, lam_rows, subln, w_bg, b_bg, w_out):
    bn, s, _ = x.shape
    shift, scale, gate = [m.reshape(bn, 1, D_MODEL) for m in jnp.split(mod, 3, axis=-1)]
    tm = min(256, s)
    tab_a, tab_b, tab_c = tabs
    lam_init = 0.8 - 0.6 * math.exp(-0.3 * l)

    a_outs = tuple((A_SLAB, kind) for kind in ("q", "k", "v") for _ in A_GROUPS)
    pa = _project(x, shift, scale, norm_g, lp["w_a"], lp["gain_a"], tab_a, a_outs, rot_shift=A_ROT // 2,
                  two_heads=False, head_dim=A_HEAD_DIM, q_scale=1.0, tm=tm, name="proj_a")
    n_g = len(A_GROUPS)
    oa = [_band_attention(pa[g], pa[n_g + g], pa[2 * n_g + g], dil) for g, (_, dil) in enumerate(A_GROUPS)]

    qb, kb, vb = _project(x, shift, scale, norm_g, lp["w_b"], lp["gain_b"], tab_b,
                          ((B_WIDTH, "qT"), (B_KV, "k"), (B_KV, "vT")), rot_shift=B_HEAD_DIM // 4, two_heads=True,
                          head_dim=B_HEAD_DIM, q_scale=B_HEAD_DIM ** -0.5 * LOG2E, tm=tm, name="proj_b")
    yb = _gqa_attention(qb, kb, vb)

    qc, kc, vc = _project(x, shift, scale, norm_g, lp["w_c"], lp["gain_c"], tab_c,
                          ((C_WIDTH, "qT"), (C_WIDTH, "k"), (C_WIDTH, "vT")), rot_shift=C_ROT // 2, two_heads=True,
                          head_dim=C_HEAD_DIM, q_scale=C_HEAD_DIM ** -0.5 * LOG2E, tm=tm, name="proj_c")
    yc = _diff_attention(qc, kc, vc, lam_rows, subln, lam_init)

    return _final(x, shift, scale, gate, norm_g, oa, yb, yc, lp["w_z"], w_bg, b_bg, lp["w_oa"], lp["w_ob"],
                  lp["w_oc"], w_out, tm=tm)


def kernel(x_prompt, x_sample, c_prompt, c_sample, norm_g, w_ada, b_ada, w_in, qn_a, kn_a, qn_b, kn_b, qn_c, kn_c,
           lam_q1, lam_k1, lam_q2, lam_k2, subln_c, w_oa, w_ob, w_oc, w_bg, b_bg, w_out):
    groups = ((x_prompt, c_prompt), (x_sample, c_sample))
    rows = [c.shape[0] for _, c in groups]
    pad = -sum(rows) % 8
    c_all = jnp.concatenate([c for _, c in groups] + [jnp.zeros((pad, D_MODEL), F32)], axis=0)
    mod_all = _ada_mod(c_all, w_ada.astype(BF16), b_ada)

    layers = [_prep_layer(w_in[l], qn_a[l], kn_a[l], qn_b[l], kn_b[l], qn_c[l], kn_c[l], w_oa[l], w_ob[l], w_oc[l])
              for l in range(DEPTH)]
    w_bg16, w_out16 = w_bg.astype(BF16), w_out.astype(BF16)

    outs = []
    row0 = 0
    for (x, c), n in zip(groups, rows):
        tabs = _tables(x.shape[1])
        for l in range(DEPTH):
            lam_rows = [p[l][None, :] for p in (lam_q1, lam_k1, lam_q2, lam_k2)]
            x = _layer(x, mod_all[l, row0:row0 + n], l, layers[l], tabs, norm_g[l][None, :], lam_rows,
                       subln_c[l][None, :], w_bg16[l], b_bg[l][None, :], w_out16[l])
        outs.append(x)
        row0 += n
    return tuple(outs)
```

```python
import functools
import math

import jax
import jax.numpy as jnp
import numpy as np
from jax import lax
from jax.experimental import pallas as pl
from jax.experimental.pallas import tpu as pltpu

D_MODEL = 1024
DEPTH = 2
GRID_W = 64
EPS = 1e-6
NEG = -1e30
ROPE_THETA = 500000.0
AXIAL_THETA = 10000.0

A_GROUPS = ((128, 1), (512, 4), (2048, 16))
A_HEADS = 4
A_HEAD_DIM = 96
A_ROT = A_HEAD_DIM // 4
A_BAND = 64
N_A = len(A_GROUPS) * A_HEADS

B_HEADS = 6
B_KV_HEADS = 2
B_HEAD_DIM = 64
C_HEADS = 4
C_HEAD_DIM = 64
C_ROT = C_HEAD_DIM // 4

A_QKV = N_A * A_HEAD_DIM
A_WIDTH = A_HEADS * A_HEAD_DIM
B_WIDTH = B_HEADS * B_HEAD_DIM
B_KV = B_KV_HEADS * B_HEAD_DIM
C_WIDTH = C_HEADS * 2 * C_HEAD_DIM
IN_SIZES = (A_QKV, A_QKV, A_QKV, A_WIDTH, B_WIDTH, B_KV, B_KV, B_WIDTH, C_WIDTH, C_WIDTH, C_WIDTH, C_WIDTH)

LANE = 128
HALF = LANE // 2
ROT_SHIFT = LANE // 2
A_SLAB = A_HEADS * LANE
BF16_SUBLANES = 16
V_ROWS = LANE + BF16_SUBLANES
FLASH_TK = 512
LOG2E = math.log2(math.e)
VMEM_LIMIT = 48 * 1024 * 1024

BF16 = jnp.bfloat16
F32 = jnp.float32


def _cparams(n_axes):
    return pltpu.CompilerParams(dimension_semantics=("arbitrary",) * n_axes, vmem_limit_bytes=VMEM_LIMIT)


def _dot(a, b):
    return jnp.dot(a, b, preferred_element_type=F32)


def _dot_nt(a, b):
    return lax.dot_general(a, b, (((1,), (1,)), ((), ())), preferred_element_type=F32)


def _silu(x):
    return x * jax.nn.sigmoid(x)


def _ada_kernel(c_ref, w_ref, b_ref, o_ref):
    c = c_ref[...]
    o_ref[...] = _dot(_silu(c).astype(BF16), w_ref[...]) + b_ref[...]


def _ada_mod(c_all, w_ada, b_ada):
    rows = c_all.shape[0]
    return pl.pallas_call(
        _ada_kernel,
        grid=(DEPTH, 3),
        in_specs=[
            pl.BlockSpec((rows, D_MODEL), lambda l, n: (0, 0)),
            pl.BlockSpec((None, D_MODEL, D_MODEL), lambda l, n: (l, 0, n)),
            pl.BlockSpec((None, 1, D_MODEL), lambda l, n: (l, 0, n)),
        ],
        out_specs=pl.BlockSpec((None, rows, D_MODEL), lambda l, n: (l, 0, n)),
        out_shape=jax.ShapeDtypeStruct((DEPTH, rows, 3 * D_MODEL), F32),
        compiler_params=_cparams(2),
        name="ada_mod",
    )(c_all, w_ada, b_ada.reshape(DEPTH, 1, 3 * D_MODEL))


def _modulated_norm(x, g, scale, shift):
    ms = jnp.mean(x * x, axis=-1, keepdims=True)
    h = x * lax.rsqrt(ms + EPS) * g
    return h * (1.0 + scale) + shift


def _first_head(index):
    return (index & (HALF - 1)) < HALF // 2


def _head_norm_rope(u, gain, cos, sin, two_heads, head_dim):
    sq = u * u
    if two_heads:
        first = _first_head(lax.broadcasted_iota(jnp.int32, (1, LANE), 1))
        s_first = jnp.sum(jnp.where(first, sq, 0.0), axis=-1, keepdims=True)
        s_all = jnp.sum(sq, axis=-1, keepdims=True)
        ms = jnp.where(first, s_first, s_all - s_first) * (1.0 / head_dim)
    else:
        ms = jnp.sum(sq, axis=-1, keepdims=True) * (1.0 / head_dim)
    y = u * lax.rsqrt(ms + EPS) * gain
    return y * cos + pltpu.roll(y, ROT_SHIFT, 1) * sin


def _proj_kernel(x_ref, shift_ref, scale_ref, g_ref, w_ref, gain_ref, cos_ref, sin_ref, *refs,
                 outs, two_heads, head_dim, q_scale):
    out_refs, stage_ref = refs[:len(outs)], refs[len(outs):]
    h = _modulated_norm(x_ref[...], g_ref[...], scale_ref[...], shift_ref[...]).astype(BF16)
    cos, sin = cos_ref[...], sin_ref[...]
    tm = x_ref.shape[0]
    col = 0
    for o_ref, (width, kind, dil) in zip(out_refs, outs):
        u = _dot(h, w_ref[:, col:col + width])
        if kind == "vT":
            for s in range(width // LANE):
                o_ref[s * V_ROWS:s * V_ROWS + LANE, :] = u[:, s * LANE:(s + 1) * LANE].T.astype(BF16)
                o_ref[s * V_ROWS + LANE:(s + 1) * V_ROWS, :] = jnp.ones((V_ROWS - LANE, tm), BF16)
        else:
            if kind != "v":
                slabs = []
                for s in range(width // LANE):
                    lo = s * LANE
                    y = _head_norm_rope(u[:, lo:lo + LANE], gain_ref[:, col + lo:col + lo + LANE], cos, sin,
                                        two_heads, head_dim)
                    slabs.append(y * q_scale if kind in ("q", "qT") and q_scale != 1.0 else y)
                u = jnp.concatenate(slabs, axis=-1)
            if kind == "qT":
                for s in range(width // LANE):
                    o_ref[s * LANE:(s + 1) * LANE, :] = u[:, s * LANE:(s + 1) * LANE].T.astype(BF16)
            elif dil == 1:
                o_ref[...] = u.astype(BF16)
            else:
                stage = stage_ref[0]
                for s in range(width // LANE):
                    stage[s] = u[:, s * LANE:(s + 1) * LANE]
                rows = lambda r: pl.ds(r, tm // dil, stride=dil)
                for r in range(dil):
                    for s in range(width // LANE):
                        lo = r * width + s * LANE
                        o_ref[:, lo:lo + LANE] = stage[s, rows(r), :].astype(BF16)
        col += width


def _proj_out(kind, width, dil, bn, s, tm):
    if kind == "qT":
        return (bn, width, s), pl.BlockSpec((None, width, tm), lambda b, i: (b, 0, i))
    if kind == "vT":
        per = FLASH_TK // tm
        rows = width // LANE * V_ROWS
        return ((bn, s // FLASH_TK, rows, FLASH_TK),
                pl.BlockSpec((None, None, rows, tm), lambda b, i: (b, i // per, 0, i % per)))
    return (bn, s // dil, dil * width), pl.BlockSpec((None, tm // dil, dil * width), lambda b, i: (b, i, 0))


def _project(x, shift, scale, norm_g, w, gain, tables, outs, *, two_heads, head_dim, q_scale, tm, name):
    bn, s, _ = x.shape
    wcols = w.shape[1]
    kern = functools.partial(_proj_kernel, outs=outs, two_heads=two_heads, head_dim=head_dim, q_scale=q_scale)
    vec = pl.BlockSpec((None, 1, D_MODEL), lambda b, i: (b, 0, 0))
    tab = pl.BlockSpec((tm, LANE), lambda b, i: (i, 0))
    shapes, specs = zip(*[_proj_out(kind, width, dil, bn, s, tm) for width, kind, dil in outs])
    n_slabs = max(width for width, _, _ in outs) // LANE
    stage = [pltpu.VMEM((n_slabs, tm, LANE), F32)] if any(d > 1 for _, _, d in outs) else []
    return pl.pallas_call(
        kern,
        grid=(bn, s // tm),
        in_specs=[
            pl.BlockSpec((None, tm, D_MODEL), lambda b, i: (b, i, 0)),
            vec, vec,
            pl.BlockSpec((1, D_MODEL), lambda b, i: (0, 0)),
            pl.BlockSpec((D_MODEL, wcols), lambda b, i: (0, 0)),
            pl.BlockSpec((1, wcols), lambda b, i: (0, 0)),
            tab, tab,
        ],
        out_specs=list(specs),
        out_shape=[jax.ShapeDtypeStruct(shape, BF16) for shape in shapes],
        scratch_shapes=stage,
        compiler_params=_cparams(2),
        name=name,
    )(x, shift, scale, norm_g, w, gain, *tables)


def _band_attn_kernel(q_ref, kp_ref, kc_ref, kn_ref, vp_ref, vc_ref, vn_ref, o_ref, *, tb, seq_len, scale):
    i = pl.program_id(2)
    nk = tb + 2 * A_BAND
    row = lax.broadcasted_iota(jnp.int32, (tb, nk), 0)
    colk = lax.broadcasted_iota(jnp.int32, (tb, nk), 1)
    dist = colk - A_BAND - row
    kpos = i * tb - A_BAND + colk
    valid = (jnp.abs(dist) <= A_BAND) & (kpos >= 0) & (kpos < seq_len)
    lane = lax.broadcasted_iota(jnp.int32, (1, LANE), 1)
    heads = [slice(j * LANE, (j + 1) * LANE) for j in range(A_HEADS)]
    band = lambda p_ref, c_ref, n_ref, sl: jnp.concatenate(
        [p_ref[tb - A_BAND:, sl], c_ref[:, sl], n_ref[:A_BAND, sl]], axis=0)
    scores = [_dot_nt(q_ref[:, sl], band(kp_ref, kc_ref, kn_ref, sl)) for sl in heads]
    stats = []
    for s in scores:
        s = jnp.where(valid, s * scale, NEG)
        m = jnp.max(s, axis=-1, keepdims=True)
        p = jnp.exp(s - m)
        stats.append((m, jnp.sum(p, axis=-1, keepdims=True), p.astype(BF16)))
    for sl, (m, l, p) in zip(heads, stats):
        o = _dot(p, band(vp_ref, vc_ref, vn_ref, sl)) / l
        o_ref[:, sl] = jnp.where(lane >= A_HEAD_DIM, m + jnp.log(l), o)


def _band_attention(q, k, v, dilation):
    bn, length, _ = q.shape
    tb = min(256, length)
    nb = length // tb
    blk = lambda f: pl.BlockSpec((None, tb, A_SLAB), f)
    cur = blk(lambda b, r, i: (b, i, r))
    prv = blk(lambda b, r, i: (b, jnp.maximum(i - 1, 0), r))
    nxt = blk(lambda b, r, i: (b, jnp.minimum(i + 1, nb - 1), r))
    kern = functools.partial(_band_attn_kernel, tb=tb, seq_len=length, scale=A_HEAD_DIM ** -0.5)
    return pl.pallas_call(
        kern,
        grid=(bn, dilation, nb),
        in_specs=[cur, prv, cur, nxt, prv, cur, nxt],
        out_specs=cur,
        out_shape=jax.ShapeDtypeStruct((bn, length, dilation * A_SLAB), F32),
        compiler_params=_cparams(3),
        name=f"band_attn_d{dilation}",
    )(q, k, k, k, v, v, v)


def _stacked_flash(qt_ref, k_ref, vt_ref, acc_sc, s_sc, *, tq, n_chunks):
    first = _first_head(lax.broadcasted_iota(jnp.int32, (LANE, 1), 0))
    qt = qt_ref[...]
    zero = jnp.zeros_like(qt)
    q2t = jnp.concatenate([jnp.where(first, qt, zero), jnp.where(first, zero, qt)], axis=1)
    acc_sc[...] = jnp.zeros(acc_sc.shape, F32)

    def scores(c, slot):
        start = pl.multiple_of(c * FLASH_TK, FLASH_TK)
        s_sc[slot] = _dot(k_ref[pl.ds(start, FLASH_TK), :], q2t)

    def consume(c, slot, m_prev):
        st = s_sc[slot]
        m_next = jnp.maximum(m_prev, jnp.max(st, axis=0, keepdims=True))
        pt = jnp.exp2(st - m_next).astype(BF16)
        acc_sc[...] = jnp.exp2(m_prev - m_next) * acc_sc[...] + _dot(vt_ref[c], pt)
        return m_next

    def pair(j, m):
        c = 2 * j
        scores(c + 1, 1)
        m = consume(c, 0, m)
        scores(c + 2, 0)
        return consume(c + 1, 1, m)

    scores(0, 0)
    m = lax.fori_loop(0, n_chunks // 2 - 1, pair, jnp.full((1, 2 * tq), -jnp.inf, F32))
    scores(n_chunks - 1, 1)
    m = consume(n_chunks - 2, 0, m)
    consume(n_chunks - 1, 1, m)
    acc = acc_sc[...]
    return acc[:LANE] / acc[LANE:LANE + 1]


def _gqa_kernel(qt_ref, k_ref, vt_ref, o_ref, acc_sc, s_sc, *, tq, n_chunks):
    o = _stacked_flash(qt_ref, k_ref, vt_ref, acc_sc, s_sc, tq=tq, n_chunks=n_chunks)
    row = lax.broadcasted_iota(jnp.int32, (LANE, 1), 0)
    o_ref[...] = jnp.where(row < HALF, o[:, :tq], o[:, tq:]).T


def _diff_kernel(qt_ref, k_ref, vt_ref, lq1_ref, lk1_ref, lq2_ref, lk2_ref, sub_ref, o_ref, acc_sc,
                 s_sc, *, tq, n_chunks, lam_init):
    o = _stacked_flash(qt_ref, k_ref, vt_ref, acc_sc, s_sc, tq=tq, n_chunks=n_chunks)
    lam = (jnp.exp(jnp.sum(lq1_ref[...] * lk1_ref[...], axis=-1, keepdims=True))
           - jnp.exp(jnp.sum(lq2_ref[...] * lk2_ref[...], axis=-1, keepdims=True)) + lam_init)
    oc = (o[:, :tq] - lam * o[:, tq:]).T
    ms = jnp.mean(oc * oc, axis=-1, keepdims=True)
    o_ref[...] = oc * lax.rsqrt(ms + EPS) * sub_ref[...] * (1.0 - lam_init)


def _flash_tq(s):
    return min(256, s)


def _flash_scratch(tq):
    return [pltpu.VMEM((V_ROWS, 2 * tq), F32), pltpu.VMEM((2, FLASH_TK, 2 * tq), F32)]


def _gqa_attention(qt, k, vt):
    bn, _, s = qt.shape
    tq, n_chunks = _flash_tq(s), s // FLASH_TK
    kern = functools.partial(_gqa_kernel, tq=tq, n_chunks=n_chunks)
    return pl.pallas_call(
        kern,
        grid=(bn, B_HEADS // 2, s // tq),
        in_specs=[pl.BlockSpec((None, LANE, tq), lambda b, p, i: (b, p, i)),
                  pl.BlockSpec((None, s, LANE), lambda b, p, i: (b, 0, 0)),
                  pl.BlockSpec((None, n_chunks, V_ROWS, FLASH_TK), lambda b, p, i: (b, 0, 0, 0))],
        out_specs=pl.BlockSpec((None, tq, LANE), lambda b, p, i: (b, i, p)),
        out_shape=jax.ShapeDtypeStruct((bn, s, B_WIDTH), F32),
        scratch_shapes=_flash_scratch(tq),
        compiler_params=_cparams(3),
        name="gqa_attn",
    )(qt, k, vt)


def _diff_attention(qt, k, vt, lam_rows, subln, lam_init):
    bn, _, s = qt.shape
    tq, n_chunks = _flash_tq(s), s // FLASH_TK
    kern = functools.partial(_diff_kernel, tq=tq, n_chunks=n_chunks, lam_init=lam_init)
    row = lambda n: pl.BlockSpec((1, n), lambda b, h, i: (0, 0))
    return pl.pallas_call(
        kern,
        grid=(bn, C_HEADS, s // tq),
        in_specs=[pl.BlockSpec((None, LANE, tq), lambda b, h, i: (b, h, i)),
                  pl.BlockSpec((None, s, LANE), lambda b, h, i: (b, 0, h)),
                  pl.BlockSpec((None, n_chunks, V_ROWS, FLASH_TK), lambda b, h, i: (b, 0, h, 0)),
                  row(C_HEAD_DIM), row(C_HEAD_DIM), row(C_HEAD_DIM), row(C_HEAD_DIM), row(LANE)],
        out_specs=pl.BlockSpec((None, tq, LANE), lambda b, h, i: (b, i, h)),
        out_shape=jax.ShapeDtypeStruct((bn, s, C_WIDTH), F32),
        scratch_shapes=_flash_scratch(tq),
        compiler_params=_cparams(3),
        name="diff_attn",
    )(qt, k, vt, *lam_rows, subln)


def _final_kernel(x_ref, shift_ref, scale_ref, gate_ref, g_ref, oa0_ref, oa1_ref, oa2_ref, yb_ref, yc_ref,
                  wz_ref, wbg_ref, bbg_ref, woa_ref, wob_ref, woc_ref, wout_ref, o_ref, *stage_refs):
    x = x_ref[...]
    tm = x.shape[0]
    h = _modulated_norm(x, g_ref[...], scale_ref[...], shift_ref[...]).astype(BF16)
    lane = lax.broadcasted_iota(jnp.int32, (1, LANE), 1)

    for oa_ref, st_ref, (_, dil) in zip((oa1_ref, oa2_ref), stage_refs, A_GROUPS[1:]):
        for r in range(dil):
            for j in range(A_HEADS):
                lo = r * A_SLAB + j * LANE
                st_ref[j, pl.ds(r, tm // dil, stride=dil), :] = oa_ref[:, lo:lo + LANE]

    ya = []
    for j in range(A_HEADS):
        slabs = [oa0_ref[:, j * LANE:(j + 1) * LANE]] + [st_ref[j] for st_ref in stage_refs]
        lses = [t[:, A_HEAD_DIM:A_HEAD_DIM + 1] for t in slabs]
        top = jnp.maximum(jnp.maximum(lses[0], lses[1]), lses[2])
        ws = [jnp.exp(t - top) for t in lses]
        mix = (ws[0] * slabs[0] + ws[1] * slabs[1] + ws[2] * slabs[2]) / (ws[0] + ws[1] + ws[2])
        ya.append(jnp.where(lane < A_HEAD_DIM, mix, 0.0))
    ya = jnp.concatenate(ya, axis=-1)

    def branch(y, z_lo, z_hi, wo_ref, g_lo):
        z = _dot(h, wz_ref[:, z_lo:z_hi])
        p = _dot((y * _silu(z)).astype(BF16), wo_ref[...])
        gate = jax.nn.sigmoid(_dot(h, wbg_ref[:, g_lo:g_lo + D_MODEL]) + bbg_ref[:, g_lo:g_lo + D_MODEL])
        return gate * p

    za_hi = A_SLAB
    zb_hi = za_hi + B_WIDTH
    zc_hi = zb_hi + C_WIDTH
    merged = (branch(ya, 0, za_hi, woa_ref, 0)
              + branch(yb_ref[...], za_hi, zb_hi, wob_ref, D_MODEL)
              + branch(yc_ref[...], zb_hi, zc_hi, woc_ref, 2 * D_MODEL))
    out = _dot(merged.astype(BF16), wout_ref[...])
    o_ref[...] = x + gate_ref[...] * out


def _final(x, shift, scale, gate, norm_g, oa, yb, yc, wz, wbg, bbg, woa, wob, woc, wout, *, tm):
    bn, s, _ = x.shape
    tok = lambda w: pl.BlockSpec((None, tm, w), lambda b, i: (b, i, 0))
    vec = pl.BlockSpec((None, 1, D_MODEL), lambda b, i: (b, 0, 0))
    full = lambda a: pl.BlockSpec(a.shape, lambda b, i: (0, 0))
    band = [pl.BlockSpec((None, tm // dil, dil * A_SLAB), lambda b, i: (b, i, 0)) for _, dil in A_GROUPS]
    return pl.pallas_call(
        _final_kernel,
        grid=(bn, s // tm),
        in_specs=[tok(D_MODEL), vec, vec, vec, full(norm_g), *band,
                  tok(B_WIDTH), tok(C_WIDTH), full(wz), full(wbg), full(bbg), full(woa), full(wob), full(woc),
                  full(wout)],
        out_specs=tok(D_MODEL),
        out_shape=jax.ShapeDtypeStruct((bn, s, D_MODEL), F32),
        scratch_shapes=[pltpu.VMEM((A_HEADS, tm, LANE), F32) for _ in A_GROUPS[1:]],
        compiler_params=_cparams(2),
        name="final",
    )(x, shift, scale, gate, norm_g, *oa, yb, yc, wz, wbg, bbg, woa, wob, woc, wout)


B_Q_ORDER = (0, 3, 1, 4, 2, 5)


def _slab_layout(head_dim, n_heads, rot_groups):
    src = -np.ones((LANE,), np.int32)
    group = -np.ones((LANE,), np.int32)
    freq = np.zeros((LANE,), np.int32)
    sign = np.zeros((LANE,), np.float32)
    width = ROT_SHIFT // n_heads
    for hd in range(n_heads):
        low = [(start + i, g, i, -1.0) for g, (start, half) in enumerate(rot_groups) for i in range(half)]
        high = [(start + half + i, g, i, 1.0) for g, (start, half) in enumerate(rot_groups) for i in range(half)]
        rotary = {d for d, _, _, _ in low + high}
        rest = [(d, -1, 0, 0.0) for d in range(head_dim) if d not in rotary]
        n_low_rest = width - len(low)
        low, high = low + rest[:n_low_rest], high + rest[n_low_rest:]
        assert len(low) == width and len(high) <= width
        for base, items in ((hd * width, low), (hd * width + ROT_SHIFT, high)):
            for lane, (d, g, i, sg) in enumerate(items, start=base):
                src[lane], group[lane], freq[lane], sign[lane] = hd * head_dim + d, g, i, sg
    return src, group, freq, sign


LAYOUT_A = _slab_layout(A_HEAD_DIM, 1, ((0, A_ROT // 2),))
LAYOUT_B = _slab_layout(B_HEAD_DIM, 2, ((0, B_HEAD_DIM // 4), (B_HEAD_DIM // 2, B_HEAD_DIM // 4)))
LAYOUT_C = _slab_layout(C_HEAD_DIM, 2, ((0, C_ROT // 2),))


def _to_slabs(w, layout, dims_per_slab):
    src = layout[0]
    n_slabs = w.shape[-1] // dims_per_slab
    idx = np.concatenate([np.maximum(src, 0) + s * dims_per_slab for s in range(n_slabs)])
    keep = np.tile(src >= 0, n_slabs)
    return jnp.where(keep, jnp.take(w, idx, axis=-1), 0.0)


def _pad_heads(w, n_heads):
    lead = w.shape[:-1]
    w = w.reshape(*lead, n_heads, A_HEAD_DIM)
    w = jnp.pad(w, [(0, 0)] * len(lead) + [(0, 0), (0, LANE - A_HEAD_DIM)])
    return w.reshape(*lead, n_heads * LANE)


def _permute_heads(w, order, dim):
    lead = w.shape[:-1]
    w = w.reshape(*lead, len(order), dim)
    return w[..., jnp.array(order), :].reshape(*lead, len(order) * dim)


def _prep_layer(w_in, qn_a, kn_a, qn_b, kn_b, qn_c, kn_c, w_oa, w_ob, w_oc):
    offs = [0]
    for n in IN_SIZES:
        offs.append(offs[-1] + n)
    qa, ka, va, za, qb, kb, vb, zb, qc, kc, vc, zc = [w_in[:, offs[i]:offs[i + 1]] for i in range(len(IN_SIZES))]

    slab_a = functools.partial(_to_slabs, layout=LAYOUT_A, dims_per_slab=A_HEAD_DIM)
    slab_b = functools.partial(_to_slabs, layout=LAYOUT_B, dims_per_slab=2 * B_HEAD_DIM)
    slab_c = functools.partial(_to_slabs, layout=LAYOUT_C, dims_per_slab=2 * C_HEAD_DIM)
    qb = _permute_heads(qb, B_Q_ORDER, B_HEAD_DIM)

    w_a = jnp.concatenate([slab_a(qa), slab_a(ka), _pad_heads(va, N_A)], axis=-1).astype(BF16)
    gain_a = jnp.concatenate([slab_a(jnp.tile(qn_a, N_A)), slab_a(jnp.tile(kn_a, N_A)),
                              jnp.zeros((N_A * LANE,), F32)])[None, :]

    w_b = jnp.concatenate([slab_b(qb), slab_b(kb), vb], axis=-1).astype(BF16)
    gain_b = jnp.concatenate([slab_b(jnp.tile(qn_b, B_HEADS)), slab_b(jnp.tile(kn_b, B_KV_HEADS)),
                              jnp.zeros((B_KV,), F32)])[None, :]

    w_c = jnp.concatenate([slab_c(qc), slab_c(kc), vc], axis=-1).astype(BF16)
    gain_c = jnp.concatenate([slab_c(jnp.tile(qn_c, 2 * C_HEADS)), slab_c(jnp.tile(kn_c, 2 * C_HEADS)),
                              jnp.zeros((C_WIDTH,), F32)])[None, :]

    w_z = jnp.concatenate([_pad_heads(za, A_HEADS), _permute_heads(zb, B_Q_ORDER, B_HEAD_DIM), zc],
                          axis=-1).astype(BF16)
    w_oa_p = _pad_heads(w_oa.T, A_HEADS).T.astype(BF16)
    w_ob_p = _permute_heads(w_ob.T, B_Q_ORDER, B_HEAD_DIM).T.astype(BF16)
    return dict(w_a=w_a, gain_a=gain_a, w_b=w_b, gain_b=gain_b, w_c=w_c, gain_c=gain_c, w_z=w_z,
                w_oa=w_oa_p, w_ob=w_ob_p, w_oc=w_oc.astype(BF16))


def _inv_freq(dim, theta):
    return theta ** (-jnp.arange(0, dim, 2, dtype=F32) / dim)


def _rope_tables(layout, inv_freqs, streams):
    _, group, freq, sign = layout
    ang = 0.0
    for g, (inv, pos) in enumerate(zip(inv_freqs, streams)):
        ang = ang + pos[:, None] * (inv[freq] * (group == g))[None, :]
    return jnp.cos(ang), jnp.sin(ang) * sign[None, :]


def _tables(s):
    pos = jnp.arange(s)
    tok, row, col = [t.astype(F32) for t in (pos, pos // GRID_W, pos % GRID_W)]
    inv_b = _inv_freq(B_HEAD_DIM // 2, AXIAL_THETA)
    tab_a = _rope_tables(LAYOUT_A, [_inv_freq(A_ROT, ROPE_THETA)], [tok])
    tab_b = _rope_tables(LAYOUT_B, [inv_b, inv_b], [row, col])
    tab_c = _rope_tables(LAYOUT_C, [_inv_freq(C_ROT, ROPE_THETA)], [tok])
    return tab_a, tab_b, tab_c


def _layer(x, mod, l, lp, tabs, norm_g, lam_rows, subln, w_bg, b_bg, w_out):
    bn, s, _ = x.shape
    shift, scale, gate = [m.reshape(bn, 1, D_MODEL) for m in jnp.split(mod, 3, axis=-1)]
    tm = min(256, s)
    tab_a, tab_b, tab_c = tabs
    lam_init = 0.8 - 0.6 * math.exp(-0.3 * l)

    a_outs = tuple((A_SLAB, kind, dil) for kind in ("q", "k", "v") for _, dil in A_GROUPS)
    pa = _project(x, shift, scale, norm_g, lp["w_a"], lp["gain_a"], tab_a, a_outs, two_heads=False,
                  head_dim=A_HEAD_DIM, q_scale=1.0, tm=tm, name="proj_a")
    n_g = len(A_GROUPS)
    oa = [_band_attention(pa[g], pa[n_g + g], pa[2 * n_g + g], dil) for g, (_, dil) in enumerate(A_GROUPS)]

    qb, kb, vb = _project(x, shift, scale, norm_g, lp["w_b"], lp["gain_b"], tab_b,
                          ((B_WIDTH, "qT", 1), (B_KV, "k", 1), (B_KV, "vT", 1)), two_heads=True,
                          head_dim=B_HEAD_DIM, q_scale=B_HEAD_DIM ** -0.5 * LOG2E, tm=tm, name="proj_b")
    yb = _gqa_attention(qb, kb, vb)

    qc, kc, vc = _project(x, shift, scale, norm_g, lp["w_c"], lp["gain_c"], tab_c,
                          ((C_WIDTH, "qT", 1), (C_WIDTH, "k", 1), (C_WIDTH, "vT", 1)), two_heads=True,
                          head_dim=C_HEAD_DIM, q_scale=C_HEAD_DIM ** -0.5 * LOG2E, tm=tm, name="proj_c")
    yc = _diff_attention(qc, kc, vc, lam_rows, subln, lam_init)

    return _final(x, shift, scale, gate, norm_g, oa, yb, yc, lp["w_z"], w_bg, b_bg, lp["w_oa"], lp["w_ob"],
                  lp["w_oc"], w_out, tm=tm)


def kernel(x_prompt, x_sample, c_prompt, c_sample, norm_g, w_ada, b_ada, w_in, qn_a, kn_a, qn_b, kn_b, qn_c, kn_c,
           lam_q1, lam_k1, lam_q2, lam_k2, subln_c, w_oa, w_ob, w_oc, w_bg, b_bg, w_out):
    groups = ((x_prompt, c_prompt), (x_sample, c_sample))
    rows = [c.shape[0] for _, c in groups]
    pad = -sum(rows) % 8
    c_all = jnp.concatenate([c for _, c in groups] + [jnp.zeros((pad, D_MODEL), F32)], axis=0)
    mod_all = _ada_mod(c_all, w_ada.astype(BF16), b_ada)

    layers = [_prep_layer(w_in[l], qn_a[l], kn_a[l], qn_b[l], kn_b[l], qn_c[l], kn_c[l], w_oa[l], w_ob[l], w_oc[l])
              for l in range(DEPTH)]
    w_bg16, w_out16 = w_bg.astype(BF16), w_out.astype(BF16)

    outs = []
    row0 = 0
    for (x, c), n in zip(groups, rows):
        tabs = _tables(x.shape[1])
        for l in range(DEPTH):
            lam_rows = [p[l][None, :] for p in (lam_q1, lam_k1, lam_q2, lam_k2)]
            x = _layer(x, mod_all[l, row0:row0 + n], l, layers[l], tabs, norm_g[l][None, :], lam_rows,
                       subln_c[l][None, :], w_bg16[l], b_bg[l][None, :], w_out16[l])
        outs.append(x)
        row0 += n
    return tuple(outs)
```

```python
import functools
import math

import jax
import jax.numpy as jnp
import numpy as np
from jax import lax
from jax.experimental import pallas as pl
from jax.experimental.pallas import tpu as pltpu

D_MODEL = 1024
DEPTH = 2
GRID_W = 64
EPS = 1e-6
NEG = -1e30
ROPE_THETA = 500000.0
AXIAL_THETA = 10000.0

A_GROUPS = ((128, 1), (512, 4), (2048, 16))
A_HEADS = 4
A_HEAD_DIM = 96
A_ROT = A_HEAD_DIM // 4
A_BAND = 64
N_A = len(A_GROUPS) * A_HEADS

B_HEADS = 6
B_KV_HEADS = 2
B_HEAD_DIM = 64
C_HEADS = 4
C_HEAD_DIM = 64
C_ROT = C_HEAD_DIM // 4

A_QKV = N_A * A_HEAD_DIM
A_WIDTH = A_HEADS * A_HEAD_DIM
B_WIDTH = B_HEADS * B_HEAD_DIM
B_KV = B_KV_HEADS * B_HEAD_DIM
C_WIDTH = C_HEADS * 2 * C_HEAD_DIM
IN_SIZES = (A_QKV, A_QKV, A_QKV, A_WIDTH, B_WIDTH, B_KV, B_KV, B_WIDTH, C_WIDTH, C_WIDTH, C_WIDTH, C_WIDTH)

LANE = 128
HALF = LANE // 2
ROT_SHIFT = LANE // 2
A_SLAB = A_HEADS * LANE
BF16_SUBLANES = 16
V_ROWS = LANE + BF16_SUBLANES
FLASH_TK = 512
KEY_PART = 256
SCORE_SLOTS = 2
FLASH_UNROLL = 4
LOG2E = math.log2(math.e)
VMEM_LIMIT = 48 * 1024 * 1024

BF16 = jnp.bfloat16
F32 = jnp.float32


def _cparams(n_axes):
    return pltpu.CompilerParams(dimension_semantics=("arbitrary",) * n_axes, vmem_limit_bytes=VMEM_LIMIT)


def _dot(a, b):
    return jnp.dot(a, b, preferred_element_type=F32)


def _dot_nt(a, b):
    return lax.dot_general(a, b, (((1,), (1,)), ((), ())), preferred_element_type=F32)


def _silu(x):
    return x * jax.nn.sigmoid(x)


def _ada_kernel(c_ref, w_ref, b_ref, o_ref):
    c = c_ref[...]
    o_ref[...] = _dot(_silu(c).astype(BF16), w_ref[...]) + b_ref[...]


def _ada_mod(c_all, w_ada, b_ada):
    rows = c_all.shape[0]
    return pl.pallas_call(
        _ada_kernel,
        grid=(DEPTH, 3),
        in_specs=[
            pl.BlockSpec((rows, D_MODEL), lambda l, n: (0, 0)),
            pl.BlockSpec((None, D_MODEL, D_MODEL), lambda l, n: (l, 0, n)),
            pl.BlockSpec((None, 1, D_MODEL), lambda l, n: (l, 0, n)),
        ],
        out_specs=pl.BlockSpec((None, rows, D_MODEL), lambda l, n: (l, 0, n)),
        out_shape=jax.ShapeDtypeStruct((DEPTH, rows, 3 * D_MODEL), F32),
        compiler_params=_cparams(2),
        name="ada_mod",
    )(c_all, w_ada, b_ada.reshape(DEPTH, 1, 3 * D_MODEL))


def _modulated_norm(x, g, scale, shift):
    ms = jnp.mean(x * x, axis=-1, keepdims=True)
    h = x * lax.rsqrt(ms + EPS) * g
    return h * (1.0 + scale) + shift


def _first_head(index):
    return (index & (HALF - 1)) < HALF // 2


def _head_norm_rope(u, gain, cos, sin, two_heads, head_dim):
    sq = u * u
    if two_heads:
        first = _first_head(lax.broadcasted_iota(jnp.int32, (1, LANE), 1))
        s_first = jnp.sum(jnp.where(first, sq, 0.0), axis=-1, keepdims=True)
        s_all = jnp.sum(sq, axis=-1, keepdims=True)
        ms = jnp.where(first, s_first, s_all - s_first) * (1.0 / head_dim)
    else:
        ms = jnp.sum(sq, axis=-1, keepdims=True) * (1.0 / head_dim)
    y = u * lax.rsqrt(ms + EPS) * gain
    return y * cos + pltpu.roll(y, ROT_SHIFT, 1) * sin


def _proj_kernel(x_ref, shift_ref, scale_ref, g_ref, w_ref, gain_ref, cos_ref, sin_ref, *refs,
                 outs, two_heads, head_dim, q_scale):
    out_refs, stage_ref = refs[:len(outs)], refs[len(outs):]
    h = _modulated_norm(x_ref[...], g_ref[...], scale_ref[...], shift_ref[...]).astype(BF16)
    cos, sin = cos_ref[...], sin_ref[...]
    tm = x_ref.shape[0]
    col = 0
    for o_ref, (width, kind, dil) in zip(out_refs, outs):
        u = _dot(h, w_ref[:, col:col + width])
        if kind == "vT":
            for s in range(width // LANE):
                o_ref[s * V_ROWS:s * V_ROWS + LANE, :] = u[:, s * LANE:(s + 1) * LANE].T.astype(BF16)
                o_ref[s * V_ROWS + LANE:(s + 1) * V_ROWS, :] = jnp.ones((V_ROWS - LANE, tm), BF16)
        else:
            if kind != "v":
                slabs = []
                for s in range(width // LANE):
                    lo = s * LANE
                    y = _head_norm_rope(u[:, lo:lo + LANE], gain_ref[:, col + lo:col + lo + LANE], cos, sin,
                                        two_heads, head_dim)
                    slabs.append(y * q_scale if kind in ("q", "qT") and q_scale != 1.0 else y)
                u = jnp.concatenate(slabs, axis=-1)
            if kind == "qT":
                for s in range(width // LANE):
                    o_ref[s * LANE:(s + 1) * LANE, :] = u[:, s * LANE:(s + 1) * LANE].T.astype(BF16)
            elif dil == 1:
                o_ref[...] = u.astype(BF16)
            else:
                stage = stage_ref[0]
                for s in range(width // LANE):
                    stage[s] = u[:, s * LANE:(s + 1) * LANE]
                rows = lambda r: pl.ds(r, tm // dil, stride=dil)
                for r in range(dil):
                    for s in range(width // LANE):
                        lo = r * width + s * LANE
                        o_ref[:, lo:lo + LANE] = stage[s, rows(r), :].astype(BF16)
        col += width


def _proj_out(kind, width, dil, bn, s, tm):
    if kind == "qT":
        return (bn, width, s), pl.BlockSpec((None, width, tm), lambda b, i: (b, 0, i))
    if kind == "vT":
        per = FLASH_TK // tm
        rows = width // LANE * V_ROWS
        return ((bn, s // FLASH_TK, rows, FLASH_TK),
                pl.BlockSpec((None, None, rows, tm), lambda b, i: (b, i // per, 0, i % per)))
    return (bn, s // dil, dil * width), pl.BlockSpec((None, tm // dil, dil * width), lambda b, i: (b, i, 0))


def _project(x, shift, scale, norm_g, w, gain, tables, outs, *, two_heads, head_dim, q_scale, tm, name):
    bn, s, _ = x.shape
    wcols = w.shape[1]
    kern = functools.partial(_proj_kernel, outs=outs, two_heads=two_heads, head_dim=head_dim, q_scale=q_scale)
    vec = pl.BlockSpec((None, 1, D_MODEL), lambda b, i: (b, 0, 0))
    tab = pl.BlockSpec((tm, LANE), lambda b, i: (i, 0))
    shapes, specs = zip(*[_proj_out(kind, width, dil, bn, s, tm) for width, kind, dil in outs])
    n_slabs = max(width for width, _, _ in outs) // LANE
    stage = [pltpu.VMEM((n_slabs, tm, LANE), F32)] if any(d > 1 for _, _, d in outs) else []
    return pl.pallas_call(
        kern,
        grid=(bn, s // tm),
        in_specs=[
            pl.BlockSpec((None, tm, D_MODEL), lambda b, i: (b, i, 0)),
            vec, vec,
            pl.BlockSpec((1, D_MODEL), lambda b, i: (0, 0)),
            pl.BlockSpec((D_MODEL, wcols), lambda b, i: (0, 0)),
            pl.BlockSpec((1, wcols), lambda b, i: (0, 0)),
            tab, tab,
        ],
        out_specs=list(specs),
        out_shape=[jax.ShapeDtypeStruct(shape, BF16) for shape in shapes],
        scratch_shapes=stage,
        compiler_params=_cparams(2),
        name=name,
    )(x, shift, scale, norm_g, w, gain, *tables)


def _band_attn_kernel(q_ref, kp_ref, kc_ref, kn_ref, vp_ref, vc_ref, vn_ref, o_ref, *, tb, seq_len, scale):
    i = pl.program_id(2)
    nk = tb + 2 * A_BAND
    row = lax.broadcasted_iota(jnp.int32, (tb, nk), 0)
    colk = lax.broadcasted_iota(jnp.int32, (tb, nk), 1)
    dist = colk - A_BAND - row
    kpos = i * tb - A_BAND + colk
    valid = (jnp.abs(dist) <= A_BAND) & (kpos >= 0) & (kpos < seq_len)
    lane = lax.broadcasted_iota(jnp.int32, (1, LANE), 1)
    heads = [slice(j * LANE, (j + 1) * LANE) for j in range(A_HEADS)]
    band = lambda p_ref, c_ref, n_ref, sl: jnp.concatenate([p_ref[:, sl], c_ref[:, sl], n_ref[:, sl]], axis=0)
    scores = [_dot_nt(q_ref[:, sl], band(kp_ref, kc_ref, kn_ref, sl)) for sl in heads]
    stats = []
    for s in scores:
        s = jnp.where(valid, s * scale, NEG)
        m = jnp.max(s, axis=-1, keepdims=True)
        p = jnp.exp(s - m)
        stats.append((m, jnp.sum(p, axis=-1, keepdims=True), p.astype(BF16)))
    for sl, (m, l, p) in zip(heads, stats):
        o = _dot(p, band(vp_ref, vc_ref, vn_ref, sl)) / l
        o_ref[:, sl] = jnp.where(lane >= A_HEAD_DIM, m + jnp.log(l), o)


def _band_attention(q, k, v, dilation):
    bn, length, _ = q.shape
    tb = min(256, length)
    nb = length // tb
    per = tb // A_BAND
    cur = pl.BlockSpec((None, tb, A_SLAB), lambda b, r, i: (b, i, r))
    prv = pl.BlockSpec((None, A_BAND, A_SLAB), lambda b, r, i: (b, jnp.maximum(i * per - 1, 0), r))
    nxt = pl.BlockSpec((None, A_BAND, A_SLAB), lambda b, r, i: (b, jnp.minimum((i + 1) * per, nb * per - 1), r))
    kern = functools.partial(_band_attn_kernel, tb=tb, seq_len=length, scale=A_HEAD_DIM ** -0.5)
    return pl.pallas_call(
        kern,
        grid=(bn, dilation, nb),
        in_specs=[cur, prv, cur, nxt, prv, cur, nxt],
        out_specs=cur,
        out_shape=jax.ShapeDtypeStruct((bn, length, dilation * A_SLAB), F32),
        compiler_params=_cparams(3),
        name=f"band_attn_d{dilation}",
    )(q, k, k, k, v, v, v)


def _stacked_flash(qt_ref, k_ref, vt_ref, acc_sc, s_sc, *, tq, n_chunks):
    first = _first_head(lax.broadcasted_iota(jnp.int32, (LANE, 1), 0))
    qt = qt_ref[...]
    zero = jnp.zeros_like(qt)
    q2t = jnp.concatenate([jnp.where(first, qt, zero), jnp.where(first, zero, qt)], axis=1)
    acc_sc[...] = jnp.zeros(acc_sc.shape, F32)

    parts = [slice(lo, lo + KEY_PART) for lo in range(0, FLASH_TK, KEY_PART)]

    def scores(c, slot, rows):
        start = pl.multiple_of(c * FLASH_TK + rows.start, KEY_PART)
        s_sc[slot, rows, :] = _dot(k_ref[pl.ds(start, KEY_PART), :], q2t)

    def step(c, slot, m_prev, c_next=None):
        m_next = jnp.maximum(m_prev, jnp.max(s_sc[slot], axis=0, keepdims=True))
        vc = vt_ref[c]
        pv = None
        for rows in parts:
            if c_next is not None:
                scores(c_next, 1 - slot, rows)
            pt = jnp.exp2(s_sc[slot, rows, :] - m_next).astype(BF16)
            part = _dot(vc[:, rows], pt)
            pv = part if pv is None else pv + part
        acc_sc[...] = jnp.exp2(m_prev - m_next) * acc_sc[...] + pv
        return m_next

    unroll = min(FLASH_UNROLL, n_chunks)

    def group(j, m):
        for u in range(unroll):
            c = unroll * j + u
            m = step(c, u % SCORE_SLOTS, m, c + 1)
        return m

    for rows in parts:
        scores(0, 0, rows)
    m = lax.fori_loop(0, n_chunks // unroll - 1, group, jnp.full((1, 2 * tq), -jnp.inf, F32))
    for c in range(n_chunks - unroll, n_chunks):
        m = step(c, c % SCORE_SLOTS, m, c + 1 if c + 1 < n_chunks else None)
    acc = acc_sc[...]
    return acc[:LANE] / acc[LANE:LANE + 1]


def _gqa_kernel(qt_ref, k_ref, vt_ref, o_ref, acc_sc, s_sc, *, tq, n_chunks):
    o = _stacked_flash(qt_ref, k_ref, vt_ref, acc_sc, s_sc, tq=tq, n_chunks=n_chunks)
    row = lax.broadcasted_iota(jnp.int32, (LANE, 1), 0)
    o_ref[...] = jnp.where(row < HALF, o[:, :tq], o[:, tq:]).T


def _diff_kernel(qt_ref, k_ref, vt_ref, lq1_ref, lk1_ref, lq2_ref, lk2_ref, sub_ref, o_ref, acc_sc,
                 s_sc, *, tq, n_chunks, lam_init):
    o = _stacked_flash(qt_ref, k_ref, vt_ref, acc_sc, s_sc, tq=tq, n_chunks=n_chunks)
    lam = (jnp.exp(jnp.sum(lq1_ref[...] * lk1_ref[...], axis=-1, keepdims=True))
           - jnp.exp(jnp.sum(lq2_ref[...] * lk2_ref[...], axis=-1, keepdims=True)) + lam_init)
    oc = (o[:, :tq] - lam * o[:, tq:]).T
    ms = jnp.mean(oc * oc, axis=-1, keepdims=True)
    o_ref[...] = oc * lax.rsqrt(ms + EPS) * sub_ref[...] * (1.0 - lam_init)


def _flash_tq(s):
    return min(512, s)


def _flash_scratch(tq):
    return [pltpu.VMEM((V_ROWS, 2 * tq), F32), pltpu.VMEM((SCORE_SLOTS, FLASH_TK, 2 * tq), F32)]


def _gqa_attention(qt, k, vt):
    bn, _, s = qt.shape
    tq, n_chunks = _flash_tq(s), s // FLASH_TK
    kern = functools.partial(_gqa_kernel, tq=tq, n_chunks=n_chunks)
    return pl.pallas_call(
        kern,
        grid=(bn, B_HEADS // 2, s // tq),
        in_specs=[pl.BlockSpec((None, LANE, tq), lambda b, p, i: (b, p, i)),
                  pl.BlockSpec((None, s, LANE), lambda b, p, i: (b, 0, 0)),
                  pl.BlockSpec((None, n_chunks, V_ROWS, FLASH_TK), lambda b, p, i: (b, 0, 0, 0))],
        out_specs=pl.BlockSpec((None, tq, LANE), lambda b, p, i: (b, i, p)),
        out_shape=jax.ShapeDtypeStruct((bn, s, B_WIDTH), F32),
        scratch_shapes=_flash_scratch(tq),
        compiler_params=_cparams(3),
        name="gqa_attn",
    )(qt, k, vt)


def _diff_attention(qt, k, vt, lam_rows, subln, lam_init):
    bn, _, s = qt.shape
    tq, n_chunks = _flash_tq(s), s // FLASH_TK
    kern = functools.partial(_diff_kernel, tq=tq, n_chunks=n_chunks, lam_init=lam_init)
    row = lambda n: pl.BlockSpec((1, n), lambda b, h, i: (0, 0))
    return pl.pallas_call(
        kern,
        grid=(bn, C_HEADS, s // tq),
        in_specs=[pl.BlockSpec((None, LANE, tq), lambda b, h, i: (b, h, i)),
                  pl.BlockSpec((None, s, LANE), lambda b, h, i: (b, 0, h)),
                  pl.BlockSpec((None, n_chunks, V_ROWS, FLASH_TK), lambda b, h, i: (b, 0, h, 0)),
                  row(C_HEAD_DIM), row(C_HEAD_DIM), row(C_HEAD_DIM), row(C_HEAD_DIM), row(LANE)],
        out_specs=pl.BlockSpec((None, tq, LANE), lambda b, h, i: (b, i, h)),
        out_shape=jax.ShapeDtypeStruct((bn, s, C_WIDTH), F32),
        scratch_shapes=_flash_scratch(tq),
        compiler_params=_cparams(3),
        name="diff_attn",
    )(qt, k, vt, *lam_rows, subln)


def _final_kernel(x_ref, shift_ref, scale_ref, gate_ref, g_ref, oa0_ref, oa1_ref, oa2_ref, yb_ref, yc_ref,
                  wz_ref, wbg_ref, bbg_ref, woa_ref, wob_ref, woc_ref, wout_ref, o_ref, *stage_refs):
    x = x_ref[...]
    tm = x.shape[0]
    h = _modulated_norm(x, g_ref[...], scale_ref[...], shift_ref[...]).astype(BF16)
    lane = lax.broadcasted_iota(jnp.int32, (1, LANE), 1)

    for oa_ref, st_ref, (_, dil) in zip((oa1_ref, oa2_ref), stage_refs, A_GROUPS[1:]):
        for r in range(dil):
            for j in range(A_HEADS):
                lo = r * A_SLAB + j * LANE
                st_ref[j, pl.ds(r, tm // dil, stride=dil), :] = oa_ref[:, lo:lo + LANE]

    ya = []
    for j in range(A_HEADS):
        slabs = [oa0_ref[:, j * LANE:(j + 1) * LANE]] + [st_ref[j] for st_ref in stage_refs]
        lses = [t[:, A_HEAD_DIM:A_HEAD_DIM + 1] for t in slabs]
        top = jnp.maximum(jnp.maximum(lses[0], lses[1]), lses[2])
        ws = [jnp.exp(t - top) for t in lses]
        mix = (ws[0] * slabs[0] + ws[1] * slabs[1] + ws[2] * slabs[2]) / (ws[0] + ws[1] + ws[2])
        ya.append(jnp.where(lane < A_HEAD_DIM, mix, 0.0))
    ya = jnp.concatenate(ya, axis=-1)

    def branch(y, z_lo, z_hi, wo_ref, g_lo):
        z = _dot(h, wz_ref[:, z_lo:z_hi])
        p = _dot((y * _silu(z)).astype(BF16), wo_ref[...])
        gate = jax.nn.sigmoid(_dot(h, wbg_ref[:, g_lo:g_lo + D_MODEL]) + bbg_ref[:, g_lo:g_lo + D_MODEL])
        return gate * p

    za_hi = A_SLAB
    zb_hi = za_hi + B_WIDTH
    zc_hi = zb_hi + C_WIDTH
    merged = (branch(ya, 0, za_hi, woa_ref, 0)
              + branch(yb_ref[...], za_hi, zb_hi, wob_ref, D_MODEL)
              + branch(yc_ref[...], zb_hi, zc_hi, woc_ref, 2 * D_MODEL))
    out = _dot(merged.astype(BF16), wout_ref[...])
    o_ref[...] = x + gate_ref[...] * out


def _final(x, shift, scale, gate, norm_g, oa, yb, yc, wz, wbg, bbg, woa, wob, woc, wout, *, tm):
    bn, s, _ = x.shape
    tok = lambda w: pl.BlockSpec((None, tm, w), lambda b, i: (b, i, 0))
    vec = pl.BlockSpec((None, 1, D_MODEL), lambda b, i: (b, 0, 0))
    full = lambda a: pl.BlockSpec(a.shape, lambda b, i: (0, 0))
    band = [pl.BlockSpec((None, tm // dil, dil * A_SLAB), lambda b, i: (b, i, 0)) for _, dil in A_GROUPS]
    return pl.pallas_call(
        _final_kernel,
        grid=(bn, s // tm),
        in_specs=[tok(D_MODEL), vec, vec, vec, full(norm_g), *band,
                  tok(B_WIDTH), tok(C_WIDTH), full(wz), full(wbg), full(bbg), full(woa), full(wob), full(woc),
                  full(wout)],
        out_specs=tok(D_MODEL),
        out_shape=jax.ShapeDtypeStruct((bn, s, D_MODEL), F32),
        scratch_shapes=[pltpu.VMEM((A_HEADS, tm, LANE), F32) for _ in A_GROUPS[1:]],
        compiler_params=_cparams(2),
        name="final",
    )(x, shift, scale, gate, norm_g, *oa, yb, yc, wz, wbg, bbg, woa, wob, woc, wout)


B_Q_ORDER = (0, 3, 1, 4, 2, 5)


def _slab_layout(head_dim, n_heads, rot_groups):
    src = -np.ones((LANE,), np.int32)
    group = -np.ones((LANE,), np.int32)
    freq = np.zeros((LANE,), np.int32)
    sign = np.zeros((LANE,), np.float32)
    width = ROT_SHIFT // n_heads
    for hd in range(n_heads):
        low = [(start + i, g, i, -1.0) for g, (start, half) in enumerate(rot_groups) for i in range(half)]
        high = [(start + half + i, g, i, 1.0) for g, (start, half) in enumerate(rot_groups) for i in range(half)]
        rotary = {d for d, _, _, _ in low + high}
        rest = [(d, -1, 0, 0.0) for d in range(head_dim) if d not in rotary]
        n_low_rest = width - len(low)
        low, high = low + rest[:n_low_rest], high + rest[n_low_rest:]
        assert len(low) == width and len(high) <= width
        for base, items in ((hd * width, low), (hd * width + ROT_SHIFT, high)):
            for lane, (d, g, i, sg) in enumerate(items, start=base):
                src[lane], group[lane], freq[lane], sign[lane] = hd * head_dim + d, g, i, sg
    return src, group, freq, sign


LAYOUT_A = _slab_layout(A_HEAD_DIM, 1, ((0, A_ROT // 2),))
LAYOUT_B = _slab_layout(B_HEAD_DIM, 2, ((0, B_HEAD_DIM // 4), (B_HEAD_DIM // 2, B_HEAD_DIM // 4)))
LAYOUT_C = _slab_layout(C_HEAD_DIM, 2, ((0, C_ROT // 2),))


def _to_slabs(w, layout, dims_per_slab):
    src = layout[0]
    n_slabs = w.shape[-1] // dims_per_slab
    idx = np.concatenate([np.maximum(src, 0) + s * dims_per_slab for s in range(n_slabs)])
    keep = np.tile(src >= 0, n_slabs)
    return jnp.where(keep, jnp.take(w, idx, axis=-1), 0.0)


def _pad_heads(w, n_heads):
    lead = w.shape[:-1]
    w = w.reshape(*lead, n_heads, A_HEAD_DIM)
    w = jnp.pad(w, [(0, 0)] * len(lead) + [(0, 0), (0, LANE - A_HEAD_DIM)])
    return w.reshape(*lead, n_heads * LANE)


def _permute_heads(w, order, dim):
    lead = w.shape[:-1]
    w = w.reshape(*lead, len(order), dim)
    return w[..., jnp.array(order), :].reshape(*lead, len(order) * dim)


def _prep_layer(w_in, qn_a, kn_a, qn_b, kn_b, qn_c, kn_c, w_oa, w_ob, w_oc):
    offs = [0]
    for n in IN_SIZES:
        offs.append(offs[-1] + n)
    qa, ka, va, za, qb, kb, vb, zb, qc, kc, vc, zc = [w_in[:, offs[i]:offs[i + 1]] for i in range(len(IN_SIZES))]

    slab_a = functools.partial(_to_slabs, layout=LAYOUT_A, dims_per_slab=A_HEAD_DIM)
    slab_b = functools.partial(_to_slabs, layout=LAYOUT_B, dims_per_slab=2 * B_HEAD_DIM)
    slab_c = functools.partial(_to_slabs, layout=LAYOUT_C, dims_per_slab=2 * C_HEAD_DIM)
    qb = _permute_heads(qb, B_Q_ORDER, B_HEAD_DIM)

    w_a = jnp.concatenate([slab_a(qa), slab_a(ka), _pad_heads(va, N_A)], axis=-1).astype(BF16)
    gain_a = jnp.concatenate([slab_a(jnp.tile(qn_a, N_A)), slab_a(jnp.tile(kn_a, N_A)),
                              jnp.zeros((N_A * LANE,), F32)])[None, :]

    w_b = jnp.concatenate([slab_b(qb), slab_b(kb), vb], axis=-1).astype(BF16)
    gain_b = jnp.concatenate([slab_b(jnp.tile(qn_b, B_HEADS)), slab_b(jnp.tile(kn_b, B_KV_HEADS)),
                              jnp.zeros((B_KV,), F32)])[None, :]

    w_c = jnp.concatenate([slab_c(qc), slab_c(kc), vc], axis=-1).astype(BF16)
    gain_c = jnp.concatenate([slab_c(jnp.tile(qn_c, 2 * C_HEADS)), slab_c(jnp.tile(kn_c, 2 * C_HEADS)),
                              jnp.zeros((C_WIDTH,), F32)])[None, :]

    w_z = jnp.concatenate([_pad_heads(za, A_HEADS), _permute_heads(zb, B_Q_ORDER, B_HEAD_DIM), zc],
                          axis=-1).astype(BF16)
    w_oa_p = _pad_heads(w_oa.T, A_HEADS).T.astype(BF16)
    w_ob_p = _permute_heads(w_ob.T, B_Q_ORDER, B_HEAD_DIM).T.astype(BF16)
    return dict(w_a=w_a, gain_a=gain_a, w_b=w_b, gain_b=gain_b, w_c=w_c, gain_c=gain_c, w_z=w_z,
                w_oa=w_oa_p, w_ob=w_ob_p, w_oc=w_oc.astype(BF16))


def _inv_freq(dim, theta):
    return theta ** (-jnp.arange(0, dim, 2, dtype=F32) / dim)


def _rope_tables(layout, inv_freqs, streams):
    _, group, freq, sign = layout
    ang = 0.0
    for g, (inv, pos) in enumerate(zip(inv_freqs, streams)):
        ang = ang + pos[:, None] * (inv[freq] * (group == g))[None, :]
    return jnp.cos(ang), jnp.sin(ang) * sign[None, :]


def _tables(s):
    pos = jnp.arange(s)
    tok, row, col = [t.astype(F32) for t in (pos, pos // GRID_W, pos % GRID_W)]
    inv_b = _inv_freq(B_HEAD_DIM // 2, AXIAL_THETA)
    tab_a = _rope_tables(LAYOUT_A, [_inv_freq(A_ROT, ROPE_THETA)], [tok])
    tab_b = _rope_tables(LAYOUT_B, [inv_b, inv_b], [row, col])
    tab_c = _rope_tables(LAYOUT_C, [_inv_freq(C_ROT, ROPE_THETA)], [tok])
    return tab_a, tab_b, tab_c


def _layer(x, mod, l, lp, tabs, norm_g, lam_rows, subln, w_bg, b_bg, w_out):
    bn, s, _ = x.shape
    shift, scale, gate = [m.reshape(bn, 1, D_MODEL) for m in jnp.split(mod, 3, axis=-1)]
    tm = min(256, s)
    tab_a, tab_b, tab_c = tabs
    lam_init = 0.8 - 0.6 * math.exp(-0.3 * l)

    a_outs = tuple((A_SLAB, kind, dil) for kind in ("q", "k", "v") for _, dil in A_GROUPS)
    pa = _project(x, shift, scale, norm_g, lp["w_a"], lp["gain_a"], tab_a, a_outs, two_heads=False,
                  head_dim=A_HEAD_DIM, q_scale=1.0, tm=tm, name="proj_a")
    n_g = len(A_GROUPS)
    oa = [_band_attention(pa[g], pa[n_g + g], pa[2 * n_g + g], dil) for g, (_, dil) in enumerate(A_GROUPS)]

    qb, kb, vb = _project(x, shift, scale, norm_g, lp["w_b"], lp["gain_b"], tab_b,
                          ((B_WIDTH, "qT", 1), (B_KV, "k", 1), (B_KV, "vT", 1)), two_heads=True,
                          head_dim=B_HEAD_DIM, q_scale=B_HEAD_DIM ** -0.5 * LOG2E, tm=tm, name="proj_b")
    yb = _gqa_attention(qb, kb, vb)

    qc, kc, vc = _project(x, shift, scale, norm_g, lp["w_c"], lp["gain_c"], tab_c,
                          ((C_WIDTH, "qT", 1), (C_WIDTH, "k", 1), (C_WIDTH, "vT", 1)), two_heads=True,
                          head_dim=C_HEAD_DIM, q_scale=C_HEAD_DIM ** -0.5 * LOG2E, tm=tm, name="proj_c")
    yc = _diff_attention(qc, kc, vc, lam_rows, subln, lam_init)

    return _final(x, shift, scale, gate, norm_g, oa, yb, yc, lp["w_z"], w_bg, b_bg, lp["w_oa"], lp["w_ob"],
                  lp["w_oc"], w_out, tm=tm)


def kernel(x_prompt, x_sample, c_prompt, c_sample, norm_g, w_ada, b_ada, w_in, qn_a, kn_a, qn_b, kn_b, qn_c, kn_c,
           lam_q1, lam_k1, lam_q2, lam_k2, subln_c, w_oa, w_ob, w_oc, w_bg, b_bg, w_out):
    groups = ((x_prompt, c_prompt), (x_sample, c_sample))
    rows = [c.shape[0] for _, c in groups]
    pad = -sum(rows) % 8
    c_all = jnp.concatenate([c for _, c in groups] + [jnp.zeros((pad, D_MODEL), F32)], axis=0)
    mod_all = _ada_mod(c_all, w_ada.astype(BF16), b_ada)

    layers = [_prep_layer(w_in[l], qn_a[l], kn_a[l], qn_b[l], kn_b[l], qn_c[l], kn_c[l], w_oa[l], w_ob[l], w_oc[l])
              for l in range(DEPTH)]
    w_bg16, w_out16 = w_bg.astype(BF16), w_out.astype(BF16)

    outs = []
    row0 = 0
    for (x, c), n in zip(groups, rows):
        tabs = _tables(x.shape[1])
        for l in range(DEPTH):
            lam_rows = [p[l][None, :] for p in (lam_q1, lam_k1, lam_q2, lam_k2)]
            x = _layer(x, mod_all[l, row0:row0 + n], l, layers[l], tabs, norm_g[l][None, :], lam_rows,
                       subln_c[l][None, :], w_bg16[l], b_bg[l][None, :], w_out16[l])
        outs.append(x)
        row0 += n
    return tuple(outs)
```

```python
import functools
import math

import jax
import jax.numpy as jnp
import numpy as np
from jax import lax
from jax.experimental import pallas as pl
from jax.experimental.pallas import tpu as pltpu

D_MODEL = 1024
DEPTH = 2
GRID_W = 64
EPS = 1e-6
NEG = -1e30
ROPE_THETA = 500000.0
AXIAL_THETA = 10000.0

A_GROUPS = ((128, 1), (512, 4), (2048, 16))
A_HEADS = 4
A_HEAD_DIM = 96
A_ROT = A_HEAD_DIM // 4
A_BAND = 64
N_A = len(A_GROUPS) * A_HEADS

B_HEADS = 6
B_KV_HEADS = 2
B_HEAD_DIM = 64
C_HEADS = 4
C_HEAD_DIM = 64
C_ROT = C_HEAD_DIM // 4

A_QKV = N_A * A_HEAD_DIM
A_WIDTH = A_HEADS * A_HEAD_DIM
B_WIDTH = B_HEADS * B_HEAD_DIM
B_KV = B_KV_HEADS * B_HEAD_DIM
C_WIDTH = C_HEADS * 2 * C_HEAD_DIM
IN_SIZES = (A_QKV, A_QKV, A_QKV, A_WIDTH, B_WIDTH, B_KV, B_KV, B_WIDTH, C_WIDTH, C_WIDTH, C_WIDTH, C_WIDTH)

LANE = 128
HALF = LANE // 2
ROT_SHIFT = LANE // 2
A_SLAB = A_HEADS * LANE
BF16_SUBLANES = 16
ONES_ROWS = BF16_SUBLANES
FLASH_TK = 512
KEY_PART = 256
SCORE_SLOTS = 2
FLASH_UNROLL = 4
PROJ_TM = 256
FINAL_TM = 256
LOG2E = math.log2(math.e)
LN2 = math.log(2.0)
VMEM_LIMIT = 48 * 1024 * 1024

BF16 = jnp.bfloat16
F32 = jnp.float32


def _cparams(n_axes):
    return pltpu.CompilerParams(dimension_semantics=("arbitrary",) * n_axes, vmem_limit_bytes=VMEM_LIMIT)


def _dot(a, b):
    return jnp.dot(a, b, preferred_element_type=F32)


def _dot_nt(a, b):
    return lax.dot_general(a, b, (((1,), (1,)), ((), ())), preferred_element_type=F32)


def _silu(x):
    return x * jax.nn.sigmoid(x)


def _ada_kernel(c_ref, w_ref, b_ref, o_ref):
    c = c_ref[...]
    o_ref[...] = _dot(_silu(c).astype(BF16), w_ref[...]) + b_ref[...]


def _ada_mod(c_all, w_ada, b_ada):
    rows = c_all.shape[0]
    return pl.pallas_call(
        _ada_kernel,
        grid=(DEPTH, 3),
        in_specs=[
            pl.BlockSpec((rows, D_MODEL), lambda l, n: (0, 0)),
            pl.BlockSpec((None, D_MODEL, D_MODEL), lambda l, n: (l, 0, n)),
            pl.BlockSpec((None, 1, D_MODEL), lambda l, n: (l, 0, n)),
        ],
        out_specs=pl.BlockSpec((None, rows, D_MODEL), lambda l, n: (l, 0, n)),
        out_shape=jax.ShapeDtypeStruct((DEPTH, rows, 3 * D_MODEL), F32),
        compiler_params=_cparams(2),
        name="ada_mod",
    )(c_all, w_ada, b_ada.reshape(DEPTH, 1, 3 * D_MODEL))


def _modulated_norm(x, g, scale, shift):
    ms = jnp.mean(x * x, axis=-1, keepdims=True)
    h = x * lax.rsqrt(ms + EPS) * g
    return h * (1.0 + scale) + shift


def _first_head(index):
    return (index & (HALF - 1)) < HALF // 2


def _head_norm_rope(u, gain, cos, sin, two_heads, head_dim):
    sq = u * u
    if two_heads:
        first = _first_head(lax.broadcasted_iota(jnp.int32, (1, LANE), 1))
        s_first = jnp.sum(jnp.where(first, sq, 0.0), axis=-1, keepdims=True)
        s_all = jnp.sum(sq, axis=-1, keepdims=True)
        ms = jnp.where(first, s_first, s_all - s_first) * (1.0 / head_dim)
    else:
        ms = jnp.sum(sq, axis=-1, keepdims=True) * (1.0 / head_dim)
    y = u * lax.rsqrt(ms + EPS) * gain
    return y * cos + pltpu.roll(y, ROT_SHIFT, 1) * sin


def _proj_kernel(x_ref, shift_ref, scale_ref, g_ref, w_ref, gain_ref, cos_ref, sin_ref, *refs,
                 outs, two_heads, head_dim, q_scale):
    out_refs, stage_ref = refs[:len(outs)], refs[len(outs):]
    h = _modulated_norm(x_ref[...], g_ref[...], scale_ref[...], shift_ref[...]).astype(BF16)
    cos, sin = cos_ref[...], sin_ref[...]
    tm = x_ref.shape[0]
    col = 0
    for o_ref, (width, kind, dil) in zip(out_refs, outs):
        u = _dot(h, w_ref[:, col:col + width])
        if kind == "vT":
            dv = LANE // dil
            for s in range(width // LANE):
                ut = u[:, s * LANE:(s + 1) * LANE].T.astype(BF16)
                for hd in range(dil):
                    base = (s * dil + hd) * (dv + ONES_ROWS)
                    o_ref[base:base + dv, :] = ut[hd * dv:(hd + 1) * dv]
                    o_ref[base + dv:base + dv + ONES_ROWS, :] = jnp.ones((ONES_ROWS, tm), BF16)
        else:
            if kind != "v":
                slabs = []
                for s in range(width // LANE):
                    lo = s * LANE
                    y = _head_norm_rope(u[:, lo:lo + LANE], gain_ref[:, col + lo:col + lo + LANE], cos, sin,
                                        two_heads, head_dim)
                    slabs.append(y * q_scale if kind in ("q", "qT") and q_scale != 1.0 else y)
                u = jnp.concatenate(slabs, axis=-1)
            if kind == "qT":
                for s in range(width // LANE):
                    o_ref[s * LANE:(s + 1) * LANE, :] = u[:, s * LANE:(s + 1) * LANE].T.astype(BF16)
            elif dil == 1:
                o_ref[...] = u.astype(BF16)
            else:
                stage = stage_ref[0]
                for s in range(width // LANE):
                    stage[s] = u[:, s * LANE:(s + 1) * LANE]
                rows = lambda r: pl.ds(r, tm // dil, stride=dil)
                for r in range(dil):
                    for s in range(width // LANE):
                        lo = r * width + s * LANE
                        o_ref[:, lo:lo + LANE] = stage[s, rows(r), :].astype(BF16)
        col += width


def _vt_rows(v_heads):
    return LANE + v_heads * ONES_ROWS


def _proj_out(kind, width, dil, bn, s, tm):
    if kind == "qT":
        return (bn, width, s), pl.BlockSpec((None, width, tm), lambda b, i: (b, 0, i))
    if kind == "vT":
        per = FLASH_TK // tm
        rows = width // LANE * _vt_rows(dil)
        return ((bn, s // FLASH_TK, rows, FLASH_TK),
                pl.BlockSpec((None, None, rows, tm), lambda b, i: (b, i // per, 0, i % per)))
    return (bn, s // dil, dil * width), pl.BlockSpec((None, tm // dil, dil * width), lambda b, i: (b, i, 0))


def _project(x, shift, scale, norm_g, w, gain, tables, outs, *, two_heads, head_dim, q_scale, tm, name):
    bn, s, _ = x.shape
    wcols = w.shape[1]
    kern = functools.partial(_proj_kernel, outs=outs, two_heads=two_heads, head_dim=head_dim, q_scale=q_scale)
    vec = pl.BlockSpec((None, 1, D_MODEL), lambda b, i: (b, 0, 0))
    tab = pl.BlockSpec((tm, LANE), lambda b, i: (i, 0))
    shapes, specs = zip(*[_proj_out(kind, width, dil, bn, s, tm) for width, kind, dil in outs])
    n_slabs = max(width for width, _, _ in outs) // LANE
    dilated = any(d > 1 and kind != "vT" for _, kind, d in outs)
    stage = [pltpu.VMEM((n_slabs, tm, LANE), F32)] if dilated else []
    return pl.pallas_call(
        kern,
        grid=(bn, s // tm),
        in_specs=[
            pl.BlockSpec((None, tm, D_MODEL), lambda b, i: (b, i, 0)),
            vec, vec,
            pl.BlockSpec((1, D_MODEL), lambda b, i: (0, 0)),
            pl.BlockSpec((D_MODEL, wcols), lambda b, i: (0, 0)),
            pl.BlockSpec((1, wcols), lambda b, i: (0, 0)),
            tab, tab,
        ],
        out_specs=list(specs),
        out_shape=[jax.ShapeDtypeStruct(shape, BF16) for shape in shapes],
        scratch_shapes=stage,
        compiler_params=_cparams(2),
        name=name,
    )(x, shift, scale, norm_g, w, gain, *tables)


def _band_attn_kernel(q_ref, kp_ref, kc_ref, kn_ref, vp_ref, vc_ref, vn_ref, bias_ref, o_ref, *, tb, n_blocks):
    i = pl.program_id(2)
    nk = tb + 2 * A_BAND
    colk = lax.broadcasted_iota(jnp.int32, (1, nk), 1)
    first_col = jnp.where(i == 0, A_BAND, 0)
    end_col = jnp.where(i == n_blocks - 1, tb + A_BAND, nk)
    bias = bias_ref[...] + jnp.where((colk < first_col) | (colk >= end_col), NEG, 0.0)
    lane = lax.broadcasted_iota(jnp.int32, (1, LANE), 1)
    heads = [slice(j * LANE, (j + 1) * LANE) for j in range(A_HEADS)]
    band = lambda p_ref, c_ref, n_ref, sl: jnp.concatenate([p_ref[:, sl], c_ref[:, sl], n_ref[:, sl]], axis=0)
    scores = [_dot_nt(q_ref[:, sl], band(kp_ref, kc_ref, kn_ref, sl)) for sl in heads]
    stats = []
    for s in scores:
        s = s + bias
        m = jnp.max(s, axis=-1, keepdims=True)
        p = jnp.exp2(s - m)
        stats.append((m, jnp.sum(p, axis=-1, keepdims=True), p.astype(BF16)))
    for sl, (m, l, p) in zip(heads, stats):
        o = _dot(p, band(vp_ref, vc_ref, vn_ref, sl)) / l
        o_ref[:, sl] = jnp.where(lane >= A_HEAD_DIM, m * LN2 + jnp.log(l), o)


def _band_attention(q, k, v, dilation):
    bn, length, _ = q.shape
    tb = min(256, length)
    nb = length // tb
    per = tb // A_BAND
    cur = pl.BlockSpec((None, tb, A_SLAB), lambda b, r, i: (b, i, r))
    prv = pl.BlockSpec((None, A_BAND, A_SLAB), lambda b, r, i: (b, jnp.maximum(i * per - 1, 0), r))
    nxt = pl.BlockSpec((None, A_BAND, A_SLAB), lambda b, r, i: (b, jnp.minimum((i + 1) * per, nb * per - 1), r))
    nk = tb + 2 * A_BAND
    dist = np.arange(nk)[None, :] - A_BAND - np.arange(tb)[:, None]
    bias = np.where(np.abs(dist) <= A_BAND, 0.0, NEG).astype(np.float32)
    kern = functools.partial(_band_attn_kernel, tb=tb, n_blocks=nb)
    return pl.pallas_call(
        kern,
        grid=(bn, dilation, nb),
        in_specs=[cur, prv, cur, nxt, prv, cur, nxt, pl.BlockSpec((tb, nk), lambda b, r, i: (0, 0))],
        out_specs=cur,
        out_shape=jax.ShapeDtypeStruct((bn, length, dilation * A_SLAB), F32),
        compiler_params=_cparams(3),
        name=f"band_attn_d{dilation}",
    )(q, k, k, k, v, v, v, bias)


def _stacked_flash(qt_ref, k_ref, vt_ref, acc_sc, s_sc, *, tq, n_chunks, v_heads):
    dv = LANE // v_heads
    v_rows = [slice(hd * (dv + ONES_ROWS), (hd + 1) * (dv + ONES_ROWS)) for hd in range(v_heads)]
    col_groups = [slice(0, 2 * tq)] if v_heads == 1 else [slice(0, tq), slice(tq, 2 * tq)]
    first = _first_head(lax.broadcasted_iota(jnp.int32, (LANE, 1), 0))
    qt = qt_ref[...]
    zero = jnp.zeros_like(qt)
    q2t = jnp.concatenate([jnp.where(first, qt, zero), jnp.where(first, zero, qt)], axis=1)
    acc_sc[...] = jnp.zeros(acc_sc.shape, F32)

    parts = [slice(lo, lo + KEY_PART) for lo in range(0, FLASH_TK, KEY_PART)]

    def scores(c, slot, rows):
        start = pl.multiple_of(c * FLASH_TK + rows.start, KEY_PART)
        s_sc[slot, rows, :] = _dot(k_ref[pl.ds(start, KEY_PART), :], q2t)

    def step(c, slot, m_prev, c_next=None):
        m_next = jnp.maximum(m_prev, jnp.max(s_sc[slot], axis=0, keepdims=True))
        vc = vt_ref[c]
        alpha = jnp.exp2(m_prev - m_next)
        pv = [None] * v_heads
        for rows in parts:
            if c_next is not None:
                scores(c_next, 1 - slot, rows)
            pt = jnp.exp2(s_sc[slot, rows, :] - m_next).astype(BF16)
            for g, (vr, cols) in enumerate(zip(v_rows, col_groups)):
                part = _dot(vc[vr, rows], pt[:, cols])
                pv[g] = part if pv[g] is None else pv[g] + part
        for cols, total in zip(col_groups, pv):
            acc_sc[:, cols] = alpha[:, cols] * acc_sc[:, cols] + total
        return m_next

    unroll = min(FLASH_UNROLL, n_chunks)

    def group(j, m):
        for u in range(unroll):
            c = unroll * j + u
            m = step(c, u % SCORE_SLOTS, m, c + 1)
        return m

    for rows in parts:
        scores(0, 0, rows)
    m = lax.fori_loop(0, n_chunks // unroll - 1, group, jnp.full((1, 2 * tq), -jnp.inf, F32))
    for c in range(n_chunks - unroll, n_chunks):
        m = step(c, c % SCORE_SLOTS, m, c + 1 if c + 1 < n_chunks else None)
    acc = acc_sc[...]
    return acc[:dv] / acc[dv:dv + 1]


def _gqa_kernel(qt_ref, k_ref, vt_ref, o_ref, acc_sc, s_sc, *, tq, n_chunks):
    o = _stacked_flash(qt_ref, k_ref, vt_ref, acc_sc, s_sc, tq=tq, n_chunks=n_chunks, v_heads=B_KV_HEADS)
    o_ref[...] = jnp.concatenate([o[:, :tq], o[:, tq:]], axis=0).T


def _diff_kernel(qt_ref, k_ref, vt_ref, lq1_ref, lk1_ref, lq2_ref, lk2_ref, sub_ref, o_ref, acc_sc,
                 s_sc, *, tq, n_chunks, lam_init):
    o = _stacked_flash(qt_ref, k_ref, vt_ref, acc_sc, s_sc, tq=tq, n_chunks=n_chunks, v_heads=1)
    lam = (jnp.exp(jnp.sum(lq1_ref[...] * lk1_ref[...], axis=-1, keepdims=True))
           - jnp.exp(jnp.sum(lq2_ref[...] * lk2_ref[...], axis=-1, keepdims=True)) + lam_init)
    oc = (o[:, :tq] - lam * o[:, tq:]).T
    ms = jnp.mean(oc * oc, axis=-1, keepdims=True)
    o_ref[...] = oc * lax.rsqrt(ms + EPS) * sub_ref[...] * (1.0 - lam_init)


def _flash_tq(s):
    return min(512, s)


def _flash_scratch(tq, v_heads):
    return [pltpu.VMEM((LANE // v_heads + ONES_ROWS, 2 * tq), F32),
            pltpu.VMEM((SCORE_SLOTS, FLASH_TK, 2 * tq), F32)]


def _gqa_attention(qt, k, vt):
    bn, _, s = qt.shape
    tq, n_chunks = _flash_tq(s), s // FLASH_TK
    kern = functools.partial(_gqa_kernel, tq=tq, n_chunks=n_chunks)
    return pl.pallas_call(
        kern,
        grid=(bn, B_HEADS // 2, s // tq),
        in_specs=[pl.BlockSpec((None, LANE, tq), lambda b, p, i: (b, p, i)),
                  pl.BlockSpec((None, s, LANE), lambda b, p, i: (b, 0, 0)),
                  pl.BlockSpec((None, n_chunks, _vt_rows(B_KV_HEADS), FLASH_TK), lambda b, p, i: (b, 0, 0, 0))],
        out_specs=pl.BlockSpec((None, tq, LANE), lambda b, p, i: (b, i, p)),
        out_shape=jax.ShapeDtypeStruct((bn, s, B_WIDTH), F32),
        scratch_shapes=_flash_scratch(tq, B_KV_HEADS),
        compiler_params=_cparams(3),
        name="gqa_attn",
    )(qt, k, vt)


def _diff_attention(qt, k, vt, lam_rows, subln, lam_init):
    bn, _, s = qt.shape
    tq, n_chunks = _flash_tq(s), s // FLASH_TK
    kern = functools.partial(_diff_kernel, tq=tq, n_chunks=n_chunks, lam_init=lam_init)
    row = lambda n: pl.BlockSpec((1, n), lambda b, h, i: (0, 0))
    return pl.pallas_call(
        kern,
        grid=(bn, C_HEADS, s // tq),
        in_specs=[pl.BlockSpec((None, LANE, tq), lambda b, h, i: (b, h, i)),
                  pl.BlockSpec((None, s, LANE), lambda b, h, i: (b, 0, h)),
                  pl.BlockSpec((None, n_chunks, _vt_rows(1), FLASH_TK), lambda b, h, i: (b, 0, h, 0)),
                  row(C_HEAD_DIM), row(C_HEAD_DIM), row(C_HEAD_DIM), row(C_HEAD_DIM), row(LANE)],
        out_specs=pl.BlockSpec((None, tq, LANE), lambda b, h, i: (b, i, h)),
        out_shape=jax.ShapeDtypeStruct((bn, s, C_WIDTH), F32),
        scratch_shapes=_flash_scratch(tq, 1),
        compiler_params=_cparams(3),
        name="diff_attn",
    )(qt, k, vt, *lam_rows, subln)


def _final_kernel(x_ref, shift_ref, scale_ref, gate_ref, g_ref, oa0_ref, oa1_ref, oa2_ref, yb_ref, yc_ref,
                  wz_ref, wbg_ref, bbg_ref, woa_ref, wob_ref, woc_ref, wout_ref, o_ref, *stage_refs):
    x = x_ref[...]
    tm = x.shape[0]
    h = _modulated_norm(x, g_ref[...], scale_ref[...], shift_ref[...]).astype(BF16)
    lane = lax.broadcasted_iota(jnp.int32, (1, LANE), 1)

    for oa_ref, st_ref, (_, dil) in zip((oa1_ref, oa2_ref), stage_refs, A_GROUPS[1:]):
        for r in range(dil):
            for j in range(A_HEADS):
                lo = r * A_SLAB + j * LANE
                st_ref[j, pl.ds(r, tm // dil, stride=dil), :] = oa_ref[:, lo:lo + LANE]

    ya = []
    for j in range(A_HEADS):
        slabs = [oa0_ref[:, j * LANE:(j + 1) * LANE]] + [st_ref[j] for st_ref in stage_refs]
        lses = [t[:, A_HEAD_DIM:A_HEAD_DIM + 1] for t in slabs]
        top = jnp.maximum(jnp.maximum(lses[0], lses[1]), lses[2])
        ws = [jnp.exp(t - top) for t in lses]
        mix = (ws[0] * slabs[0] + ws[1] * slabs[1] + ws[2] * slabs[2]) / (ws[0] + ws[1] + ws[2])
        ya.append(jnp.where(lane < A_HEAD_DIM, mix, 0.0))
    ya = jnp.concatenate(ya, axis=-1)

    def branch(y, z_lo, z_hi, wo_ref, g_lo):
        z = _dot(h, wz_ref[:, z_lo:z_hi])
        p = _dot((y * _silu(z)).astype(BF16), wo_ref[...])
        gate = jax.nn.sigmoid(_dot(h, wbg_ref[:, g_lo:g_lo + D_MODEL]) + bbg_ref[:, g_lo:g_lo + D_MODEL])
        return gate * p

    za_hi = A_SLAB
    zb_hi = za_hi + B_WIDTH
    zc_hi = zb_hi + C_WIDTH
    merged = (branch(ya, 0, za_hi, woa_ref, 0)
              + branch(yb_ref[...], za_hi, zb_hi, wob_ref, D_MODEL)
              + branch(yc_ref[...], zb_hi, zc_hi, woc_ref, 2 * D_MODEL))
    out = _dot(merged.astype(BF16), wout_ref[...])
    o_ref[...] = x + gate_ref[...] * out


def _final(x, shift, scale, gate, norm_g, oa, yb, yc, wz, wbg, bbg, woa, wob, woc, wout, *, tm):
    bn, s, _ = x.shape
    tok = lambda w: pl.BlockSpec((None, tm, w), lambda b, i: (b, i, 0))
    vec = pl.BlockSpec((None, 1, D_MODEL), lambda b, i: (b, 0, 0))
    full = lambda a: pl.BlockSpec(a.shape, lambda b, i: (0, 0))
    band = [pl.BlockSpec((None, tm // dil, dil * A_SLAB), lambda b, i: (b, i, 0)) for _, dil in A_GROUPS]
    return pl.pallas_call(
        _final_kernel,
        grid=(bn, s // tm),
        in_specs=[tok(D_MODEL), vec, vec, vec, full(norm_g), *band,
                  tok(B_WIDTH), tok(C_WIDTH), full(wz), full(wbg), full(bbg), full(woa), full(wob), full(woc),
                  full(wout)],
        out_specs=tok(D_MODEL),
        out_shape=jax.ShapeDtypeStruct((bn, s, D_MODEL), F32),
        scratch_shapes=[pltpu.VMEM((A_HEADS, tm, LANE), F32) for _ in A_GROUPS[1:]],
        compiler_params=_cparams(2),
        name="final",
    )(x, shift, scale, gate, norm_g, *oa, yb, yc, wz, wbg, bbg, woa, wob, woc, wout)


B_Q_ORDER = (0, 3, 1, 4, 2, 5)


def _slab_layout(head_dim, n_heads, rot_groups):
    src = -np.ones((LANE,), np.int32)
    group = -np.ones((LANE,), np.int32)
    freq = np.zeros((LANE,), np.int32)
    sign = np.zeros((LANE,), np.float32)
    width = ROT_SHIFT // n_heads
    for hd in range(n_heads):
        low = [(start + i, g, i, -1.0) for g, (start, half) in enumerate(rot_groups) for i in range(half)]
        high = [(start + half + i, g, i, 1.0) for g, (start, half) in enumerate(rot_groups) for i in range(half)]
        rotary = {d for d, _, _, _ in low + high}
        rest = [(d, -1, 0, 0.0) for d in range(head_dim) if d not in rotary]
        n_low_rest = width - len(low)
        low, high = low + rest[:n_low_rest], high + rest[n_low_rest:]
        assert len(low) == width and len(high) <= width
        for base, items in ((hd * width, low), (hd * width + ROT_SHIFT, high)):
            for lane, (d, g, i, sg) in enumerate(items, start=base):
                src[lane], group[lane], freq[lane], sign[lane] = hd * head_dim + d, g, i, sg
    return src, group, freq, sign


LAYOUT_A = _slab_layout(A_HEAD_DIM, 1, ((0, A_ROT // 2),))
LAYOUT_B = _slab_layout(B_HEAD_DIM, 2, ((0, B_HEAD_DIM // 4), (B_HEAD_DIM // 2, B_HEAD_DIM // 4)))
LAYOUT_C = _slab_layout(C_HEAD_DIM, 2, ((0, C_ROT // 2),))


def _to_slabs(w, layout, dims_per_slab):
    src = layout[0]
    n_slabs = w.shape[-1] // dims_per_slab
    idx = np.concatenate([np.maximum(src, 0) + s * dims_per_slab for s in range(n_slabs)])
    keep = np.tile(src >= 0, n_slabs)
    return jnp.where(keep, jnp.take(w, idx, axis=-1), 0.0)


def _pad_heads(w, n_heads):
    lead = w.shape[:-1]
    w = w.reshape(*lead, n_heads, A_HEAD_DIM)
    w = jnp.pad(w, [(0, 0)] * len(lead) + [(0, 0), (0, LANE - A_HEAD_DIM)])
    return w.reshape(*lead, n_heads * LANE)


def _permute_heads(w, order, dim):
    lead = w.shape[:-1]
    w = w.reshape(*lead, len(order), dim)
    return w[..., jnp.array(order), :].reshape(*lead, len(order) * dim)


def _prep_layer(w_in, qn_a, kn_a, qn_b, kn_b, qn_c, kn_c, w_oa, w_ob, w_oc):
    offs = [0]
    for n in IN_SIZES:
        offs.append(offs[-1] + n)
    qa, ka, va, za, qb, kb, vb, zb, qc, kc, vc, zc = [w_in[:, offs[i]:offs[i + 1]] for i in range(len(IN_SIZES))]

    slab_a = functools.partial(_to_slabs, layout=LAYOUT_A, dims_per_slab=A_HEAD_DIM)
    slab_b = functools.partial(_to_slabs, layout=LAYOUT_B, dims_per_slab=2 * B_HEAD_DIM)
    slab_c = functools.partial(_to_slabs, layout=LAYOUT_C, dims_per_slab=2 * C_HEAD_DIM)
    qb = _permute_heads(qb, B_Q_ORDER, B_HEAD_DIM)

    w_a = jnp.concatenate([slab_a(qa), slab_a(ka), _pad_heads(va, N_A)], axis=-1).astype(BF16)
    gain_a = jnp.concatenate([slab_a(jnp.tile(qn_a, N_A)), slab_a(jnp.tile(kn_a, N_A)),
                              jnp.zeros((N_A * LANE,), F32)])[None, :]

    w_b = jnp.concatenate([slab_b(qb), slab_b(kb), vb], axis=-1).astype(BF16)
    gain_b = jnp.concatenate([slab_b(jnp.tile(qn_b, B_HEADS)), slab_b(jnp.tile(kn_b, B_KV_HEADS)),
                              jnp.zeros((B_KV,), F32)])[None, :]

    w_c = jnp.concatenate([slab_c(qc), slab_c(kc), vc], axis=-1).astype(BF16)
    gain_c = jnp.concatenate([slab_c(jnp.tile(qn_c, 2 * C_HEADS)), slab_c(jnp.tile(kn_c, 2 * C_HEADS)),
                              jnp.zeros((C_WIDTH,), F32)])[None, :]

    w_z = jnp.concatenate([_pad_heads(za, A_HEADS), _permute_heads(zb, B_Q_ORDER, B_HEAD_DIM), zc],
                          axis=-1).astype(BF16)
    w_oa_p = _pad_heads(w_oa.T, A_HEADS).T.astype(BF16)
    w_ob_p = _permute_heads(w_ob.T, B_Q_ORDER, B_HEAD_DIM).T.astype(BF16)
    return dict(w_a=w_a, gain_a=gain_a, w_b=w_b, gain_b=gain_b, w_c=w_c, gain_c=gain_c, w_z=w_z,
                w_oa=w_oa_p, w_ob=w_ob_p, w_oc=w_oc.astype(BF16))


def _inv_freq(dim, theta):
    return theta ** (-jnp.arange(0, dim, 2, dtype=F32) / dim)


def _rope_tables(layout, inv_freqs, streams):
    _, group, freq, sign = layout
    ang = 0.0
    for g, (inv, pos) in enumerate(zip(inv_freqs, streams)):
        ang = ang + pos[:, None] * (inv[freq] * (group == g))[None, :]
    return jnp.cos(ang), jnp.sin(ang) * sign[None, :]


def _tables(s):
    pos = jnp.arange(s)
    tok, row, col = [t.astype(F32) for t in (pos, pos // GRID_W, pos % GRID_W)]
    inv_b = _inv_freq(B_HEAD_DIM // 2, AXIAL_THETA)
    tab_a = _rope_tables(LAYOUT_A, [_inv_freq(A_ROT, ROPE_THETA)], [tok])
    tab_b = _rope_tables(LAYOUT_B, [inv_b, inv_b], [row, col])
    tab_c = _rope_tables(LAYOUT_C, [_inv_freq(C_ROT, ROPE_THETA)], [tok])
    return tab_a, tab_b, tab_c


def _layer(x, mod, l, lp, tabs, norm_g, lam_rows, subln, w_bg, b_bg, w_out):
    bn, s, _ = x.shape
    shift, scale, gate = [m.reshape(bn, 1, D_MODEL) for m in jnp.split(mod, 3, axis=-1)]
    tab_a, tab_b, tab_c = tabs
    lam_init = 0.8 - 0.6 * math.exp(-0.3 * l)

    a_outs = tuple((A_SLAB, kind, dil) for kind in ("q", "k", "v") for _, dil in A_GROUPS)
    pa = _project(x, shift, scale, norm_g, lp["w_a"], lp["gain_a"], tab_a, a_outs, two_heads=False,
                  head_dim=A_HEAD_DIM, q_scale=A_HEAD_DIM ** -0.5 * LOG2E, tm=min(PROJ_TM, s), name="proj_a")
    n_g = len(A_GROUPS)
    oa = [_band_attention(pa[g], pa[n_g + g], pa[2 * n_g + g], dil) for g, (_, dil) in enumerate(A_GROUPS)]

    qb, kb, vb = _project(x, shift, scale, norm_g, lp["w_b"], lp["gain_b"], tab_b,
                          ((B_WIDTH, "qT", 1), (B_KV, "k", 1), (B_KV, "vT", B_KV_HEADS)), two_heads=True,
                          head_dim=B_HEAD_DIM, q_scale=B_HEAD_DIM ** -0.5 * LOG2E, tm=min(PROJ_TM, s), name="proj_b")
    yb = _gqa_attention(qb, kb, vb)

    qc, kc, vc = _project(x, shift, scale, norm_g, lp["w_c"], lp["gain_c"], tab_c,
                          ((C_WIDTH, "qT", 1), (C_WIDTH, "k", 1), (C_WIDTH, "vT", 1)), two_heads=True,
                          head_dim=C_HEAD_DIM, q_scale=C_HEAD_DIM ** -0.5 * LOG2E, tm=min(PROJ_TM, s), name="proj_c")
    yc = _diff_attention(qc, kc, vc, lam_rows, subln, lam_init)

    return _final(x, shift, scale, gate, norm_g, oa, yb, yc, lp["w_z"], w_bg, b_bg, lp["w_oa"], lp["w_ob"],
                  lp["w_oc"], w_out, tm=min(FINAL_TM, s))


def kernel(x_prompt, x_sample, c_prompt, c_sample, norm_g, w_ada, b_ada, w_in, qn_a, kn_a, qn_b, kn_b, qn_c, kn_c,
           lam_q1, lam_k1, lam_q2, lam_k2, subln_c, w_oa, w_ob, w_oc, w_bg, b_bg, w_out):
    groups = ((x_prompt, c_prompt), (x_sample, c_sample))
    rows = [c.shape[0] for _, c in groups]
    pad = -sum(rows) % 8
    c_all = jnp.concatenate([c for _, c in groups] + [jnp.zeros((pad, D_MODEL), F32)], axis=0)
    mod_all = _ada_mod(c_all, w_ada.astype(BF16), b_ada)

    layers = [_prep_layer(w_in[l], qn_a[l], kn_a[l], qn_b[l], kn_b[l], qn_c[l], kn_c[l], w_oa[l], w_ob[l], w_oc[l])
              for l in range(DEPTH)]
    w_bg16, w_out16 = w_bg.astype(BF16), w_out.astype(BF16)

    outs = []
    row0 = 0
    for (x, c), n in zip(groups, rows):
        tabs = _tables(x.shape[1])
        for l in range(DEPTH):
            lam_rows = [p[l][None, :] for p in (lam_q1, lam_k1, lam_q2, lam_k2)]
            x = _layer(x, mod_all[l, row0:row0 + n], l, layers[l], tabs, norm_g[l][None, :], lam_rows,
                       subln_c[l][None, :], w_bg16[l], b_bg[l][None, :], w_out16[l])
        outs.append(x)
        row0 += n
    return tuple(outs)
```

```python
import functools
import math

import jax
import jax.numpy as jnp
import numpy as np
from jax import lax
from jax.experimental import pallas as pl
from jax.experimental.pallas import tpu as pltpu

D_MODEL = 1024
DEPTH = 2
GRID_W = 64
EPS = 1e-6
NEG = -1e30
ROPE_THETA = 500000.0
AXIAL_THETA = 10000.0

A_GROUPS = ((128, 1), (512, 4), (2048, 16))
A_HEADS = 4
A_HEAD_DIM = 96
A_ROT = A_HEAD_DIM // 4
A_BAND = 64
N_A = len(A_GROUPS) * A_HEADS

B_HEADS = 6
B_KV_HEADS = 2
B_HEAD_DIM = 64
C_HEADS = 4
C_HEAD_DIM = 64
C_ROT = C_HEAD_DIM // 4

A_QKV = N_A * A_HEAD_DIM
A_WIDTH = A_HEADS * A_HEAD_DIM
B_WIDTH = B_HEADS * B_HEAD_DIM
B_KV = B_KV_HEADS * B_HEAD_DIM
C_WIDTH = C_HEADS * 2 * C_HEAD_DIM
IN_SIZES = (A_QKV, A_QKV, A_QKV, A_WIDTH, B_WIDTH, B_KV, B_KV, B_WIDTH, C_WIDTH, C_WIDTH, C_WIDTH, C_WIDTH)

LANE = 128
HALF = LANE // 2
ROT_SHIFT = LANE // 2
A_SLAB = A_HEADS * LANE
BF16_SUBLANES = 16
ONES_ROWS = BF16_SUBLANES
FLASH_TK = 512
KEY_PART = 512
SCORE_SLOTS = 2
FLASH_UNROLL = 4
PROJ_TM = 256
FINAL_TM = 256
LOG2E = math.log2(math.e)
LN2 = math.log(2.0)
VMEM_LIMIT = 48 * 1024 * 1024

BF16 = jnp.bfloat16
F32 = jnp.float32


def _cparams(n_axes):
    return pltpu.CompilerParams(dimension_semantics=("arbitrary",) * n_axes, vmem_limit_bytes=VMEM_LIMIT)


def _dot(a, b):
    return jnp.dot(a, b, preferred_element_type=F32)


def _dot_nt(a, b):
    return lax.dot_general(a, b, (((1,), (1,)), ((), ())), preferred_element_type=F32)


def _silu(x):
    return x * jax.nn.sigmoid(x)


def _ada_kernel(c_ref, w_ref, b_ref, o_ref):
    c = c_ref[...]
    o_ref[...] = _dot(_silu(c).astype(BF16), w_ref[...]) + b_ref[...]


def _ada_mod(c_all, w_ada, b_ada):
    rows = c_all.shape[0]
    return pl.pallas_call(
        _ada_kernel,
        grid=(DEPTH, 3),
        in_specs=[
            pl.BlockSpec((rows, D_MODEL), lambda l, n: (0, 0)),
            pl.BlockSpec((None, D_MODEL, D_MODEL), lambda l, n: (l, 0, n)),
            pl.BlockSpec((None, 1, D_MODEL), lambda l, n: (l, 0, n)),
        ],
        out_specs=pl.BlockSpec((None, rows, D_MODEL), lambda l, n: (l, 0, n)),
        out_shape=jax.ShapeDtypeStruct((DEPTH, rows, 3 * D_MODEL), F32),
        compiler_params=_cparams(2),
        name="ada_mod",
    )(c_all, w_ada, b_ada.reshape(DEPTH, 1, 3 * D_MODEL))


def _modulated_norm(x, g, scale, shift):
    ms = jnp.mean(x * x, axis=-1, keepdims=True)
    h = x * lax.rsqrt(ms + EPS) * g
    return h * (1.0 + scale) + shift


def _first_head(index):
    return (index & (HALF - 1)) < HALF // 2


def _head_norm_rope(u, gain, cos, sin, two_heads, head_dim):
    sq = u * u
    if two_heads:
        first = _first_head(lax.broadcasted_iota(jnp.int32, (1, LANE), 1))
        s_first = jnp.sum(jnp.where(first, sq, 0.0), axis=-1, keepdims=True)
        s_all = jnp.sum(sq, axis=-1, keepdims=True)
        ms = jnp.where(first, s_first, s_all - s_first) * (1.0 / head_dim)
    else:
        ms = jnp.sum(sq, axis=-1, keepdims=True) * (1.0 / head_dim)
    y = u * lax.rsqrt(ms + EPS) * gain
    return y * cos + pltpu.roll(y, ROT_SHIFT, 1) * sin


def _proj_kernel(x_ref, shift_ref, scale_ref, g_ref, w_ref, gain_ref, cos_ref, sin_ref, *refs,
                 outs, two_heads, head_dim, q_scale):
    out_refs, stage_ref = refs[:len(outs)], refs[len(outs):]
    h = _modulated_norm(x_ref[...], g_ref[...], scale_ref[...], shift_ref[...]).astype(BF16)
    cos, sin = cos_ref[...], sin_ref[...]
    tm = x_ref.shape[0]
    col = 0
    for o_ref, (width, kind, dil) in zip(out_refs, outs):
        u = _dot(h, w_ref[:, col:col + width])
        if kind == "vT":
            dv = LANE // dil
            for s in range(width // LANE):
                ut = u[:, s * LANE:(s + 1) * LANE].T.astype(BF16)
                for hd in range(dil):
                    base = (s * dil + hd) * (dv + ONES_ROWS)
                    o_ref[base:base + dv, :] = ut[hd * dv:(hd + 1) * dv]
                    o_ref[base + dv:base + dv + ONES_ROWS, :] = jnp.ones((ONES_ROWS, tm), BF16)
        else:
            if kind != "v":
                slabs = []
                for s in range(width // LANE):
                    lo = s * LANE
                    y = _head_norm_rope(u[:, lo:lo + LANE], gain_ref[:, col + lo:col + lo + LANE], cos, sin,
                                        two_heads, head_dim)
                    slabs.append(y * q_scale if kind in ("q", "qT") and q_scale != 1.0 else y)
                u = jnp.concatenate(slabs, axis=-1)
            if kind == "qT":
                for s in range(width // LANE):
                    o_ref[s * LANE:(s + 1) * LANE, :] = u[:, s * LANE:(s + 1) * LANE].T.astype(BF16)
            elif dil == 1:
                o_ref[...] = u.astype(BF16)
            else:
                stage = stage_ref[0]
                for s in range(width // LANE):
                    stage[s] = u[:, s * LANE:(s + 1) * LANE]
                rows = lambda r: pl.ds(r, tm // dil, stride=dil)
                for r in range(dil):
                    for s in range(width // LANE):
                        lo = r * width + s * LANE
                        o_ref[:, lo:lo + LANE] = stage[s, rows(r), :].astype(BF16)
        col += width


def _vt_rows(v_heads):
    return LANE + v_heads * ONES_ROWS


def _proj_out(kind, width, dil, bn, s, tm):
    if kind == "qT":
        return (bn, width, s), pl.BlockSpec((None, width, tm), lambda b, i: (b, 0, i))
    if kind == "vT":
        per = FLASH_TK // tm
        rows = width // LANE * _vt_rows(dil)
        return ((bn, s // FLASH_TK, rows, FLASH_TK),
                pl.BlockSpec((None, None, rows, tm), lambda b, i: (b, i // per, 0, i % per)))
    return (bn, s // dil, dil * width), pl.BlockSpec((None, tm // dil, dil * width), lambda b, i: (b, i, 0))


def _project(x, shift, scale, norm_g, w, gain, tables, outs, *, two_heads, head_dim, q_scale, tm, name):
    bn, s, _ = x.shape
    wcols = w.shape[1]
    kern = functools.partial(_proj_kernel, outs=outs, two_heads=two_heads, head_dim=head_dim, q_scale=q_scale)
    vec = pl.BlockSpec((None, 1, D_MODEL), lambda b, i: (b, 0, 0))
    tab = pl.BlockSpec((tm, LANE), lambda b, i: (i, 0))
    shapes, specs = zip(*[_proj_out(kind, width, dil, bn, s, tm) for width, kind, dil in outs])
    n_slabs = max(width for width, _, _ in outs) // LANE
    dilated = any(d > 1 and kind != "vT" for _, kind, d in outs)
    stage = [pltpu.VMEM((n_slabs, tm, LANE), F32)] if dilated else []
    return pl.pallas_call(
        kern,
        grid=(bn, s // tm),
        in_specs=[
            pl.BlockSpec((None, tm, D_MODEL), lambda b, i: (b, i, 0)),
            vec, vec,
            pl.BlockSpec((1, D_MODEL), lambda b, i: (0, 0)),
            pl.BlockSpec((D_MODEL, wcols), lambda b, i: (0, 0)),
            pl.BlockSpec((1, wcols), lambda b, i: (0, 0)),
            tab, tab,
        ],
        out_specs=list(specs),
        out_shape=[jax.ShapeDtypeStruct(shape, BF16) for shape in shapes],
        scratch_shapes=stage,
        compiler_params=_cparams(2),
        name=name,
    )(x, shift, scale, norm_g, w, gain, *tables)


def _band_attn_kernel(q_ref, kp_ref, kc_ref, kn_ref, vp_ref, vc_ref, vn_ref, bias_ref, o_ref, *, tb, n_blocks):
    i = pl.program_id(2)
    nk = tb + 2 * A_BAND
    colk = lax.broadcasted_iota(jnp.int32, (1, nk), 1)
    first_col = jnp.where(i == 0, A_BAND, 0)
    end_col = jnp.where(i == n_blocks - 1, tb + A_BAND, nk)
    bias = bias_ref[...] + jnp.where((colk < first_col) | (colk >= end_col), NEG, 0.0)
    lane = lax.broadcasted_iota(jnp.int32, (1, LANE), 1)
    heads = [slice(j * LANE, (j + 1) * LANE) for j in range(A_HEADS)]
    band = lambda p_ref, c_ref, n_ref, sl: jnp.concatenate([p_ref[:, sl], c_ref[:, sl], n_ref[:, sl]], axis=0)
    scores = [_dot_nt(q_ref[:, sl], band(kp_ref, kc_ref, kn_ref, sl)) for sl in heads]
    stats = []
    for s in scores:
        s = s + bias
        m = jnp.max(s, axis=-1, keepdims=True)
        p = jnp.exp2(s - m)
        stats.append((m, jnp.sum(p, axis=-1, keepdims=True), p.astype(BF16)))
    for sl, (m, l, p) in zip(heads, stats):
        o = _dot(p, band(vp_ref, vc_ref, vn_ref, sl)) / l
        o_ref[:, sl] = jnp.where(lane >= A_HEAD_DIM, m * LN2 + jnp.log(l), o)


def _band_attention(q, k, v, dilation):
    bn, length, _ = q.shape
    tb = min(256, length)
    nb = length // tb
    per = tb // A_BAND
    cur = pl.BlockSpec((None, tb, A_SLAB), lambda b, r, i: (b, i, r))
    prv = pl.BlockSpec((None, A_BAND, A_SLAB), lambda b, r, i: (b, jnp.maximum(i * per - 1, 0), r))
    nxt = pl.BlockSpec((None, A_BAND, A_SLAB), lambda b, r, i: (b, jnp.minimum((i + 1) * per, nb * per - 1), r))
    nk = tb + 2 * A_BAND
    dist = np.arange(nk)[None, :] - A_BAND - np.arange(tb)[:, None]
    bias = np.where(np.abs(dist) <= A_BAND, 0.0, NEG).astype(np.float32)
    kern = functools.partial(_band_attn_kernel, tb=tb, n_blocks=nb)
    return pl.pallas_call(
        kern,
        grid=(bn, dilation, nb),
        in_specs=[cur, prv, cur, nxt, prv, cur, nxt, pl.BlockSpec((tb, nk), lambda b, r, i: (0, 0))],
        out_specs=cur,
        out_shape=jax.ShapeDtypeStruct((bn, length, dilation * A_SLAB), F32),
        compiler_params=_cparams(3),
        name=f"band_attn_d{dilation}",
    )(q, k, k, k, v, v, v, bias)


def _stacked_flash(qt_ref, k_ref, vt_ref, acc_sc, s_sc, *, tq, n_chunks, v_heads):
    dv = LANE // v_heads
    v_rows = [slice(hd * (dv + ONES_ROWS), (hd + 1) * (dv + ONES_ROWS)) for hd in range(v_heads)]
    col_groups = [slice(0, 2 * tq)] if v_heads == 1 else [slice(0, tq), slice(tq, 2 * tq)]
    first = _first_head(lax.broadcasted_iota(jnp.int32, (LANE, 1), 0))
    qt = qt_ref[...]
    zero = jnp.zeros_like(qt)
    q2t = jnp.concatenate([jnp.where(first, qt, zero), jnp.where(first, zero, qt)], axis=1)
    acc_sc[...] = jnp.zeros(acc_sc.shape, F32)

    parts = [slice(lo, lo + KEY_PART) for lo in range(0, FLASH_TK, KEY_PART)]

    def scores(c, slot, rows):
        start = pl.multiple_of(c * FLASH_TK + rows.start, KEY_PART)
        s_sc[slot, rows, :] = _dot(k_ref[pl.ds(start, KEY_PART), :], q2t)

    def step(c, slot, m_prev, c_next=None):
        m_next = jnp.maximum(m_prev, jnp.max(s_sc[slot], axis=0, keepdims=True))
        vc = vt_ref[c]
        alpha = jnp.exp2(m_prev - m_next)
        pv = [None] * v_heads
        for rows in parts:
            if c_next is not None:
                scores(c_next, 1 - slot, rows)
            pt = jnp.exp2(s_sc[slot, rows, :] - m_next).astype(BF16)
            for g, (vr, cols) in enumerate(zip(v_rows, col_groups)):
                part = _dot(vc[vr, rows], pt[:, cols])
                pv[g] = part if pv[g] is None else pv[g] + part
        for cols, total in zip(col_groups, pv):
            acc_sc[:, cols] = alpha[:, cols] * acc_sc[:, cols] + total
        return m_next

    unroll = min(FLASH_UNROLL, n_chunks)

    def group(j, m):
        for u in range(unroll):
            c = unroll * j + u
            m = step(c, u % SCORE_SLOTS, m, c + 1)
        return m

    for rows in parts:
        scores(0, 0, rows)
    m = lax.fori_loop(0, n_chunks // unroll - 1, group, jnp.full((1, 2 * tq), -jnp.inf, F32))
    for c in range(n_chunks - unroll, n_chunks):
        m = step(c, c % SCORE_SLOTS, m, c + 1 if c + 1 < n_chunks else None)
    acc = acc_sc[...]
    return acc[:dv] / acc[dv:dv + 1]


def _gqa_kernel(qt_ref, k_ref, vt_ref, o_ref, acc_sc, s_sc, *, tq, n_chunks):
    o = _stacked_flash(qt_ref, k_ref, vt_ref, acc_sc, s_sc, tq=tq, n_chunks=n_chunks, v_heads=B_KV_HEADS)
    o_ref[...] = jnp.concatenate([o[:, :tq], o[:, tq:]], axis=0).T


def _diff_kernel(qt_ref, k_ref, vt_ref, lq1_ref, lk1_ref, lq2_ref, lk2_ref, sub_ref, o_ref, acc_sc,
                 s_sc, *, tq, n_chunks, lam_init):
    o = _stacked_flash(qt_ref, k_ref, vt_ref, acc_sc, s_sc, tq=tq, n_chunks=n_chunks, v_heads=1)
    lam = (jnp.exp(jnp.sum(lq1_ref[...] * lk1_ref[...], axis=-1, keepdims=True))
           - jnp.exp(jnp.sum(lq2_ref[...] * lk2_ref[...], axis=-1, keepdims=True)) + lam_init)
    oc = (o[:, :tq] - lam * o[:, tq:]).T
    ms = jnp.mean(oc * oc, axis=-1, keepdims=True)
    o_ref[...] = oc * lax.rsqrt(ms + EPS) * sub_ref[...] * (1.0 - lam_init)


def _flash_tq(s):
    return min(512, s)


def _flash_scratch(tq, v_heads):
    return [pltpu.VMEM((LANE // v_heads + ONES_ROWS, 2 * tq), F32),
            pltpu.VMEM((SCORE_SLOTS, FLASH_TK, 2 * tq), F32)]


def _gqa_attention(qt, k, vt):
    bn, _, s = qt.shape
    tq, n_chunks = _flash_tq(s), s // FLASH_TK
    kern = functools.partial(_gqa_kernel, tq=tq, n_chunks=n_chunks)
    return pl.pallas_call(
        kern,
        grid=(bn, B_HEADS // 2, s // tq),
        in_specs=[pl.BlockSpec((None, LANE, tq), lambda b, p, i: (b, p, i)),
                  pl.BlockSpec((None, s, LANE), lambda b, p, i: (b, 0, 0)),
                  pl.BlockSpec((None, n_chunks, _vt_rows(B_KV_HEADS), FLASH_TK), lambda b, p, i: (b, 0, 0, 0))],
        out_specs=pl.BlockSpec((None, tq, LANE), lambda b, p, i: (b, i, p)),
        out_shape=jax.ShapeDtypeStruct((bn, s, B_WIDTH), F32),
        scratch_shapes=_flash_scratch(tq, B_KV_HEADS),
        compiler_params=_cparams(3),
        name="gqa_attn",
    )(qt, k, vt)


def _diff_attention(qt, k, vt, lam_rows, subln, lam_init):
    bn, _, s = qt.shape
    tq, n_chunks = _flash_tq(s), s // FLASH_TK
    kern = functools.partial(_diff_kernel, tq=tq, n_chunks=n_chunks, lam_init=lam_init)
    row = lambda n: pl.BlockSpec((1, n), lambda b, h, i: (0, 0))
    return pl.pallas_call(
        kern,
        grid=(bn, C_HEADS, s // tq),
        in_specs=[pl.BlockSpec((None, LANE, tq), lambda b, h, i: (b, h, i)),
                  pl.BlockSpec((None, s, LANE), lambda b, h, i: (b, 0, h)),
                  pl.BlockSpec((None, n_chunks, _vt_rows(1), FLASH_TK), lambda b, h, i: (b, 0, h, 0)),
                  row(C_HEAD_DIM), row(C_HEAD_DIM), row(C_HEAD_DIM), row(C_HEAD_DIM), row(LANE)],
        out_specs=pl.BlockSpec((None, tq, LANE), lambda b, h, i: (b, i, h)),
        out_shape=jax.ShapeDtypeStruct((bn, s, C_WIDTH), F32),
        scratch_shapes=_flash_scratch(tq, 1),
        compiler_params=_cparams(3),
        name="diff_attn",
    )(qt, k, vt, *lam_rows, subln)


def _final_kernel(x_ref, shift_ref, scale_ref, gate_ref, g_ref, oa0_ref, oa1_ref, oa2_ref, yb_ref, yc_ref,
                  wz_ref, wbg_ref, bbg_ref, woa_ref, wob_ref, woc_ref, wout_ref, o_ref, *stage_refs):
    x = x_ref[...]
    tm = x.shape[0]
    h = _modulated_norm(x, g_ref[...], scale_ref[...], shift_ref[...]).astype(BF16)
    lane = lax.broadcasted_iota(jnp.int32, (1, LANE), 1)

    for oa_ref, st_ref, (_, dil) in zip((oa1_ref, oa2_ref), stage_refs, A_GROUPS[1:]):
        for r in range(dil):
            for j in range(A_HEADS):
                lo = r * A_SLAB + j * LANE
                st_ref[j, pl.ds(r, tm // dil, stride=dil), :] = oa_ref[:, lo:lo + LANE]

    ya = []
    for j in range(A_HEADS):
        slabs = [oa0_ref[:, j * LANE:(j + 1) * LANE]] + [st_ref[j] for st_ref in stage_refs]
        lses = [t[:, A_HEAD_DIM:A_HEAD_DIM + 1] for t in slabs]
        top = jnp.maximum(jnp.maximum(lses[0], lses[1]), lses[2])
        ws = [jnp.exp(t - top) for t in lses]
        mix = (ws[0] * slabs[0] + ws[1] * slabs[1] + ws[2] * slabs[2]) / (ws[0] + ws[1] + ws[2])
        ya.append(jnp.where(lane < A_HEAD_DIM, mix, 0.0))
    ya = jnp.concatenate(ya, axis=-1)

    def branch(y, z_lo, z_hi, wo_ref, g_lo):
        z = _dot(h, wz_ref[:, z_lo:z_hi])
        p = _dot((y * _silu(z)).astype(BF16), wo_ref[...])
        gate = jax.nn.sigmoid(_dot(h, wbg_ref[:, g_lo:g_lo + D_MODEL]) + bbg_ref[:, g_lo:g_lo + D_MODEL])
        return gate * p

    za_hi = A_SLAB
    zb_hi = za_hi + B_WIDTH
    zc_hi = zb_hi + C_WIDTH
    merged = (branch(ya, 0, za_hi, woa_ref, 0)
              + branch(yb_ref[...], za_hi, zb_hi, wob_ref, D_MODEL)
              + branch(yc_ref[...], zb_hi, zc_hi, woc_ref, 2 * D_MODEL))
    out = _dot(merged.astype(BF16), wout_ref[...])
    o_ref[...] = x + gate_ref[...] * out


def _final(x, shift, scale, gate, norm_g, oa, yb, yc, wz, wbg, bbg, woa, wob, woc, wout, *, tm):
    bn, s, _ = x.shape
    tok = lambda w: pl.BlockSpec((None, tm, w), lambda b, i: (b, i, 0))
    vec = pl.BlockSpec((None, 1, D_MODEL), lambda b, i: (b, 0, 0))
    full = lambda a: pl.BlockSpec(a.shape, lambda b, i: (0, 0))
    band = [pl.BlockSpec((None, tm // dil, dil * A_SLAB), lambda b, i: (b, i, 0)) for _, dil in A_GROUPS]
    return pl.pallas_call(
        _final_kernel,
        grid=(bn, s // tm),
        in_specs=[tok(D_MODEL), vec, vec, vec, full(norm_g), *band,
                  tok(B_WIDTH), tok(C_WIDTH), full(wz), full(wbg), full(bbg), full(woa), full(wob), full(woc),
                  full(wout)],
        out_specs=tok(D_MODEL),
        out_shape=jax.ShapeDtypeStruct((bn, s, D_MODEL), F32),
        scratch_shapes=[pltpu.VMEM((A_HEADS, tm, LANE), F32) for _ in A_GROUPS[1:]],
        compiler_params=_cparams(2),
        name="final",
    )(x, shift, scale, gate, norm_g, *oa, yb, yc, wz, wbg, bbg, woa, wob, woc, wout)


B_Q_ORDER = (0, 3, 1, 4, 2, 5)


def _slab_layout(head_dim, n_heads, rot_groups):
    src = -np.ones((LANE,), np.int32)
    group = -np.ones((LANE,), np.int32)
    freq = np.zeros((LANE,), np.int32)
    sign = np.zeros((LANE,), np.float32)
    width = ROT_SHIFT // n_heads
    for hd in range(n_heads):
        low = [(start + i, g, i, -1.0) for g, (start, half) in enumerate(rot_groups) for i in range(half)]
        high = [(start + half + i, g, i, 1.0) for g, (start, half) in enumerate(rot_groups) for i in range(half)]
        rotary = {d for d, _, _, _ in low + high}
        rest = [(d, -1, 0, 0.0) for d in range(head_dim) if d not in rotary]
        n_low_rest = width - len(low)
        low, high = low + rest[:n_low_rest], high + rest[n_low_rest:]
        assert len(low) == width and len(high) <= width
        for base, items in ((hd * width, low), (hd * width + ROT_SHIFT, high)):
            for lane, (d, g, i, sg) in enumerate(items, start=base):
                src[lane], group[lane], freq[lane], sign[lane] = hd * head_dim + d, g, i, sg
    return src, group, freq, sign


LAYOUT_A = _slab_layout(A_HEAD_DIM, 1, ((0, A_ROT // 2),))
LAYOUT_B = _slab_layout(B_HEAD_DIM, 2, ((0, B_HEAD_DIM // 4), (B_HEAD_DIM // 2, B_HEAD_DIM // 4)))
LAYOUT_C = _slab_layout(C_HEAD_DIM, 2, ((0, C_ROT // 2),))


def _to_slabs(w, layout, dims_per_slab):
    src = layout[0]
    n_slabs = w.shape[-1] // dims_per_slab
    idx = np.concatenate([np.maximum(src, 0) + s * dims_per_slab for s in range(n_slabs)])
    keep = np.tile(src >= 0, n_slabs)
    return jnp.where(keep, jnp.take(w, idx, axis=-1), 0.0)


def _pad_heads(w, n_heads):
    lead = w.shape[:-1]
    w = w.reshape(*lead, n_heads, A_HEAD_DIM)
    w = jnp.pad(w, [(0, 0)] * len(lead) + [(0, 0), (0, LANE - A_HEAD_DIM)])
    return w.reshape(*lead, n_heads * LANE)


def _permute_heads(w, order, dim):
    lead = w.shape[:-1]
    w = w.reshape(*lead, len(order), dim)
    return w[..., jnp.array(order), :].reshape(*lead, len(order) * dim)


def _prep_layer(w_in, qn_a, kn_a, qn_b, kn_b, qn_c, kn_c, w_oa, w_ob, w_oc):
    offs = [0]
    for n in IN_SIZES:
        offs.append(offs[-1] + n)
    qa, ka, va, za, qb, kb, vb, zb, qc, kc, vc, zc = [w_in[:, offs[i]:offs[i + 1]] for i in range(len(IN_SIZES))]

    slab_a = functools.partial(_to_slabs, layout=LAYOUT_A, dims_per_slab=A_HEAD_DIM)
    slab_b = functools.partial(_to_slabs, layout=LAYOUT_B, dims_per_slab=2 * B_HEAD_DIM)
    slab_c = functools.partial(_to_slabs, layout=LAYOUT_C, dims_per_slab=2 * C_HEAD_DIM)
    qb = _permute_heads(qb, B_Q_ORDER, B_HEAD_DIM)

    w_a = jnp.concatenate([slab_a(qa), slab_a(ka), _pad_heads(va, N_A)], axis=-1).astype(BF16)
    gain_a = jnp.concatenate([slab_a(jnp.tile(qn_a, N_A)), slab_a(jnp.tile(kn_a, N_A)),
                              jnp.zeros((N_A * LANE,), F32)])[None, :]

    w_b = jnp.concatenate([slab_b(qb), slab_b(kb), vb], axis=-1).astype(BF16)
    gain_b = jnp.concatenate([slab_b(jnp.tile(qn_b, B_HEADS)), slab_b(jnp.tile(kn_b, B_KV_HEADS)),
                              jnp.zeros((B_KV,), F32)])[None, :]

    w_c = jnp.concatenate([slab_c(qc), slab_c(kc), vc], axis=-1).astype(BF16)
    gain_c = jnp.concatenate([slab_c(jnp.tile(qn_c, 2 * C_HEADS)), slab_c(jnp.tile(kn_c, 2 * C_HEADS)),
                              jnp.zeros((C_WIDTH,), F32)])[None, :]

    w_z = jnp.concatenate([_pad_heads(za, A_HEADS), _permute_heads(zb, B_Q_ORDER, B_HEAD_DIM), zc],
                          axis=-1).astype(BF16)
    w_oa_p = _pad_heads(w_oa.T, A_HEADS).T.astype(BF16)
    w_ob_p = _permute_heads(w_ob.T, B_Q_ORDER, B_HEAD_DIM).T.astype(BF16)
    return dict(w_a=w_a, gain_a=gain_a, w_b=w_b, gain_b=gain_b, w_c=w_c, gain_c=gain_c, w_z=w_z,
                w_oa=w_oa_p, w_ob=w_ob_p, w_oc=w_oc.astype(BF16))


def _inv_freq(dim, theta):
    return theta ** (-jnp.arange(0, dim, 2, dtype=F32) / dim)


def _rope_tables(layout, inv_freqs, streams):
    _, group, freq, sign = layout
    ang = 0.0
    for g, (inv, pos) in enumerate(zip(inv_freqs, streams)):
        ang = ang + pos[:, None] * (inv[freq] * (group == g))[None, :]
    return jnp.cos(ang), jnp.sin(ang) * sign[None, :]


def _tables(s):
    pos = jnp.arange(s)
    tok, row, col = [t.astype(F32) for t in (pos, pos // GRID_W, pos % GRID_W)]
    inv_b = _inv_freq(B_HEAD_DIM // 2, AXIAL_THETA)
    tab_a = _rope_tables(LAYOUT_A, [_inv_freq(A_ROT, ROPE_THETA)], [tok])
    tab_b = _rope_tables(LAYOUT_B, [inv_b, inv_b], [row, col])
    tab_c = _rope_tables(LAYOUT_C, [_inv_freq(C_ROT, ROPE_THETA)], [tok])
    return tab_a, tab_b, tab_c


def _layer(x, mod, l, lp, tabs, norm_g, lam_rows, subln, w_bg, b_bg, w_out):
    bn, s, _ = x.shape
    shift, scale, gate = [m.reshape(bn, 1, D_MODEL) for m in jnp.split(mod, 3, axis=-1)]
    tab_a, tab_b, tab_c = tabs
    lam_init = 0.8 - 0.6 * math.exp(-0.3 * l)

    a_outs = tuple((A_SLAB, kind, dil) for kind in ("q", "k", "v") for _, dil in A_GROUPS)
    pa = _project(x, shift, scale, norm_g, lp["w_a"], lp["gain_a"], tab_a, a_outs, two_heads=False,
                  head_dim=A_HEAD_DIM, q_scale=A_HEAD_DIM ** -0.5 * LOG2E, tm=min(PROJ_TM, s), name="proj_a")
    n_g = len(A_GROUPS)
    oa = [_band_attention(pa[g], pa[n_g + g], pa[2 * n_g + g], dil) for g, (_, dil) in enumerate(A_GROUPS)]

    qb, kb, vb = _project(x, shift, scale, norm_g, lp["w_b"], lp["gain_b"], tab_b,
                          ((B_WIDTH, "qT", 1), (B_KV, "k", 1), (B_KV, "vT", B_KV_HEADS)), two_heads=True,
                          head_dim=B_HEAD_DIM, q_scale=B_HEAD_DIM ** -0.5 * LOG2E, tm=min(PROJ_TM, s), name="proj_b")
    yb = _gqa_attention(qb, kb, vb)

    qc, kc, vc = _project(x, shift, scale, norm_g, lp["w_c"], lp["gain_c"], tab_c,
                          ((C_WIDTH, "qT", 1), (C_WIDTH, "k", 1), (C_WIDTH, "vT", 1)), two_heads=True,
                          head_dim=C_HEAD_DIM, q_scale=C_HEAD_DIM ** -0.5 * LOG2E, tm=min(PROJ_TM, s), name="proj_c")
    yc = _diff_attention(qc, kc, vc, lam_rows, subln, lam_init)

    return _final(x, shift, scale, gate, norm_g, oa, yb, yc, lp["w_z"], w_bg, b_bg, lp["w_oa"], lp["w_ob"],
                  lp["w_oc"], w_out, tm=min(FINAL_TM, s))


def kernel(x_prompt, x_sample, c_prompt, c_sample, norm_g, w_ada, b_ada, w_in, qn_a, kn_a, qn_b, kn_b, qn_c, kn_c,
           lam_q1, lam_k1, lam_q2, lam_k2, subln_c, w_oa, w_ob, w_oc, w_bg, b_bg, w_out):
    groups = ((x_prompt, c_prompt), (x_sample, c_sample))
    rows = [c.shape[0] for _, c in groups]
    pad = -sum(rows) % 8
    c_all = jnp.concatenate([c for _, c in groups] + [jnp.zeros((pad, D_MODEL), F32)], axis=0)
    mod_all = _ada_mod(c_all, w_ada.astype(BF16), b_ada)

    layers = [_prep_layer(w_in[l], qn_a[l], kn_a[l], qn_b[l], kn_b[l], qn_c[l], kn_c[l], w_oa[l], w_ob[l], w_oc[l])
              for l in range(DEPTH)]
    w_bg16, w_out16 = w_bg.astype(BF16), w_out.astype(BF16)

    outs = []
    row0 = 0
    for (x, c), n in zip(groups, rows):
        tabs = _tables(x.shape[1])
        for l in range(DEPTH):
            lam_rows = [p[l][None, :] for p in (lam_q1, lam_k1, lam_q2, lam_k2)]
            x = _layer(x, mod_all[l, row0:row0 + n], l, layers[l], tabs, norm_g[l][None, :], lam_rows,
                       subln_c[l][None, :], w_bg16[l], b_bg[l][None, :], w_out16[l])
        outs.append(x)
        row0 += n
    return tuple(outs)
```

```python
import functools
import math

import jax
import jax.numpy as jnp
import numpy as np
from jax import lax
from jax.experimental import pallas as pl
from jax.experimental.pallas import tpu as pltpu

D_MODEL = 1024
DEPTH = 2
GRID_W = 64
EPS = 1e-6
NEG = -1e30
ROPE_THETA = 500000.0
AXIAL_THETA = 10000.0

A_GROUPS = ((128, 1), (512, 4), (2048, 16))
A_HEADS = 4
A_HEAD_DIM = 96
A_ROT = A_HEAD_DIM // 4
A_BAND = 64
N_A = len(A_GROUPS) * A_HEADS

B_HEADS = 6
B_KV_HEADS = 2
B_HEAD_DIM = 64
C_HEADS = 4
C_HEAD_DIM = 64
C_ROT = C_HEAD_DIM // 4

A_QKV = N_A * A_HEAD_DIM
A_WIDTH = A_HEADS * A_HEAD_DIM
B_WIDTH = B_HEADS * B_HEAD_DIM
B_KV = B_KV_HEADS * B_HEAD_DIM
C_WIDTH = C_HEADS * 2 * C_HEAD_DIM
IN_SIZES = (A_QKV, A_QKV, A_QKV, A_WIDTH, B_WIDTH, B_KV, B_KV, B_WIDTH, C_WIDTH, C_WIDTH, C_WIDTH, C_WIDTH)

LANE = 128
HALF = LANE // 2
ROT_SHIFT = LANE // 2
A_SLAB = A_HEADS * LANE
BF16_SUBLANES = 16
ONES_ROWS = BF16_SUBLANES
FLASH_TK = 1024
KEY_PART = 1024
SCORE_SLOTS = 2
FLASH_UNROLL = 2
PROJ_TM = 256
FINAL_TM = 256
LOG2E = math.log2(math.e)
LN2 = math.log(2.0)
VMEM_LIMIT = 48 * 1024 * 1024

BF16 = jnp.bfloat16
F32 = jnp.float32


def _cparams(n_axes):
    return pltpu.CompilerParams(dimension_semantics=("arbitrary",) * n_axes, vmem_limit_bytes=VMEM_LIMIT)


def _dot(a, b):
    return jnp.dot(a, b, preferred_element_type=F32)


def _dot_nt(a, b):
    return lax.dot_general(a, b, (((1,), (1,)), ((), ())), preferred_element_type=F32)


def _silu(x):
    return x * jax.nn.sigmoid(x)


def _ada_kernel(c_ref, w_ref, b_ref, o_ref):
    c = c_ref[...]
    o_ref[...] = _dot(_silu(c).astype(BF16), w_ref[...]) + b_ref[...]


def _ada_mod(c_all, w_ada, b_ada):
    rows = c_all.shape[0]
    return pl.pallas_call(
        _ada_kernel,
        grid=(DEPTH, 3),
        in_specs=[
            pl.BlockSpec((rows, D_MODEL), lambda l, n: (0, 0)),
            pl.BlockSpec((None, D_MODEL, D_MODEL), lambda l, n: (l, 0, n)),
            pl.BlockSpec((None, 1, D_MODEL), lambda l, n: (l, 0, n)),
        ],
        out_specs=pl.BlockSpec((None, rows, D_MODEL), lambda l, n: (l, 0, n)),
        out_shape=jax.ShapeDtypeStruct((DEPTH, rows, 3 * D_MODEL), F32),
        compiler_params=_cparams(2),
        name="ada_mod",
    )(c_all, w_ada, b_ada.reshape(DEPTH, 1, 3 * D_MODEL))


def _modulated_norm(x, g, scale, shift):
    ms = jnp.mean(x * x, axis=-1, keepdims=True)
    h = x * lax.rsqrt(ms + EPS) * g
    return h * (1.0 + scale) + shift


def _first_head(index):
    return (index & (HALF - 1)) < HALF // 2


def _head_norm_rope(u, gain, cos, sin, two_heads, head_dim):
    sq = u * u
    if two_heads:
        first = _first_head(lax.broadcasted_iota(jnp.int32, (1, LANE), 1))
        s_first = jnp.sum(jnp.where(first, sq, 0.0), axis=-1, keepdims=True)
        s_all = jnp.sum(sq, axis=-1, keepdims=True)
        ms = jnp.where(first, s_first, s_all - s_first) * (1.0 / head_dim)
    else:
        ms = jnp.sum(sq, axis=-1, keepdims=True) * (1.0 / head_dim)
    y = u * lax.rsqrt(ms + EPS) * gain
    return y * cos + pltpu.roll(y, ROT_SHIFT, 1) * sin


def _proj_kernel(x_ref, shift_ref, scale_ref, g_ref, w_ref, gain_ref, cos_ref, sin_ref, *refs,
                 outs, two_heads, head_dim, q_scale):
    out_refs, stage_ref = refs[:len(outs)], refs[len(outs):]
    h = _modulated_norm(x_ref[...], g_ref[...], scale_ref[...], shift_ref[...]).astype(BF16)
    cos, sin = cos_ref[...], sin_ref[...]
    tm = x_ref.shape[0]
    col = 0
    for o_ref, (width, kind, dil) in zip(out_refs, outs):
        u = _dot(h, w_ref[:, col:col + width])
        if kind == "vT":
            dv = LANE // dil
            for s in range(width // LANE):
                ut = u[:, s * LANE:(s + 1) * LANE].T.astype(BF16)
                for hd in range(dil):
                    base = (s * dil + hd) * (dv + ONES_ROWS)
                    o_ref[base:base + dv, :] = ut[hd * dv:(hd + 1) * dv]
                    o_ref[base + dv:base + dv + ONES_ROWS, :] = jnp.ones((ONES_ROWS, tm), BF16)
        else:
            if kind != "v":
                slabs = []
                for s in range(width // LANE):
                    lo = s * LANE
                    y = _head_norm_rope(u[:, lo:lo + LANE], gain_ref[:, col + lo:col + lo + LANE], cos, sin,
                                        two_heads, head_dim)
                    slabs.append(y * q_scale if kind in ("q", "qT") and q_scale != 1.0 else y)
                u = jnp.concatenate(slabs, axis=-1)
            if kind == "qT":
                for s in range(width // LANE):
                    o_ref[s * LANE:(s + 1) * LANE, :] = u[:, s * LANE:(s + 1) * LANE].T.astype(BF16)
            elif dil == 1:
                o_ref[...] = u.astype(BF16)
            else:
                stage = stage_ref[0]
                for s in range(width // LANE):
                    stage[s] = u[:, s * LANE:(s + 1) * LANE]
                rows = lambda r: pl.ds(r, tm // dil, stride=dil)
                for r in range(dil):
                    for s in range(width // LANE):
                        lo = r * width + s * LANE
                        o_ref[:, lo:lo + LANE] = stage[s, rows(r), :].astype(BF16)
        col += width


def _vt_rows(v_heads):
    return LANE + v_heads * ONES_ROWS


def _proj_out(kind, width, dil, bn, s, tm):
    if kind == "qT":
        return (bn, width, s), pl.BlockSpec((None, width, tm), lambda b, i: (b, 0, i))
    if kind == "vT":
        per = FLASH_TK // tm
        rows = width // LANE * _vt_rows(dil)
        return ((bn, s // FLASH_TK, rows, FLASH_TK),
                pl.BlockSpec((None, None, rows, tm), lambda b, i: (b, i // per, 0, i % per)))
    return (bn, s // dil, dil * width), pl.BlockSpec((None, tm // dil, dil * width), lambda b, i: (b, i, 0))


def _project(x, shift, scale, norm_g, w, gain, tables, outs, *, two_heads, head_dim, q_scale, tm, name):
    bn, s, _ = x.shape
    wcols = w.shape[1]
    kern = functools.partial(_proj_kernel, outs=outs, two_heads=two_heads, head_dim=head_dim, q_scale=q_scale)
    vec = pl.BlockSpec((None, 1, D_MODEL), lambda b, i: (b, 0, 0))
    tab = pl.BlockSpec((tm, LANE), lambda b, i: (i, 0))
    shapes, specs = zip(*[_proj_out(kind, width, dil, bn, s, tm) for width, kind, dil in outs])
    n_slabs = max(width for width, _, _ in outs) // LANE
    dilated = any(d > 1 and kind != "vT" for _, kind, d in outs)
    stage = [pltpu.VMEM((n_slabs, tm, LANE), F32)] if dilated else []
    return pl.pallas_call(
        kern,
        grid=(bn, s // tm),
        in_specs=[
            pl.BlockSpec((None, tm, D_MODEL), lambda b, i: (b, i, 0)),
            vec, vec,
            pl.BlockSpec((1, D_MODEL), lambda b, i: (0, 0)),
            pl.BlockSpec((D_MODEL, wcols), lambda b, i: (0, 0)),
            pl.BlockSpec((1, wcols), lambda b, i: (0, 0)),
            tab, tab,
        ],
        out_specs=list(specs),
        out_shape=[jax.ShapeDtypeStruct(shape, BF16) for shape in shapes],
        scratch_shapes=stage,
        compiler_params=_cparams(2),
        name=name,
    )(x, shift, scale, norm_g, w, gain, *tables)


def _band_attn_kernel(q_ref, kp_ref, kc_ref, kn_ref, vp_ref, vc_ref, vn_ref, bias_ref, o_ref, *, tb, n_blocks):
    i = pl.program_id(2)
    nk = tb + 2 * A_BAND
    colk = lax.broadcasted_iota(jnp.int32, (1, nk), 1)
    first_col = jnp.where(i == 0, A_BAND, 0)
    end_col = jnp.where(i == n_blocks - 1, tb + A_BAND, nk)
    bias = bias_ref[...] + jnp.where((colk < first_col) | (colk >= end_col), NEG, 0.0)
    lane = lax.broadcasted_iota(jnp.int32, (1, LANE), 1)
    heads = [slice(j * LANE, (j + 1) * LANE) for j in range(A_HEADS)]
    band = lambda p_ref, c_ref, n_ref, sl: jnp.concatenate([p_ref[:, sl], c_ref[:, sl], n_ref[:, sl]], axis=0)
    scores = [_dot_nt(q_ref[:, sl], band(kp_ref, kc_ref, kn_ref, sl)) for sl in heads]
    stats = []
    for s in scores:
        s = s + bias
        m = jnp.max(s, axis=-1, keepdims=True)
        p = jnp.exp2(s - m)
        stats.append((m, jnp.sum(p, axis=-1, keepdims=True), p.astype(BF16)))
    for sl, (m, l, p) in zip(heads, stats):
        o = _dot(p, band(vp_ref, vc_ref, vn_ref, sl)) / l
        o_ref[:, sl] = jnp.where(lane >= A_HEAD_DIM, m * LN2 + jnp.log(l), o)


def _band_attention(q, k, v, dilation):
    bn, length, _ = q.shape
    tb = min(256, length)
    nb = length // tb
    per = tb // A_BAND
    cur = pl.BlockSpec((None, tb, A_SLAB), lambda b, r, i: (b, i, r))
    prv = pl.BlockSpec((None, A_BAND, A_SLAB), lambda b, r, i: (b, jnp.maximum(i * per - 1, 0), r))
    nxt = pl.BlockSpec((None, A_BAND, A_SLAB), lambda b, r, i: (b, jnp.minimum((i + 1) * per, nb * per - 1), r))
    nk = tb + 2 * A_BAND
    dist = np.arange(nk)[None, :] - A_BAND - np.arange(tb)[:, None]
    bias = np.where(np.abs(dist) <= A_BAND, 0.0, NEG).astype(np.float32)
    kern = functools.partial(_band_attn_kernel, tb=tb, n_blocks=nb)
    return pl.pallas_call(
        kern,
        grid=(bn, dilation, nb),
        in_specs=[cur, prv, cur, nxt, prv, cur, nxt, pl.BlockSpec((tb, nk), lambda b, r, i: (0, 0))],
        out_specs=cur,
        out_shape=jax.ShapeDtypeStruct((bn, length, dilation * A_SLAB), F32),
        compiler_params=_cparams(3),
        name=f"band_attn_d{dilation}",
    )(q, k, k, k, v, v, v, bias)


def _stacked_flash(qt_ref, k_ref, vt_ref, acc_sc, s_sc, *, tq, n_chunks, v_heads):
    dv = LANE // v_heads
    v_rows = [slice(hd * (dv + ONES_ROWS), (hd + 1) * (dv + ONES_ROWS)) for hd in range(v_heads)]
    col_groups = [slice(0, 2 * tq)] if v_heads == 1 else [slice(0, tq), slice(tq, 2 * tq)]
    first = _first_head(lax.broadcasted_iota(jnp.int32, (LANE, 1), 0))
    qt = qt_ref[...]
    zero = jnp.zeros_like(qt)
    q2t = jnp.concatenate([jnp.where(first, qt, zero), jnp.where(first, zero, qt)], axis=1)
    acc_sc[...] = jnp.zeros(acc_sc.shape, F32)

    parts = [slice(lo, lo + KEY_PART) for lo in range(0, FLASH_TK, KEY_PART)]

    def scores(c, slot, rows):
        start = pl.multiple_of(c * FLASH_TK + rows.start, KEY_PART)
        s_sc[slot, rows, :] = _dot(k_ref[pl.ds(start, KEY_PART), :], q2t)

    def step(c, slot, m_prev, c_next=None):
        m_next = jnp.maximum(m_prev, jnp.max(s_sc[slot], axis=0, keepdims=True))
        vc = vt_ref[c]
        alpha = jnp.exp2(m_prev - m_next)
        pv = [None] * v_heads
        for rows in parts:
            if c_next is not None:
                scores(c_next, 1 - slot, rows)
            pt = jnp.exp2(s_sc[slot, rows, :] - m_next).astype(BF16)
            for g, (vr, cols) in enumerate(zip(v_rows, col_groups)):
                part = _dot(vc[vr, rows], pt[:, cols])
                pv[g] = part if pv[g] is None else pv[g] + part
        for cols, total in zip(col_groups, pv):
            acc_sc[:, cols] = alpha[:, cols] * acc_sc[:, cols] + total
        return m_next

    unroll = min(FLASH_UNROLL, n_chunks)

    def group(j, m):
        for u in range(unroll):
            c = unroll * j + u
            m = step(c, u % SCORE_SLOTS, m, c + 1)
        return m

    for rows in parts:
        scores(0, 0, rows)
    m = lax.fori_loop(0, n_chunks // unroll - 1, group, jnp.full((1, 2 * tq), -jnp.inf, F32))
    for c in range(n_chunks - unroll, n_chunks):
        m = step(c, c % SCORE_SLOTS, m, c + 1 if c + 1 < n_chunks else None)
    acc = acc_sc[...]
    return acc[:dv] / acc[dv:dv + 1]


def _gqa_kernel(qt_ref, k_ref, vt_ref, o_ref, acc_sc, s_sc, *, tq, n_chunks):
    o = _stacked_flash(qt_ref, k_ref, vt_ref, acc_sc, s_sc, tq=tq, n_chunks=n_chunks, v_heads=B_KV_HEADS)
    o_ref[...] = jnp.concatenate([o[:, :tq], o[:, tq:]], axis=0).T


def _diff_kernel(qt_ref, k_ref, vt_ref, lq1_ref, lk1_ref, lq2_ref, lk2_ref, sub_ref, o_ref, acc_sc,
                 s_sc, *, tq, n_chunks, lam_init):
    o = _stacked_flash(qt_ref, k_ref, vt_ref, acc_sc, s_sc, tq=tq, n_chunks=n_chunks, v_heads=1)
    lam = (jnp.exp(jnp.sum(lq1_ref[...] * lk1_ref[...], axis=-1, keepdims=True))
           - jnp.exp(jnp.sum(lq2_ref[...] * lk2_ref[...], axis=-1, keepdims=True)) + lam_init)
    oc = (o[:, :tq] - lam * o[:, tq:]).T
    ms = jnp.mean(oc * oc, axis=-1, keepdims=True)
    o_ref[...] = oc * lax.rsqrt(ms + EPS) * sub_ref[...] * (1.0 - lam_init)


def _flash_tq(s):
    return min(512, s)


def _flash_scratch(tq, v_heads):
    return [pltpu.VMEM((LANE // v_heads + ONES_ROWS, 2 * tq), F32),
            pltpu.VMEM((SCORE_SLOTS, FLASH_TK, 2 * tq), F32)]


def _gqa_attention(qt, k, vt):
    bn, _, s = qt.shape
    tq, n_chunks = _flash_tq(s), s // FLASH_TK
    kern = functools.partial(_gqa_kernel, tq=tq, n_chunks=n_chunks)
    return pl.pallas_call(
        kern,
        grid=(bn, B_HEADS // 2, s // tq),
        in_specs=[pl.BlockSpec((None, LANE, tq), lambda b, p, i: (b, p, i)),
                  pl.BlockSpec((None, s, LANE), lambda b, p, i: (b, 0, 0)),
                  pl.BlockSpec((None, n_chunks, _vt_rows(B_KV_HEADS), FLASH_TK), lambda b, p, i: (b, 0, 0, 0))],
        out_specs=pl.BlockSpec((None, tq, LANE), lambda b, p, i: (b, i, p)),
        out_shape=jax.ShapeDtypeStruct((bn, s, B_WIDTH), F32),
        scratch_shapes=_flash_scratch(tq, B_KV_HEADS),
        compiler_params=_cparams(3),
        name="gqa_attn",
    )(qt, k, vt)


def _diff_attention(qt, k, vt, lam_rows, subln, lam_init):
    bn, _, s = qt.shape
    tq, n_chunks = _flash_tq(s), s // FLASH_TK
    kern = functools.partial(_diff_kernel, tq=tq, n_chunks=n_chunks, lam_init=lam_init)
    row = lambda n: pl.BlockSpec((1, n), lambda b, h, i: (0, 0))
    return pl.pallas_call(
        kern,
        grid=(bn, C_HEADS, s // tq),
        in_specs=[pl.BlockSpec((None, LANE, tq), lambda b, h, i: (b, h, i)),
                  pl.BlockSpec((None, s, LANE), lambda b, h, i: (b, 0, h)),
                  pl.BlockSpec((None, n_chunks, _vt_rows(1), FLASH_TK), lambda b, h, i: (b, 0, h, 0)),
                  row(C_HEAD_DIM), row(C_HEAD_DIM), row(C_HEAD_DIM), row(C_HEAD_DIM), row(LANE)],
        out_specs=pl.BlockSpec((None, tq, LANE), lambda b, h, i: (b, i, h)),
        out_shape=jax.ShapeDtypeStruct((bn, s, C_WIDTH), F32),
        scratch_shapes=_flash_scratch(tq, 1),
        compiler_params=_cparams(3),
        name="diff_attn",
    )(qt, k, vt, *lam_rows, subln)


def _final_kernel(x_ref, shift_ref, scale_ref, gate_ref, g_ref, oa0_ref, oa1_ref, oa2_ref, yb_ref, yc_ref,
                  wz_ref, wbg_ref, bbg_ref, woa_ref, wob_ref, woc_ref, wout_ref, o_ref, *stage_refs):
    x = x_ref[...]
    tm = x.shape[0]
    h = _modulated_norm(x, g_ref[...], scale_ref[...], shift_ref[...]).astype(BF16)
    lane = lax.broadcasted_iota(jnp.int32, (1, LANE), 1)

    for oa_ref, st_ref, (_, dil) in zip((oa1_ref, oa2_ref), stage_refs, A_GROUPS[1:]):
        for r in range(dil):
            for j in range(A_HEADS):
                lo = r * A_SLAB + j * LANE
                st_ref[j, pl.ds(r, tm // dil, stride=dil), :] = oa_ref[:, lo:lo + LANE]

    ya = []
    for j in range(A_HEADS):
        slabs = [oa0_ref[:, j * LANE:(j + 1) * LANE]] + [st_ref[j] for st_ref in stage_refs]
        lses = [t[:, A_HEAD_DIM:A_HEAD_DIM + 1] for t in slabs]
        top = jnp.maximum(jnp.maximum(lses[0], lses[1]), lses[2])
        ws = [jnp.exp(t - top) for t in lses]
        mix = (ws[0] * slabs[0] + ws[1] * slabs[1] + ws[2] * slabs[2]) / (ws[0] + ws[1] + ws[2])
        ya.append(jnp.where(lane < A_HEAD_DIM, mix, 0.0))
    ya = jnp.concatenate(ya, axis=-1)

    def branch(y, z_lo, z_hi, wo_ref, g_lo):
        z = _dot(h, wz_ref[:, z_lo:z_hi])
        p = _dot((y * _silu(z)).astype(BF16), wo_ref[...])
        gate = jax.nn.sigmoid(_dot(h, wbg_ref[:, g_lo:g_lo + D_MODEL]) + bbg_ref[:, g_lo:g_lo + D_MODEL])
        return gate * p

    za_hi = A_SLAB
    zb_hi = za_hi + B_WIDTH
    zc_hi = zb_hi + C_WIDTH
    merged = (branch(ya, 0, za_hi, woa_ref, 0)
              + branch(yb_ref[...], za_hi, zb_hi, wob_ref, D_MODEL)
              + branch(yc_ref[...], zb_hi, zc_hi, woc_ref, 2 * D_MODEL))
    out = _dot(merged.astype(BF16), wout_ref[...])
    o_ref[...] = x + gate_ref[...] * out


def _final(x, shift, scale, gate, norm_g, oa, yb, yc, wz, wbg, bbg, woa, wob, woc, wout, *, tm):
    bn, s, _ = x.shape
    tok = lambda w: pl.BlockSpec((None, tm, w), lambda b, i: (b, i, 0))
    vec = pl.BlockSpec((None, 1, D_MODEL), lambda b, i: (b, 0, 0))
    full = lambda a: pl.BlockSpec(a.shape, lambda b, i: (0, 0))
    band = [pl.BlockSpec((None, tm // dil, dil * A_SLAB), lambda b, i: (b, i, 0)) for _, dil in A_GROUPS]
    return pl.pallas_call(
        _final_kernel,
        grid=(bn, s // tm),
        in_specs=[tok(D_MODEL), vec, vec, vec, full(norm_g), *band,
                  tok(B_WIDTH), tok(C_WIDTH), full(wz), full(wbg), full(bbg), full(woa), full(wob), full(woc),
                  full(wout)],
        out_specs=tok(D_MODEL),
        out_shape=jax.ShapeDtypeStruct((bn, s, D_MODEL), F32),
        scratch_shapes=[pltpu.VMEM((A_HEADS, tm, LANE), F32) for _ in A_GROUPS[1:]],
        compiler_params=_cparams(2),
        name="final",
    )(x, shift, scale, gate, norm_g, *oa, yb, yc, wz, wbg, bbg, woa, wob, woc, wout)


B_Q_ORDER = (0, 3, 1, 4, 2, 5)


def _slab_layout(head_dim, n_heads, rot_groups):
    src = -np.ones((LANE,), np.int32)
    group = -np.ones((LANE,), np.int32)
    freq = np.zeros((LANE,), np.int32)
    sign = np.zeros((LANE,), np.float32)
    width = ROT_SHIFT // n_heads
    for hd in range(n_heads):
        low = [(start + i, g, i, -1.0) for g, (start, half) in enumerate(rot_groups) for i in range(half)]
        high = [(start + half + i, g, i, 1.0) for g, (start, half) in enumerate(rot_groups) for i in range(half)]
        rotary = {d for d, _, _, _ in low + high}
        rest = [(d, -1, 0, 0.0) for d in range(head_dim) if d not in rotary]
        n_low_rest = width - len(low)
        low, high = low + rest[:n_low_rest], high + rest[n_low_rest:]
        assert len(low) == width and len(high) <= width
        for base, items in ((hd * width, low), (hd * width + ROT_SHIFT, high)):
            for lane, (d, g, i, sg) in enumerate(items, start=base):
                src[lane], group[lane], freq[lane], sign[lane] = hd * head_dim + d, g, i, sg
    return src, group, freq, sign


LAYOUT_A = _slab_layout(A_HEAD_DIM, 1, ((0, A_ROT // 2),))
LAYOUT_B = _slab_layout(B_HEAD_DIM, 2, ((0, B_HEAD_DIM // 4), (B_HEAD_DIM // 2, B_HEAD_DIM // 4)))
LAYOUT_C = _slab_layout(C_HEAD_DIM, 2, ((0, C_ROT // 2),))


def _to_slabs(w, layout, dims_per_slab):
    src = layout[0]
    n_slabs = w.shape[-1] // dims_per_slab
    idx = np.concatenate([np.maximum(src, 0) + s * dims_per_slab for s in range(n_slabs)])
    keep = np.tile(src >= 0, n_slabs)
    return jnp.where(keep, jnp.take(w, idx, axis=-1), 0.0)


def _pad_heads(w, n_heads):
    lead = w.shape[:-1]
    w = w.reshape(*lead, n_heads, A_HEAD_DIM)
    w = jnp.pad(w, [(0, 0)] * len(lead) + [(0, 0), (0, LANE - A_HEAD_DIM)])
    return w.reshape(*lead, n_heads * LANE)


def _permute_heads(w, order, dim):
    lead = w.shape[:-1]
    w = w.reshape(*lead, len(order), dim)
    return w[..., jnp.array(order), :].reshape(*lead, len(order) * dim)


def _prep_layer(w_in, qn_a, kn_a, qn_b, kn_b, qn_c, kn_c, w_oa, w_ob, w_oc):
    offs = [0]
    for n in IN_SIZES:
        offs.append(offs[-1] + n)
    qa, ka, va, za, qb, kb, vb, zb, qc, kc, vc, zc = [w_in[:, offs[i]:offs[i + 1]] for i in range(len(IN_SIZES))]

    slab_a = functools.partial(_to_slabs, layout=LAYOUT_A, dims_per_slab=A_HEAD_DIM)
    slab_b = functools.partial(_to_slabs, layout=LAYOUT_B, dims_per_slab=2 * B_HEAD_DIM)
    slab_c = functools.partial(_to_slabs, layout=LAYOUT_C, dims_per_slab=2 * C_HEAD_DIM)
    qb = _permute_heads(qb, B_Q_ORDER, B_HEAD_DIM)

    w_a = jnp.concatenate([slab_a(qa), slab_a(ka), _pad_heads(va, N_A)], axis=-1).astype(BF16)
    gain_a = jnp.concatenate([slab_a(jnp.tile(qn_a, N_A)), slab_a(jnp.tile(kn_a, N_A)),
                              jnp.zeros((N_A * LANE,), F32)])[None, :]

    w_b = jnp.concatenate([slab_b(qb), slab_b(kb), vb], axis=-1).astype(BF16)
    gain_b = jnp.concatenate([slab_b(jnp.tile(qn_b, B_HEADS)), slab_b(jnp.tile(kn_b, B_KV_HEADS)),
                              jnp.zeros((B_KV,), F32)])[None, :]

    w_c = jnp.concatenate([slab_c(qc), slab_c(kc), vc], axis=-1).astype(BF16)
    gain_c = jnp.concatenate([slab_c(jnp.tile(qn_c, 2 * C_HEADS)), slab_c(jnp.tile(kn_c, 2 * C_HEADS)),
                              jnp.zeros((C_WIDTH,), F32)])[None, :]

    w_z = jnp.concatenate([_pad_heads(za, A_HEADS), _permute_heads(zb, B_Q_ORDER, B_HEAD_DIM), zc],
                          axis=-1).astype(BF16)
    w_oa_p = _pad_heads(w_oa.T, A_HEADS).T.astype(BF16)
    w_ob_p = _permute_heads(w_ob.T, B_Q_ORDER, B_HEAD_DIM).T.astype(BF16)
    return dict(w_a=w_a, gain_a=gain_a, w_b=w_b, gain_b=gain_b, w_c=w_c, gain_c=gain_c, w_z=w_z,
                w_oa=w_oa_p, w_ob=w_ob_p, w_oc=w_oc.astype(BF16))


def _inv_freq(dim, theta):
    return theta ** (-jnp.arange(0, dim, 2, dtype=F32) / dim)


def _rope_tables(layout, inv_freqs, streams):
    _, group, freq, sign = layout
    ang = 0.0
    for g, (inv, pos) in enumerate(zip(inv_freqs, streams)):
        ang = ang + pos[:, None] * (inv[freq] * (group == g))[None, :]
    return jnp.cos(ang), jnp.sin(ang) * sign[None, :]


def _tables(s):
    pos = jnp.arange(s)
    tok, row, col = [t.astype(F32) for t in (pos, pos // GRID_W, pos % GRID_W)]
    inv_b = _inv_freq(B_HEAD_DIM // 2, AXIAL_THETA)
    tab_a = _rope_tables(LAYOUT_A, [_inv_freq(A_ROT, ROPE_THETA)], [tok])
    tab_b = _rope_tables(LAYOUT_B, [inv_b, inv_b], [row, col])
    tab_c = _rope_tables(LAYOUT_C, [_inv_freq(C_ROT, ROPE_THETA)], [tok])
    return tab_a, tab_b, tab_c


def _layer(x, mod, l, lp, tabs, norm_g, lam_rows, subln, w_bg, b_bg, w_out):
    bn, s, _ = x.shape
    shift, scale, gate = [m.reshape(bn, 1, D_MODEL) for m in jnp.split(mod, 3, axis=-1)]
    tab_a, tab_b, tab_c = tabs
    lam_init = 0.8 - 0.6 * math.exp(-0.3 * l)

    a_outs = tuple((A_SLAB, kind, dil) for kind in ("q", "k", "v") for _, dil in A_GROUPS)
    pa = _project(x, shift, scale, norm_g, lp["w_a"], lp["gain_a"], tab_a, a_outs, two_heads=False,
                  head_dim=A_HEAD_DIM, q_scale=A_HEAD_DIM ** -0.5 * LOG2E, tm=min(PROJ_TM, s), name="proj_a")
    n_g = len(A_GROUPS)
    oa = [_band_attention(pa[g], pa[n_g + g], pa[2 * n_g + g], dil) for g, (_, dil) in enumerate(A_GROUPS)]

    qb, kb, vb = _project(x, shift, scale, norm_g, lp["w_b"], lp["gain_b"], tab_b,
                          ((B_WIDTH, "qT", 1), (B_KV, "k", 1), (B_KV, "vT", B_KV_HEADS)), two_heads=True,
                          head_dim=B_HEAD_DIM, q_scale=B_HEAD_DIM ** -0.5 * LOG2E, tm=min(PROJ_TM, s), name="proj_b")
    yb = _gqa_attention(qb, kb, vb)

    qc, kc, vc = _project(x, shift, scale, norm_g, lp["w_c"], lp["gain_c"], tab_c,
                          ((C_WIDTH, "qT", 1), (C_WIDTH, "k", 1), (C_WIDTH, "vT", 1)), two_heads=True,
                          head_dim=C_HEAD_DIM, q_scale=C_HEAD_DIM ** -0.5 * LOG2E, tm=min(PROJ_TM, s), name="proj_c")
    yc = _diff_attention(qc, kc, vc, lam_rows, subln, lam_init)

    return _final(x, shift, scale, gate, norm_g, oa, yb, yc, lp["w_z"], w_bg, b_bg, lp["w_oa"], lp["w_ob"],
                  lp["w_oc"], w_out, tm=min(FINAL_TM, s))


def kernel(x_prompt, x_sample, c_prompt, c_sample, norm_g, w_ada, b_ada, w_in, qn_a, kn_a, qn_b, kn_b, qn_c, kn_c,
           lam_q1, lam_k1, lam_q2, lam_k2, subln_c, w_oa, w_ob, w_oc, w_bg, b_bg, w_out):
    groups = ((x_prompt, c_prompt), (x_sample, c_sample))
    rows = [c.shape[0] for _, c in groups]
    pad = -sum(rows) % 8
    c_all = jnp.concatenate([c for _, c in groups] + [jnp.zeros((pad, D_MODEL), F32)], axis=0)
    mod_all = _ada_mod(c_all, w_ada.astype(BF16), b_ada)

    layers = [_prep_layer(w_in[l], qn_a[l], kn_a[l], qn_b[l], kn_b[l], qn_c[l], kn_c[l], w_oa[l], w_ob[l], w_oc[l])
              for l in range(DEPTH)]
    w_bg16, w_out16 = w_bg.astype(BF16), w_out.astype(BF16)

    outs = []
    row0 = 0
    for (x, c), n in zip(groups, rows):
        tabs = _tables(x.shape[1])
        for l in range(DEPTH):
            lam_rows = [p[l][None, :] for p in (lam_q1, lam_k1, lam_q2, lam_k2)]
            x = _layer(x, mod_all[l, row0:row0 + n], l, layers[l], tabs, norm_g[l][None, :], lam_rows,
                       subln_c[l][None, :], w_bg16[l], b_bg[l][None, :], w_out16[l])
        outs.append(x)
        row0 += n
    return tuple(outs)
```

```python
import functools
import math

import jax
import jax.numpy as jnp
import numpy as np
from jax import lax
from jax.experimental import pallas as pl
from jax.experimental.pallas import tpu as pltpu

D_MODEL = 1024
DEPTH = 2
GRID_W = 64
EPS = 1e-6
NEG = -1e30
ROPE_THETA = 500000.0
AXIAL_THETA = 10000.0

A_GROUPS = ((128, 1), (512, 4), (2048, 16))
A_HEADS = 4
A_HEAD_DIM = 96
A_ROT = A_HEAD_DIM // 4
A_BAND = 64
N_A = len(A_GROUPS) * A_HEADS

B_HEADS = 6
B_KV_HEADS = 2
B_HEAD_DIM = 64
C_HEADS = 4
C_HEAD_DIM = 64
C_ROT = C_HEAD_DIM // 4

A_QKV = N_A * A_HEAD_DIM
A_WIDTH = A_HEADS * A_HEAD_DIM
B_WIDTH = B_HEADS * B_HEAD_DIM
B_KV = B_KV_HEADS * B_HEAD_DIM
C_WIDTH = C_HEADS * 2 * C_HEAD_DIM
IN_SIZES = (A_QKV, A_QKV, A_QKV, A_WIDTH, B_WIDTH, B_KV, B_KV, B_WIDTH, C_WIDTH, C_WIDTH, C_WIDTH, C_WIDTH)

LANE = 128
HALF = LANE // 2
ROT_SHIFT = LANE // 2
A_SLAB = A_HEADS * LANE
BF16_SUBLANES = 16
ONES_ROWS = BF16_SUBLANES
FLASH_TK = 512
KEY_PART = 512
SCORE_SLOTS = 2
FLASH_UNROLL = 4
PROJ_TM = 512
FINAL_TM = 256
LOG2E = math.log2(math.e)
LN2 = math.log(2.0)
VMEM_LIMIT = 48 * 1024 * 1024

BF16 = jnp.bfloat16
F32 = jnp.float32


def _cparams(n_axes):
    return pltpu.CompilerParams(dimension_semantics=("arbitrary",) * n_axes, vmem_limit_bytes=VMEM_LIMIT)


def _dot(a, b):
    return jnp.dot(a, b, preferred_element_type=F32)


def _dot_nt(a, b):
    return lax.dot_general(a, b, (((1,), (1,)), ((), ())), preferred_element_type=F32)


def _silu(x):
    return x * jax.nn.sigmoid(x)


def _ada_kernel(c_ref, w_ref, b_ref, o_ref):
    c = c_ref[...]
    o_ref[...] = _dot(_silu(c).astype(BF16), w_ref[...]) + b_ref[...]


def _ada_mod(c_all, w_ada, b_ada):
    rows = c_all.shape[0]
    return pl.pallas_call(
        _ada_kernel,
        grid=(DEPTH, 3),
        in_specs=[
            pl.BlockSpec((rows, D_MODEL), lambda l, n: (0, 0)),
            pl.BlockSpec((None, D_MODEL, D_MODEL), lambda l, n: (l, 0, n)),
            pl.BlockSpec((None, 1, D_MODEL), lambda l, n: (l, 0, n)),
        ],
        out_specs=pl.BlockSpec((None, rows, D_MODEL), lambda l, n: (l, 0, n)),
        out_shape=jax.ShapeDtypeStruct((DEPTH, rows, 3 * D_MODEL), F32),
        compiler_params=_cparams(2),
        name="ada_mod",
    )(c_all, w_ada, b_ada.reshape(DEPTH, 1, 3 * D_MODEL))


def _modulated_norm(x, g, scale, shift):
    ms = jnp.mean(x * x, axis=-1, keepdims=True)
    h = x * lax.rsqrt(ms + EPS) * g
    return h * (1.0 + scale) + shift


def _first_head(index):
    return (index & (HALF - 1)) < HALF // 2


def _head_norm_rope(u, gain, cos, sin, two_heads, head_dim):
    sq = u * u
    if two_heads:
        first = _first_head(lax.broadcasted_iota(jnp.int32, (1, LANE), 1))
        s_first = jnp.sum(jnp.where(first, sq, 0.0), axis=-1, keepdims=True)
        s_all = jnp.sum(sq, axis=-1, keepdims=True)
        ms = jnp.where(first, s_first, s_all - s_first) * (1.0 / head_dim)
    else:
        ms = jnp.sum(sq, axis=-1, keepdims=True) * (1.0 / head_dim)
    y = u * lax.rsqrt(ms + EPS) * gain
    return y * cos + pltpu.roll(y, ROT_SHIFT, 1) * sin


def _proj_kernel(x_ref, shift_ref, scale_ref, g_ref, w_ref, gain_ref, cos_ref, sin_ref, *refs,
                 outs, two_heads, head_dim, q_scale):
    out_refs, stage_ref = refs[:len(outs)], refs[len(outs):]
    h = _modulated_norm(x_ref[...], g_ref[...], scale_ref[...], shift_ref[...]).astype(BF16)
    cos, sin = cos_ref[...], sin_ref[...]
    tm = x_ref.shape[0]
    col = 0
    for o_ref, (width, kind, dil) in zip(out_refs, outs):
        u = _dot(h, w_ref[:, col:col + width])
        if kind == "vT":
            dv = LANE // dil
            for s in range(width // LANE):
                ut = u[:, s * LANE:(s + 1) * LANE].T.astype(BF16)
                for hd in range(dil):
                    base = (s * dil + hd) * (dv + ONES_ROWS)
                    o_ref[base:base + dv, :] = ut[hd * dv:(hd + 1) * dv]
                    o_ref[base + dv:base + dv + ONES_ROWS, :] = jnp.ones((ONES_ROWS, tm), BF16)
        else:
            if kind != "v":
                slabs = []
                for s in range(width // LANE):
                    lo = s * LANE
                    y = _head_norm_rope(u[:, lo:lo + LANE], gain_ref[:, col + lo:col + lo + LANE], cos, sin,
                                        two_heads, head_dim)
                    slabs.append(y * q_scale if kind in ("q", "qT") and q_scale != 1.0 else y)
                u = jnp.concatenate(slabs, axis=-1)
            if kind == "qT":
                for s in range(width // LANE):
                    o_ref[s * LANE:(s + 1) * LANE, :] = u[:, s * LANE:(s + 1) * LANE].T.astype(BF16)
            elif dil == 1:
                o_ref[...] = u.astype(BF16)
            else:
                stage = stage_ref[0]
                for s in range(width // LANE):
                    stage[s] = u[:, s * LANE:(s + 1) * LANE]
                rows = lambda r: pl.ds(r, tm // dil, stride=dil)
                for r in range(dil):
                    for s in range(width // LANE):
                        lo = r * width + s * LANE
                        o_ref[:, lo:lo + LANE] = stage[s, rows(r), :].astype(BF16)
        col += width


def _vt_rows(v_heads):
    return LANE + v_heads * ONES_ROWS


def _proj_out(kind, width, dil, bn, s, tm):
    if kind == "qT":
        return (bn, width, s), pl.BlockSpec((None, width, tm), lambda b, i: (b, 0, i))
    if kind == "vT":
        per = FLASH_TK // tm
        rows = width // LANE * _vt_rows(dil)
        return ((bn, s // FLASH_TK, rows, FLASH_TK),
                pl.BlockSpec((None, None, rows, tm), lambda b, i: (b, i // per, 0, i % per)))
    return (bn, s // dil, dil * width), pl.BlockSpec((None, tm // dil, dil * width), lambda b, i: (b, i, 0))


def _project(x, shift, scale, norm_g, w, gain, tables, outs, *, two_heads, head_dim, q_scale, tm, name):
    bn, s, _ = x.shape
    wcols = w.shape[1]
    kern = functools.partial(_proj_kernel, outs=outs, two_heads=two_heads, head_dim=head_dim, q_scale=q_scale)
    vec = pl.BlockSpec((None, 1, D_MODEL), lambda b, i: (b, 0, 0))
    tab = pl.BlockSpec((tm, LANE), lambda b, i: (i, 0))
    shapes, specs = zip(*[_proj_out(kind, width, dil, bn, s, tm) for width, kind, dil in outs])
    n_slabs = max(width for width, _, _ in outs) // LANE
    dilated = any(d > 1 and kind != "vT" for _, kind, d in outs)
    stage = [pltpu.VMEM((n_slabs, tm, LANE), F32)] if dilated else []
    return pl.pallas_call(
        kern,
        grid=(bn, s // tm),
        in_specs=[
            pl.BlockSpec((None, tm, D_MODEL), lambda b, i: (b, i, 0)),
            vec, vec,
            pl.BlockSpec((1, D_MODEL), lambda b, i: (0, 0)),
            pl.BlockSpec((D_MODEL, wcols), lambda b, i: (0, 0)),
            pl.BlockSpec((1, wcols), lambda b, i: (0, 0)),
            tab, tab,
        ],
        out_specs=list(specs),
        out_shape=[jax.ShapeDtypeStruct(shape, BF16) for shape in shapes],
        scratch_shapes=stage,
        compiler_params=_cparams(2),
        name=name,
    )(x, shift, scale, norm_g, w, gain, *tables)


def _band_attn_kernel(q_ref, kp_ref, kc_ref, kn_ref, vp_ref, vc_ref, vn_ref, bias_ref, o_ref, *, tb, n_blocks):
    i = pl.program_id(2)
    nk = tb + 2 * A_BAND
    colk = lax.broadcasted_iota(jnp.int32, (1, nk), 1)
    first_col = jnp.where(i == 0, A_BAND, 0)
    end_col = jnp.where(i == n_blocks - 1, tb + A_BAND, nk)
    bias = bias_ref[...] + jnp.where((colk < first_col) | (colk >= end_col), NEG, 0.0)
    lane = lax.broadcasted_iota(jnp.int32, (1, LANE), 1)
    heads = [slice(j * LANE, (j + 1) * LANE) for j in range(A_HEADS)]
    band = lambda p_ref, c_ref, n_ref, sl: jnp.concatenate([p_ref[:, sl], c_ref[:, sl], n_ref[:, sl]], axis=0)
    scores = [_dot_nt(q_ref[:, sl], band(kp_ref, kc_ref, kn_ref, sl)) for sl in heads]
    stats = []
    for s in scores:
        s = s + bias
        m = jnp.max(s, axis=-1, keepdims=True)
        p = jnp.exp2(s - m)
        stats.append((m, jnp.sum(p, axis=-1, keepdims=True), p.astype(BF16)))
    for sl, (m, l, p) in zip(heads, stats):
        o = _dot(p, band(vp_ref, vc_ref, vn_ref, sl)) / l
        o_ref[:, sl] = jnp.where(lane >= A_HEAD_DIM, m * LN2 + jnp.log(l), o)


def _band_attention(q, k, v, dilation):
    bn, length, _ = q.shape
    tb = min(256, length)
    nb = length // tb
    per = tb // A_BAND
    cur = pl.BlockSpec((None, tb, A_SLAB), lambda b, r, i: (b, i, r))
    prv = pl.BlockSpec((None, A_BAND, A_SLAB), lambda b, r, i: (b, jnp.maximum(i * per - 1, 0), r))
    nxt = pl.BlockSpec((None, A_BAND, A_SLAB), lambda b, r, i: (b, jnp.minimum((i + 1) * per, nb * per - 1), r))
    nk = tb + 2 * A_BAND
    dist = np.arange(nk)[None, :] - A_BAND - np.arange(tb)[:, None]
    bias = np.where(np.abs(dist) <= A_BAND, 0.0, NEG).astype(np.float32)
    kern = functools.partial(_band_attn_kernel, tb=tb, n_blocks=nb)
    return pl.pallas_call(
        kern,
        grid=(bn, dilation, nb),
        in_specs=[cur, prv, cur, nxt, prv, cur, nxt, pl.BlockSpec((tb, nk), lambda b, r, i: (0, 0))],
        out_specs=cur,
        out_shape=jax.ShapeDtypeStruct((bn, length, dilation * A_SLAB), F32),
        compiler_params=_cparams(3),
        name=f"band_attn_d{dilation}",
    )(q, k, k, k, v, v, v, bias)


def _stacked_flash(qt_ref, k_ref, vt_ref, acc_sc, s_sc, *, tq, n_chunks, v_heads):
    dv = LANE // v_heads
    v_rows = [slice(hd * (dv + ONES_ROWS), (hd + 1) * (dv + ONES_ROWS)) for hd in range(v_heads)]
    col_groups = [slice(0, 2 * tq)] if v_heads == 1 else [slice(0, tq), slice(tq, 2 * tq)]
    first = _first_head(lax.broadcasted_iota(jnp.int32, (LANE, 1), 0))
    qt = qt_ref[...]
    zero = jnp.zeros_like(qt)
    q2t = jnp.concatenate([jnp.where(first, qt, zero), jnp.where(first, zero, qt)], axis=1)
    acc_sc[...] = jnp.zeros(acc_sc.shape, F32)

    parts = [slice(lo, lo + KEY_PART) for lo in range(0, FLASH_TK, KEY_PART)]

    def scores(c, slot, rows):
        start = pl.multiple_of(c * FLASH_TK + rows.start, KEY_PART)
        s_sc[slot, rows, :] = _dot(k_ref[pl.ds(start, KEY_PART), :], q2t)

    def step(c, slot, m_prev, c_next=None):
        m_next = jnp.maximum(m_prev, jnp.max(s_sc[slot], axis=0, keepdims=True))
        vc = vt_ref[c]
        alpha = jnp.exp2(m_prev - m_next)
        pv = [None] * v_heads
        for rows in parts:
            if c_next is not None:
                scores(c_next, 1 - slot, rows)
            pt = jnp.exp2(s_sc[slot, rows, :] - m_next).astype(BF16)
            for g, (vr, cols) in enumerate(zip(v_rows, col_groups)):
                part = _dot(vc[vr, rows], pt[:, cols])
                pv[g] = part if pv[g] is None else pv[g] + part
        for cols, total in zip(col_groups, pv):
            acc_sc[:, cols] = alpha[:, cols] * acc_sc[:, cols] + total
        return m_next

    unroll = min(FLASH_UNROLL, n_chunks)

    def group(j, m):
        for u in range(unroll):
            c = unroll * j + u
            m = step(c, u % SCORE_SLOTS, m, c + 1)
        return m

    for rows in parts:
        scores(0, 0, rows)
    m = lax.fori_loop(0, n_chunks // unroll - 1, group, jnp.full((1, 2 * tq), -jnp.inf, F32))
    for c in range(n_chunks - unroll, n_chunks):
        m = step(c, c % SCORE_SLOTS, m, c + 1 if c + 1 < n_chunks else None)
    acc = acc_sc[...]
    return acc[:dv] / acc[dv:dv + 1]


def _gqa_kernel(qt_ref, k_ref, vt_ref, o_ref, acc_sc, s_sc, *, tq, n_chunks):
    o = _stacked_flash(qt_ref, k_ref, vt_ref, acc_sc, s_sc, tq=tq, n_chunks=n_chunks, v_heads=B_KV_HEADS)
    o_ref[...] = jnp.concatenate([o[:, :tq], o[:, tq:]], axis=0).T


def _diff_kernel(qt_ref, k_ref, vt_ref, lq1_ref, lk1_ref, lq2_ref, lk2_ref, sub_ref, o_ref, acc_sc,
                 s_sc, *, tq, n_chunks, lam_init):
    o = _stacked_flash(qt_ref, k_ref, vt_ref, acc_sc, s_sc, tq=tq, n_chunks=n_chunks, v_heads=1)
    lam = (jnp.exp(jnp.sum(lq1_ref[...] * lk1_ref[...], axis=-1, keepdims=True))
           - jnp.exp(jnp.sum(lq2_ref[...] * lk2_ref[...], axis=-1, keepdims=True)) + lam_init)
    oc = (o[:, :tq] - lam * o[:, tq:]).T
    ms = jnp.mean(oc * oc, axis=-1, keepdims=True)
    o_ref[...] = oc * lax.rsqrt(ms + EPS) * sub_ref[...] * (1.0 - lam_init)


def _flash_tq(s):
    return min(512, s)


def _flash_scratch(tq, v_heads):
    return [pltpu.VMEM((LANE // v_heads + ONES_ROWS, 2 * tq), F32),
            pltpu.VMEM((SCORE_SLOTS, FLASH_TK, 2 * tq), F32)]


def _gqa_attention(qt, k, vt):
    bn, _, s = qt.shape
    tq, n_chunks = _flash_tq(s), s // FLASH_TK
    kern = functools.partial(_gqa_kernel, tq=tq, n_chunks=n_chunks)
    return pl.pallas_call(
        kern,
        grid=(bn, B_HEADS // 2, s // tq),
        in_specs=[pl.BlockSpec((None, LANE, tq), lambda b, p, i: (b, p, i)),
                  pl.BlockSpec((None, s, LANE), lambda b, p, i: (b, 0, 0)),
                  pl.BlockSpec((None, n_chunks, _vt_rows(B_KV_HEADS), FLASH_TK), lambda b, p, i: (b, 0, 0, 0))],
        out_specs=pl.BlockSpec((None, tq, LANE), lambda b, p, i: (b, i, p)),
        out_shape=jax.ShapeDtypeStruct((bn, s, B_WIDTH), F32),
        scratch_shapes=_flash_scratch(tq, B_KV_HEADS),
        compiler_params=_cparams(3),
        name="gqa_attn",
    )(qt, k, vt)


def _diff_attention(qt, k, vt, lam_rows, subln, lam_init):
    bn, _, s = qt.shape
    tq, n_chunks = _flash_tq(s), s // FLASH_TK
    kern = functools.partial(_diff_kernel, tq=tq, n_chunks=n_chunks, lam_init=lam_init)
    row = lambda n: pl.BlockSpec((1, n), lambda b, h, i: (0, 0))
    return pl.pallas_call(
        kern,
        grid=(bn, C_HEADS, s // tq),
        in_specs=[pl.BlockSpec((None, LANE, tq), lambda b, h, i: (b, h, i)),
                  pl.BlockSpec((None, s, LANE), lambda b, h, i: (b, 0, h)),
                  pl.BlockSpec((None, n_chunks, _vt_rows(1), FLASH_TK), lambda b, h, i: (b, 0, h, 0)),
                  row(C_HEAD_DIM), row(C_HEAD_DIM), row(C_HEAD_DIM), row(C_HEAD_DIM), row(LANE)],
        out_specs=pl.BlockSpec((None, tq, LANE), lambda b, h, i: (b, i, h)),
        out_shape=jax.ShapeDtypeStruct((bn, s, C_WIDTH), F32),
        scratch_shapes=_flash_scratch(tq, 1),
        compiler_params=_cparams(3),
        name="diff_attn",
    )(qt, k, vt, *lam_rows, subln)


def _final_kernel(x_ref, shift_ref, scale_ref, gate_ref, g_ref, oa0_ref, oa1_ref, oa2_ref, yb_ref, yc_ref,
                  wz_ref, wbg_ref, bbg_ref, woa_ref, wob_ref, woc_ref, wout_ref, o_ref, *stage_refs):
    x = x_ref[...]
    tm = x.shape[0]
    h = _modulated_norm(x, g_ref[...], scale_ref[...], shift_ref[...]).astype(BF16)
    lane = lax.broadcasted_iota(jnp.int32, (1, LANE), 1)

    for oa_ref, st_ref, (_, dil) in zip((oa1_ref, oa2_ref), stage_refs, A_GROUPS[1:]):
        for r in range(dil):
            for j in range(A_HEADS):
                lo = r * A_SLAB + j * LANE
                st_ref[j, pl.ds(r, tm // dil, stride=dil), :] = oa_ref[:, lo:lo + LANE]

    ya = []
    for j in range(A_HEADS):
        slabs = [oa0_ref[:, j * LANE:(j + 1) * LANE]] + [st_ref[j] for st_ref in stage_refs]
        lses = [t[:, A_HEAD_DIM:A_HEAD_DIM + 1] for t in slabs]
        top = jnp.maximum(jnp.maximum(lses[0], lses[1]), lses[2])
        ws = [jnp.exp(t - top) for t in lses]
        mix = (ws[0] * slabs[0] + ws[1] * slabs[1] + ws[2] * slabs[2]) / (ws[0] + ws[1] + ws[2])
        ya.append(jnp.where(lane < A_HEAD_DIM, mix, 0.0))
    ya = jnp.concatenate(ya, axis=-1)

    def branch(y, z_lo, z_hi, wo_ref, g_lo):
        z = _dot(h, wz_ref[:, z_lo:z_hi])
        p = _dot((y * _silu(z)).astype(BF16), wo_ref[...])
        gate = jax.nn.sigmoid(_dot(h, wbg_ref[:, g_lo:g_lo + D_MODEL]) + bbg_ref[:, g_lo:g_lo + D_MODEL])
        return gate * p

    za_hi = A_SLAB
    zb_hi = za_hi + B_WIDTH
    zc_hi = zb_hi + C_WIDTH
    merged = (branch(ya, 0, za_hi, woa_ref, 0)
              + branch(yb_ref[...], za_hi, zb_hi, wob_ref, D_MODEL)
              + branch(yc_ref[...], zb_hi, zc_hi, woc_ref, 2 * D_MODEL))
    out = _dot(merged.astype(BF16), wout_ref[...])
    o_ref[...] = x + gate_ref[...] * out


def _final(x, shift, scale, gate, norm_g, oa, yb, yc, wz, wbg, bbg, woa, wob, woc, wout, *, tm):
    bn, s, _ = x.shape
    tok = lambda w: pl.BlockSpec((None, tm, w), lambda b, i: (b, i, 0))
    vec = pl.BlockSpec((None, 1, D_MODEL), lambda b, i: (b, 0, 0))
    full = lambda a: pl.BlockSpec(a.shape, lambda b, i: (0, 0))
    band = [pl.BlockSpec((None, tm // dil, dil * A_SLAB), lambda b, i: (b, i, 0)) for _, dil in A_GROUPS]
    return pl.pallas_call(
        _final_kernel,
        grid=(bn, s // tm),
        in_specs=[tok(D_MODEL), vec, vec, vec, full(norm_g), *band,
                  tok(B_WIDTH), tok(C_WIDTH), full(wz), full(wbg), full(bbg), full(woa), full(wob), full(woc),
                  full(wout)],
        out_specs=tok(D_MODEL),
        out_shape=jax.ShapeDtypeStruct((bn, s, D_MODEL), F32),
        scratch_shapes=[pltpu.VMEM((A_HEADS, tm, LANE), F32) for _ in A_GROUPS[1:]],
        compiler_params=_cparams(2),
        name="final",
    )(x, shift, scale, gate, norm_g, *oa, yb, yc, wz, wbg, bbg, woa, wob, woc, wout)


B_Q_ORDER = (0, 3, 1, 4, 2, 5)


def _slab_layout(head_dim, n_heads, rot_groups):
    src = -np.ones((LANE,), np.int32)
    group = -np.ones((LANE,), np.int32)
    freq = np.zeros((LANE,), np.int32)
    sign = np.zeros((LANE,), np.float32)
    width = ROT_SHIFT // n_heads
    for hd in range(n_heads):
        low = [(start + i, g, i, -1.0) for g, (start, half) in enumerate(rot_groups) for i in range(half)]
        high = [(start + half + i, g, i, 1.0) for g, (start, half) in enumerate(rot_groups) for i in range(half)]
        rotary = {d for d, _, _, _ in low + high}
        rest = [(d, -1, 0, 0.0) for d in range(head_dim) if d not in rotary]
        n_low_rest = width - len(low)
        low, high = low + rest[:n_low_rest], high + rest[n_low_rest:]
        assert len(low) == width and len(high) <= width
        for base, items in ((hd * width, low), (hd * width + ROT_SHIFT, high)):
            for lane, (d, g, i, sg) in enumerate(items, start=base):
                src[lane], group[lane], freq[lane], sign[lane] = hd * head_dim + d, g, i, sg
    return src, group, freq, sign


LAYOUT_A = _slab_layout(A_HEAD_DIM, 1, ((0, A_ROT // 2),))
LAYOUT_B = _slab_layout(B_HEAD_DIM, 2, ((0, B_HEAD_DIM // 4), (B_HEAD_DIM // 2, B_HEAD_DIM // 4)))
LAYOUT_C = _slab_layout(C_HEAD_DIM, 2, ((0, C_ROT // 2),))


def _to_slabs(w, layout, dims_per_slab):
    src = layout[0]
    n_slabs = w.shape[-1] // dims_per_slab
    idx = np.concatenate([np.maximum(src, 0) + s * dims_per_slab for s in range(n_slabs)])
    keep = np.tile(src >= 0, n_slabs)
    return jnp.where(keep, jnp.take(w, idx, axis=-1), 0.0)


def _pad_heads(w, n_heads):
    lead = w.shape[:-1]
    w = w.reshape(*lead, n_heads, A_HEAD_DIM)
    w = jnp.pad(w, [(0, 0)] * len(lead) + [(0, 0), (0, LANE - A_HEAD_DIM)])
    return w.reshape(*lead, n_heads * LANE)


def _permute_heads(w, order, dim):
    lead = w.shape[:-1]
    w = w.reshape(*lead, len(order), dim)
    return w[..., jnp.array(order), :].reshape(*lead, len(order) * dim)


def _prep_layer(w_in, qn_a, kn_a, qn_b, kn_b, qn_c, kn_c, w_oa, w_ob, w_oc):
    offs = [0]
    for n in IN_SIZES:
        offs.append(offs[-1] + n)
    qa, ka, va, za, qb, kb, vb, zb, qc, kc, vc, zc = [w_in[:, offs[i]:offs[i + 1]] for i in range(len(IN_SIZES))]

    slab_a = functools.partial(_to_slabs, layout=LAYOUT_A, dims_per_slab=A_HEAD_DIM)
    slab_b = functools.partial(_to_slabs, layout=LAYOUT_B, dims_per_slab=2 * B_HEAD_DIM)
    slab_c = functools.partial(_to_slabs, layout=LAYOUT_C, dims_per_slab=2 * C_HEAD_DIM)
    qb = _permute_heads(qb, B_Q_ORDER, B_HEAD_DIM)

    w_a = jnp.concatenate([slab_a(qa), slab_a(ka), _pad_heads(va, N_A)], axis=-1).astype(BF16)
    gain_a = jnp.concatenate([slab_a(jnp.tile(qn_a, N_A)), slab_a(jnp.tile(kn_a, N_A)),
                              jnp.zeros((N_A * LANE,), F32)])[None, :]

    w_b = jnp.concatenate([slab_b(qb), slab_b(kb), vb], axis=-1).astype(BF16)
    gain_b = jnp.concatenate([slab_b(jnp.tile(qn_b, B_HEADS)), slab_b(jnp.tile(kn_b, B_KV_HEADS)),
                              jnp.zeros((B_KV,), F32)])[None, :]

    w_c = jnp.concatenate([slab_c(qc), slab_c(kc), vc], axis=-1).astype(BF16)
    gain_c = jnp.concatenate([slab_c(jnp.tile(qn_c, 2 * C_HEADS)), slab_c(jnp.tile(kn_c, 2 * C_HEADS)),
                              jnp.zeros((C_WIDTH,), F32)])[None, :]

    w_z = jnp.concatenate([_pad_heads(za, A_HEADS), _permute_heads(zb, B_Q_ORDER, B_HEAD_DIM), zc],
                          axis=-1).astype(BF16)
    w_oa_p = _pad_heads(w_oa.T, A_HEADS).T.astype(BF16)
    w_ob_p = _permute_heads(w_ob.T, B_Q_ORDER, B_HEAD_DIM).T.astype(BF16)
    return dict(w_a=w_a, gain_a=gain_a, w_b=w_b, gain_b=gain_b, w_c=w_c, gain_c=gain_c, w_z=w_z,
                w_oa=w_oa_p, w_ob=w_ob_p, w_oc=w_oc.astype(BF16))


def _inv_freq(dim, theta):
    return theta ** (-jnp.arange(0, dim, 2, dtype=F32) / dim)


def _rope_tables(layout, inv_freqs, streams):
    _, group, freq, sign = layout
    ang = 0.0
    for g, (inv, pos) in enumerate(zip(inv_freqs, streams)):
        ang = ang + pos[:, None] * (inv[freq] * (group == g))[None, :]
    return jnp.cos(ang), jnp.sin(ang) * sign[None, :]


def _tables(s):
    pos = jnp.arange(s)
    tok, row, col = [t.astype(F32) for t in (pos, pos // GRID_W, pos % GRID_W)]
    inv_b = _inv_freq(B_HEAD_DIM // 2, AXIAL_THETA)
    tab_a = _rope_tables(LAYOUT_A, [_inv_freq(A_ROT, ROPE_THETA)], [tok])
    tab_b = _rope_tables(LAYOUT_B, [inv_b, inv_b], [row, col])
    tab_c = _rope_tables(LAYOUT_C, [_inv_freq(C_ROT, ROPE_THETA)], [tok])
    return tab_a, tab_b, tab_c


def _layer(x, mod, l, lp, tabs, norm_g, lam_rows, subln, w_bg, b_bg, w_out):
    bn, s, _ = x.shape
    shift, scale, gate = [m.reshape(bn, 1, D_MODEL) for m in jnp.split(mod, 3, axis=-1)]
    tab_a, tab_b, tab_c = tabs
    lam_init = 0.8 - 0.6 * math.exp(-0.3 * l)

    a_outs = tuple((A_SLAB, kind, dil) for kind in ("q", "k", "v") for _, dil in A_GROUPS)
    pa = _project(x, shift, scale, norm_g, lp["w_a"], lp["gain_a"], tab_a, a_outs, two_heads=False,
                  head_dim=A_HEAD_DIM, q_scale=A_HEAD_DIM ** -0.5 * LOG2E, tm=min(PROJ_TM, s), name="proj_a")
    n_g = len(A_GROUPS)
    oa = [_band_attention(pa[g], pa[n_g + g], pa[2 * n_g + g], dil) for g, (_, dil) in enumerate(A_GROUPS)]

    qb, kb, vb = _project(x, shift, scale, norm_g, lp["w_b"], lp["gain_b"], tab_b,
                          ((B_WIDTH, "qT", 1), (B_KV, "k", 1), (B_KV, "vT", B_KV_HEADS)), two_heads=True,
                          head_dim=B_HEAD_DIM, q_scale=B_HEAD_DIM ** -0.5 * LOG2E, tm=min(PROJ_TM, s), name="proj_b")
    yb = _gqa_attention(qb, kb, vb)

    qc, kc, vc = _project(x, shift, scale, norm_g, lp["w_c"], lp["gain_c"], tab_c,
                          ((C_WIDTH, "qT", 1), (C_WIDTH, "k", 1), (C_WIDTH, "vT", 1)), two_heads=True,
                          head_dim=C_HEAD_DIM, q_scale=C_HEAD_DIM ** -0.5 * LOG2E, tm=min(PROJ_TM, s), name="proj_c")
    yc = _diff_attention(qc, kc, vc, lam_rows, subln, lam_init)

    return _final(x, shift, scale, gate, norm_g, oa, yb, yc, lp["w_z"], w_bg, b_bg, lp["w_oa"], lp["w_ob"],
                  lp["w_oc"], w_out, tm=min(FINAL_TM, s))


def kernel(x_prompt, x_sample, c_prompt, c_sample, norm_g, w_ada, b_ada, w_in, qn_a, kn_a, qn_b, kn_b, qn_c, kn_c,
           lam_q1, lam_k1, lam_q2, lam_k2, subln_c, w_oa, w_ob, w_oc, w_bg, b_bg, w_out):
    groups = ((x_prompt, c_prompt), (x_sample, c_sample))
    rows = [c.shape[0] for _, c in groups]
    pad = -sum(rows) % 8
    c_all = jnp.concatenate([c for _, c in groups] + [jnp.zeros((pad, D_MODEL), F32)], axis=0)
    mod_all = _ada_mod(c_all, w_ada.astype(BF16), b_ada)

    layers = [_prep_layer(w_in[l], qn_a[l], kn_a[l], qn_b[l], kn_b[l], qn_c[l], kn_c[l], w_oa[l], w_ob[l], w_oc[l])
              for l in range(DEPTH)]
    w_bg16, w_out16 = w_bg.astype(BF16), w_out.astype(BF16)

    outs = []
    row0 = 0
    for (x, c), n in zip(groups, rows):
        tabs = _tables(x.shape[1])
        for l in range(DEPTH):
            lam_rows = [p[l][None, :] for p in (lam_q1, lam_k1, lam_q2, lam_k2)]
            x = _layer(x, mod_all[l, row0:row0 + n], l, layers[l], tabs, norm_g[l][None, :], lam_rows,
                       subln_c[l][None, :], w_bg16[l], b_bg[l][None, :], w_out16[l])
        outs.append(x)
        row0 += n
    return tuple(outs)
```

```python
import functools
import math

import jax
import jax.numpy as jnp
import numpy as np
from jax import lax
from jax.experimental import pallas as pl
from jax.experimental.pallas import tpu as pltpu

D_MODEL = 1024
DEPTH = 2
GRID_W = 64
EPS = 1e-6
NEG = -1e30
ROPE_THETA = 500000.0
AXIAL_THETA = 10000.0

A_GROUPS = ((128, 1), (512, 4), (2048, 16))
A_HEADS = 4
A_HEAD_DIM = 96
A_ROT = A_HEAD_DIM // 4
A_BAND = 64
N_A = len(A_GROUPS) * A_HEADS

B_HEADS = 6
B_KV_HEADS = 2
B_HEAD_DIM = 64
C_HEADS = 4
C_HEAD_DIM = 64
C_ROT = C_HEAD_DIM // 4

A_QKV = N_A * A_HEAD_DIM
A_WIDTH = A_HEADS * A_HEAD_DIM
B_WIDTH = B_HEADS * B_HEAD_DIM
B_KV = B_KV_HEADS * B_HEAD_DIM
C_WIDTH = C_HEADS * 2 * C_HEAD_DIM
IN_SIZES = (A_QKV, A_QKV, A_QKV, A_WIDTH, B_WIDTH, B_KV, B_KV, B_WIDTH, C_WIDTH, C_WIDTH, C_WIDTH, C_WIDTH)

LANE = 128
HALF = LANE // 2
ROT_SHIFT = LANE // 2
A_SLAB = A_HEADS * LANE
BF16_SUBLANES = 16
ONES_ROWS = BF16_SUBLANES
FLASH_TK = 512
KEY_PART = 512
SCORE_SLOTS = 2
FLASH_UNROLL = 4
PROJ_TM = 512
PROJ_A_TM = 256
FINAL_TM = 512
LOG2E = math.log2(math.e)
LN2 = math.log(2.0)
VMEM_LIMIT = 48 * 1024 * 1024

BF16 = jnp.bfloat16
F32 = jnp.float32


def _cparams(n_axes):
    return pltpu.CompilerParams(dimension_semantics=("arbitrary",) * n_axes, vmem_limit_bytes=VMEM_LIMIT)


def _dot(a, b):
    return jnp.dot(a, b, preferred_element_type=F32)


def _dot_nt(a, b):
    return lax.dot_general(a, b, (((1,), (1,)), ((), ())), preferred_element_type=F32)


def _silu(x):
    return x * jax.nn.sigmoid(x)


def _ada_kernel(c_ref, w_ref, b_ref, o_ref):
    c = c_ref[...]
    o_ref[...] = _dot(_silu(c).astype(BF16), w_ref[...]) + b_ref[...]


def _ada_mod(c_all, w_ada, b_ada):
    rows = c_all.shape[0]
    return pl.pallas_call(
        _ada_kernel,
        grid=(DEPTH, 3),
        in_specs=[
            pl.BlockSpec((rows, D_MODEL), lambda l, n: (0, 0)),
            pl.BlockSpec((None, D_MODEL, D_MODEL), lambda l, n: (l, 0, n)),
            pl.BlockSpec((None, 1, D_MODEL), lambda l, n: (l, 0, n)),
        ],
        out_specs=pl.BlockSpec((None, rows, D_MODEL), lambda l, n: (l, 0, n)),
        out_shape=jax.ShapeDtypeStruct((DEPTH, rows, 3 * D_MODEL), F32),
        compiler_params=_cparams(2),
        name="ada_mod",
    )(c_all, w_ada, b_ada.reshape(DEPTH, 1, 3 * D_MODEL))


def _modulated_norm(x, g, scale, shift):
    ms = jnp.mean(x * x, axis=-1, keepdims=True)
    h = x * lax.rsqrt(ms + EPS) * g
    return h * (1.0 + scale) + shift


def _first_head(index):
    return (index & (HALF - 1)) < HALF // 2


def _head_norm_rope(u, gain, cos, sin, two_heads, head_dim):
    sq = u * u
    if two_heads:
        first = _first_head(lax.broadcasted_iota(jnp.int32, (1, LANE), 1))
        s_first = jnp.sum(jnp.where(first, sq, 0.0), axis=-1, keepdims=True)
        s_all = jnp.sum(sq, axis=-1, keepdims=True)
        ms = jnp.where(first, s_first, s_all - s_first) * (1.0 / head_dim)
    else:
        ms = jnp.sum(sq, axis=-1, keepdims=True) * (1.0 / head_dim)
    y = u * lax.rsqrt(ms + EPS) * gain
    return y * cos + pltpu.roll(y, ROT_SHIFT, 1) * sin


def _proj_kernel(x_ref, shift_ref, scale_ref, g_ref, w_ref, gain_ref, cos_ref, sin_ref, *refs,
                 outs, two_heads, head_dim, q_scale):
    out_refs, stage_ref = refs[:len(outs)], refs[len(outs):]
    h = _modulated_norm(x_ref[...], g_ref[...], scale_ref[...], shift_ref[...]).astype(BF16)
    cos, sin = cos_ref[...], sin_ref[...]
    tm = x_ref.shape[0]
    col = 0
    for o_ref, (width, kind, dil) in zip(out_refs, outs):
        u = _dot(h, w_ref[:, col:col + width])
        if kind == "vT":
            dv = LANE // dil
            for s in range(width // LANE):
                ut = u[:, s * LANE:(s + 1) * LANE].T.astype(BF16)
                for hd in range(dil):
                    base = (s * dil + hd) * (dv + ONES_ROWS)
                    o_ref[base:base + dv, :] = ut[hd * dv:(hd + 1) * dv]
                    o_ref[base + dv:base + dv + ONES_ROWS, :] = jnp.ones((ONES_ROWS, tm), BF16)
        else:
            if kind != "v":
                slabs = []
                for s in range(width // LANE):
                    lo = s * LANE
                    y = _head_norm_rope(u[:, lo:lo + LANE], gain_ref[:, col + lo:col + lo + LANE], cos, sin,
                                        two_heads, head_dim)
                    slabs.append(y * q_scale if kind in ("q", "qT") and q_scale != 1.0 else y)
                u = jnp.concatenate(slabs, axis=-1)
            if kind == "qT":
                for s in range(width // LANE):
                    o_ref[s * LANE:(s + 1) * LANE, :] = u[:, s * LANE:(s + 1) * LANE].T.astype(BF16)
            elif dil == 1:
                o_ref[...] = u.astype(BF16)
            else:
                stage = stage_ref[0]
                for s in range(width // LANE):
                    stage[s] = u[:, s * LANE:(s + 1) * LANE]
                rows = lambda r: pl.ds(r, tm // dil, stride=dil)
                for r in range(dil):
                    for s in range(width // LANE):
                        lo = r * width + s * LANE
                        o_ref[:, lo:lo + LANE] = stage[s, rows(r), :].astype(BF16)
        col += width


def _vt_rows(v_heads):
    return LANE + v_heads * ONES_ROWS


def _proj_out(kind, width, dil, bn, s, tm):
    if kind == "qT":
        return (bn, width, s), pl.BlockSpec((None, width, tm), lambda b, i: (b, 0, i))
    if kind == "vT":
        per = FLASH_TK // tm
        rows = width // LANE * _vt_rows(dil)
        return ((bn, s // FLASH_TK, rows, FLASH_TK),
                pl.BlockSpec((None, None, rows, tm), lambda b, i: (b, i // per, 0, i % per)))
    return (bn, s // dil, dil * width), pl.BlockSpec((None, tm // dil, dil * width), lambda b, i: (b, i, 0))


def _project(x, shift, scale, norm_g, w, gain, tables, outs, *, two_heads, head_dim, q_scale, tm, name):
    bn, s, _ = x.shape
    wcols = w.shape[1]
    kern = functools.partial(_proj_kernel, outs=outs, two_heads=two_heads, head_dim=head_dim, q_scale=q_scale)
    vec = pl.BlockSpec((None, 1, D_MODEL), lambda b, i: (b, 0, 0))
    tab = pl.BlockSpec((tm, LANE), lambda b, i: (i, 0))
    shapes, specs = zip(*[_proj_out(kind, width, dil, bn, s, tm) for width, kind, dil in outs])
    n_slabs = max(width for width, _, _ in outs) // LANE
    dilated = any(d > 1 and kind != "vT" for _, kind, d in outs)
    stage = [pltpu.VMEM((n_slabs, tm, LANE), F32)] if dilated else []
    return pl.pallas_call(
        kern,
        grid=(bn, s // tm),
        in_specs=[
            pl.BlockSpec((None, tm, D_MODEL), lambda b, i: (b, i, 0)),
            vec, vec,
            pl.BlockSpec((1, D_MODEL), lambda b, i: (0, 0)),
            pl.BlockSpec((D_MODEL, wcols), lambda b, i: (0, 0)),
            pl.BlockSpec((1, wcols), lambda b, i: (0, 0)),
            tab, tab,
        ],
        out_specs=list(specs),
        out_shape=[jax.ShapeDtypeStruct(shape, BF16) for shape in shapes],
        scratch_shapes=stage,
        compiler_params=_cparams(2),
        name=name,
    )(x, shift, scale, norm_g, w, gain, *tables)


def _band_attn_kernel(q_ref, kp_ref, kc_ref, kn_ref, vp_ref, vc_ref, vn_ref, bias_ref, o_ref, *, tb, n_blocks):
    i = pl.program_id(2)
    nk = tb + 2 * A_BAND
    colk = lax.broadcasted_iota(jnp.int32, (1, nk), 1)
    first_col = jnp.where(i == 0, A_BAND, 0)
    end_col = jnp.where(i == n_blocks - 1, tb + A_BAND, nk)
    bias = bias_ref[...] + jnp.where((colk < first_col) | (colk >= end_col), NEG, 0.0)
    lane = lax.broadcasted_iota(jnp.int32, (1, LANE), 1)
    heads = [slice(j * LANE, (j + 1) * LANE) for j in range(A_HEADS)]
    band = lambda p_ref, c_ref, n_ref, sl: jnp.concatenate([p_ref[:, sl], c_ref[:, sl], n_ref[:, sl]], axis=0)
    scores = [_dot_nt(q_ref[:, sl], band(kp_ref, kc_ref, kn_ref, sl)) for sl in heads]
    stats = []
    for s in scores:
        s = s + bias
        m = jnp.max(s, axis=-1, keepdims=True)
        p = jnp.exp2(s - m)
        stats.append((m, jnp.sum(p, axis=-1, keepdims=True), p.astype(BF16)))
    for sl, (m, l, p) in zip(heads, stats):
        o = _dot(p, band(vp_ref, vc_ref, vn_ref, sl)) / l
        o_ref[:, sl] = jnp.where(lane >= A_HEAD_DIM, m * LN2 + jnp.log(l), o)


def _band_attention(q, k, v, dilation):
    bn, length, _ = q.shape
    tb = min(256, length)
    nb = length // tb
    per = tb // A_BAND
    cur = pl.BlockSpec((None, tb, A_SLAB), lambda b, r, i: (b, i, r))
    prv = pl.BlockSpec((None, A_BAND, A_SLAB), lambda b, r, i: (b, jnp.maximum(i * per - 1, 0), r))
    nxt = pl.BlockSpec((None, A_BAND, A_SLAB), lambda b, r, i: (b, jnp.minimum((i + 1) * per, nb * per - 1), r))
    nk = tb + 2 * A_BAND
    dist = np.arange(nk)[None, :] - A_BAND - np.arange(tb)[:, None]
    bias = np.where(np.abs(dist) <= A_BAND, 0.0, NEG).astype(np.float32)
    kern = functools.partial(_band_attn_kernel, tb=tb, n_blocks=nb)
    return pl.pallas_call(
        kern,
        grid=(bn, dilation, nb),
        in_specs=[cur, prv, cur, nxt, prv, cur, nxt, pl.BlockSpec((tb, nk), lambda b, r, i: (0, 0))],
        out_specs=cur,
        out_shape=jax.ShapeDtypeStruct((bn, length, dilation * A_SLAB), F32),
        compiler_params=_cparams(3),
        name=f"band_attn_d{dilation}",
    )(q, k, k, k, v, v, v, bias)


def _stacked_flash(qt_ref, k_ref, vt_ref, acc_sc, s_sc, *, tq, n_chunks, v_heads):
    dv = LANE // v_heads
    v_rows = [slice(hd * (dv + ONES_ROWS), (hd + 1) * (dv + ONES_ROWS)) for hd in range(v_heads)]
    col_groups = [slice(0, 2 * tq)] if v_heads == 1 else [slice(0, tq), slice(tq, 2 * tq)]
    first = _first_head(lax.broadcasted_iota(jnp.int32, (LANE, 1), 0))
    qt = qt_ref[...]
    zero = jnp.zeros_like(qt)
    q2t = jnp.concatenate([jnp.where(first, qt, zero), jnp.where(first, zero, qt)], axis=1)
    acc_sc[...] = jnp.zeros(acc_sc.shape, F32)

    parts = [slice(lo, lo + KEY_PART) for lo in range(0, FLASH_TK, KEY_PART)]

    def scores(c, slot, rows):
        start = pl.multiple_of(c * FLASH_TK + rows.start, KEY_PART)
        s_sc[slot, rows, :] = _dot(k_ref[pl.ds(start, KEY_PART), :], q2t)

    def step(c, slot, m_prev, c_next=None):
        m_next = jnp.maximum(m_prev, jnp.max(s_sc[slot], axis=0, keepdims=True))
        vc = vt_ref[c]
        alpha = jnp.exp2(m_prev - m_next)
        pv = [None] * v_heads
        for rows in parts:
            if c_next is not None:
                scores(c_next, 1 - slot, rows)
            pt = jnp.exp2(s_sc[slot, rows, :] - m_next).astype(BF16)
            for g, (vr, cols) in enumerate(zip(v_rows, col_groups)):
                part = _dot(vc[vr, rows], pt[:, cols])
                pv[g] = part if pv[g] is None else pv[g] + part
        for cols, total in zip(col_groups, pv):
            acc_sc[:, cols] = alpha[:, cols] * acc_sc[:, cols] + total
        return m_next

    unroll = min(FLASH_UNROLL, n_chunks)

    def group(j, m):
        for u in range(unroll):
            c = unroll * j + u
            m = step(c, u % SCORE_SLOTS, m, c + 1)
        return m

    for rows in parts:
        scores(0, 0, rows)
    m = lax.fori_loop(0, n_chunks // unroll - 1, group, jnp.full((1, 2 * tq), -jnp.inf, F32))
    for c in range(n_chunks - unroll, n_chunks):
        m = step(c, c % SCORE_SLOTS, m, c + 1 if c + 1 < n_chunks else None)
    acc = acc_sc[...]
    return acc[:dv] / acc[dv:dv + 1]


def _gqa_kernel(qt_ref, k_ref, vt_ref, o_ref, acc_sc, s_sc, *, tq, n_chunks):
    o = _stacked_flash(qt_ref, k_ref, vt_ref, acc_sc, s_sc, tq=tq, n_chunks=n_chunks, v_heads=B_KV_HEADS)
    o_ref[...] = jnp.concatenate([o[:, :tq], o[:, tq:]], axis=0).T


def _diff_kernel(qt_ref, k_ref, vt_ref, lq1_ref, lk1_ref, lq2_ref, lk2_ref, sub_ref, o_ref, acc_sc,
                 s_sc, *, tq, n_chunks, lam_init):
    o = _stacked_flash(qt_ref, k_ref, vt_ref, acc_sc, s_sc, tq=tq, n_chunks=n_chunks, v_heads=1)
    lam = (jnp.exp(jnp.sum(lq1_ref[...] * lk1_ref[...], axis=-1, keepdims=True))
           - jnp.exp(jnp.sum(lq2_ref[...] * lk2_ref[...], axis=-1, keepdims=True)) + lam_init)
    oc = (o[:, :tq] - lam * o[:, tq:]).T
    ms = jnp.mean(oc * oc, axis=-1, keepdims=True)
    o_ref[...] = oc * lax.rsqrt(ms + EPS) * sub_ref[...] * (1.0 - lam_init)


def _flash_tq(s):
    return min(512, s)


def _flash_scratch(tq, v_heads):
    return [pltpu.VMEM((LANE // v_heads + ONES_ROWS, 2 * tq), F32),
            pltpu.VMEM((SCORE_SLOTS, FLASH_TK, 2 * tq), F32)]


def _gqa_attention(qt, k, vt):
    bn, _, s = qt.shape
    tq, n_chunks = _flash_tq(s), s // FLASH_TK
    kern = functools.partial(_gqa_kernel, tq=tq, n_chunks=n_chunks)
    return pl.pallas_call(
        kern,
        grid=(bn, B_HEADS // 2, s // tq),
        in_specs=[pl.BlockSpec((None, LANE, tq), lambda b, p, i: (b, p, i)),
                  pl.BlockSpec((None, s, LANE), lambda b, p, i: (b, 0, 0)),
                  pl.BlockSpec((None, n_chunks, _vt_rows(B_KV_HEADS), FLASH_TK), lambda b, p, i: (b, 0, 0, 0))],
        out_specs=pl.BlockSpec((None, tq, LANE), lambda b, p, i: (b, i, p)),
        out_shape=jax.ShapeDtypeStruct((bn, s, B_WIDTH), F32),
        scratch_shapes=_flash_scratch(tq, B_KV_HEADS),
        compiler_params=_cparams(3),
        name="gqa_attn",
    )(qt, k, vt)


def _diff_attention(qt, k, vt, lam_rows, subln, lam_init):
    bn, _, s = qt.shape
    tq, n_chunks = _flash_tq(s), s // FLASH_TK
    kern = functools.partial(_diff_kernel, tq=tq, n_chunks=n_chunks, lam_init=lam_init)
    row = lambda n: pl.BlockSpec((1, n), lambda b, h, i: (0, 0))
    return pl.pallas_call(
        kern,
        grid=(bn, C_HEADS, s // tq),
        in_specs=[pl.BlockSpec((None, LANE, tq), lambda b, h, i: (b, h, i)),
                  pl.BlockSpec((None, s, LANE), lambda b, h, i: (b, 0, h)),
                  pl.BlockSpec((None, n_chunks, _vt_rows(1), FLASH_TK), lambda b, h, i: (b, 0, h, 0)),
                  row(C_HEAD_DIM), row(C_HEAD_DIM), row(C_HEAD_DIM), row(C_HEAD_DIM), row(LANE)],
        out_specs=pl.BlockSpec((None, tq, LANE), lambda b, h, i: (b, i, h)),
        out_shape=jax.ShapeDtypeStruct((bn, s, C_WIDTH), F32),
        scratch_shapes=_flash_scratch(tq, 1),
        compiler_params=_cparams(3),
        name="diff_attn",
    )(qt, k, vt, *lam_rows, subln)


def _final_kernel(x_ref, shift_ref, scale_ref, gate_ref, g_ref, oa0_ref, oa1_ref, oa2_ref, yb_ref, yc_ref,
                  wz_ref, wbg_ref, bbg_ref, woa_ref, wob_ref, woc_ref, wout_ref, o_ref, *stage_refs):
    x = x_ref[...]
    tm = x.shape[0]
    h = _modulated_norm(x, g_ref[...], scale_ref[...], shift_ref[...]).astype(BF16)
    lane = lax.broadcasted_iota(jnp.int32, (1, LANE), 1)

    for oa_ref, st_ref, (_, dil) in zip((oa1_ref, oa2_ref), stage_refs, A_GROUPS[1:]):
        for r in range(dil):
            for j in range(A_HEADS):
                lo = r * A_SLAB + j * LANE
                st_ref[j, pl.ds(r, tm // dil, stride=dil), :] = oa_ref[:, lo:lo + LANE]

    ya = []
    for j in range(A_HEADS):
        slabs = [oa0_ref[:, j * LANE:(j + 1) * LANE]] + [st_ref[j] for st_ref in stage_refs]
        lses = [t[:, A_HEAD_DIM:A_HEAD_DIM + 1] for t in slabs]
        top = jnp.maximum(jnp.maximum(lses[0], lses[1]), lses[2])
        ws = [jnp.exp(t - top) for t in lses]
        mix = (ws[0] * slabs[0] + ws[1] * slabs[1] + ws[2] * slabs[2]) / (ws[0] + ws[1] + ws[2])
        ya.append(jnp.where(lane < A_HEAD_DIM, mix, 0.0))
    ya = jnp.concatenate(ya, axis=-1)

    def branch(y, z_lo, z_hi, wo_ref, g_lo):
        z = _dot(h, wz_ref[:, z_lo:z_hi])
        p = _dot((y * _silu(z)).astype(BF16), wo_ref[...])
        gate = jax.nn.sigmoid(_dot(h, wbg_ref[:, g_lo:g_lo + D_MODEL]) + bbg_ref[:, g_lo:g_lo + D_MODEL])
        return gate * p

    za_hi = A_SLAB
    zb_hi = za_hi + B_WIDTH
    zc_hi = zb_hi + C_WIDTH
    merged = (branch(ya, 0, za_hi, woa_ref, 0)
              + branch(yb_ref[...], za_hi, zb_hi, wob_ref, D_MODEL)
              + branch(yc_ref[...], zb_hi, zc_hi, woc_ref, 2 * D_MODEL))
    out = _dot(merged.astype(BF16), wout_ref[...])
    o_ref[...] = x + gate_ref[...] * out


def _final(x, shift, scale, gate, norm_g, oa, yb, yc, wz, wbg, bbg, woa, wob, woc, wout, *, tm):
    bn, s, _ = x.shape
    tok = lambda w: pl.BlockSpec((None, tm, w), lambda b, i: (b, i, 0))
    vec = pl.BlockSpec((None, 1, D_MODEL), lambda b, i: (b, 0, 0))
    full = lambda a: pl.BlockSpec(a.shape, lambda b, i: (0, 0), pipeline_mode=pl.Buffered(1))
    band = [pl.BlockSpec((None, tm // dil, dil * A_SLAB), lambda b, i: (b, i, 0)) for _, dil in A_GROUPS]
    return pl.pallas_call(
        _final_kernel,
        grid=(bn, s // tm),
        in_specs=[tok(D_MODEL), vec, vec, vec, full(norm_g), *band,
                  tok(B_WIDTH), tok(C_WIDTH), full(wz), full(wbg), full(bbg), full(woa), full(wob), full(woc),
                  full(wout)],
        out_specs=tok(D_MODEL),
        out_shape=jax.ShapeDtypeStruct((bn, s, D_MODEL), F32),
        scratch_shapes=[pltpu.VMEM((A_HEADS, tm, LANE), F32) for _ in A_GROUPS[1:]],
        compiler_params=_cparams(2),
        name="final",
    )(x, shift, scale, gate, norm_g, *oa, yb, yc, wz, wbg, bbg, woa, wob, woc, wout)


B_Q_ORDER = (0, 3, 1, 4, 2, 5)


def _slab_layout(head_dim, n_heads, rot_groups):
    src = -np.ones((LANE,), np.int32)
    group = -np.ones((LANE,), np.int32)
    freq = np.zeros((LANE,), np.int32)
    sign = np.zeros((LANE,), np.float32)
    width = ROT_SHIFT // n_heads
    for hd in range(n_heads):
        low = [(start + i, g, i, -1.0) for g, (start, half) in enumerate(rot_groups) for i in range(half)]
        high = [(start + half + i, g, i, 1.0) for g, (start, half) in enumerate(rot_groups) for i in range(half)]
        rotary = {d for d, _, _, _ in low + high}
        rest = [(d, -1, 0, 0.0) for d in range(head_dim) if d not in rotary]
        n_low_rest = width - len(low)
        low, high = low + rest[:n_low_rest], high + rest[n_low_rest:]
        assert len(low) == width and len(high) <= width
        for base, items in ((hd * width, low), (hd * width + ROT_SHIFT, high)):
            for lane, (d, g, i, sg) in enumerate(items, start=base):
                src[lane], group[lane], freq[lane], sign[lane] = hd * head_dim + d, g, i, sg
    return src, group, freq, sign


LAYOUT_A = _slab_layout(A_HEAD_DIM, 1, ((0, A_ROT // 2),))
LAYOUT_B = _slab_layout(B_HEAD_DIM, 2, ((0, B_HEAD_DIM // 4), (B_HEAD_DIM // 2, B_HEAD_DIM // 4)))
LAYOUT_C = _slab_layout(C_HEAD_DIM, 2, ((0, C_ROT // 2),))


def _to_slabs(w, layout, dims_per_slab):
    src = layout[0]
    n_slabs = w.shape[-1] // dims_per_slab
    idx = np.concatenate([np.maximum(src, 0) + s * dims_per_slab for s in range(n_slabs)])
    keep = np.tile(src >= 0, n_slabs)
    return jnp.where(keep, jnp.take(w, idx, axis=-1), 0.0)


def _pad_heads(w, n_heads):
    lead = w.shape[:-1]
    w = w.reshape(*lead, n_heads, A_HEAD_DIM)
    w = jnp.pad(w, [(0, 0)] * len(lead) + [(0, 0), (0, LANE - A_HEAD_DIM)])
    return w.reshape(*lead, n_heads * LANE)


def _permute_heads(w, order, dim):
    lead = w.shape[:-1]
    w = w.reshape(*lead, len(order), dim)
    return w[..., jnp.array(order), :].reshape(*lead, len(order) * dim)


def _prep_layer(w_in, qn_a, kn_a, qn_b, kn_b, qn_c, kn_c, w_oa, w_ob, w_oc):
    offs = [0]
    for n in IN_SIZES:
        offs.append(offs[-1] + n)
    qa, ka, va, za, qb, kb, vb, zb, qc, kc, vc, zc = [w_in[:, offs[i]:offs[i + 1]] for i in range(len(IN_SIZES))]

    slab_a = functools.partial(_to_slabs, layout=LAYOUT_A, dims_per_slab=A_HEAD_DIM)
    slab_b = functools.partial(_to_slabs, layout=LAYOUT_B, dims_per_slab=2 * B_HEAD_DIM)
    slab_c = functools.partial(_to_slabs, layout=LAYOUT_C, dims_per_slab=2 * C_HEAD_DIM)
    qb = _permute_heads(qb, B_Q_ORDER, B_HEAD_DIM)

    w_a = jnp.concatenate([slab_a(qa), slab_a(ka), _pad_heads(va, N_A)], axis=-1).astype(BF16)
    gain_a = jnp.concatenate([slab_a(jnp.tile(qn_a, N_A)), slab_a(jnp.tile(kn_a, N_A)),
                              jnp.zeros((N_A * LANE,), F32)])[None, :]

    w_b = jnp.concatenate([slab_b(qb), slab_b(kb), vb], axis=-1).astype(BF16)
    gain_b = jnp.concatenate([slab_b(jnp.tile(qn_b, B_HEADS)), slab_b(jnp.tile(kn_b, B_KV_HEADS)),
                              jnp.zeros((B_KV,), F32)])[None, :]

    w_c = jnp.concatenate([slab_c(qc), slab_c(kc), vc], axis=-1).astype(BF16)
    gain_c = jnp.concatenate([slab_c(jnp.tile(qn_c, 2 * C_HEADS)), slab_c(jnp.tile(kn_c, 2 * C_HEADS)),
                              jnp.zeros((C_WIDTH,), F32)])[None, :]

    w_z = jnp.concatenate([_pad_heads(za, A_HEADS), _permute_heads(zb, B_Q_ORDER, B_HEAD_DIM), zc],
                          axis=-1).astype(BF16)
    w_oa_p = _pad_heads(w_oa.T, A_HEADS).T.astype(BF16)
    w_ob_p = _permute_heads(w_ob.T, B_Q_ORDER, B_HEAD_DIM).T.astype(BF16)
    return dict(w_a=w_a, gain_a=gain_a, w_b=w_b, gain_b=gain_b, w_c=w_c, gain_c=gain_c, w_z=w_z,
                w_oa=w_oa_p, w_ob=w_ob_p, w_oc=w_oc.astype(BF16))


def _inv_freq(dim, theta):
    return theta ** (-jnp.arange(0, dim, 2, dtype=F32) / dim)


def _rope_tables(layout, inv_freqs, streams):
    _, group, freq, sign = layout
    ang = 0.0
    for g, (inv, pos) in enumerate(zip(inv_freqs, streams)):
        ang = ang + pos[:, None] * (inv[freq] * (group == g))[None, :]
    return jnp.cos(ang), jnp.sin(ang) * sign[None, :]


def _tables(s):
    pos = jnp.arange(s)
    tok, row, col = [t.astype(F32) for t in (pos, pos // GRID_W, pos % GRID_W)]
    inv_b = _inv_freq(B_HEAD_DIM // 2, AXIAL_THETA)
    tab_a = _rope_tables(LAYOUT_A, [_inv_freq(A_ROT, ROPE_THETA)], [tok])
    tab_b = _rope_tables(LAYOUT_B, [inv_b, inv_b], [row, col])
    tab_c = _rope_tables(LAYOUT_C, [_inv_freq(C_ROT, ROPE_THETA)], [tok])
    return tab_a, tab_b, tab_c


def _layer(x, mod, l, lp, tabs, norm_g, lam_rows, subln, w_bg, b_bg, w_out):
    bn, s, _ = x.shape
    shift, scale, gate = [m.reshape(bn, 1, D_MODEL) for m in jnp.split(mod, 3, axis=-1)]
    tab_a, tab_b, tab_c = tabs
    lam_init = 0.8 - 0.6 * math.exp(-0.3 * l)

    a_outs = tuple((A_SLAB, kind, dil) for kind in ("q", "k", "v") for _, dil in A_GROUPS)
    pa = _project(x, shift, scale, norm_g, lp["w_a"], lp["gain_a"], tab_a, a_outs, two_heads=False,
                  head_dim=A_HEAD_DIM, q_scale=A_HEAD_DIM ** -0.5 * LOG2E, tm=min(PROJ_A_TM, s), name="proj_a")
    n_g = len(A_GROUPS)
    oa = [_band_attention(pa[g], pa[n_g + g], pa[2 * n_g + g], dil) for g, (_, dil) in enumerate(A_GROUPS)]

    qb, kb, vb = _project(x, shift, scale, norm_g, lp["w_b"], lp["gain_b"], tab_b,
                          ((B_WIDTH, "qT", 1), (B_KV, "k", 1), (B_KV, "vT", B_KV_HEADS)), two_heads=True,
                          head_dim=B_HEAD_DIM, q_scale=B_HEAD_DIM ** -0.5 * LOG2E, tm=min(PROJ_TM, s), name="proj_b")
    yb = _gqa_attention(qb, kb, vb)

    qc, kc, vc = _project(x, shift, scale, norm_g, lp["w_c"], lp["gain_c"], tab_c,
                          ((C_WIDTH, "qT", 1), (C_WIDTH, "k", 1), (C_WIDTH, "vT", 1)), two_heads=True,
                          head_dim=C_HEAD_DIM, q_scale=C_HEAD_DIM ** -0.5 * LOG2E, tm=min(PROJ_TM, s), name="proj_c")
    yc = _diff_attention(qc, kc, vc, lam_rows, subln, lam_init)

    return _final(x, shift, scale, gate, norm_g, oa, yb, yc, lp["w_z"], w_bg, b_bg, lp["w_oa"], lp["w_ob"],
                  lp["w_oc"], w_out, tm=min(FINAL_TM, s))


def kernel(x_prompt, x_sample, c_prompt, c_sample, norm_g, w_ada, b_ada, w_in, qn_a, kn_a, qn_b, kn_b, qn_c, kn_c,
           lam_q1, lam_k1, lam_q2, lam_k2, subln_c, w_oa, w_ob, w_oc, w_bg, b_bg, w_out):
    groups = ((x_prompt, c_prompt), (x_sample, c_sample))
    rows = [c.shape[0] for _, c in groups]
    pad = -sum(rows) % 8
    c_all = jnp.concatenate([c for _, c in groups] + [jnp.zeros((pad, D_MODEL), F32)], axis=0)
    mod_all = _ada_mod(c_all, w_ada.astype(BF16), b_ada)

    layers = [_prep_layer(w_in[l], qn_a[l], kn_a[l], qn_b[l], kn_b[l], qn_c[l], kn_c[l], w_oa[l], w_ob[l], w_oc[l])
              for l in range(DEPTH)]
    w_bg16, w_out16 = w_bg.astype(BF16), w_out.astype(BF16)

    outs = []
    row0 = 0
    for (x, c), n in zip(groups, rows):
        tabs = _tables(x.shape[1])
        for l in range(DEPTH):
            lam_rows = [p[l][None, :] for p in (lam_q1, lam_k1, lam_q2, lam_k2)]
            x = _layer(x, mod_all[l, row0:row0 + n], l, layers[l], tabs, norm_g[l][None, :], lam_rows,
                       subln_c[l][None, :], w_bg16[l], b_bg[l][None, :], w_out16[l])
        outs.append(x)
        row0 += n
    return tuple(outs)
```

```python
import functools
import math

import jax
import jax.numpy as jnp
import numpy as np
from jax import lax
from jax.experimental import pallas as pl
from jax.experimental.pallas import tpu as pltpu

D_MODEL = 1024
DEPTH = 2
GRID_W = 64
EPS = 1e-6
NEG = -1e30
ROPE_THETA = 500000.0
AXIAL_THETA = 10000.0

A_GROUPS = ((128, 1), (512, 4), (2048, 16))
A_HEADS = 4
A_HEAD_DIM = 96
A_ROT = A_HEAD_DIM // 4
A_BAND = 64
N_A = len(A_GROUPS) * A_HEADS

B_HEADS = 6
B_KV_HEADS = 2
B_HEAD_DIM = 64
C_HEADS = 4
C_HEAD_DIM = 64
C_ROT = C_HEAD_DIM // 4

A_QKV = N_A * A_HEAD_DIM
A_WIDTH = A_HEADS * A_HEAD_DIM
B_WIDTH = B_HEADS * B_HEAD_DIM
B_KV = B_KV_HEADS * B_HEAD_DIM
C_WIDTH = C_HEADS * 2 * C_HEAD_DIM
IN_SIZES = (A_QKV, A_QKV, A_QKV, A_WIDTH, B_WIDTH, B_KV, B_KV, B_WIDTH, C_WIDTH, C_WIDTH, C_WIDTH, C_WIDTH)

LANE = 128
HALF = LANE // 2
ROT_SHIFT = LANE // 2
A_SLAB = A_HEADS * LANE
BF16_SUBLANES = 16
ONES_ROWS = BF16_SUBLANES
FLASH_TK = 512
KEY_PART = 512
SCORE_SLOTS = 2
FLASH_UNROLL = 4
PROJ_TM = 512
PROJ_A_TM = 256
FINAL_TM = 512
BAND_TB = 512
LOG2E = math.log2(math.e)
LN2 = math.log(2.0)
VMEM_LIMIT = 48 * 1024 * 1024

BF16 = jnp.bfloat16
F32 = jnp.float32


def _cparams(n_axes):
    return pltpu.CompilerParams(dimension_semantics=("arbitrary",) * n_axes, vmem_limit_bytes=VMEM_LIMIT)


def _dot(a, b):
    return jnp.dot(a, b, preferred_element_type=F32)


def _dot_nt(a, b):
    return lax.dot_general(a, b, (((1,), (1,)), ((), ())), preferred_element_type=F32)


def _silu(x):
    return x * jax.nn.sigmoid(x)


def _ada_kernel(c_ref, w_ref, b_ref, o_ref):
    c = c_ref[...]
    o_ref[...] = _dot(_silu(c).astype(BF16), w_ref[...]) + b_ref[...]


def _ada_mod(c_all, w_ada, b_ada):
    rows = c_all.shape[0]
    return pl.pallas_call(
        _ada_kernel,
        grid=(DEPTH, 3),
        in_specs=[
            pl.BlockSpec((rows, D_MODEL), lambda l, n: (0, 0)),
            pl.BlockSpec((None, D_MODEL, D_MODEL), lambda l, n: (l, 0, n)),
            pl.BlockSpec((None, 1, D_MODEL), lambda l, n: (l, 0, n)),
        ],
        out_specs=pl.BlockSpec((None, rows, D_MODEL), lambda l, n: (l, 0, n)),
        out_shape=jax.ShapeDtypeStruct((DEPTH, rows, 3 * D_MODEL), F32),
        compiler_params=_cparams(2),
        name="ada_mod",
    )(c_all, w_ada, b_ada.reshape(DEPTH, 1, 3 * D_MODEL))


def _modulated_norm(x, g, scale, shift):
    ms = jnp.mean(x * x, axis=-1, keepdims=True)
    h = x * lax.rsqrt(ms + EPS) * g
    return h * (1.0 + scale) + shift


def _first_head(index):
    return (index & (HALF - 1)) < HALF // 2


def _head_norm_rope(u, gain, cos, sin, two_heads, head_dim):
    sq = u * u
    if two_heads:
        first = _first_head(lax.broadcasted_iota(jnp.int32, (1, LANE), 1))
        s_first = jnp.sum(jnp.where(first, sq, 0.0), axis=-1, keepdims=True)
        s_all = jnp.sum(sq, axis=-1, keepdims=True)
        ms = jnp.where(first, s_first, s_all - s_first) * (1.0 / head_dim)
    else:
        ms = jnp.sum(sq, axis=-1, keepdims=True) * (1.0 / head_dim)
    y = u * lax.rsqrt(ms + EPS) * gain
    return y * cos + pltpu.roll(y, ROT_SHIFT, 1) * sin


def _proj_kernel(x_ref, shift_ref, scale_ref, g_ref, w_ref, gain_ref, cos_ref, sin_ref, *refs,
                 outs, two_heads, head_dim, q_scale):
    out_refs, stage_ref = refs[:len(outs)], refs[len(outs):]
    h = _modulated_norm(x_ref[...], g_ref[...], scale_ref[...], shift_ref[...]).astype(BF16)
    cos, sin = cos_ref[...], sin_ref[...]
    tm = x_ref.shape[0]
    col = 0
    for o_ref, (width, kind, dil) in zip(out_refs, outs):
        u = _dot(h, w_ref[:, col:col + width])
        if kind == "vT":
            dv = LANE // dil
            for s in range(width // LANE):
                ut = u[:, s * LANE:(s + 1) * LANE].T.astype(BF16)
                for hd in range(dil):
                    base = (s * dil + hd) * (dv + ONES_ROWS)
                    o_ref[base:base + dv, :] = ut[hd * dv:(hd + 1) * dv]
                    o_ref[base + dv:base + dv + ONES_ROWS, :] = jnp.ones((ONES_ROWS, tm), BF16)
        else:
            n_slabs = width // LANE
            n_rope = {"v": 0, "kv": n_slabs // 2}.get(kind, n_slabs)
            if n_rope:
                slabs = []
                for s in range(n_rope):
                    lo = s * LANE
                    y = _head_norm_rope(u[:, lo:lo + LANE], gain_ref[:, col + lo:col + lo + LANE], cos, sin,
                                        two_heads, head_dim)
                    slabs.append(y * q_scale if kind in ("q", "qT") and q_scale != 1.0 else y)
                plain = [u[:, n_rope * LANE:]] if n_rope < n_slabs else []
                u = jnp.concatenate(slabs + plain, axis=-1)
            if kind == "qT":
                for s in range(width // LANE):
                    o_ref[s * LANE:(s + 1) * LANE, :] = u[:, s * LANE:(s + 1) * LANE].T.astype(BF16)
            elif dil == 1:
                o_ref[...] = u.astype(BF16)
            else:
                stage = stage_ref[0]
                for s in range(width // LANE):
                    stage[s] = u[:, s * LANE:(s + 1) * LANE]
                rows = lambda r: pl.ds(r, tm // dil, stride=dil)
                for r in range(dil):
                    for s in range(width // LANE):
                        lo = r * width + s * LANE
                        o_ref[:, lo:lo + LANE] = stage[s, rows(r), :].astype(BF16)
        col += width


def _vt_rows(v_heads):
    return LANE + v_heads * ONES_ROWS


def _proj_out(kind, width, dil, bn, s, tm):
    if kind == "qT":
        return (bn, width, s), pl.BlockSpec((None, width, tm), lambda b, i: (b, 0, i))
    if kind == "vT":
        per = FLASH_TK // tm
        rows = width // LANE * _vt_rows(dil)
        return ((bn, s // FLASH_TK, rows, FLASH_TK),
                pl.BlockSpec((None, None, rows, tm), lambda b, i: (b, i // per, 0, i % per)))
    return (bn, s // dil, dil * width), pl.BlockSpec((None, tm // dil, dil * width), lambda b, i: (b, i, 0))


def _project(x, shift, scale, norm_g, w, gain, tables, outs, *, two_heads, head_dim, q_scale, tm, name):
    bn, s, _ = x.shape
    wcols = w.shape[1]
    kern = functools.partial(_proj_kernel, outs=outs, two_heads=two_heads, head_dim=head_dim, q_scale=q_scale)
    vec = pl.BlockSpec((None, 1, D_MODEL), lambda b, i: (b, 0, 0))
    tab = pl.BlockSpec((tm, LANE), lambda b, i: (i, 0))
    shapes, specs = zip(*[_proj_out(kind, width, dil, bn, s, tm) for width, kind, dil in outs])
    n_slabs = max(width for width, _, _ in outs) // LANE
    dilated = any(d > 1 and kind != "vT" for _, kind, d in outs)
    stage = [pltpu.VMEM((n_slabs, tm, LANE), F32)] if dilated else []
    return pl.pallas_call(
        kern,
        grid=(bn, s // tm),
        in_specs=[
            pl.BlockSpec((None, tm, D_MODEL), lambda b, i: (b, i, 0)),
            vec, vec,
            pl.BlockSpec((1, D_MODEL), lambda b, i: (0, 0)),
            pl.BlockSpec((D_MODEL, wcols), lambda b, i: (0, 0)),
            pl.BlockSpec((1, wcols), lambda b, i: (0, 0)),
            tab, tab,
        ],
        out_specs=list(specs),
        out_shape=[jax.ShapeDtypeStruct(shape, BF16) for shape in shapes],
        scratch_shapes=stage,
        compiler_params=_cparams(2),
        name=name,
    )(x, shift, scale, norm_g, w, gain, *tables)


def _band_attn_kernel(q_ref, kvp_ref, kvc_ref, kvn_ref, bias_ref, o_ref, *, tb, n_blocks):
    i = pl.program_id(2)
    nk = tb + 2 * A_BAND
    colk = lax.broadcasted_iota(jnp.int32, (1, nk), 1)
    first_col = jnp.where(i == 0, A_BAND, 0)
    end_col = jnp.where(i == n_blocks - 1, tb + A_BAND, nk)
    bias = bias_ref[...] + jnp.where((colk < first_col) | (colk >= end_col), NEG, 0.0)
    lane = lax.broadcasted_iota(jnp.int32, (1, LANE), 1)
    heads = [slice(j * LANE, (j + 1) * LANE) for j in range(A_HEADS)]
    band = lambda lo, sl: jnp.concatenate([r[:, lo + sl.start:lo + sl.stop] for r in (kvp_ref, kvc_ref, kvn_ref)],
                                          axis=0)
    scores = [_dot_nt(q_ref[:, sl], band(0, sl)) for sl in heads]
    stats = []
    for s in scores:
        s = s + bias
        m = jnp.max(s, axis=-1, keepdims=True)
        p = jnp.exp2(s - m)
        stats.append((m, jnp.sum(p, axis=-1, keepdims=True), p.astype(BF16)))
    for sl, (m, l, p) in zip(heads, stats):
        o = _dot(p, band(A_SLAB, sl)) / l
        o_ref[:, sl] = jnp.where(lane >= A_HEAD_DIM, m * LN2 + jnp.log(l), o)


def _band_attention(q, kv, dilation):
    bn, length, _ = q.shape
    tb = min(BAND_TB, length)
    nb = length // tb
    per = tb // A_BAND
    cur = pl.BlockSpec((None, tb, A_SLAB), lambda b, r, i: (b, i, r))
    kv_cur = pl.BlockSpec((None, tb, 2 * A_SLAB), lambda b, r, i: (b, i, r))
    prv = pl.BlockSpec((None, A_BAND, 2 * A_SLAB), lambda b, r, i: (b, jnp.maximum(i * per - 1, 0), r))
    nxt = pl.BlockSpec((None, A_BAND, 2 * A_SLAB), lambda b, r, i: (b, jnp.minimum((i + 1) * per, nb * per - 1), r))
    nk = tb + 2 * A_BAND
    dist = np.arange(nk)[None, :] - A_BAND - np.arange(tb)[:, None]
    bias = np.where(np.abs(dist) <= A_BAND, 0.0, NEG).astype(np.float32)
    kern = functools.partial(_band_attn_kernel, tb=tb, n_blocks=nb)
    return pl.pallas_call(
        kern,
        grid=(bn, dilation, nb),
        in_specs=[cur, prv, kv_cur, nxt, pl.BlockSpec((tb, nk), lambda b, r, i: (0, 0))],
        out_specs=cur,
        out_shape=jax.ShapeDtypeStruct((bn, length, dilation * A_SLAB), F32),
        compiler_params=_cparams(3),
        name=f"band_attn_d{dilation}",
    )(q, kv, kv, kv, bias)


def _stacked_flash(qt_ref, k_ref, vt_ref, acc_sc, s_sc, *, tq, n_chunks, v_heads):
    dv = LANE // v_heads
    v_rows = [slice(hd * (dv + ONES_ROWS), (hd + 1) * (dv + ONES_ROWS)) for hd in range(v_heads)]
    col_groups = [slice(0, 2 * tq)] if v_heads == 1 else [slice(0, tq), slice(tq, 2 * tq)]
    first = _first_head(lax.broadcasted_iota(jnp.int32, (LANE, 1), 0))
    qt = qt_ref[...]
    zero = jnp.zeros_like(qt)
    q2t = jnp.concatenate([jnp.where(first, qt, zero), jnp.where(first, zero, qt)], axis=1)
    acc_sc[...] = jnp.zeros(acc_sc.shape, F32)

    parts = [slice(lo, lo + KEY_PART) for lo in range(0, FLASH_TK, KEY_PART)]

    def scores(c, slot, rows):
        start = pl.multiple_of(c * FLASH_TK + rows.start, KEY_PART)
        s_sc[slot, rows, :] = _dot(k_ref[pl.ds(start, KEY_PART), :], q2t)

    def step(c, slot, m_prev, c_next=None):
        m_next = jnp.maximum(m_prev, jnp.max(s_sc[slot], axis=0, keepdims=True))
        vc = vt_ref[c]
        alpha = jnp.exp2(m_prev - m_next)
        pv = [None] * v_heads
        for rows in parts:
            if c_next is not None:
                scores(c_next, 1 - slot, rows)
            pt = jnp.exp2(s_sc[slot, rows, :] - m_next).astype(BF16)
            for g, (vr, cols) in enumerate(zip(v_rows, col_groups)):
                part = _dot(vc[vr, rows], pt[:, cols])
                pv[g] = part if pv[g] is None else pv[g] + part
        for cols, total in zip(col_groups, pv):
            acc_sc[:, cols] = alpha[:, cols] * acc_sc[:, cols] + total
        return m_next

    unroll = min(FLASH_UNROLL, n_chunks)

    def group(j, m):
        for u in range(unroll):
            c = unroll * j + u
            m = step(c, u % SCORE_SLOTS, m, c + 1)
        return m

    for rows in parts:
        scores(0, 0, rows)
    m = lax.fori_loop(0, n_chunks // unroll - 1, group, jnp.full((1, 2 * tq), -jnp.inf, F32))
    for c in range(n_chunks - unroll, n_chunks):
        m = step(c, c % SCORE_SLOTS, m, c + 1 if c + 1 < n_chunks else None)
    acc = acc_sc[...]
    return acc[:dv] / acc[dv:dv + 1]


def _gqa_kernel(qt_ref, k_ref, vt_ref, o_ref, acc_sc, s_sc, *, tq, n_chunks):
    o = _stacked_flash(qt_ref, k_ref, vt_ref, acc_sc, s_sc, tq=tq, n_chunks=n_chunks, v_heads=B_KV_HEADS)
    o_ref[...] = jnp.concatenate([o[:, :tq], o[:, tq:]], axis=0).T


def _diff_kernel(qt_ref, k_ref, vt_ref, lq1_ref, lk1_ref, lq2_ref, lk2_ref, sub_ref, o_ref, acc_sc,
                 s_sc, *, tq, n_chunks, lam_init):
    o = _stacked_flash(qt_ref, k_ref, vt_ref, acc_sc, s_sc, tq=tq, n_chunks=n_chunks, v_heads=1)
    lam = (jnp.exp(jnp.sum(lq1_ref[...] * lk1_ref[...], axis=-1, keepdims=True))
           - jnp.exp(jnp.sum(lq2_ref[...] * lk2_ref[...], axis=-1, keepdims=True)) + lam_init)
    oc = (o[:, :tq] - lam * o[:, tq:]).T
    ms = jnp.mean(oc * oc, axis=-1, keepdims=True)
    o_ref[...] = oc * lax.rsqrt(ms + EPS) * sub_ref[...] * (1.0 - lam_init)


def _flash_tq(s):
    return min(512, s)


def _flash_scratch(tq, v_heads):
    return [pltpu.VMEM((LANE // v_heads + ONES_ROWS, 2 * tq), F32),
            pltpu.VMEM((SCORE_SLOTS, FLASH_TK, 2 * tq), F32)]


def _gqa_attention(qt, k, vt):
    bn, _, s = qt.shape
    tq, n_chunks = _flash_tq(s), s // FLASH_TK
    kern = functools.partial(_gqa_kernel, tq=tq, n_chunks=n_chunks)
    return pl.pallas_call(
        kern,
        grid=(bn, B_HEADS // 2, s // tq),
        in_specs=[pl.BlockSpec((None, LANE, tq), lambda b, p, i: (b, p, i)),
                  pl.BlockSpec((None, s, LANE), lambda b, p, i: (b, 0, 0)),
                  pl.BlockSpec((None, n_chunks, _vt_rows(B_KV_HEADS), FLASH_TK), lambda b, p, i: (b, 0, 0, 0))],
        out_specs=pl.BlockSpec((None, tq, LANE), lambda b, p, i: (b, i, p)),
        out_shape=jax.ShapeDtypeStruct((bn, s, B_WIDTH), F32),
        scratch_shapes=_flash_scratch(tq, B_KV_HEADS),
        compiler_params=_cparams(3),
        name="gqa_attn",
    )(qt, k, vt)


def _diff_attention(qt, k, vt, lam_rows, subln, lam_init):
    bn, _, s = qt.shape
    tq, n_chunks = _flash_tq(s), s // FLASH_TK
    kern = functools.partial(_diff_kernel, tq=tq, n_chunks=n_chunks, lam_init=lam_init)
    row = lambda n: pl.BlockSpec((1, n), lambda b, h, i: (0, 0))
    return pl.pallas_call(
        kern,
        grid=(bn, C_HEADS, s // tq),
        in_specs=[pl.BlockSpec((None, LANE, tq), lambda b, h, i: (b, h, i)),
                  pl.BlockSpec((None, s, LANE), lambda b, h, i: (b, 0, h)),
                  pl.BlockSpec((None, n_chunks, _vt_rows(1), FLASH_TK), lambda b, h, i: (b, 0, h, 0)),
                  row(C_HEAD_DIM), row(C_HEAD_DIM), row(C_HEAD_DIM), row(C_HEAD_DIM), row(LANE)],
        out_specs=pl.BlockSpec((None, tq, LANE), lambda b, h, i: (b, i, h)),
        out_shape=jax.ShapeDtypeStruct((bn, s, C_WIDTH), F32),
        scratch_shapes=_flash_scratch(tq, 1),
        compiler_params=_cparams(3),
        name="diff_attn",
    )(qt, k, vt, *lam_rows, subln)


def _final_kernel(x_ref, shift_ref, scale_ref, gate_ref, g_ref, oa0_ref, oa1_ref, oa2_ref, yb_ref, yc_ref,
                  wz_ref, wbg_ref, bbg_ref, woa_ref, wob_ref, woc_ref, wout_ref, o_ref, *stage_refs):
    x = x_ref[...]
    tm = x.shape[0]
    h = _modulated_norm(x, g_ref[...], scale_ref[...], shift_ref[...]).astype(BF16)
    lane = lax.broadcasted_iota(jnp.int32, (1, LANE), 1)

    for oa_ref, st_ref, (_, dil) in zip((oa1_ref, oa2_ref), stage_refs, A_GROUPS[1:]):
        for r in range(dil):
            for j in range(A_HEADS):
                lo = r * A_SLAB + j * LANE
                st_ref[j, pl.ds(r, tm // dil, stride=dil), :] = oa_ref[:, lo:lo + LANE]

    ya = []
    for j in range(A_HEADS):
        slabs = [oa0_ref[:, j * LANE:(j + 1) * LANE]] + [st_ref[j] for st_ref in stage_refs]
        lses = [t[:, A_HEAD_DIM:A_HEAD_DIM + 1] for t in slabs]
        top = jnp.maximum(jnp.maximum(lses[0], lses[1]), lses[2])
        ws = [jnp.exp(t - top) for t in lses]
        mix = (ws[0] * slabs[0] + ws[1] * slabs[1] + ws[2] * slabs[2]) / (ws[0] + ws[1] + ws[2])
        ya.append(jnp.where(lane < A_HEAD_DIM, mix, 0.0))
    ya = jnp.concatenate(ya, axis=-1)

    def branch(y, z_lo, z_hi, wo_ref, g_lo):
        z = _dot(h, wz_ref[:, z_lo:z_hi])
        p = _dot((y * _silu(z)).astype(BF16), wo_ref[...])
        gate = jax.nn.sigmoid(_dot(h, wbg_ref[:, g_lo:g_lo + D_MODEL]) + bbg_ref[:, g_lo:g_lo + D_MODEL])
        return gate * p

    za_hi = A_SLAB
    zb_hi = za_hi + B_WIDTH
    zc_hi = zb_hi + C_WIDTH
    merged = (branch(ya, 0, za_hi, woa_ref, 0)
              + branch(yb_ref[...], za_hi, zb_hi, wob_ref, D_MODEL)
              + branch(yc_ref[...], zb_hi, zc_hi, woc_ref, 2 * D_MODEL))
    out = _dot(merged.astype(BF16), wout_ref[...])
    o_ref[...] = x + gate_ref[...] * out


def _final(x, shift, scale, gate, norm_g, oa, yb, yc, wz, wbg, bbg, woa, wob, woc, wout, *, tm):
    bn, s, _ = x.shape
    tok = lambda w: pl.BlockSpec((None, tm, w), lambda b, i: (b, i, 0))
    vec = pl.BlockSpec((None, 1, D_MODEL), lambda b, i: (b, 0, 0))
    full = lambda a: pl.BlockSpec(a.shape, lambda b, i: (0, 0), pipeline_mode=pl.Buffered(1))
    band = [pl.BlockSpec((None, tm // dil, dil * A_SLAB), lambda b, i: (b, i, 0)) for _, dil in A_GROUPS]
    return pl.pallas_call(
        _final_kernel,
        grid=(bn, s // tm),
        in_specs=[tok(D_MODEL), vec, vec, vec, full(norm_g), *band,
                  tok(B_WIDTH), tok(C_WIDTH), full(wz), full(wbg), full(bbg), full(woa), full(wob), full(woc),
                  full(wout)],
        out_specs=tok(D_MODEL),
        out_shape=jax.ShapeDtypeStruct((bn, s, D_MODEL), F32),
        scratch_shapes=[pltpu.VMEM((A_HEADS, tm, LANE), F32) for _ in A_GROUPS[1:]],
        compiler_params=_cparams(2),
        name="final",
    )(x, shift, scale, gate, norm_g, *oa, yb, yc, wz, wbg, bbg, woa, wob, woc, wout)


B_Q_ORDER = (0, 3, 1, 4, 2, 5)


def _slab_layout(head_dim, n_heads, rot_groups):
    src = -np.ones((LANE,), np.int32)
    group = -np.ones((LANE,), np.int32)
    freq = np.zeros((LANE,), np.int32)
    sign = np.zeros((LANE,), np.float32)
    width = ROT_SHIFT // n_heads
    for hd in range(n_heads):
        low = [(start + i, g, i, -1.0) for g, (start, half) in enumerate(rot_groups) for i in range(half)]
        high = [(start + half + i, g, i, 1.0) for g, (start, half) in enumerate(rot_groups) for i in range(half)]
        rotary = {d for d, _, _, _ in low + high}
        rest = [(d, -1, 0, 0.0) for d in range(head_dim) if d not in rotary]
        n_low_rest = width - len(low)
        low, high = low + rest[:n_low_rest], high + rest[n_low_rest:]
        assert len(low) == width and len(high) <= width
        for base, items in ((hd * width, low), (hd * width + ROT_SHIFT, high)):
            for lane, (d, g, i, sg) in enumerate(items, start=base):
                src[lane], group[lane], freq[lane], sign[lane] = hd * head_dim + d, g, i, sg
    return src, group, freq, sign


LAYOUT_A = _slab_layout(A_HEAD_DIM, 1, ((0, A_ROT // 2),))
LAYOUT_B = _slab_layout(B_HEAD_DIM, 2, ((0, B_HEAD_DIM // 4), (B_HEAD_DIM // 2, B_HEAD_DIM // 4)))
LAYOUT_C = _slab_layout(C_HEAD_DIM, 2, ((0, C_ROT // 2),))


def _to_slabs(w, layout, dims_per_slab):
    src = layout[0]
    n_slabs = w.shape[-1] // dims_per_slab
    idx = np.concatenate([np.maximum(src, 0) + s * dims_per_slab for s in range(n_slabs)])
    keep = np.tile(src >= 0, n_slabs)
    return jnp.where(keep, jnp.take(w, idx, axis=-1), 0.0)


def _pad_heads(w, n_heads):
    lead = w.shape[:-1]
    w = w.reshape(*lead, n_heads, A_HEAD_DIM)
    w = jnp.pad(w, [(0, 0)] * len(lead) + [(0, 0), (0, LANE - A_HEAD_DIM)])
    return w.reshape(*lead, n_heads * LANE)


def _permute_heads(w, order, dim):
    lead = w.shape[:-1]
    w = w.reshape(*lead, len(order), dim)
    return w[..., jnp.array(order), :].reshape(*lead, len(order) * dim)


def _prep_layer(w_in, qn_a, kn_a, qn_b, kn_b, qn_c, kn_c, w_oa, w_ob, w_oc):
    offs = [0]
    for n in IN_SIZES:
        offs.append(offs[-1] + n)
    qa, ka, va, za, qb, kb, vb, zb, qc, kc, vc, zc = [w_in[:, offs[i]:offs[i + 1]] for i in range(len(IN_SIZES))]

    slab_a = functools.partial(_to_slabs, layout=LAYOUT_A, dims_per_slab=A_HEAD_DIM)
    slab_b = functools.partial(_to_slabs, layout=LAYOUT_B, dims_per_slab=2 * B_HEAD_DIM)
    slab_c = functools.partial(_to_slabs, layout=LAYOUT_C, dims_per_slab=2 * C_HEAD_DIM)
    qb = _permute_heads(qb, B_Q_ORDER, B_HEAD_DIM)

    by_group = lambda w: jnp.split(w, len(A_GROUPS), axis=-1)
    kv_a = [t for pair in zip(by_group(slab_a(ka)), by_group(_pad_heads(va, N_A))) for t in pair]
    w_a = jnp.concatenate([slab_a(qa)] + kv_a, axis=-1).astype(BF16)
    kv_gain = [t for kg in by_group(slab_a(jnp.tile(kn_a, N_A))) for t in (kg, jnp.zeros((A_SLAB,), F32))]
    gain_a = jnp.concatenate([slab_a(jnp.tile(qn_a, N_A))] + kv_gain)[None, :]

    w_b = jnp.concatenate([slab_b(qb), slab_b(kb), vb], axis=-1).astype(BF16)
    gain_b = jnp.concatenate([slab_b(jnp.tile(qn_b, B_HEADS)), slab_b(jnp.tile(kn_b, B_KV_HEADS)),
                              jnp.zeros((B_KV,), F32)])[None, :]

    w_c = jnp.concatenate([slab_c(qc), slab_c(kc), vc], axis=-1).astype(BF16)
    gain_c = jnp.concatenate([slab_c(jnp.tile(qn_c, 2 * C_HEADS)), slab_c(jnp.tile(kn_c, 2 * C_HEADS)),
                              jnp.zeros((C_WIDTH,), F32)])[None, :]

    w_z = jnp.concatenate([_pad_heads(za, A_HEADS), _permute_heads(zb, B_Q_ORDER, B_HEAD_DIM), zc],
                          axis=-1).astype(BF16)
    w_oa_p = _pad_heads(w_oa.T, A_HEADS).T.astype(BF16)
    w_ob_p = _permute_heads(w_ob.T, B_Q_ORDER, B_HEAD_DIM).T.astype(BF16)
    return dict(w_a=w_a, gain_a=gain_a, w_b=w_b, gain_b=gain_b, w_c=w_c, gain_c=gain_c, w_z=w_z,
                w_oa=w_oa_p, w_ob=w_ob_p, w_oc=w_oc.astype(BF16))


def _inv_freq(dim, theta):
    return theta ** (-jnp.arange(0, dim, 2, dtype=F32) / dim)


def _rope_tables(layout, inv_freqs, streams):
    _, group, freq, sign = layout
    ang = 0.0
    for g, (inv, pos) in enumerate(zip(inv_freqs, streams)):
        ang = ang + pos[:, None] * (inv[freq] * (group == g))[None, :]
    return jnp.cos(ang), jnp.sin(ang) * sign[None, :]


def _tables(s):
    pos = jnp.arange(s)
    tok, row, col = [t.astype(F32) for t in (pos, pos // GRID_W, pos % GRID_W)]
    inv_b = _inv_freq(B_HEAD_DIM // 2, AXIAL_THETA)
    tab_a = _rope_tables(LAYOUT_A, [_inv_freq(A_ROT, ROPE_THETA)], [tok])
    tab_b = _rope_tables(LAYOUT_B, [inv_b, inv_b], [row, col])
    tab_c = _rope_tables(LAYOUT_C, [_inv_freq(C_ROT, ROPE_THETA)], [tok])
    return tab_a, tab_b, tab_c


def _layer(x, mod, l, lp, tabs, norm_g, lam_rows, subln, w_bg, b_bg, w_out):
    bn, s, _ = x.shape
    shift, scale, gate = [m.reshape(bn, 1, D_MODEL) for m in jnp.split(mod, 3, axis=-1)]
    tab_a, tab_b, tab_c = tabs
    lam_init = 0.8 - 0.6 * math.exp(-0.3 * l)

    a_outs = (tuple((A_SLAB, "q", dil) for _, dil in A_GROUPS)
              + tuple((2 * A_SLAB, "kv", dil) for _, dil in A_GROUPS))
    pa = _project(x, shift, scale, norm_g, lp["w_a"], lp["gain_a"], tab_a, a_outs, two_heads=False,
                  head_dim=A_HEAD_DIM, q_scale=A_HEAD_DIM ** -0.5 * LOG2E, tm=min(PROJ_A_TM, s), name="proj_a")
    n_g = len(A_GROUPS)
    oa = [_band_attention(pa[g], pa[n_g + g], dil) for g, (_, dil) in enumerate(A_GROUPS)]

    qb, kb, vb = _project(x, shift, scale, norm_g, lp["w_b"], lp["gain_b"], tab_b,
                          ((B_WIDTH, "qT", 1), (B_KV, "k", 1), (B_KV, "vT", B_KV_HEADS)), two_heads=True,
                          head_dim=B_HEAD_DIM, q_scale=B_HEAD_DIM ** -0.5 * LOG2E, tm=min(PROJ_TM, s), name="proj_b")
    yb = _gqa_attention(qb, kb, vb)

    qc, kc, vc = _project(x, shift, scale, norm_g, lp["w_c"], lp["gain_c"], tab_c,
                          ((C_WIDTH, "qT", 1), (C_WIDTH, "k", 1), (C_WIDTH, "vT", 1)), two_heads=True,
                          head_dim=C_HEAD_DIM, q_scale=C_HEAD_DIM ** -0.5 * LOG2E, tm=min(PROJ_TM, s), name="proj_c")
    yc = _diff_attention(qc, kc, vc, lam_rows, subln, lam_init)

    return _final(x, shift, scale, gate, norm_g, oa, yb, yc, lp["w_z"], w_bg, b_bg, lp["w_oa"], lp["w_ob"],
                  lp["w_oc"], w_out, tm=min(FINAL_TM, s))


def kernel(x_prompt, x_sample, c_prompt, c_sample, norm_g, w_ada, b_ada, w_in, qn_a, kn_a, qn_b, kn_b, qn_c, kn_c,
           lam_q1, lam_k1, lam_q2, lam_k2, subln_c, w_oa, w_ob, w_oc, w_bg, b_bg, w_out):
    groups = ((x_prompt, c_prompt), (x_sample, c_sample))
    rows = [c.shape[0] for _, c in groups]
    pad = -sum(rows) % 8
    c_all = jnp.concatenate([c for _, c in groups] + [jnp.zeros((pad, D_MODEL), F32)], axis=0)
    mod_all = _ada_mod(c_all, w_ada.astype(BF16), b_ada)

    layers = [_prep_layer(w_in[l], qn_a[l], kn_a[l], qn_b[l], kn_b[l], qn_c[l], kn_c[l], w_oa[l], w_ob[l], w_oc[l])
              for l in range(DEPTH)]
    w_bg16, w_out16 = w_bg.astype(BF16), w_out.astype(BF16)

    outs = []
    row0 = 0
    for (x, c), n in zip(groups, rows):
        tabs = _tables(x.shape[1])
        for l in range(DEPTH):
            lam_rows = [p[l][None, :] for p in (lam_q1, lam_k1, lam_q2, lam_k2)]
            x = _layer(x, mod_all[l, row0:row0 + n], l, layers[l], tabs, norm_g[l][None, :], lam_rows,
                       subln_c[l][None, :], w_bg16[l], b_bg[l][None, :], w_out16[l])
        outs.append(x)
        row0 += n
    return tuple(outs)
```

```python
import functools
import math

import jax
import jax.numpy as jnp
import numpy as np
from jax import lax
from jax.experimental import pallas as pl
from jax.experimental.pallas import tpu as pltpu

D_MODEL = 1024
DEPTH = 2
GRID_W = 64
EPS = 1e-6
NEG = -1e30
ROPE_THETA = 500000.0
AXIAL_THETA = 10000.0

A_GROUPS = ((128, 1), (512, 4), (2048, 16))
A_HEADS = 4
A_HEAD_DIM = 96
A_ROT = A_HEAD_DIM // 4
A_BAND = 64
N_A = len(A_GROUPS) * A_HEADS

B_HEADS = 6
B_KV_HEADS = 2
B_HEAD_DIM = 64
C_HEADS = 4
C_HEAD_DIM = 64
C_ROT = C_HEAD_DIM // 4

A_QKV = N_A * A_HEAD_DIM
A_WIDTH = A_HEADS * A_HEAD_DIM
B_WIDTH = B_HEADS * B_HEAD_DIM
B_KV = B_KV_HEADS * B_HEAD_DIM
C_WIDTH = C_HEADS * 2 * C_HEAD_DIM
IN_SIZES = (A_QKV, A_QKV, A_QKV, A_WIDTH, B_WIDTH, B_KV, B_KV, B_WIDTH, C_WIDTH, C_WIDTH, C_WIDTH, C_WIDTH)

LANE = 128
HALF = LANE // 2
ROT_SHIFT = LANE // 2
A_SLAB = A_HEADS * LANE
BF16_SUBLANES = 16
ONES_ROWS = BF16_SUBLANES
FLASH_TK = 512
SCORE_SLOTS = 2
FLASH_UNROLL = 4
PROJ_TM = 512
PROJ_A_TM = 256
FINAL_TM = 512
LOG2E = math.log2(math.e)
LN2 = math.log(2.0)
VMEM_LIMIT = 48 * 1024 * 1024

BF16 = jnp.bfloat16
F32 = jnp.float32


def _cparams(n_axes):
    return pltpu.CompilerParams(dimension_semantics=("arbitrary",) * n_axes, vmem_limit_bytes=VMEM_LIMIT)


def _dot(a, b):
    return jnp.dot(a, b, preferred_element_type=F32)


def _dot_nt(a, b):
    return lax.dot_general(a, b, (((1,), (1,)), ((), ())), preferred_element_type=F32)


def _silu(x):
    return x * jax.nn.sigmoid(x)


def _ada_kernel(c_ref, w_ref, b_ref, o_ref):
    c = c_ref[...]
    o_ref[...] = _dot(_silu(c).astype(BF16), w_ref[...]) + b_ref[...]


def _ada_mod(c_all, w_ada, b_ada):
    rows = c_all.shape[0]
    return pl.pallas_call(
        _ada_kernel,
        grid=(DEPTH, 3),
        in_specs=[
            pl.BlockSpec((rows, D_MODEL), lambda l, n: (0, 0)),
            pl.BlockSpec((None, D_MODEL, D_MODEL), lambda l, n: (l, 0, n)),
            pl.BlockSpec((None, 1, D_MODEL), lambda l, n: (l, 0, n)),
        ],
        out_specs=pl.BlockSpec((None, rows, D_MODEL), lambda l, n: (l, 0, n)),
        out_shape=jax.ShapeDtypeStruct((DEPTH, rows, 3 * D_MODEL), F32),
        compiler_params=_cparams(2),
        name="ada_mod",
    )(c_all, w_ada, b_ada.reshape(DEPTH, 1, 3 * D_MODEL))


def _modulated_norm(x, g, scale, shift):
    ms = jnp.mean(x * x, axis=-1, keepdims=True)
    h = x * lax.rsqrt(ms + EPS) * g
    return h * (1.0 + scale) + shift


def _first_head(index):
    return (index & (HALF - 1)) < HALF // 2


def _head_norm_rope(u, gain, cos, sin, two_heads, head_dim):
    sq = u * u
    if two_heads:
        first = _first_head(lax.broadcasted_iota(jnp.int32, (1, LANE), 1))
        s_first = jnp.sum(jnp.where(first, sq, 0.0), axis=-1, keepdims=True)
        s_all = jnp.sum(sq, axis=-1, keepdims=True)
        ms = jnp.where(first, s_first, s_all - s_first) * (1.0 / head_dim)
    else:
        ms = jnp.sum(sq, axis=-1, keepdims=True) * (1.0 / head_dim)
    y = u * lax.rsqrt(ms + EPS) * gain
    return y * cos + pltpu.roll(y, ROT_SHIFT, 1) * sin


def _proj_kernel(x_ref, shift_ref, scale_ref, g_ref, w_ref, gain_ref, cos_ref, sin_ref, *refs,
                 outs, two_heads, head_dim, q_scale):
    out_refs, stage_ref = refs[:len(outs)], refs[len(outs):]
    h = _modulated_norm(x_ref[...], g_ref[...], scale_ref[...], shift_ref[...]).astype(BF16)
    cos, sin = cos_ref[...], sin_ref[...]
    tm = x_ref.shape[0]
    col = 0
    for o_ref, (width, kind, dil) in zip(out_refs, outs):
        u = _dot(h, w_ref[:, col:col + width])
        if kind == "vT":
            dv = LANE // dil
            for s in range(width // LANE):
                ut = u[:, s * LANE:(s + 1) * LANE].T.astype(BF16)
                for hd in range(dil):
                    base = (s * dil + hd) * (dv + ONES_ROWS)
                    o_ref[base:base + dv, :] = ut[hd * dv:(hd + 1) * dv]
                    o_ref[base + dv:base + dv + ONES_ROWS, :] = jnp.ones((ONES_ROWS, tm), BF16)
        else:
            if kind != "v":
                slabs = []
                for s in range(width // LANE):
                    lo = s * LANE
                    y = _head_norm_rope(u[:, lo:lo + LANE], gain_ref[:, col + lo:col + lo + LANE], cos, sin,
                                        two_heads, head_dim)
                    slabs.append(y * q_scale if kind in ("q", "qT") and q_scale != 1.0 else y)
                u = jnp.concatenate(slabs, axis=-1)
            if kind == "qT":
                for s in range(width // LANE):
                    o_ref[s * LANE:(s + 1) * LANE, :] = u[:, s * LANE:(s + 1) * LANE].T.astype(BF16)
            elif dil == 1:
                o_ref[...] = u.astype(BF16)
            else:
                stage = stage_ref[0]
                for s in range(width // LANE):
                    stage[s] = u[:, s * LANE:(s + 1) * LANE]
                rows = lambda r: pl.ds(r, tm // dil, stride=dil)
                for r in range(dil):
                    for s in range(width // LANE):
                        lo = r * width + s * LANE
                        o_ref[:, lo:lo + LANE] = stage[s, rows(r), :].astype(BF16)
        col += width


def _vt_rows(v_heads):
    return LANE + v_heads * ONES_ROWS


def _proj_out(kind, width, dil, bn, s, tm):
    if kind == "qT":
        return (bn, width, s), pl.BlockSpec((None, width, tm), lambda b, i: (b, 0, i))
    if kind == "vT":
        per = FLASH_TK // tm
        rows = width // LANE * _vt_rows(dil)
        return ((bn, s // FLASH_TK, rows, FLASH_TK),
                pl.BlockSpec((None, None, rows, tm), lambda b, i: (b, i // per, 0, i % per)))
    return (bn, s // dil, dil * width), pl.BlockSpec((None, tm // dil, dil * width), lambda b, i: (b, i, 0))


def _project(x, shift, scale, norm_g, w, gain, tables, outs, *, two_heads, head_dim, q_scale, tm, name):
    bn, s, _ = x.shape
    wcols = w.shape[1]
    kern = functools.partial(_proj_kernel, outs=outs, two_heads=two_heads, head_dim=head_dim, q_scale=q_scale)
    vec = pl.BlockSpec((None, 1, D_MODEL), lambda b, i: (b, 0, 0))
    tab = pl.BlockSpec((tm, LANE), lambda b, i: (i, 0))
    shapes, specs = zip(*[_proj_out(kind, width, dil, bn, s, tm) for width, kind, dil in outs])
    n_slabs = max(width for width, _, _ in outs) // LANE
    dilated = any(d > 1 and kind != "vT" for _, kind, d in outs)
    stage = [pltpu.VMEM((n_slabs, tm, LANE), F32)] if dilated else []
    return pl.pallas_call(
        kern,
        grid=(bn, s // tm),
        in_specs=[
            pl.BlockSpec((None, tm, D_MODEL), lambda b, i: (b, i, 0)),
            vec, vec,
            pl.BlockSpec((1, D_MODEL), lambda b, i: (0, 0)),
            pl.BlockSpec((D_MODEL, wcols), lambda b, i: (0, 0)),
            pl.BlockSpec((1, wcols), lambda b, i: (0, 0)),
            tab, tab,
        ],
        out_specs=list(specs),
        out_shape=[jax.ShapeDtypeStruct(shape, BF16) for shape in shapes],
        scratch_shapes=stage,
        compiler_params=_cparams(2),
        name=name,
    )(x, shift, scale, norm_g, w, gain, *tables)


def _band_attn_kernel(q_ref, kp_ref, kc_ref, kn_ref, vp_ref, vc_ref, vn_ref, bias_ref, o_ref, *, tb, n_blocks):
    i = pl.program_id(2)
    nk = tb + 2 * A_BAND
    colk = lax.broadcasted_iota(jnp.int32, (1, nk), 1)
    first_col = jnp.where(i == 0, A_BAND, 0)
    end_col = jnp.where(i == n_blocks - 1, tb + A_BAND, nk)
    bias = bias_ref[...] + jnp.where((colk < first_col) | (colk >= end_col), NEG, 0.0)
    lane = lax.broadcasted_iota(jnp.int32, (1, LANE), 1)
    heads = [slice(j * LANE, (j + 1) * LANE) for j in range(A_HEADS)]
    band = lambda p_ref, c_ref, n_ref, sl: jnp.concatenate([p_ref[:, sl], c_ref[:, sl], n_ref[:, sl]], axis=0)
    scores = [_dot_nt(q_ref[:, sl], band(kp_ref, kc_ref, kn_ref, sl)) for sl in heads]
    stats = []
    for s in scores:
        s = s + bias
        m = jnp.max(s, axis=-1, keepdims=True)
        p = jnp.exp2(s - m)
        stats.append((m, jnp.sum(p, axis=-1, keepdims=True), p.astype(BF16)))
    for sl, (m, l, p) in zip(heads, stats):
        o = _dot(p, band(vp_ref, vc_ref, vn_ref, sl)) / l
        o_ref[:, sl] = jnp.where(lane >= A_HEAD_DIM, m * LN2 + jnp.log(l), o)


def _band_attention(q, k, v, dilation):
    bn, length, _ = q.shape
    tb = min(256, length)
    nb = length // tb
    per = tb // A_BAND
    cur = pl.BlockSpec((None, tb, A_SLAB), lambda b, r, i: (b, i, r))
    prv = pl.BlockSpec((None, A_BAND, A_SLAB), lambda b, r, i: (b, jnp.maximum(i * per - 1, 0), r))
    nxt = pl.BlockSpec((None, A_BAND, A_SLAB), lambda b, r, i: (b, jnp.minimum((i + 1) * per, nb * per - 1), r))
    nk = tb + 2 * A_BAND
    dist = np.arange(nk)[None, :] - A_BAND - np.arange(tb)[:, None]
    bias = np.where(np.abs(dist) <= A_BAND, 0.0, NEG).astype(np.float32)
    kern = functools.partial(_band_attn_kernel, tb=tb, n_blocks=nb)
    return pl.pallas_call(
        kern,
        grid=(bn, dilation, nb),
        in_specs=[cur, prv, cur, nxt, prv, cur, nxt, pl.BlockSpec((tb, nk), lambda b, r, i: (0, 0))],
        out_specs=cur,
        out_shape=jax.ShapeDtypeStruct((bn, length, dilation * A_SLAB), F32),
        compiler_params=_cparams(3),
        name=f"band_attn_d{dilation}",
    )(q, k, k, k, v, v, v, bias)


def _stacked_flash(qt_ref, k_ref, vt_ref, acc_sc, s_sc, *, tq, n_chunks, v_heads):
    dv = LANE // v_heads
    v_rows = [slice(hd * (dv + ONES_ROWS), (hd + 1) * (dv + ONES_ROWS)) for hd in range(v_heads)]
    first = _first_head(lax.broadcasted_iota(jnp.int32, (LANE, 1), 0))
    qt = qt_ref[...]
    zero = jnp.zeros_like(qt)
    q2t = jnp.concatenate([jnp.where(first, qt, zero), jnp.where(first, zero, qt)], axis=1)
    acc_sc[...] = jnp.zeros(acc_sc.shape, F32)

    halves = [slice(0, tq), slice(tq, 2 * tq)]

    def scores(c, slot, cols):
        start = pl.multiple_of(c * FLASH_TK, FLASH_TK)
        s_sc[slot, :, cols] = _dot(k_ref[pl.ds(start, FLASH_TK), :], q2t[:, cols])

    def step(c, slot, m_prev, c_next=None):
        m_next = jnp.maximum(m_prev, jnp.max(s_sc[slot], axis=0, keepdims=True))
        vc = vt_ref[c]
        alpha = jnp.exp2(m_prev - m_next)
        for g, cols in enumerate(halves):
            if c_next is not None:
                scores(c_next, 1 - slot, cols)
            pt = jnp.exp2(s_sc[slot, :, cols] - m_next[:, cols]).astype(BF16)
            acc_sc[:, cols] = alpha[:, cols] * acc_sc[:, cols] + _dot(vc[v_rows[g % v_heads]], pt)
        return m_next

    unroll = min(FLASH_UNROLL, n_chunks)

    def group(j, m):
        for u in range(unroll):
            c = unroll * j + u
            m = step(c, u % SCORE_SLOTS, m, c + 1)
        return m

    for cols in halves:
        scores(0, 0, cols)
    m = lax.fori_loop(0, n_chunks // unroll - 1, group, jnp.full((1, 2 * tq), -jnp.inf, F32))
    for c in range(n_chunks - unroll, n_chunks):
        m = step(c, c % SCORE_SLOTS, m, c + 1 if c + 1 < n_chunks else None)
    acc = acc_sc[...]
    return acc[:dv] / acc[dv:dv + 1]


def _gqa_kernel(qt_ref, k_ref, vt_ref, o_ref, acc_sc, s_sc, *, tq, n_chunks):
    o = _stacked_flash(qt_ref, k_ref, vt_ref, acc_sc, s_sc, tq=tq, n_chunks=n_chunks, v_heads=B_KV_HEADS)
    o_ref[...] = jnp.concatenate([o[:, :tq], o[:, tq:]], axis=0).T


def _diff_kernel(qt_ref, k_ref, vt_ref, lq1_ref, lk1_ref, lq2_ref, lk2_ref, sub_ref, o_ref, acc_sc,
                 s_sc, *, tq, n_chunks, lam_init):
    o = _stacked_flash(qt_ref, k_ref, vt_ref, acc_sc, s_sc, tq=tq, n_chunks=n_chunks, v_heads=1)
    lam = (jnp.exp(jnp.sum(lq1_ref[...] * lk1_ref[...], axis=-1, keepdims=True))
           - jnp.exp(jnp.sum(lq2_ref[...] * lk2_ref[...], axis=-1, keepdims=True)) + lam_init)
    oc = (o[:, :tq] - lam * o[:, tq:]).T
    ms = jnp.mean(oc * oc, axis=-1, keepdims=True)
    o_ref[...] = oc * lax.rsqrt(ms + EPS) * sub_ref[...] * (1.0 - lam_init)


def _flash_tq(s):
    return min(512, s)


def _flash_scratch(tq, v_heads):
    return [pltpu.VMEM((LANE // v_heads + ONES_ROWS, 2 * tq), F32),
            pltpu.VMEM((SCORE_SLOTS, FLASH_TK, 2 * tq), F32)]


def _gqa_attention(qt, k, vt):
    bn, _, s = qt.shape
    tq, n_chunks = _flash_tq(s), s // FLASH_TK
    kern = functools.partial(_gqa_kernel, tq=tq, n_chunks=n_chunks)
    return pl.pallas_call(
        kern,
        grid=(bn, B_HEADS // 2, s // tq),
        in_specs=[pl.BlockSpec((None, LANE, tq), lambda b, p, i: (b, p, i)),
                  pl.BlockSpec((None, s, LANE), lambda b, p, i: (b, 0, 0)),
                  pl.BlockSpec((None, n_chunks, _vt_rows(B_KV_HEADS), FLASH_TK), lambda b, p, i: (b, 0, 0, 0))],
        out_specs=pl.BlockSpec((None, tq, LANE), lambda b, p, i: (b, i, p)),
        out_shape=jax.ShapeDtypeStruct((bn, s, B_WIDTH), F32),
        scratch_shapes=_flash_scratch(tq, B_KV_HEADS),
        compiler_params=_cparams(3),
        name="gqa_attn",
    )(qt, k, vt)


def _diff_attention(qt, k, vt, lam_rows, subln, lam_init):
    bn, _, s = qt.shape
    tq, n_chunks = _flash_tq(s), s // FLASH_TK
    kern = functools.partial(_diff_kernel, tq=tq, n_chunks=n_chunks, lam_init=lam_init)
    row = lambda n: pl.BlockSpec((1, n), lambda b, h, i: (0, 0))
    return pl.pallas_call(
        kern,
        grid=(bn, C_HEADS, s // tq),
        in_specs=[pl.BlockSpec((None, LANE, tq), lambda b, h, i: (b, h, i)),
                  pl.BlockSpec((None, s, LANE), lambda b, h, i: (b, 0, h)),
                  pl.BlockSpec((None, n_chunks, _vt_rows(1), FLASH_TK), lambda b, h, i: (b, 0, h, 0)),
                  row(C_HEAD_DIM), row(C_HEAD_DIM), row(C_HEAD_DIM), row(C_HEAD_DIM), row(LANE)],
        out_specs=pl.BlockSpec((None, tq, LANE), lambda b, h, i: (b, i, h)),
        out_shape=jax.ShapeDtypeStruct((bn, s, C_WIDTH), F32),
        scratch_shapes=_flash_scratch(tq, 1),
        compiler_params=_cparams(3),
        name="diff_attn",
    )(qt, k, vt, *lam_rows, subln)


def _final_kernel(x_ref, shift_ref, scale_ref, gate_ref, g_ref, oa0_ref, oa1_ref, oa2_ref, yb_ref, yc_ref,
                  wz_ref, wbg_ref, bbg_ref, woa_ref, wob_ref, woc_ref, wout_ref, o_ref, *stage_refs):
    x = x_ref[...]
    tm = x.shape[0]
    h = _modulated_norm(x, g_ref[...], scale_ref[...], shift_ref[...]).astype(BF16)
    lane = lax.broadcasted_iota(jnp.int32, (1, LANE), 1)

    for oa_ref, st_ref, (_, dil) in zip((oa1_ref, oa2_ref), stage_refs, A_GROUPS[1:]):
        for r in range(dil):
            for j in range(A_HEADS):
                lo = r * A_SLAB + j * LANE
                st_ref[j, pl.ds(r, tm // dil, stride=dil), :] = oa_ref[:, lo:lo + LANE]

    ya = []
    for j in range(A_HEADS):
        slabs = [oa0_ref[:, j * LANE:(j + 1) * LANE]] + [st_ref[j] for st_ref in stage_refs]
        lses = [t[:, A_HEAD_DIM:A_HEAD_DIM + 1] for t in slabs]
        top = jnp.maximum(jnp.maximum(lses[0], lses[1]), lses[2])
        ws = [jnp.exp(t - top) for t in lses]
        mix = (ws[0] * slabs[0] + ws[1] * slabs[1] + ws[2] * slabs[2]) / (ws[0] + ws[1] + ws[2])
        ya.append(jnp.where(lane < A_HEAD_DIM, mix, 0.0))
    ya = jnp.concatenate(ya, axis=-1)

    def branch(y, z_lo, z_hi, wo_ref, g_lo):
        z = _dot(h, wz_ref[:, z_lo:z_hi])
        p = _dot((y * _silu(z)).astype(BF16), wo_ref[...])
        gate = jax.nn.sigmoid(_dot(h, wbg_ref[:, g_lo:g_lo + D_MODEL]) + bbg_ref[:, g_lo:g_lo + D_MODEL])
        return gate * p

    za_hi = A_SLAB
    zb_hi = za_hi + B_WIDTH
    zc_hi = zb_hi + C_WIDTH
    merged = (branch(ya, 0, za_hi, woa_ref, 0)
              + branch(yb_ref[...], za_hi, zb_hi, wob_ref, D_MODEL)
              + branch(yc_ref[...], zb_hi, zc_hi, woc_ref, 2 * D_MODEL))
    out = _dot(merged.astype(BF16), wout_ref[...])
    o_ref[...] = x + gate_ref[...] * out


def _final(x, shift, scale, gate, norm_g, oa, yb, yc, wz, wbg, bbg, woa, wob, woc, wout, *, tm):
    bn, s, _ = x.shape
    tok = lambda w: pl.BlockSpec((None, tm, w), lambda b, i: (b, i, 0))
    vec = pl.BlockSpec((None, 1, D_MODEL), lambda b, i: (b, 0, 0))
    full = lambda a: pl.BlockSpec(a.shape, lambda b, i: (0, 0), pipeline_mode=pl.Buffered(1))
    band = [pl.BlockSpec((None, tm // dil, dil * A_SLAB), lambda b, i: (b, i, 0)) for _, dil in A_GROUPS]
    return pl.pallas_call(
        _final_kernel,
        grid=(bn, s // tm),
        in_specs=[tok(D_MODEL), vec, vec, vec, full(norm_g), *band,
                  tok(B_WIDTH), tok(C_WIDTH), full(wz), full(wbg), full(bbg), full(woa), full(wob), full(woc),
                  full(wout)],
        out_specs=tok(D_MODEL),
        out_shape=jax.ShapeDtypeStruct((bn, s, D_MODEL), F32),
        scratch_shapes=[pltpu.VMEM((A_HEADS, tm, LANE), F32) for _ in A_GROUPS[1:]],
        compiler_params=_cparams(2),
        name="final",
    )(x, shift, scale, gate, norm_g, *oa, yb, yc, wz, wbg, bbg, woa, wob, woc, wout)


B_Q_ORDER = (0, 3, 1, 4, 2, 5)


def _slab_layout(head_dim, n_heads, rot_groups):
    src = -np.ones((LANE,), np.int32)
    group = -np.ones((LANE,), np.int32)
    freq = np.zeros((LANE,), np.int32)
    sign = np.zeros((LANE,), np.float32)
    width = ROT_SHIFT // n_heads
    for hd in range(n_heads):
        low = [(start + i, g, i, -1.0) for g, (start, half) in enumerate(rot_groups) for i in range(half)]
        high = [(start + half + i, g, i, 1.0) for g, (start, half) in enumerate(rot_groups) for i in range(half)]
        rotary = {d for d, _, _, _ in low + high}
        rest = [(d, -1, 0, 0.0) for d in range(head_dim) if d not in rotary]
        n_low_rest = width - len(low)
        low, high = low + rest[:n_low_rest], high + rest[n_low_rest:]
        assert len(low) == width and len(high) <= width
        for base, items in ((hd * width, low), (hd * width + ROT_SHIFT, high)):
            for lane, (d, g, i, sg) in enumerate(items, start=base):
                src[lane], group[lane], freq[lane], sign[lane] = hd * head_dim + d, g, i, sg
    return src, group, freq, sign


LAYOUT_A = _slab_layout(A_HEAD_DIM, 1, ((0, A_ROT // 2),))
LAYOUT_B = _slab_layout(B_HEAD_DIM, 2, ((0, B_HEAD_DIM // 4), (B_HEAD_DIM // 2, B_HEAD_DIM // 4)))
LAYOUT_C = _slab_layout(C_HEAD_DIM, 2, ((0, C_ROT // 2),))


def _to_slabs(w, layout, dims_per_slab):
    src = layout[0]
    n_slabs = w.shape[-1] // dims_per_slab
    idx = np.concatenate([np.maximum(src, 0) + s * dims_per_slab for s in range(n_slabs)])
    keep = np.tile(src >= 0, n_slabs)
    return jnp.where(keep, jnp.take(w, idx, axis=-1), 0.0)


def _pad_heads(w, n_heads):
    lead = w.shape[:-1]
    w = w.reshape(*lead, n_heads, A_HEAD_DIM)
    w = jnp.pad(w, [(0, 0)] * len(lead) + [(0, 0), (0, LANE - A_HEAD_DIM)])
    return w.reshape(*lead, n_heads * LANE)


def _permute_heads(w, order, dim):
    lead = w.shape[:-1]
    w = w.reshape(*lead, len(order), dim)
    return w[..., jnp.array(order), :].reshape(*lead, len(order) * dim)


def _prep_layer(w_in, qn_a, kn_a, qn_b, kn_b, qn_c, kn_c, w_oa, w_ob, w_oc):
    offs = [0]
    for n in IN_SIZES:
        offs.append(offs[-1] + n)
    qa, ka, va, za, qb, kb, vb, zb, qc, kc, vc, zc = [w_in[:, offs[i]:offs[i + 1]] for i in range(len(IN_SIZES))]

    slab_a = functools.partial(_to_slabs, layout=LAYOUT_A, dims_per_slab=A_HEAD_DIM)
    slab_b = functools.partial(_to_slabs, layout=LAYOUT_B, dims_per_slab=2 * B_HEAD_DIM)
    slab_c = functools.partial(_to_slabs, layout=LAYOUT_C, dims_per_slab=2 * C_HEAD_DIM)
    qb = _permute_heads(qb, B_Q_ORDER, B_HEAD_DIM)

    w_a = jnp.concatenate([slab_a(qa), slab_a(ka), _pad_heads(va, N_A)], axis=-1).astype(BF16)
    gain_a = jnp.concatenate([slab_a(jnp.tile(qn_a, N_A)), slab_a(jnp.tile(kn_a, N_A)),
                              jnp.zeros((N_A * LANE,), F32)])[None, :]

    w_b = jnp.concatenate([slab_b(qb), slab_b(kb), vb], axis=-1).astype(BF16)
    gain_b = jnp.concatenate([slab_b(jnp.tile(qn_b, B_HEADS)), slab_b(jnp.tile(kn_b, B_KV_HEADS)),
                              jnp.zeros((B_KV,), F32)])[None, :]

    w_c = jnp.concatenate([slab_c(qc), slab_c(kc), vc], axis=-1).astype(BF16)
    gain_c = jnp.concatenate([slab_c(jnp.tile(qn_c, 2 * C_HEADS)), slab_c(jnp.tile(kn_c, 2 * C_HEADS)),
                              jnp.zeros((C_WIDTH,), F32)])[None, :]

    w_z = jnp.concatenate([_pad_heads(za, A_HEADS), _permute_heads(zb, B_Q_ORDER, B_HEAD_DIM), zc],
                          axis=-1).astype(BF16)
    w_oa_p = _pad_heads(w_oa.T, A_HEADS).T.astype(BF16)
    w_ob_p = _permute_heads(w_ob.T, B_Q_ORDER, B_HEAD_DIM).T.astype(BF16)
    return dict(w_a=w_a, gain_a=gain_a, w_b=w_b, gain_b=gain_b, w_c=w_c, gain_c=gain_c, w_z=w_z,
                w_oa=w_oa_p, w_ob=w_ob_p, w_oc=w_oc.astype(BF16))


def _inv_freq(dim, theta):
    return theta ** (-jnp.arange(0, dim, 2, dtype=F32) / dim)


def _rope_tables(layout, inv_freqs, streams):
    _, group, freq, sign = layout
    ang = 0.0
    for g, (inv, pos) in enumerate(zip(inv_freqs, streams)):
        ang = ang + pos[:, None] * (inv[freq] * (group == g))[None, :]
    return jnp.cos(ang), jnp.sin(ang) * sign[None, :]


def _tables(s):
    pos = jnp.arange(s)
    tok, row, col = [t.astype(F32) for t in (pos, pos // GRID_W, pos % GRID_W)]
    inv_b = _inv_freq(B_HEAD_DIM // 2, AXIAL_THETA)
    tab_a = _rope_tables(LAYOUT_A, [_inv_freq(A_ROT, ROPE_THETA)], [tok])
    tab_b = _rope_tables(LAYOUT_B, [inv_b, inv_b], [row, col])
    tab_c = _rope_tables(LAYOUT_C, [_inv_freq(C_ROT, ROPE_THETA)], [tok])
    return tab_a, tab_b, tab_c


def _layer(x, mod, l, lp, tabs, norm_g, lam_rows, subln, w_bg, b_bg, w_out):
    bn, s, _ = x.shape
    shift, scale, gate = [m.reshape(bn, 1, D_MODEL) for m in jnp.split(mod, 3, axis=-1)]
    tab_a, tab_b, tab_c = tabs
    lam_init = 0.8 - 0.6 * math.exp(-0.3 * l)

    a_outs = tuple((A_SLAB, kind, dil) for kind in ("q", "k", "v") for _, dil in A_GROUPS)
    pa = _project(x, shift, scale, norm_g, lp["w_a"], lp["gain_a"], tab_a, a_outs, two_heads=False,
                  head_dim=A_HEAD_DIM, q_scale=A_HEAD_DIM ** -0.5 * LOG2E, tm=min(PROJ_A_TM, s), name="proj_a")
    n_g = len(A_GROUPS)
    oa = [_band_attention(pa[g], pa[n_g + g], pa[2 * n_g + g], dil) for g, (_, dil) in enumerate(A_GROUPS)]

    qb, kb, vb = _project(x, shift, scale, norm_g, lp["w_b"], lp["gain_b"], tab_b,
                          ((B_WIDTH, "qT", 1), (B_KV, "k", 1), (B_KV, "vT", B_KV_HEADS)), two_heads=True,
                          head_dim=B_HEAD_DIM, q_scale=B_HEAD_DIM ** -0.5 * LOG2E, tm=min(PROJ_TM, s), name="proj_b")
    yb = _gqa_attention(qb, kb, vb)

    qc, kc, vc = _project(x, shift, scale, norm_g, lp["w_c"], lp["gain_c"], tab_c,
                          ((C_WIDTH, "qT", 1), (C_WIDTH, "k", 1), (C_WIDTH, "vT", 1)), two_heads=True,
                          head_dim=C_HEAD_DIM, q_scale=C_HEAD_DIM ** -0.5 * LOG2E, tm=min(PROJ_TM, s), name="proj_c")
    yc = _diff_attention(qc, kc, vc, lam_rows, subln, lam_init)

    return _final(x, shift, scale, gate, norm_g, oa, yb, yc, lp["w_z"], w_bg, b_bg, lp["w_oa"], lp["w_ob"],
                  lp["w_oc"], w_out, tm=min(FINAL_TM, s))


def kernel(x_prompt, x_sample, c_prompt, c_sample, norm_g, w_ada, b_ada, w_in, qn_a, kn_a, qn_b, kn_b, qn_c, kn_c,
           lam_q1, lam_k1, lam_q2, lam_k2, subln_c, w_oa, w_ob, w_oc, w_bg, b_bg, w_out):
    groups = ((x_prompt, c_prompt), (x_sample, c_sample))
    rows = [c.shape[0] for _, c in groups]
    pad = -sum(rows) % 8
    c_all = jnp.concatenate([c for _, c in groups] + [jnp.zeros((pad, D_MODEL), F32)], axis=0)
    mod_all = _ada_mod(c_all, w_ada.astype(BF16), b_ada)

    layers = [_prep_layer(w_in[l], qn_a[l], kn_a[l], qn_b[l], kn_b[l], qn_c[l], kn_c[l], w_oa[l], w_ob[l], w_oc[l])
              for l in range(DEPTH)]
    w_bg16, w_out16 = w_bg.astype(BF16), w_out.astype(BF16)

    outs = []
    row0 = 0
    for (x, c), n in zip(groups, rows):
        tabs = _tables(x.shape[1])
        for l in range(DEPTH):
            lam_rows = [p[l][None, :] for p in (lam_q1, lam_k1, lam_q2, lam_k2)]
            x = _layer(x, mod_all[l, row0:row0 + n], l, layers[l], tabs, norm_g[l][None, :], lam_rows,
                       subln_c[l][None, :], w_bg16[l], b_bg[l][None, :], w_out16[l])
        outs.append(x)
        row0 += n
    return tuple(outs)
```

```python
import functools
import math

import jax
import jax.numpy as jnp
import numpy as np
from jax import lax
from jax.experimental import pallas as pl
from jax.experimental.pallas import tpu as pltpu

D_MODEL = 1024
DEPTH = 2
GRID_W = 64
EPS = 1e-6
NEG = -1e30
ROPE_THETA = 500000.0
AXIAL_THETA = 10000.0

A_GROUPS = ((128, 1), (512, 4), (2048, 16))
A_HEADS = 4
A_HEAD_DIM = 96
A_ROT = A_HEAD_DIM // 4
A_BAND = 64
N_A = len(A_GROUPS) * A_HEADS

B_HEADS = 6
B_KV_HEADS = 2
B_HEAD_DIM = 64
C_HEADS = 4
C_HEAD_DIM = 64
C_ROT = C_HEAD_DIM // 4

A_QKV = N_A * A_HEAD_DIM
A_WIDTH = A_HEADS * A_HEAD_DIM
B_WIDTH = B_HEADS * B_HEAD_DIM
B_KV = B_KV_HEADS * B_HEAD_DIM
C_WIDTH = C_HEADS * 2 * C_HEAD_DIM
IN_SIZES = (A_QKV, A_QKV, A_QKV, A_WIDTH, B_WIDTH, B_KV, B_KV, B_WIDTH, C_WIDTH, C_WIDTH, C_WIDTH, C_WIDTH)

LANE = 128
HALF = LANE // 2
ROT_SHIFT = LANE // 2
A_SLAB = A_HEADS * LANE
BF16_SUBLANES = 16
ONES_ROWS = BF16_SUBLANES
FLASH_TK = 1024
SCORE_SLOTS = 2
FLASH_UNROLL = 2
PROJ_TM = 512
PROJ_A_TM = 256
FINAL_TM = 512
LOG2E = math.log2(math.e)
LN2 = math.log(2.0)
VMEM_LIMIT = 48 * 1024 * 1024

BF16 = jnp.bfloat16
F32 = jnp.float32


def _cparams(n_axes):
    return pltpu.CompilerParams(dimension_semantics=("arbitrary",) * n_axes, vmem_limit_bytes=VMEM_LIMIT)


def _dot(a, b):
    return jnp.dot(a, b, preferred_element_type=F32)


def _dot_nt(a, b):
    return lax.dot_general(a, b, (((1,), (1,)), ((), ())), preferred_element_type=F32)


def _silu(x):
    return x * jax.nn.sigmoid(x)


def _ada_kernel(c_ref, w_ref, b_ref, o_ref):
    c = c_ref[...]
    o_ref[...] = _dot(_silu(c).astype(BF16), w_ref[...]) + b_ref[...]


def _ada_mod(c_all, w_ada, b_ada):
    rows = c_all.shape[0]
    return pl.pallas_call(
        _ada_kernel,
        grid=(DEPTH, 3),
        in_specs=[
            pl.BlockSpec((rows, D_MODEL), lambda l, n: (0, 0)),
            pl.BlockSpec((None, D_MODEL, D_MODEL), lambda l, n: (l, 0, n)),
            pl.BlockSpec((None, 1, D_MODEL), lambda l, n: (l, 0, n)),
        ],
        out_specs=pl.BlockSpec((None, rows, D_MODEL), lambda l, n: (l, 0, n)),
        out_shape=jax.ShapeDtypeStruct((DEPTH, rows, 3 * D_MODEL), F32),
        compiler_params=_cparams(2),
        name="ada_mod",
    )(c_all, w_ada, b_ada.reshape(DEPTH, 1, 3 * D_MODEL))


def _modulated_norm(x, g, scale, shift):
    ms = jnp.mean(x * x, axis=-1, keepdims=True)
    h = x * lax.rsqrt(ms + EPS) * g
    return h * (1.0 + scale) + shift


def _first_head(index):
    return (index & (HALF - 1)) < HALF // 2


def _head_norm_rope(u, gain, cos, sin, two_heads, head_dim):
    sq = u * u
    if two_heads:
        first = _first_head(lax.broadcasted_iota(jnp.int32, (1, LANE), 1))
        s_first = jnp.sum(jnp.where(first, sq, 0.0), axis=-1, keepdims=True)
        s_all = jnp.sum(sq, axis=-1, keepdims=True)
        ms = jnp.where(first, s_first, s_all - s_first) * (1.0 / head_dim)
    else:
        ms = jnp.sum(sq, axis=-1, keepdims=True) * (1.0 / head_dim)
    y = u * lax.rsqrt(ms + EPS) * gain
    return y * cos + pltpu.roll(y, ROT_SHIFT, 1) * sin


def _proj_kernel(x_ref, shift_ref, scale_ref, g_ref, w_ref, gain_ref, cos_ref, sin_ref, *refs,
                 outs, two_heads, head_dim, q_scale):
    out_refs, stage_ref = refs[:len(outs)], refs[len(outs):]
    h = _modulated_norm(x_ref[...], g_ref[...], scale_ref[...], shift_ref[...]).astype(BF16)
    cos, sin = cos_ref[...], sin_ref[...]
    tm = x_ref.shape[0]
    col = 0
    for o_ref, (width, kind, dil) in zip(out_refs, outs):
        u = _dot(h, w_ref[:, col:col + width])
        if kind == "vT":
            dv = LANE // dil
            for s in range(width // LANE):
                ut = u[:, s * LANE:(s + 1) * LANE].T.astype(BF16)
                for hd in range(dil):
                    base = (s * dil + hd) * (dv + ONES_ROWS)
                    o_ref[base:base + dv, :] = ut[hd * dv:(hd + 1) * dv]
                    o_ref[base + dv:base + dv + ONES_ROWS, :] = jnp.ones((ONES_ROWS, tm), BF16)
        else:
            if kind != "v":
                slabs = []
                for s in range(width // LANE):
                    lo = s * LANE
                    y = _head_norm_rope(u[:, lo:lo + LANE], gain_ref[:, col + lo:col + lo + LANE], cos, sin,
                                        two_heads, head_dim)
                    slabs.append(y * q_scale if kind in ("q", "qT") and q_scale != 1.0 else y)
                u = jnp.concatenate(slabs, axis=-1)
            if kind == "qT":
                for s in range(width // LANE):
                    o_ref[s * LANE:(s + 1) * LANE, :] = u[:, s * LANE:(s + 1) * LANE].T.astype(BF16)
            elif dil == 1:
                o_ref[...] = u.astype(BF16)
            else:
                stage = stage_ref[0]
                for s in range(width // LANE):
                    stage[s] = u[:, s * LANE:(s + 1) * LANE]
                rows = lambda r: pl.ds(r, tm // dil, stride=dil)
                for r in range(dil):
                    for s in range(width // LANE):
                        lo = r * width + s * LANE
                        o_ref[:, lo:lo + LANE] = stage[s, rows(r), :].astype(BF16)
        col += width


def _vt_rows(v_heads):
    return LANE + v_heads * ONES_ROWS


def _proj_out(kind, width, dil, bn, s, tm):
    if kind == "qT":
        return (bn, width, s), pl.BlockSpec((None, width, tm), lambda b, i: (b, 0, i))
    if kind == "vT":
        per = FLASH_TK // tm
        rows = width // LANE * _vt_rows(dil)
        return ((bn, s // FLASH_TK, rows, FLASH_TK),
                pl.BlockSpec((None, None, rows, tm), lambda b, i: (b, i // per, 0, i % per)))
    return (bn, s // dil, dil * width), pl.BlockSpec((None, tm // dil, dil * width), lambda b, i: (b, i, 0))


def _project(x, shift, scale, norm_g, w, gain, tables, outs, *, two_heads, head_dim, q_scale, tm, name):
    bn, s, _ = x.shape
    wcols = w.shape[1]
    kern = functools.partial(_proj_kernel, outs=outs, two_heads=two_heads, head_dim=head_dim, q_scale=q_scale)
    vec = pl.BlockSpec((None, 1, D_MODEL), lambda b, i: (b, 0, 0))
    tab = pl.BlockSpec((tm, LANE), lambda b, i: (i, 0))
    shapes, specs = zip(*[_proj_out(kind, width, dil, bn, s, tm) for width, kind, dil in outs])
    n_slabs = max(width for width, _, _ in outs) // LANE
    dilated = any(d > 1 and kind != "vT" for _, kind, d in outs)
    stage = [pltpu.VMEM((n_slabs, tm, LANE), F32)] if dilated else []
    return pl.pallas_call(
        kern,
        grid=(bn, s // tm),
        in_specs=[
            pl.BlockSpec((None, tm, D_MODEL), lambda b, i: (b, i, 0)),
            vec, vec,
            pl.BlockSpec((1, D_MODEL), lambda b, i: (0, 0)),
            pl.BlockSpec((D_MODEL, wcols), lambda b, i: (0, 0)),
            pl.BlockSpec((1, wcols), lambda b, i: (0, 0)),
            tab, tab,
        ],
        out_specs=list(specs),
        out_shape=[jax.ShapeDtypeStruct(shape, BF16) for shape in shapes],
        scratch_shapes=stage,
        compiler_params=_cparams(2),
        name=name,
    )(x, shift, scale, norm_g, w, gain, *tables)


def _band_attn_kernel(q_ref, kp_ref, kc_ref, kn_ref, vp_ref, vc_ref, vn_ref, bias_ref, o_ref, *, tb, n_blocks):
    i = pl.program_id(2)
    nk = tb + 2 * A_BAND
    colk = lax.broadcasted_iota(jnp.int32, (1, nk), 1)
    first_col = jnp.where(i == 0, A_BAND, 0)
    end_col = jnp.where(i == n_blocks - 1, tb + A_BAND, nk)
    bias = bias_ref[...] + jnp.where((colk < first_col) | (colk >= end_col), NEG, 0.0)
    lane = lax.broadcasted_iota(jnp.int32, (1, LANE), 1)
    heads = [slice(j * LANE, (j + 1) * LANE) for j in range(A_HEADS)]
    band = lambda p_ref, c_ref, n_ref, sl: jnp.concatenate([p_ref[:, sl], c_ref[:, sl], n_ref[:, sl]], axis=0)
    scores = [_dot_nt(q_ref[:, sl], band(kp_ref, kc_ref, kn_ref, sl)) for sl in heads]
    stats = []
    for s in scores:
        s = s + bias
        m = jnp.max(s, axis=-1, keepdims=True)
        p = jnp.exp2(s - m)
        stats.append((m, jnp.sum(p, axis=-1, keepdims=True), p.astype(BF16)))
    for sl, (m, l, p) in zip(heads, stats):
        o = _dot(p, band(vp_ref, vc_ref, vn_ref, sl)) / l
        o_ref[:, sl] = jnp.where(lane >= A_HEAD_DIM, m * LN2 + jnp.log(l), o)


def _band_attention(q, k, v, dilation):
    bn, length, _ = q.shape
    tb = min(256, length)
    nb = length // tb
    per = tb // A_BAND
    cur = pl.BlockSpec((None, tb, A_SLAB), lambda b, r, i: (b, i, r))
    prv = pl.BlockSpec((None, A_BAND, A_SLAB), lambda b, r, i: (b, jnp.maximum(i * per - 1, 0), r))
    nxt = pl.BlockSpec((None, A_BAND, A_SLAB), lambda b, r, i: (b, jnp.minimum((i + 1) * per, nb * per - 1), r))
    nk = tb + 2 * A_BAND
    dist = np.arange(nk)[None, :] - A_BAND - np.arange(tb)[:, None]
    bias = np.where(np.abs(dist) <= A_BAND, 0.0, NEG).astype(np.float32)
    kern = functools.partial(_band_attn_kernel, tb=tb, n_blocks=nb)
    return pl.pallas_call(
        kern,
        grid=(bn, dilation, nb),
        in_specs=[cur, prv, cur, nxt, prv, cur, nxt, pl.BlockSpec((tb, nk), lambda b, r, i: (0, 0))],
        out_specs=cur,
        out_shape=jax.ShapeDtypeStruct((bn, length, dilation * A_SLAB), F32),
        compiler_params=_cparams(3),
        name=f"band_attn_d{dilation}",
    )(q, k, k, k, v, v, v, bias)


def _stacked_flash(qt_ref, k_ref, vt_ref, acc_sc, s_sc, *, tq, n_chunks, v_heads):
    dv = LANE // v_heads
    v_rows = [slice(hd * (dv + ONES_ROWS), (hd + 1) * (dv + ONES_ROWS)) for hd in range(v_heads)]
    first = _first_head(lax.broadcasted_iota(jnp.int32, (LANE, 1), 0))
    qt = qt_ref[...]
    zero = jnp.zeros_like(qt)
    q2t = jnp.concatenate([jnp.where(first, qt, zero), jnp.where(first, zero, qt)], axis=1)
    acc_sc[...] = jnp.zeros(acc_sc.shape, F32)

    halves = [slice(0, tq), slice(tq, 2 * tq)]

    def scores(c, slot, cols):
        start = pl.multiple_of(c * FLASH_TK, FLASH_TK)
        s_sc[slot, :, cols] = _dot(k_ref[pl.ds(start, FLASH_TK), :], q2t[:, cols])

    def step(c, slot, m_prev, c_next=None):
        m_next = jnp.maximum(m_prev, jnp.max(s_sc[slot], axis=0, keepdims=True))
        vc = vt_ref[c]
        alpha = jnp.exp2(m_prev - m_next)
        for g, cols in enumerate(halves):
            if c_next is not None:
                scores(c_next, 1 - slot, cols)
            pt = jnp.exp2(s_sc[slot, :, cols] - m_next[:, cols]).astype(BF16)
            acc_sc[:, cols] = alpha[:, cols] * acc_sc[:, cols] + _dot(vc[v_rows[g % v_heads]], pt)
        return m_next

    unroll = min(FLASH_UNROLL, n_chunks)

    def group(j, m):
        for u in range(unroll):
            c = unroll * j + u
            m = step(c, u % SCORE_SLOTS, m, c + 1)
        return m

    for cols in halves:
        scores(0, 0, cols)
    m = lax.fori_loop(0, n_chunks // unroll - 1, group, jnp.full((1, 2 * tq), -jnp.inf, F32))
    for c in range(n_chunks - unroll, n_chunks):
        m = step(c, c % SCORE_SLOTS, m, c + 1 if c + 1 < n_chunks else None)
    acc = acc_sc[...]
    return acc[:dv] / acc[dv:dv + 1]


def _gqa_kernel(qt_ref, k_ref, vt_ref, o_ref, acc_sc, s_sc, *, tq, n_chunks):
    o = _stacked_flash(qt_ref, k_ref, vt_ref, acc_sc, s_sc, tq=tq, n_chunks=n_chunks, v_heads=B_KV_HEADS)
    o_ref[...] = jnp.concatenate([o[:, :tq], o[:, tq:]], axis=0).T


def _diff_kernel(qt_ref, k_ref, vt_ref, lq1_ref, lk1_ref, lq2_ref, lk2_ref, sub_ref, o_ref, acc_sc,
                 s_sc, *, tq, n_chunks, lam_init):
    o = _stacked_flash(qt_ref, k_ref, vt_ref, acc_sc, s_sc, tq=tq, n_chunks=n_chunks, v_heads=1)
    lam = (jnp.exp(jnp.sum(lq1_ref[...] * lk1_ref[...], axis=-1, keepdims=True))
           - jnp.exp(jnp.sum(lq2_ref[...] * lk2_ref[...], axis=-1, keepdims=True)) + lam_init)
    oc = (o[:, :tq] - lam * o[:, tq:]).T
    ms = jnp.mean(oc * oc, axis=-1, keepdims=True)
    o_ref[...] = oc * lax.rsqrt(ms + EPS) * sub_ref[...] * (1.0 - lam_init)


def _flash_tq(s):
    return min(512, s)


def _flash_scratch(tq, v_heads):
    return [pltpu.VMEM((LANE // v_heads + ONES_ROWS, 2 * tq), F32),
            pltpu.VMEM((SCORE_SLOTS, FLASH_TK, 2 * tq), F32)]


def _gqa_attention(qt, k, vt):
    bn, _, s = qt.shape
    tq, n_chunks = _flash_tq(s), s // FLASH_TK
    kern = functools.partial(_gqa_kernel, tq=tq, n_chunks=n_chunks)
    return pl.pallas_call(
        kern,
        grid=(bn, B_HEADS // 2, s // tq),
        in_specs=[pl.BlockSpec((None, LANE, tq), lambda b, p, i: (b, p, i)),
                  pl.BlockSpec((None, s, LANE), lambda b, p, i: (b, 0, 0)),
                  pl.BlockSpec((None, n_chunks, _vt_rows(B_KV_HEADS), FLASH_TK), lambda b, p, i: (b, 0, 0, 0))],
        out_specs=pl.BlockSpec((None, tq, LANE), lambda b, p, i: (b, i, p)),
        out_shape=jax.ShapeDtypeStruct((bn, s, B_WIDTH), F32),
        scratch_shapes=_flash_scratch(tq, B_KV_HEADS),
        compiler_params=_cparams(3),
        name="gqa_attn",
    )(qt, k, vt)


def _diff_attention(qt, k, vt, lam_rows, subln, lam_init):
    bn, _, s = qt.shape
    tq, n_chunks = _flash_tq(s), s // FLASH_TK
    kern = functools.partial(_diff_kernel, tq=tq, n_chunks=n_chunks, lam_init=lam_init)
    row = lambda n: pl.BlockSpec((1, n), lambda b, h, i: (0, 0))
    return pl.pallas_call(
        kern,
        grid=(bn, C_HEADS, s // tq),
        in_specs=[pl.BlockSpec((None, LANE, tq), lambda b, h, i: (b, h, i)),
                  pl.BlockSpec((None, s, LANE), lambda b, h, i: (b, 0, h)),
                  pl.BlockSpec((None, n_chunks, _vt_rows(1), FLASH_TK), lambda b, h, i: (b, 0, h, 0)),
                  row(C_HEAD_DIM), row(C_HEAD_DIM), row(C_HEAD_DIM), row(C_HEAD_DIM), row(LANE)],
        out_specs=pl.BlockSpec((None, tq, LANE), lambda b, h, i: (b, i, h)),
        out_shape=jax.ShapeDtypeStruct((bn, s, C_WIDTH), F32),
        scratch_shapes=_flash_scratch(tq, 1),
        compiler_params=_cparams(3),
        name="diff_attn",
    )(qt, k, vt, *lam_rows, subln)


def _final_kernel(x_ref, shift_ref, scale_ref, gate_ref, g_ref, oa0_ref, oa1_ref, oa2_ref, yb_ref, yc_ref,
                  wz_ref, wbg_ref, bbg_ref, woa_ref, wob_ref, woc_ref, wout_ref, o_ref, *stage_refs):
    x = x_ref[...]
    tm = x.shape[0]
    h = _modulated_norm(x, g_ref[...], scale_ref[...], shift_ref[...]).astype(BF16)
    lane = lax.broadcasted_iota(jnp.int32, (1, LANE), 1)

    for oa_ref, st_ref, (_, dil) in zip((oa1_ref, oa2_ref), stage_refs, A_GROUPS[1:]):
        for r in range(dil):
            for j in range(A_HEADS):
                lo = r * A_SLAB + j * LANE
                st_ref[j, pl.ds(r, tm // dil, stride=dil), :] = oa_ref[:, lo:lo + LANE]

    ya = []
    for j in range(A_HEADS):
        slabs = [oa0_ref[:, j * LANE:(j + 1) * LANE]] + [st_ref[j] for st_ref in stage_refs]
        lses = [t[:, A_HEAD_DIM:A_HEAD_DIM + 1] for t in slabs]
        top = jnp.maximum(jnp.maximum(lses[0], lses[1]), lses[2])
        ws = [jnp.exp(t - top) for t in lses]
        mix = (ws[0] * slabs[0] + ws[1] * slabs[1] + ws[2] * slabs[2]) / (ws[0] + ws[1] + ws[2])
        ya.append(jnp.where(lane < A_HEAD_DIM, mix, 0.0))
    ya = jnp.concatenate(ya, axis=-1)

    def branch(y, z_lo, z_hi, wo_ref, g_lo):
        z = _dot(h, wz_ref[:, z_lo:z_hi])
        p = _dot((y * _silu(z)).astype(BF16), wo_ref[...])
        gate = jax.nn.sigmoid(_dot(h, wbg_ref[:, g_lo:g_lo + D_MODEL]) + bbg_ref[:, g_lo:g_lo + D_MODEL])
        return gate * p

    za_hi = A_SLAB
    zb_hi = za_hi + B_WIDTH
    zc_hi = zb_hi + C_WIDTH
    merged = (branch(ya, 0, za_hi, woa_ref, 0)
              + branch(yb_ref[...], za_hi, zb_hi, wob_ref, D_MODEL)
              + branch(yc_ref[...], zb_hi, zc_hi, woc_ref, 2 * D_MODEL))
    out = _dot(merged.astype(BF16), wout_ref[...])
    o_ref[...] = x + gate_ref[...] * out


def _final(x, shift, scale, gate, norm_g, oa, yb, yc, wz, wbg, bbg, woa, wob, woc, wout, *, tm):
    bn, s, _ = x.shape
    tok = lambda w: pl.BlockSpec((None, tm, w), lambda b, i: (b, i, 0))
    vec = pl.BlockSpec((None, 1, D_MODEL), lambda b, i: (b, 0, 0))
    full = lambda a: pl.BlockSpec(a.shape, lambda b, i: (0, 0), pipeline_mode=pl.Buffered(1))
    band = [pl.BlockSpec((None, tm // dil, dil * A_SLAB), lambda b, i: (b, i, 0)) for _, dil in A_GROUPS]
    return pl.pallas_call(
        _final_kernel,
        grid=(bn, s // tm),
        in_specs=[tok(D_MODEL), vec, vec, vec, full(norm_g), *band,
                  tok(B_WIDTH), tok(C_WIDTH), full(wz), full(wbg), full(bbg), full(woa), full(wob), full(woc),
                  full(wout)],
        out_specs=tok(D_MODEL),
        out_shape=jax.ShapeDtypeStruct((bn, s, D_MODEL), F32),
        scratch_shapes=[pltpu.VMEM((A_HEADS, tm, LANE), F32) for _ in A_GROUPS[1:]],
        compiler_params=_cparams(2),
        name="final",
    )(x, shift, scale, gate, norm_g, *oa, yb, yc, wz, wbg, bbg, woa, wob, woc, wout)


B_Q_ORDER = (0, 3, 1, 4, 2, 5)


def _slab_layout(head_dim, n_heads, rot_groups):
    src = -np.ones((LANE,), np.int32)
    group = -np.ones((LANE,), np.int32)
    freq = np.zeros((LANE,), np.int32)
    sign = np.zeros((LANE,), np.float32)
    width = ROT_SHIFT // n_heads
    for hd in range(n_heads):
        low = [(start + i, g, i, -1.0) for g, (start, half) in enumerate(rot_groups) for i in range(half)]
        high = [(start + half + i, g, i, 1.0) for g, (start, half) in enumerate(rot_groups) for i in range(half)]
        rotary = {d for d, _, _, _ in low + high}
        rest = [(d, -1, 0, 0.0) for d in range(head_dim) if d not in rotary]
        n_low_rest = width - len(low)
        low, high = low + rest[:n_low_rest], high + rest[n_low_rest:]
        assert len(low) == width and len(high) <= width
        for base, items in ((hd * width, low), (hd * width + ROT_SHIFT, high)):
            for lane, (d, g, i, sg) in enumerate(items, start=base):
                src[lane], group[lane], freq[lane], sign[lane] = hd * head_dim + d, g, i, sg
    return src, group, freq, sign


LAYOUT_A = _slab_layout(A_HEAD_DIM, 1, ((0, A_ROT // 2),))
LAYOUT_B = _slab_layout(B_HEAD_DIM, 2, ((0, B_HEAD_DIM // 4), (B_HEAD_DIM // 2, B_HEAD_DIM // 4)))
LAYOUT_C = _slab_layout(C_HEAD_DIM, 2, ((0, C_ROT // 2),))


def _to_slabs(w, layout, dims_per_slab):
    src = layout[0]
    n_slabs = w.shape[-1] // dims_per_slab
    idx = np.concatenate([np.maximum(src, 0) + s * dims_per_slab for s in range(n_slabs)])
    keep = np.tile(src >= 0, n_slabs)
    return jnp.where(keep, jnp.take(w, idx, axis=-1), 0.0)


def _pad_heads(w, n_heads):
    lead = w.shape[:-1]
    w = w.reshape(*lead, n_heads, A_HEAD_DIM)
    w = jnp.pad(w, [(0, 0)] * len(lead) + [(0, 0), (0, LANE - A_HEAD_DIM)])
    return w.reshape(*lead, n_heads * LANE)


def _permute_heads(w, order, dim):
    lead = w.shape[:-1]
    w = w.reshape(*lead, len(order), dim)
    return w[..., jnp.array(order), :].reshape(*lead, len(order) * dim)


def _prep_layer(w_in, qn_a, kn_a, qn_b, kn_b, qn_c, kn_c, w_oa, w_ob, w_oc):
    offs = [0]
    for n in IN_SIZES:
        offs.append(offs[-1] + n)
    qa, ka, va, za, qb, kb, vb, zb, qc, kc, vc, zc = [w_in[:, offs[i]:offs[i + 1]] for i in range(len(IN_SIZES))]

    slab_a = functools.partial(_to_slabs, layout=LAYOUT_A, dims_per_slab=A_HEAD_DIM)
    slab_b = functools.partial(_to_slabs, layout=LAYOUT_B, dims_per_slab=2 * B_HEAD_DIM)
    slab_c = functools.partial(_to_slabs, layout=LAYOUT_C, dims_per_slab=2 * C_HEAD_DIM)
    qb = _permute_heads(qb, B_Q_ORDER, B_HEAD_DIM)

    w_a = jnp.concatenate([slab_a(qa), slab_a(ka), _pad_heads(va, N_A)], axis=-1).astype(BF16)
    gain_a = jnp.concatenate([slab_a(jnp.tile(qn_a, N_A)), slab_a(jnp.tile(kn_a, N_A)),
                              jnp.zeros((N_A * LANE,), F32)])[None, :]

    w_b = jnp.concatenate([slab_b(qb), slab_b(kb), vb], axis=-1).astype(BF16)
    gain_b = jnp.concatenate([slab_b(jnp.tile(qn_b, B_HEADS)), slab_b(jnp.tile(kn_b, B_KV_HEADS)),
                              jnp.zeros((B_KV,), F32)])[None, :]

    w_c = jnp.concatenate([slab_c(qc), slab_c(kc), vc], axis=-1).astype(BF16)
    gain_c = jnp.concatenate([slab_c(jnp.tile(qn_c, 2 * C_HEADS)), slab_c(jnp.tile(kn_c, 2 * C_HEADS)),
                              jnp.zeros((C_WIDTH,), F32)])[None, :]

    w_z = jnp.concatenate([_pad_heads(za, A_HEADS), _permute_heads(zb, B_Q_ORDER, B_HEAD_DIM), zc],
                          axis=-1).astype(BF16)
    w_oa_p = _pad_heads(w_oa.T, A_HEADS).T.astype(BF16)
    w_ob_p = _permute_heads(w_ob.T, B_Q_ORDER, B_HEAD_DIM).T.astype(BF16)
    return dict(w_a=w_a, gain_a=gain_a, w_b=w_b, gain_b=gain_b, w_c=w_c, gain_c=gain_c, w_z=w_z,
                w_oa=w_oa_p, w_ob=w_ob_p, w_oc=w_oc.astype(BF16))


def _inv_freq(dim, theta):
    return theta ** (-jnp.arange(0, dim, 2, dtype=F32) / dim)


def _rope_tables(layout, inv_freqs, streams):
    _, group, freq, sign = layout
    ang = 0.0
    for g, (inv, pos) in enumerate(zip(inv_freqs, streams)):
        ang = ang + pos[:, None] * (inv[freq] * (group == g))[None, :]
    return jnp.cos(ang), jnp.sin(ang) * sign[None, :]


def _tables(s):
    pos = jnp.arange(s)
    tok, row, col = [t.astype(F32) for t in (pos, pos // GRID_W, pos % GRID_W)]
    inv_b = _inv_freq(B_HEAD_DIM // 2, AXIAL_THETA)
    tab_a = _rope_tables(LAYOUT_A, [_inv_freq(A_ROT, ROPE_THETA)], [tok])
    tab_b = _rope_tables(LAYOUT_B, [inv_b, inv_b], [row, col])
    tab_c = _rope_tables(LAYOUT_C, [_inv_freq(C_ROT, ROPE_THETA)], [tok])
    return tab_a, tab_b, tab_c


def _layer(x, mod, l, lp, tabs, norm_g, lam_rows, subln, w_bg, b_bg, w_out):
    bn, s, _ = x.shape
    shift, scale, gate = [m.reshape(bn, 1, D_MODEL) for m in jnp.split(mod, 3, axis=-1)]
    tab_a, tab_b, tab_c = tabs
    lam_init = 0.8 - 0.6 * math.exp(-0.3 * l)

    a_outs = tuple((A_SLAB, kind, dil) for kind in ("q", "k", "v") for _, dil in A_GROUPS)
    pa = _project(x, shift, scale, norm_g, lp["w_a"], lp["gain_a"], tab_a, a_outs, two_heads=False,
                  head_dim=A_HEAD_DIM, q_scale=A_HEAD_DIM ** -0.5 * LOG2E, tm=min(PROJ_A_TM, s), name="proj_a")
    n_g = len(A_GROUPS)
    oa = [_band_attention(pa[g], pa[n_g + g], pa[2 * n_g + g], dil) for g, (_, dil) in enumerate(A_GROUPS)]

    qb, kb, vb = _project(x, shift, scale, norm_g, lp["w_b"], lp["gain_b"], tab_b,
                          ((B_WIDTH, "qT", 1), (B_KV, "k", 1), (B_KV, "vT", B_KV_HEADS)), two_heads=True,
                          head_dim=B_HEAD_DIM, q_scale=B_HEAD_DIM ** -0.5 * LOG2E, tm=min(PROJ_TM, s), name="proj_b")
    yb = _gqa_attention(qb, kb, vb)

    qc, kc, vc = _project(x, shift, scale, norm_g, lp["w_c"], lp["gain_c"], tab_c,
                          ((C_WIDTH, "qT", 1), (C_WIDTH, "k", 1), (C_WIDTH, "vT", 1)), two_heads=True,
                          head_dim=C_HEAD_DIM, q_scale=C_HEAD_DIM ** -0.5 * LOG2E, tm=min(PROJ_TM, s), name="proj_c")
    yc = _diff_attention(qc, kc, vc, lam_rows, subln, lam_init)

    return _final(x, shift, scale, gate, norm_g, oa, yb, yc, lp["w_z"], w_bg, b_bg, lp["w_oa"], lp["w_ob"],
                  lp["w_oc"], w_out, tm=min(FINAL_TM, s))


def kernel(x_prompt, x_sample, c_prompt, c_sample, norm_g, w_ada, b_ada, w_in, qn_a, kn_a, qn_b, kn_b, qn_c, kn_c,
           lam_q1, lam_k1, lam_q2, lam_k2, subln_c, w_oa, w_ob, w_oc, w_bg, b_bg, w_out):
    groups = ((x_prompt, c_prompt), (x_sample, c_sample))
    rows = [c.shape[0] for _, c in groups]
    pad = -sum(rows) % 8
    c_all = jnp.concatenate([c for _, c in groups] + [jnp.zeros((pad, D_MODEL), F32)], axis=0)
    mod_all = _ada_mod(c_all, w_ada.astype(BF16), b_ada)

    layers = [_prep_layer(w_in[l], qn_a[l], kn_a[l], qn_b[l], kn_b[l], qn_c[l], kn_c[l], w_oa[l], w_ob[l], w_oc[l])
              for l in range(DEPTH)]
    w_bg16, w_out16 = w_bg.astype(BF16), w_out.astype(BF16)

    outs = []
    row0 = 0
    for (x, c), n in zip(groups, rows):
        tabs = _tables(x.shape[1])
        for l in range(DEPTH):
            lam_rows = [p[l][None, :] for p in (lam_q1, lam_k1, lam_q2, lam_k2)]
            x = _layer(x, mod_all[l, row0:row0 + n], l, layers[l], tabs, norm_g[l][None, :], lam_rows,
                       subln_c[l][None, :], w_bg16[l], b_bg[l][None, :], w_out16[l])
        outs.append(x)
        row0 += n
    return tuple(outs)
```

```python
import functools
import math

import jax
import jax.numpy as jnp
import numpy as np
from jax import lax
from jax.experimental import pallas as pl
from jax.experimental.pallas import tpu as pltpu

D_MODEL = 1024
DEPTH = 2
GRID_W = 64
EPS = 1e-6
NEG = -1e30
ROPE_THETA = 500000.0
AXIAL_THETA = 10000.0

A_GROUPS = ((128, 1), (512, 4), (2048, 16))
A_HEADS = 4
A_HEAD_DIM = 96
A_ROT = A_HEAD_DIM // 4
A_BAND = 64
N_A = len(A_GROUPS) * A_HEADS

B_HEADS = 6
B_KV_HEADS = 2
B_HEAD_DIM = 64
C_HEADS = 4
C_HEAD_DIM = 64
C_ROT = C_HEAD_DIM // 4

A_QKV = N_A * A_HEAD_DIM
A_WIDTH = A_HEADS * A_HEAD_DIM
B_WIDTH = B_HEADS * B_HEAD_DIM
B_KV = B_KV_HEADS * B_HEAD_DIM
C_WIDTH = C_HEADS * 2 * C_HEAD_DIM
IN_SIZES = (A_QKV, A_QKV, A_QKV, A_WIDTH, B_WIDTH, B_KV, B_KV, B_WIDTH, C_WIDTH, C_WIDTH, C_WIDTH, C_WIDTH)

LANE = 128
HALF = LANE // 2
ROT_SHIFT = LANE // 2
A_SLAB = A_HEADS * LANE
BF16_SUBLANES = 16
ONES_ROWS = BF16_SUBLANES
FLASH_TK = 512
COL_GROUP = 256
SCORE_SLOTS = 2
FLASH_UNROLL = 4
PROJ_TM = 512
PROJ_A_TM = 256
FINAL_TM = 512
LOG2E = math.log2(math.e)
LN2 = math.log(2.0)
VMEM_LIMIT = 48 * 1024 * 1024

BF16 = jnp.bfloat16
F32 = jnp.float32


def _cparams(n_axes):
    return pltpu.CompilerParams(dimension_semantics=("arbitrary",) * n_axes, vmem_limit_bytes=VMEM_LIMIT)


def _dot(a, b):
    return jnp.dot(a, b, preferred_element_type=F32)


def _dot_nt(a, b):
    return lax.dot_general(a, b, (((1,), (1,)), ((), ())), preferred_element_type=F32)


def _silu(x):
    return x * jax.nn.sigmoid(x)


def _ada_kernel(c_ref, w_ref, b_ref, o_ref):
    c = c_ref[...]
    o_ref[...] = _dot(_silu(c).astype(BF16), w_ref[...]) + b_ref[...]


def _ada_mod(c_all, w_ada, b_ada):
    rows = c_all.shape[0]
    return pl.pallas_call(
        _ada_kernel,
        grid=(DEPTH, 3),
        in_specs=[
            pl.BlockSpec((rows, D_MODEL), lambda l, n: (0, 0)),
            pl.BlockSpec((None, D_MODEL, D_MODEL), lambda l, n: (l, 0, n)),
            pl.BlockSpec((None, 1, D_MODEL), lambda l, n: (l, 0, n)),
        ],
        out_specs=pl.BlockSpec((None, rows, D_MODEL), lambda l, n: (l, 0, n)),
        out_shape=jax.ShapeDtypeStruct((DEPTH, rows, 3 * D_MODEL), F32),
        compiler_params=_cparams(2),
        name="ada_mod",
    )(c_all, w_ada, b_ada.reshape(DEPTH, 1, 3 * D_MODEL))


def _modulated_norm(x, g, scale, shift):
    ms = jnp.mean(x * x, axis=-1, keepdims=True)
    h = x * lax.rsqrt(ms + EPS) * g
    return h * (1.0 + scale) + shift


def _first_head(index):
    return (index & (HALF - 1)) < HALF // 2


def _head_norm_rope(u, gain, cos, sin, two_heads, head_dim):
    sq = u * u
    if two_heads:
        first = _first_head(lax.broadcasted_iota(jnp.int32, (1, LANE), 1))
        s_first = jnp.sum(jnp.where(first, sq, 0.0), axis=-1, keepdims=True)
        s_all = jnp.sum(sq, axis=-1, keepdims=True)
        ms = jnp.where(first, s_first, s_all - s_first) * (1.0 / head_dim)
    else:
        ms = jnp.sum(sq, axis=-1, keepdims=True) * (1.0 / head_dim)
    y = u * lax.rsqrt(ms + EPS) * gain
    return y * cos + pltpu.roll(y, ROT_SHIFT, 1) * sin


def _proj_kernel(x_ref, shift_ref, scale_ref, g_ref, w_ref, gain_ref, cos_ref, sin_ref, *refs,
                 outs, two_heads, head_dim, q_scale):
    out_refs, stage_ref = refs[:len(outs)], refs[len(outs):]
    h = _modulated_norm(x_ref[...], g_ref[...], scale_ref[...], shift_ref[...]).astype(BF16)
    cos, sin = cos_ref[...], sin_ref[...]
    tm = x_ref.shape[0]
    col = 0
    for o_ref, (width, kind, dil) in zip(out_refs, outs):
        u = _dot(h, w_ref[:, col:col + width])
        if kind == "vT":
            dv = LANE // dil
            for s in range(width // LANE):
                ut = u[:, s * LANE:(s + 1) * LANE].T.astype(BF16)
                for hd in range(dil):
                    base = (s * dil + hd) * (dv + ONES_ROWS)
                    o_ref[base:base + dv, :] = ut[hd * dv:(hd + 1) * dv]
                    o_ref[base + dv:base + dv + ONES_ROWS, :] = jnp.ones((ONES_ROWS, tm), BF16)
        else:
            if kind != "v":
                slabs = []
                for s in range(width // LANE):
                    lo = s * LANE
                    y = _head_norm_rope(u[:, lo:lo + LANE], gain_ref[:, col + lo:col + lo + LANE], cos, sin,
                                        two_heads, head_dim)
                    slabs.append(y * q_scale if kind in ("q", "qT") and q_scale != 1.0 else y)
                u = jnp.concatenate(slabs, axis=-1)
            if kind == "qT":
                for s in range(width // LANE):
                    o_ref[s * LANE:(s + 1) * LANE, :] = u[:, s * LANE:(s + 1) * LANE].T.astype(BF16)
            elif dil == 1:
                o_ref[...] = u.astype(BF16)
            else:
                stage = stage_ref[0]
                for s in range(width // LANE):
                    stage[s] = u[:, s * LANE:(s + 1) * LANE]
                rows = lambda r: pl.ds(r, tm // dil, stride=dil)
                for r in range(dil):
                    for s in range(width // LANE):
                        lo = r * width + s * LANE
                        o_ref[:, lo:lo + LANE] = stage[s, rows(r), :].astype(BF16)
        col += width


def _vt_rows(v_heads):
    return LANE + v_heads * ONES_ROWS


def _proj_out(kind, width, dil, bn, s, tm):
    if kind == "qT":
        return (bn, width, s), pl.BlockSpec((None, width, tm), lambda b, i: (b, 0, i))
    if kind == "vT":
        per = FLASH_TK // tm
        rows = width // LANE * _vt_rows(dil)
        return ((bn, s // FLASH_TK, rows, FLASH_TK),
                pl.BlockSpec((None, None, rows, tm), lambda b, i: (b, i // per, 0, i % per)))
    return (bn, s // dil, dil * width), pl.BlockSpec((None, tm // dil, dil * width), lambda b, i: (b, i, 0))


def _project(x, shift, scale, norm_g, w, gain, tables, outs, *, two_heads, head_dim, q_scale, tm, name):
    bn, s, _ = x.shape
    wcols = w.shape[1]
    kern = functools.partial(_proj_kernel, outs=outs, two_heads=two_heads, head_dim=head_dim, q_scale=q_scale)
    vec = pl.BlockSpec((None, 1, D_MODEL), lambda b, i: (b, 0, 0))
    tab = pl.BlockSpec((tm, LANE), lambda b, i: (i, 0))
    shapes, specs = zip(*[_proj_out(kind, width, dil, bn, s, tm) for width, kind, dil in outs])
    n_slabs = max(width for width, _, _ in outs) // LANE
    dilated = any(d > 1 and kind != "vT" for _, kind, d in outs)
    stage = [pltpu.VMEM((n_slabs, tm, LANE), F32)] if dilated else []
    return pl.pallas_call(
        kern,
        grid=(bn, s // tm),
        in_specs=[
            pl.BlockSpec((None, tm, D_MODEL), lambda b, i: (b, i, 0)),
            vec, vec,
            pl.BlockSpec((1, D_MODEL), lambda b, i: (0, 0)),
            pl.BlockSpec((D_MODEL, wcols), lambda b, i: (0, 0)),
            pl.BlockSpec((1, wcols), lambda b, i: (0, 0)),
            tab, tab,
        ],
        out_specs=list(specs),
        out_shape=[jax.ShapeDtypeStruct(shape, BF16) for shape in shapes],
        scratch_shapes=stage,
        compiler_params=_cparams(2),
        name=name,
    )(x, shift, scale, norm_g, w, gain, *tables)


def _band_attn_kernel(q_ref, kp_ref, kc_ref, kn_ref, vp_ref, vc_ref, vn_ref, bias_ref, o_ref, *, tb, n_blocks):
    i = pl.program_id(2)
    nk = tb + 2 * A_BAND
    colk = lax.broadcasted_iota(jnp.int32, (1, nk), 1)
    first_col = jnp.where(i == 0, A_BAND, 0)
    end_col = jnp.where(i == n_blocks - 1, tb + A_BAND, nk)
    bias = bias_ref[...] + jnp.where((colk < first_col) | (colk >= end_col), NEG, 0.0)
    lane = lax.broadcasted_iota(jnp.int32, (1, LANE), 1)
    heads = [slice(j * LANE, (j + 1) * LANE) for j in range(A_HEADS)]
    band = lambda p_ref, c_ref, n_ref, sl: jnp.concatenate([p_ref[:, sl], c_ref[:, sl], n_ref[:, sl]], axis=0)
    scores = [_dot_nt(q_ref[:, sl], band(kp_ref, kc_ref, kn_ref, sl)) for sl in heads]
    stats = []
    for s in scores:
        s = s + bias
        m = jnp.max(s, axis=-1, keepdims=True)
        p = jnp.exp2(s - m)
        stats.append((m, jnp.sum(p, axis=-1, keepdims=True), p.astype(BF16)))
    for sl, (m, l, p) in zip(heads, stats):
        o = _dot(p, band(vp_ref, vc_ref, vn_ref, sl)) / l
        o_ref[:, sl] = jnp.where(lane >= A_HEAD_DIM, m * LN2 + jnp.log(l), o)


def _band_attention(q, k, v, dilation):
    bn, length, _ = q.shape
    tb = min(256, length)
    nb = length // tb
    per = tb // A_BAND
    cur = pl.BlockSpec((None, tb, A_SLAB), lambda b, r, i: (b, i, r))
    prv = pl.BlockSpec((None, A_BAND, A_SLAB), lambda b, r, i: (b, jnp.maximum(i * per - 1, 0), r))
    nxt = pl.BlockSpec((None, A_BAND, A_SLAB), lambda b, r, i: (b, jnp.minimum((i + 1) * per, nb * per - 1), r))
    nk = tb + 2 * A_BAND
    dist = np.arange(nk)[None, :] - A_BAND - np.arange(tb)[:, None]
    bias = np.where(np.abs(dist) <= A_BAND, 0.0, NEG).astype(np.float32)
    kern = functools.partial(_band_attn_kernel, tb=tb, n_blocks=nb)
    return pl.pallas_call(
        kern,
        grid=(bn, dilation, nb),
        in_specs=[cur, prv, cur, nxt, prv, cur, nxt, pl.BlockSpec((tb, nk), lambda b, r, i: (0, 0))],
        out_specs=cur,
        out_shape=jax.ShapeDtypeStruct((bn, length, dilation * A_SLAB), F32),
        compiler_params=_cparams(3),
        name=f"band_attn_d{dilation}",
    )(q, k, k, k, v, v, v, bias)


def _stacked_flash(qt_ref, k_ref, vt_ref, acc_sc, s_sc, *, tq, n_chunks, v_heads):
    dv = LANE // v_heads
    v_rows = [slice(hd * (dv + ONES_ROWS), (hd + 1) * (dv + ONES_ROWS)) for hd in range(v_heads)]
    first = _first_head(lax.broadcasted_iota(jnp.int32, (LANE, 1), 0))
    qt = qt_ref[...]
    zero = jnp.zeros_like(qt)
    q2t = jnp.concatenate([jnp.where(first, qt, zero), jnp.where(first, zero, qt)], axis=1)
    acc_sc[...] = jnp.zeros(acc_sc.shape, F32)

    width = min(COL_GROUP, tq)
    halves = [slice(lo, lo + width) for lo in range(0, 2 * tq, width)]

    def scores(c, slot, cols):
        start = pl.multiple_of(c * FLASH_TK, FLASH_TK)
        s_sc[slot, :, cols] = _dot(k_ref[pl.ds(start, FLASH_TK), :], q2t[:, cols])

    def step(c, slot, m_prev, c_next=None):
        m_next = jnp.maximum(m_prev, jnp.max(s_sc[slot], axis=0, keepdims=True))
        vc = vt_ref[c]
        alpha = jnp.exp2(m_prev - m_next)
        for cols in halves:
            if c_next is not None:
                scores(c_next, 1 - slot, cols)
            pt = jnp.exp2(s_sc[slot, :, cols] - m_next[:, cols]).astype(BF16)
            head = cols.start // tq
            acc_sc[:, cols] = alpha[:, cols] * acc_sc[:, cols] + _dot(vc[v_rows[head % v_heads]], pt)
        return m_next

    unroll = min(FLASH_UNROLL, n_chunks)

    def group(j, m):
        for u in range(unroll):
            c = unroll * j + u
            m = step(c, u % SCORE_SLOTS, m, c + 1)
        return m

    for cols in halves:
        scores(0, 0, cols)
    m = lax.fori_loop(0, n_chunks // unroll - 1, group, jnp.full((1, 2 * tq), -jnp.inf, F32))
    for c in range(n_chunks - unroll, n_chunks):
        m = step(c, c % SCORE_SLOTS, m, c + 1 if c + 1 < n_chunks else None)
    acc = acc_sc[...]
    return acc[:dv] / acc[dv:dv + 1]


def _gqa_kernel(qt_ref, k_ref, vt_ref, o_ref, acc_sc, s_sc, *, tq, n_chunks):
    o = _stacked_flash(qt_ref, k_ref, vt_ref, acc_sc, s_sc, tq=tq, n_chunks=n_chunks, v_heads=B_KV_HEADS)
    o_ref[...] = jnp.concatenate([o[:, :tq], o[:, tq:]], axis=0).T


def _diff_kernel(qt_ref, k_ref, vt_ref, lq1_ref, lk1_ref, lq2_ref, lk2_ref, sub_ref, o_ref, acc_sc,
                 s_sc, *, tq, n_chunks, lam_init):
    o = _stacked_flash(qt_ref, k_ref, vt_ref, acc_sc, s_sc, tq=tq, n_chunks=n_chunks, v_heads=1)
    lam = (jnp.exp(jnp.sum(lq1_ref[...] * lk1_ref[...], axis=-1, keepdims=True))
           - jnp.exp(jnp.sum(lq2_ref[...] * lk2_ref[...], axis=-1, keepdims=True)) + lam_init)
    oc = (o[:, :tq] - lam * o[:, tq:]).T
    ms = jnp.mean(oc * oc, axis=-1, keepdims=True)
    o_ref[...] = oc * lax.rsqrt(ms + EPS) * sub_ref[...] * (1.0 - lam_init)


def _flash_tq(s):
    return min(512, s)


def _flash_scratch(tq, v_heads):
    return [pltpu.VMEM((LANE // v_heads + ONES_ROWS, 2 * tq), F32),
            pltpu.VMEM((SCORE_SLOTS, FLASH_TK, 2 * tq), F32)]


def _gqa_attention(qt, k, vt):
    bn, _, s = qt.shape
    tq, n_chunks = _flash_tq(s), s // FLASH_TK
    kern = functools.partial(_gqa_kernel, tq=tq, n_chunks=n_chunks)
    return pl.pallas_call(
        kern,
        grid=(bn, B_HEADS // 2, s // tq),
        in_specs=[pl.BlockSpec((None, LANE, tq), lambda b, p, i: (b, p, i)),
                  pl.BlockSpec((None, s, LANE), lambda b, p, i: (b, 0, 0)),
                  pl.BlockSpec((None, n_chunks, _vt_rows(B_KV_HEADS), FLASH_TK), lambda b, p, i: (b, 0, 0, 0))],
        out_specs=pl.BlockSpec((None, tq, LANE), lambda b, p, i: (b, i, p)),
        out_shape=jax.ShapeDtypeStruct((bn, s, B_WIDTH), F32),
        scratch_shapes=_flash_scratch(tq, B_KV_HEADS),
        compiler_params=_cparams(3),
        name="gqa_attn",
    )(qt, k, vt)


def _diff_attention(qt, k, vt, lam_rows, subln, lam_init):
    bn, _, s = qt.shape
    tq, n_chunks = _flash_tq(s), s // FLASH_TK
    kern = functools.partial(_diff_kernel, tq=tq, n_chunks=n_chunks, lam_init=lam_init)
    row = lambda n: pl.BlockSpec((1, n), lambda b, h, i: (0, 0))
    return pl.pallas_call(
        kern,
        grid=(bn, C_HEADS, s // tq),
        in_specs=[pl.BlockSpec((None, LANE, tq), lambda b, h, i: (b, h, i)),
                  pl.BlockSpec((None, s, LANE), lambda b, h, i: (b, 0, h)),
                  pl.BlockSpec((None, n_chunks, _vt_rows(1), FLASH_TK), lambda b, h, i: (b, 0, h, 0)),
                  row(C_HEAD_DIM), row(C_HEAD_DIM), row(C_HEAD_DIM), row(C_HEAD_DIM), row(LANE)],
        out_specs=pl.BlockSpec((None, tq, LANE), lambda b, h, i: (b, i, h)),
        out_shape=jax.ShapeDtypeStruct((bn, s, C_WIDTH), F32),
        scratch_shapes=_flash_scratch(tq, 1),
        compiler_params=_cparams(3),
        name="diff_attn",
    )(qt, k, vt, *lam_rows, subln)


def _final_kernel(x_ref, shift_ref, scale_ref, gate_ref, g_ref, oa0_ref, oa1_ref, oa2_ref, yb_ref, yc_ref,
                  wz_ref, wbg_ref, bbg_ref, woa_ref, wob_ref, woc_ref, wout_ref, o_ref, *stage_refs):
    x = x_ref[...]
    tm = x.shape[0]
    h = _modulated_norm(x, g_ref[...], scale_ref[...], shift_ref[...]).astype(BF16)
    lane = lax.broadcasted_iota(jnp.int32, (1, LANE), 1)

    for oa_ref, st_ref, (_, dil) in zip((oa1_ref, oa2_ref), stage_refs, A_GROUPS[1:]):
        for r in range(dil):
            for j in range(A_HEADS):
                lo = r * A_SLAB + j * LANE
                st_ref[j, pl.ds(r, tm // dil, stride=dil), :] = oa_ref[:, lo:lo + LANE]

    ya = []
    for j in range(A_HEADS):
        slabs = [oa0_ref[:, j * LANE:(j + 1) * LANE]] + [st_ref[j] for st_ref in stage_refs]
        lses = [t[:, A_HEAD_DIM:A_HEAD_DIM + 1] for t in slabs]
        top = jnp.maximum(jnp.maximum(lses[0], lses[1]), lses[2])
        ws = [jnp.exp(t - top) for t in lses]
        mix = (ws[0] * slabs[0] + ws[1] * slabs[1] + ws[2] * slabs[2]) / (ws[0] + ws[1] + ws[2])
        ya.append(jnp.where(lane < A_HEAD_DIM, mix, 0.0))
    ya = jnp.concatenate(ya, axis=-1)

    def branch(y, z_lo, z_hi, wo_ref, g_lo):
        z = _dot(h, wz_ref[:, z_lo:z_hi])
        p = _dot((y * _silu(z)).astype(BF16), wo_ref[...])
        gate = jax.nn.sigmoid(_dot(h, wbg_ref[:, g_lo:g_lo + D_MODEL]) + bbg_ref[:, g_lo:g_lo + D_MODEL])
        return gate * p

    za_hi = A_SLAB
    zb_hi = za_hi + B_WIDTH
    zc_hi = zb_hi + C_WIDTH
    merged = (branch(ya, 0, za_hi, woa_ref, 0)
              + branch(yb_ref[...], za_hi, zb_hi, wob_ref, D_MODEL)
              + branch(yc_ref[...], zb_hi, zc_hi, woc_ref, 2 * D_MODEL))
    out = _dot(merged.astype(BF16), wout_ref[...])
    o_ref[...] = x + gate_ref[...] * out


def _final(x, shift, scale, gate, norm_g, oa, yb, yc, wz, wbg, bbg, woa, wob, woc, wout, *, tm):
    bn, s, _ = x.shape
    tok = lambda w: pl.BlockSpec((None, tm, w), lambda b, i: (b, i, 0))
    vec = pl.BlockSpec((None, 1, D_MODEL), lambda b, i: (b, 0, 0))
    full = lambda a: pl.BlockSpec(a.shape, lambda b, i: (0, 0), pipeline_mode=pl.Buffered(1))
    band = [pl.BlockSpec((None, tm // dil, dil * A_SLAB), lambda b, i: (b, i, 0)) for _, dil in A_GROUPS]
    return pl.pallas_call(
        _final_kernel,
        grid=(bn, s // tm),
        in_specs=[tok(D_MODEL), vec, vec, vec, full(norm_g), *band,
                  tok(B_WIDTH), tok(C_WIDTH), full(wz), full(wbg), full(bbg), full(woa), full(wob), full(woc),
                  full(wout)],
        out_specs=tok(D_MODEL),
        out_shape=jax.ShapeDtypeStruct((bn, s, D_MODEL), F32),
        scratch_shapes=[pltpu.VMEM((A_HEADS, tm, LANE), F32) for _ in A_GROUPS[1:]],
        compiler_params=_cparams(2),
        name="final",
    )(x, shift, scale, gate, norm_g, *oa, yb, yc, wz, wbg, bbg, woa, wob, woc, wout)


B_Q_ORDER = (0, 3, 1, 4, 2, 5)


def _slab_layout(head_dim, n_heads, rot_groups):
    src = -np.ones((LANE,), np.int32)
    group = -np.ones((LANE,), np.int32)
    freq = np.zeros((LANE,), np.int32)
    sign = np.zeros((LANE,), np.float32)
    width = ROT_SHIFT // n_heads
    for hd in range(n_heads):
        low = [(start + i, g, i, -1.0) for g, (start, half) in enumerate(rot_groups) for i in range(half)]
        high = [(start + half + i, g, i, 1.0) for g, (start, half) in enumerate(rot_groups) for i in range(half)]
        rotary = {d for d, _, _, _ in low + high}
        rest = [(d, -1, 0, 0.0) for d in range(head_dim) if d not in rotary]
        n_low_rest = width - len(low)
        low, high = low + rest[:n_low_rest], high + rest[n_low_rest:]
        assert len(low) == width and len(high) <= width
        for base, items in ((hd * width, low), (hd * width + ROT_SHIFT, high)):
            for lane, (d, g, i, sg) in enumerate(items, start=base):
                src[lane], group[lane], freq[lane], sign[lane] = hd * head_dim + d, g, i, sg
    return src, group, freq, sign


LAYOUT_A = _slab_layout(A_HEAD_DIM, 1, ((0, A_ROT // 2),))
LAYOUT_B = _slab_layout(B_HEAD_DIM, 2, ((0, B_HEAD_DIM // 4), (B_HEAD_DIM // 2, B_HEAD_DIM // 4)))
LAYOUT_C = _slab_layout(C_HEAD_DIM, 2, ((0, C_ROT // 2),))


def _to_slabs(w, layout, dims_per_slab):
    src = layout[0]
    n_slabs = w.shape[-1] // dims_per_slab
    idx = np.concatenate([np.maximum(src, 0) + s * dims_per_slab for s in range(n_slabs)])
    keep = np.tile(src >= 0, n_slabs)
    return jnp.where(keep, jnp.take(w, idx, axis=-1), 0.0)


def _pad_heads(w, n_heads):
    lead = w.shape[:-1]
    w = w.reshape(*lead, n_heads, A_HEAD_DIM)
    w = jnp.pad(w, [(0, 0)] * len(lead) + [(0, 0), (0, LANE - A_HEAD_DIM)])
    return w.reshape(*lead, n_heads * LANE)


def _permute_heads(w, order, dim):
    lead = w.shape[:-1]
    w = w.reshape(*lead, len(order), dim)
    return w[..., jnp.array(order), :].reshape(*lead, len(order) * dim)


def _prep_layer(w_in, qn_a, kn_a, qn_b, kn_b, qn_c, kn_c, w_oa, w_ob, w_oc):
    offs = [0]
    for n in IN_SIZES:
        offs.append(offs[-1] + n)
    qa, ka, va, za, qb, kb, vb, zb, qc, kc, vc, zc = [w_in[:, offs[i]:offs[i + 1]] for i in range(len(IN_SIZES))]

    slab_a = functools.partial(_to_slabs, layout=LAYOUT_A, dims_per_slab=A_HEAD_DIM)
    slab_b = functools.partial(_to_slabs, layout=LAYOUT_B, dims_per_slab=2 * B_HEAD_DIM)
    slab_c = functools.partial(_to_slabs, layout=LAYOUT_C, dims_per_slab=2 * C_HEAD_DIM)
    qb = _permute_heads(qb, B_Q_ORDER, B_HEAD_DIM)

    w_a = jnp.concatenate([slab_a(qa), slab_a(ka), _pad_heads(va, N_A)], axis=-1).astype(BF16)
    gain_a = jnp.concatenate([slab_a(jnp.tile(qn_a, N_A)), slab_a(jnp.tile(kn_a, N_A)),
                              jnp.zeros((N_A * LANE,), F32)])[None, :]

    w_b = jnp.concatenate([slab_b(qb), slab_b(kb), vb], axis=-1).astype(BF16)
    gain_b = jnp.concatenate([slab_b(jnp.tile(qn_b, B_HEADS)), slab_b(jnp.tile(kn_b, B_KV_HEADS)),
                              jnp.zeros((B_KV,), F32)])[None, :]

    w_c = jnp.concatenate([slab_c(qc), slab_c(kc), vc], axis=-1).astype(BF16)
    gain_c = jnp.concatenate([slab_c(jnp.tile(qn_c, 2 * C_HEADS)), slab_c(jnp.tile(kn_c, 2 * C_HEADS)),
                              jnp.zeros((C_WIDTH,), F32)])[None, :]

    w_z = jnp.concatenate([_pad_heads(za, A_HEADS), _permute_heads(zb, B_Q_ORDER, B_HEAD_DIM), zc],
                          axis=-1).astype(BF16)
    w_oa_p = _pad_heads(w_oa.T, A_HEADS).T.astype(BF16)
    w_ob_p = _permute_heads(w_ob.T, B_Q_ORDER, B_HEAD_DIM).T.astype(BF16)
    return dict(w_a=w_a, gain_a=gain_a, w_b=w_b, gain_b=gain_b, w_c=w_c, gain_c=gain_c, w_z=w_z,
                w_oa=w_oa_p, w_ob=w_ob_p, w_oc=w_oc.astype(BF16))


def _inv_freq(dim, theta):
    return theta ** (-jnp.arange(0, dim, 2, dtype=F32) / dim)


def _rope_tables(layout, inv_freqs, streams):
    _, group, freq, sign = layout
    ang = 0.0
    for g, (inv, pos) in enumerate(zip(inv_freqs, streams)):
        ang = ang + pos[:, None] * (inv[freq] * (group == g))[None, :]
    return jnp.cos(ang), jnp.sin(ang) * sign[None, :]


def _tables(s):
    pos = jnp.arange(s)
    tok, row, col = [t.astype(F32) for t in (pos, pos // GRID_W, pos % GRID_W)]
    inv_b = _inv_freq(B_HEAD_DIM // 2, AXIAL_THETA)
    tab_a = _rope_tables(LAYOUT_A, [_inv_freq(A_ROT, ROPE_THETA)], [tok])
    tab_b = _rope_tables(LAYOUT_B, [inv_b, inv_b], [row, col])
    tab_c = _rope_tables(LAYOUT_C, [_inv_freq(C_ROT, ROPE_THETA)], [tok])
    return tab_a, tab_b, tab_c


def _layer(x, mod, l, lp, tabs, norm_g, lam_rows, subln, w_bg, b_bg, w_out):
    bn, s, _ = x.shape
    shift, scale, gate = [m.reshape(bn, 1, D_MODEL) for m in jnp.split(mod, 3, axis=-1)]
    tab_a, tab_b, tab_c = tabs
    lam_init = 0.8 - 0.6 * math.exp(-0.3 * l)

    a_outs = tuple((A_SLAB, kind, dil) for kind in ("q", "k", "v") for _, dil in A_GROUPS)
    pa = _project(x, shift, scale, norm_g, lp["w_a"], lp["gain_a"], tab_a, a_outs, two_heads=False,
                  head_dim=A_HEAD_DIM, q_scale=A_HEAD_DIM ** -0.5 * LOG2E, tm=min(PROJ_A_TM, s), name="proj_a")
    n_g = len(A_GROUPS)
    oa = [_band_attention(pa[g], pa[n_g + g], pa[2 * n_g + g], dil) for g, (_, dil) in enumerate(A_GROUPS)]

    qb, kb, vb = _project(x, shift, scale, norm_g, lp["w_b"], lp["gain_b"], tab_b,
                          ((B_WIDTH, "qT", 1), (B_KV, "k", 1), (B_KV, "vT", B_KV_HEADS)), two_heads=True,
                          head_dim=B_HEAD_DIM, q_scale=B_HEAD_DIM ** -0.5 * LOG2E, tm=min(PROJ_TM, s), name="proj_b")
    yb = _gqa_attention(qb, kb, vb)

    qc, kc, vc = _project(x, shift, scale, norm_g, lp["w_c"], lp["gain_c"], tab_c,
                          ((C_WIDTH, "qT", 1), (C_WIDTH, "k", 1), (C_WIDTH, "vT", 1)), two_heads=True,
                          head_dim=C_HEAD_DIM, q_scale=C_HEAD_DIM ** -0.5 * LOG2E, tm=min(PROJ_TM, s), name="proj_c")
    yc = _diff_attention(qc, kc, vc, lam_rows, subln, lam_init)

    return _final(x, shift, scale, gate, norm_g, oa, yb, yc, lp["w_z"], w_bg, b_bg, lp["w_oa"], lp["w_ob"],
                  lp["w_oc"], w_out, tm=min(FINAL_TM, s))


def kernel(x_prompt, x_sample, c_prompt, c_sample, norm_g, w_ada, b_ada, w_in, qn_a, kn_a, qn_b, kn_b, qn_c, kn_c,
           lam_q1, lam_k1, lam_q2, lam_k2, subln_c, w_oa, w_ob, w_oc, w_bg, b_bg, w_out):
    groups = ((x_prompt, c_prompt), (x_sample, c_sample))
    rows = [c.shape[0] for _, c in groups]
    pad = -sum(rows) % 8
    c_all = jnp.concatenate([c for _, c in groups] + [jnp.zeros((pad, D_MODEL), F32)], axis=0)
    mod_all = _ada_mod(c_all, w_ada.astype(BF16), b_ada)

    layers = [_prep_layer(w_in[l], qn_a[l], kn_a[l], qn_b[l], kn_b[l], qn_c[l], kn_c[l], w_oa[l], w_ob[l], w_oc[l])
              for l in range(DEPTH)]
    w_bg16, w_out16 = w_bg.astype(BF16), w_out.astype(BF16)

    outs = []
    row0 = 0
    for (x, c), n in zip(groups, rows):
        tabs = _tables(x.shape[1])
        for l in range(DEPTH):
            lam_rows = [p[l][None, :] for p in (lam_q1, lam_k1, lam_q2, lam_k2)]
            x = _layer(x, mod_all[l, row0:row0 + n], l, layers[l], tabs, norm_g[l][None, :], lam_rows,
                       subln_c[l][None, :], w_bg16[l], b_bg[l][None, :], w_out16[l])
        outs.append(x)
        row0 += n
    return tuple(outs)
```

```python
import functools
import math

import jax
import jax.numpy as jnp
import numpy as np
from jax import lax
from jax.experimental import pallas as pl
from jax.experimental.pallas import tpu as pltpu

D_MODEL = 1024
DEPTH = 2
GRID_W = 64
EPS = 1e-6
NEG = -1e30
ROPE_THETA = 500000.0
AXIAL_THETA = 10000.0

A_GROUPS = ((128, 1), (512, 4), (2048, 16))
A_HEADS = 4
A_HEAD_DIM = 96
A_ROT = A_HEAD_DIM // 4
A_BAND = 64
N_A = len(A_GROUPS) * A_HEADS

B_HEADS = 6
B_KV_HEADS = 2
B_HEAD_DIM = 64
C_HEADS = 4
C_HEAD_DIM = 64
C_ROT = C_HEAD_DIM // 4

A_QKV = N_A * A_HEAD_DIM
A_WIDTH = A_HEADS * A_HEAD_DIM
B_WIDTH = B_HEADS * B_HEAD_DIM
B_KV = B_KV_HEADS * B_HEAD_DIM
C_WIDTH = C_HEADS * 2 * C_HEAD_DIM
IN_SIZES = (A_QKV, A_QKV, A_QKV, A_WIDTH, B_WIDTH, B_KV, B_KV, B_WIDTH, C_WIDTH, C_WIDTH, C_WIDTH, C_WIDTH)

LANE = 128
HALF = LANE // 2
ROT_SHIFT = LANE // 2
A_SLAB = A_HEADS * LANE
BF16_SUBLANES = 16
ONES_ROWS = BF16_SUBLANES
FLASH_TK = 512
COL_GROUP = 256
SCORE_SLOTS = 2
FLASH_UNROLL = 4
PROJ_TM = 512
PROJ_A_TM = 256
FINAL_TM = 512
LOG2E = math.log2(math.e)
LN2 = math.log(2.0)
VMEM_LIMIT = 48 * 1024 * 1024

BF16 = jnp.bfloat16
F32 = jnp.float32


def _cparams(n_axes):
    return pltpu.CompilerParams(dimension_semantics=("arbitrary",) * n_axes, vmem_limit_bytes=VMEM_LIMIT)


def _dot(a, b):
    return jnp.dot(a, b, preferred_element_type=F32)


def _dot_nt(a, b):
    return lax.dot_general(a, b, (((1,), (1,)), ((), ())), preferred_element_type=F32)


def _silu(x):
    return x * jax.nn.sigmoid(x)


def _ada_kernel(c_ref, w_ref, b_ref, o_ref):
    c = c_ref[...]
    o_ref[...] = _dot(_silu(c).astype(BF16), w_ref[...]) + b_ref[...]


def _ada_mod(c_all, w_ada, b_ada):
    rows = c_all.shape[0]
    return pl.pallas_call(
        _ada_kernel,
        grid=(DEPTH, 3),
        in_specs=[
            pl.BlockSpec((rows, D_MODEL), lambda l, n: (0, 0)),
            pl.BlockSpec((None, D_MODEL, D_MODEL), lambda l, n: (l, 0, n)),
            pl.BlockSpec((None, 1, D_MODEL), lambda l, n: (l, 0, n)),
        ],
        out_specs=pl.BlockSpec((None, rows, D_MODEL), lambda l, n: (l, 0, n)),
        out_shape=jax.ShapeDtypeStruct((DEPTH, rows, 3 * D_MODEL), F32),
        compiler_params=_cparams(2),
        name="ada_mod",
    )(c_all, w_ada, b_ada.reshape(DEPTH, 1, 3 * D_MODEL))


def _modulated_norm(x, g, scale, shift):
    ms = jnp.mean(x * x, axis=-1, keepdims=True)
    h = x * lax.rsqrt(ms + EPS) * g
    return h * (1.0 + scale) + shift


def _first_head(index):
    return (index & (HALF - 1)) < HALF // 2


def _head_norm_rope(u, gain, cos, sin, two_heads, head_dim):
    sq = u * u
    if two_heads:
        first = _first_head(lax.broadcasted_iota(jnp.int32, (1, LANE), 1))
        s_first = jnp.sum(jnp.where(first, sq, 0.0), axis=-1, keepdims=True)
        s_all = jnp.sum(sq, axis=-1, keepdims=True)
        ms = jnp.where(first, s_first, s_all - s_first) * (1.0 / head_dim)
    else:
        ms = jnp.sum(sq, axis=-1, keepdims=True) * (1.0 / head_dim)
    y = u * lax.rsqrt(ms + EPS) * gain
    return y * cos + pltpu.roll(y, ROT_SHIFT, 1) * sin


def _proj_kernel(x_ref, shift_ref, scale_ref, g_ref, w_ref, gain_ref, cos_ref, sin_ref, *refs,
                 outs, two_heads, head_dim, q_scale):
    out_refs, stage_ref = refs[:len(outs)], refs[len(outs):]
    h = _modulated_norm(x_ref[...], g_ref[...], scale_ref[...], shift_ref[...]).astype(BF16)
    cos, sin = cos_ref[...], sin_ref[...]
    tm = x_ref.shape[0]
    col = 0
    for o_ref, (width, kind, dil) in zip(out_refs, outs):
        u = _dot(h, w_ref[:, col:col + width])
        if kind == "vT":
            dv = LANE // dil
            for s in range(width // LANE):
                ut = u[:, s * LANE:(s + 1) * LANE].T.astype(BF16)
                for hd in range(dil):
                    base = (s * dil + hd) * (dv + ONES_ROWS)
                    o_ref[base:base + dv, :] = ut[hd * dv:(hd + 1) * dv]
                    o_ref[base + dv:base + dv + ONES_ROWS, :] = jnp.ones((ONES_ROWS, tm), BF16)
        else:
            if kind != "v":
                slabs = []
                for s in range(width // LANE):
                    lo = s * LANE
                    y = _head_norm_rope(u[:, lo:lo + LANE], gain_ref[:, col + lo:col + lo + LANE], cos, sin,
                                        two_heads, head_dim)
                    slabs.append(y * q_scale if kind in ("q", "qT") and q_scale != 1.0 else y)
                u = jnp.concatenate(slabs, axis=-1)
            if kind == "qT":
                for s in range(width // LANE):
                    o_ref[s * LANE:(s + 1) * LANE, :] = u[:, s * LANE:(s + 1) * LANE].T.astype(BF16)
            elif dil == 1:
                o_ref[...] = u.astype(BF16)
            else:
                stage = stage_ref[0]
                for s in range(width // LANE):
                    stage[s] = u[:, s * LANE:(s + 1) * LANE]
                rows = lambda r: pl.ds(r, tm // dil, stride=dil)
                for r in range(dil):
                    for s in range(width // LANE):
                        lo = r * width + s * LANE
                        o_ref[:, lo:lo + LANE] = stage[s, rows(r), :].astype(BF16)
        col += width


def _vt_rows(v_heads):
    return LANE + v_heads * ONES_ROWS


def _proj_out(kind, width, dil, bn, s, tm):
    if kind == "qT":
        return (bn, width, s), pl.BlockSpec((None, width, tm), lambda b, i: (b, 0, i))
    if kind == "vT":
        per = FLASH_TK // tm
        rows = width // LANE * _vt_rows(dil)
        return ((bn, s // FLASH_TK, rows, FLASH_TK),
                pl.BlockSpec((None, None, rows, tm), lambda b, i: (b, i // per, 0, i % per)))
    return (bn, s // dil, dil * width), pl.BlockSpec((None, tm // dil, dil * width), lambda b, i: (b, i, 0))


def _project(x, shift, scale, norm_g, w, gain, tables, outs, *, two_heads, head_dim, q_scale, tm, name):
    bn, s, _ = x.shape
    wcols = w.shape[1]
    kern = functools.partial(_proj_kernel, outs=outs, two_heads=two_heads, head_dim=head_dim, q_scale=q_scale)
    vec = pl.BlockSpec((None, 1, D_MODEL), lambda b, i: (b, 0, 0))
    tab = pl.BlockSpec((tm, LANE), lambda b, i: (i, 0))
    shapes, specs = zip(*[_proj_out(kind, width, dil, bn, s, tm) for width, kind, dil in outs])
    n_slabs = max(width for width, _, _ in outs) // LANE
    dilated = any(d > 1 and kind != "vT" for _, kind, d in outs)
    stage = [pltpu.VMEM((n_slabs, tm, LANE), F32)] if dilated else []
    return pl.pallas_call(
        kern,
        grid=(bn, s // tm),
        in_specs=[
            pl.BlockSpec((None, tm, D_MODEL), lambda b, i: (b, i, 0)),
            vec, vec,
            pl.BlockSpec((1, D_MODEL), lambda b, i: (0, 0)),
            pl.BlockSpec((D_MODEL, wcols), lambda b, i: (0, 0)),
            pl.BlockSpec((1, wcols), lambda b, i: (0, 0)),
            tab, tab,
        ],
        out_specs=list(specs),
        out_shape=[jax.ShapeDtypeStruct(shape, BF16) for shape in shapes],
        scratch_shapes=stage,
        compiler_params=_cparams(2),
        name=name,
    )(x, shift, scale, norm_g, w, gain, *tables)


def _band_attn_kernel(q_ref, kp_ref, kc_ref, kn_ref, vp_ref, vc_ref, vn_ref, bias_ref, o_ref, *, tb, n_blocks):
    i = pl.program_id(2)
    nk = tb + 2 * A_BAND
    colk = lax.broadcasted_iota(jnp.int32, (1, nk), 1)
    first_col = jnp.where(i == 0, A_BAND, 0)
    end_col = jnp.where(i == n_blocks - 1, tb + A_BAND, nk)
    bias = bias_ref[...] + jnp.where((colk < first_col) | (colk >= end_col), NEG, 0.0)
    lane = lax.broadcasted_iota(jnp.int32, (1, LANE), 1)
    heads = [slice(j * LANE, (j + 1) * LANE) for j in range(A_HEADS)]
    band = lambda p_ref, c_ref, n_ref, sl: jnp.concatenate([p_ref[:, sl], c_ref[:, sl], n_ref[:, sl]], axis=0)
    scores = [_dot_nt(q_ref[:, sl], band(kp_ref, kc_ref, kn_ref, sl)) for sl in heads]
    stats = []
    for s in scores:
        s = s + bias
        m = jnp.max(s, axis=-1, keepdims=True)
        p = jnp.exp2(s - m)
        stats.append((m, jnp.sum(p, axis=-1, keepdims=True), p.astype(BF16)))
    for sl, (m, l, p) in zip(heads, stats):
        o = _dot(p, band(vp_ref, vc_ref, vn_ref, sl)) / l
        o_ref[:, sl] = jnp.where(lane >= A_HEAD_DIM, m * LN2 + jnp.log(l), o)


def _band_attention(q, k, v, dilation):
    bn, length, _ = q.shape
    tb = min(256, length)
    nb = length // tb
    per = tb // A_BAND
    cur = pl.BlockSpec((None, tb, A_SLAB), lambda b, r, i: (b, i, r))
    prv = pl.BlockSpec((None, A_BAND, A_SLAB), lambda b, r, i: (b, jnp.maximum(i * per - 1, 0), r))
    nxt = pl.BlockSpec((None, A_BAND, A_SLAB), lambda b, r, i: (b, jnp.minimum((i + 1) * per, nb * per - 1), r))
    nk = tb + 2 * A_BAND
    dist = np.arange(nk)[None, :] - A_BAND - np.arange(tb)[:, None]
    bias = np.where(np.abs(dist) <= A_BAND, 0.0, NEG).astype(np.float32)
    kern = functools.partial(_band_attn_kernel, tb=tb, n_blocks=nb)
    return pl.pallas_call(
        kern,
        grid=(bn, dilation, nb),
        in_specs=[cur, prv, cur, nxt, prv, cur, nxt, pl.BlockSpec((tb, nk), lambda b, r, i: (0, 0))],
        out_specs=cur,
        out_shape=jax.ShapeDtypeStruct((bn, length, dilation * A_SLAB), F32),
        compiler_params=_cparams(3),
        name=f"band_attn_d{dilation}",
    )(q, k, k, k, v, v, v, bias)


def _stacked_flash(qt_ref, k_ref, vt_ref, acc_sc, s_sc, *, tq, n_chunks, v_heads):
    dv = LANE // v_heads
    v_rows = [slice(hd * (dv + ONES_ROWS), (hd + 1) * (dv + ONES_ROWS)) for hd in range(v_heads)]
    first = _first_head(lax.broadcasted_iota(jnp.int32, (LANE, 1), 0))
    qt = qt_ref[...]
    zero = jnp.zeros_like(qt)
    q2t = jnp.concatenate([jnp.where(first, qt, zero), jnp.where(first, zero, qt)], axis=1)
    acc_sc[...] = jnp.zeros(acc_sc.shape, F32)

    wide = v_heads == 1 and n_chunks > FLASH_UNROLL
    width = min(2 * COL_GROUP if wide else COL_GROUP, tq)
    halves = [slice(lo, lo + width) for lo in range(0, 2 * tq, width)]

    def scores(c, slot, cols):
        start = pl.multiple_of(c * FLASH_TK, FLASH_TK)
        s_sc[slot, :, cols] = _dot(k_ref[pl.ds(start, FLASH_TK), :], q2t[:, cols])

    def step(c, slot, m_prev, c_next=None):
        m_next = jnp.maximum(m_prev, jnp.max(s_sc[slot], axis=0, keepdims=True))
        vc = vt_ref[c]
        alpha = jnp.exp2(m_prev - m_next)
        for cols in halves:
            if c_next is not None:
                scores(c_next, 1 - slot, cols)
            pt = jnp.exp2(s_sc[slot, :, cols] - m_next[:, cols]).astype(BF16)
            head = cols.start // tq
            acc_sc[:, cols] = alpha[:, cols] * acc_sc[:, cols] + _dot(vc[v_rows[head % v_heads]], pt)
        return m_next

    unroll = min(FLASH_UNROLL, n_chunks)

    def group(j, m):
        for u in range(unroll):
            c = unroll * j + u
            m = step(c, u % SCORE_SLOTS, m, c + 1)
        return m

    for cols in halves:
        scores(0, 0, cols)
    m = lax.fori_loop(0, n_chunks // unroll - 1, group, jnp.full((1, 2 * tq), -jnp.inf, F32))
    for c in range(n_chunks - unroll, n_chunks):
        m = step(c, c % SCORE_SLOTS, m, c + 1 if c + 1 < n_chunks else None)
    acc = acc_sc[...]
    return acc[:dv] / acc[dv:dv + 1]


def _gqa_kernel(qt_ref, k_ref, vt_ref, o_ref, acc_sc, s_sc, *, tq, n_chunks):
    o = _stacked_flash(qt_ref, k_ref, vt_ref, acc_sc, s_sc, tq=tq, n_chunks=n_chunks, v_heads=B_KV_HEADS)
    o_ref[...] = jnp.concatenate([o[:, :tq], o[:, tq:]], axis=0).T


def _diff_kernel(qt_ref, k_ref, vt_ref, lq1_ref, lk1_ref, lq2_ref, lk2_ref, sub_ref, o_ref, acc_sc,
                 s_sc, *, tq, n_chunks, lam_init):
    o = _stacked_flash(qt_ref, k_ref, vt_ref, acc_sc, s_sc, tq=tq, n_chunks=n_chunks, v_heads=1)
    lam = (jnp.exp(jnp.sum(lq1_ref[...] * lk1_ref[...], axis=-1, keepdims=True))
           - jnp.exp(jnp.sum(lq2_ref[...] * lk2_ref[...], axis=-1, keepdims=True)) + lam_init)
    oc = (o[:, :tq] - lam * o[:, tq:]).T
    ms = jnp.mean(oc * oc, axis=-1, keepdims=True)
    o_ref[...] = oc * lax.rsqrt(ms + EPS) * sub_ref[...] * (1.0 - lam_init)


def _flash_tq(s):
    return min(512, s)


def _flash_scratch(tq, v_heads):
    return [pltpu.VMEM((LANE // v_heads + ONES_ROWS, 2 * tq), F32),
            pltpu.VMEM((SCORE_SLOTS, FLASH_TK, 2 * tq), F32)]


def _gqa_attention(qt, k, vt):
    bn, _, s = qt.shape
    tq, n_chunks = _flash_tq(s), s // FLASH_TK
    kern = functools.partial(_gqa_kernel, tq=tq, n_chunks=n_chunks)
    return pl.pallas_call(
        kern,
        grid=(bn, B_HEADS // 2, s // tq),
        in_specs=[pl.BlockSpec((None, LANE, tq), lambda b, p, i: (b, p, i)),
                  pl.BlockSpec((None, s, LANE), lambda b, p, i: (b, 0, 0)),
                  pl.BlockSpec((None, n_chunks, _vt_rows(B_KV_HEADS), FLASH_TK), lambda b, p, i: (b, 0, 0, 0))],
        out_specs=pl.BlockSpec((None, tq, LANE), lambda b, p, i: (b, i, p)),
        out_shape=jax.ShapeDtypeStruct((bn, s, B_WIDTH), F32),
        scratch_shapes=_flash_scratch(tq, B_KV_HEADS),
        compiler_params=_cparams(3),
        name="gqa_attn",
    )(qt, k, vt)


def _diff_attention(qt, k, vt, lam_rows, subln, lam_init):
    bn, _, s = qt.shape
    tq, n_chunks = _flash_tq(s), s // FLASH_TK
    kern = functools.partial(_diff_kernel, tq=tq, n_chunks=n_chunks, lam_init=lam_init)
    row = lambda n: pl.BlockSpec((1, n), lambda b, h, i: (0, 0))
    return pl.pallas_call(
        kern,
        grid=(bn, C_HEADS, s // tq),
        in_specs=[pl.BlockSpec((None, LANE, tq), lambda b, h, i: (b, h, i)),
                  pl.BlockSpec((None, s, LANE), lambda b, h, i: (b, 0, h)),
                  pl.BlockSpec((None, n_chunks, _vt_rows(1), FLASH_TK), lambda b, h, i: (b, 0, h, 0)),
                  row(C_HEAD_DIM), row(C_HEAD_DIM), row(C_HEAD_DIM), row(C_HEAD_DIM), row(LANE)],
        out_specs=pl.BlockSpec((None, tq, LANE), lambda b, h, i: (b, i, h)),
        out_shape=jax.ShapeDtypeStruct((bn, s, C_WIDTH), F32),
        scratch_shapes=_flash_scratch(tq, 1),
        compiler_params=_cparams(3),
        name="diff_attn",
    )(qt, k, vt, *lam_rows, subln)


def _final_kernel(x_ref, shift_ref, scale_ref, gate_ref, g_ref, oa0_ref, oa1_ref, oa2_ref, yb_ref, yc_ref,
                  wz_ref, wbg_ref, bbg_ref, woa_ref, wob_ref, woc_ref, wout_ref, o_ref, *stage_refs):
    x = x_ref[...]
    tm = x.shape[0]
    h = _modulated_norm(x, g_ref[...], scale_ref[...], shift_ref[...]).astype(BF16)
    lane = lax.broadcasted_iota(jnp.int32, (1, LANE), 1)

    for oa_ref, st_ref, (_, dil) in zip((oa1_ref, oa2_ref), stage_refs, A_GROUPS[1:]):
        for r in range(dil):
            for j in range(A_HEADS):
                lo = r * A_SLAB + j * LANE
                st_ref[j, pl.ds(r, tm // dil, stride=dil), :] = oa_ref[:, lo:lo + LANE]

    ya = []
    for j in range(A_HEADS):
        slabs = [oa0_ref[:, j * LANE:(j + 1) * LANE]] + [st_ref[j] for st_ref in stage_refs]
        lses = [t[:, A_HEAD_DIM:A_HEAD_DIM + 1] for t in slabs]
        top = jnp.maximum(jnp.maximum(lses[0], lses[1]), lses[2])
        ws = [jnp.exp(t - top) for t in lses]
        mix = (ws[0] * slabs[0] + ws[1] * slabs[1] + ws[2] * slabs[2]) / (ws[0] + ws[1] + ws[2])
        ya.append(jnp.where(lane < A_HEAD_DIM, mix, 0.0))
    ya = jnp.concatenate(ya, axis=-1)

    def branch(y, z_lo, z_hi, wo_ref, g_lo):
        z = _dot(h, wz_ref[:, z_lo:z_hi])
        p = _dot((y * _silu(z)).astype(BF16), wo_ref[...])
        gate = jax.nn.sigmoid(_dot(h, wbg_ref[:, g_lo:g_lo + D_MODEL]) + bbg_ref[:, g_lo:g_lo + D_MODEL])
        return gate * p

    za_hi = A_SLAB
    zb_hi = za_hi + B_WIDTH
    zc_hi = zb_hi + C_WIDTH
    merged = (branch(ya, 0, za_hi, woa_ref, 0)
              + branch(yb_ref[...], za_hi, zb_hi, wob_ref, D_MODEL)
              + branch(yc_ref[...], zb_hi, zc_hi, woc_ref, 2 * D_MODEL))
    out = _dot(merged.astype(BF16), wout_ref[...])
    o_ref[...] = x + gate_ref[...] * out


def _final(x, shift, scale, gate, norm_g, oa, yb, yc, wz, wbg, bbg, woa, wob, woc, wout, *, tm):
    bn, s, _ = x.shape
    tok = lambda w: pl.BlockSpec((None, tm, w), lambda b, i: (b, i, 0))
    vec = pl.BlockSpec((None, 1, D_MODEL), lambda b, i: (b, 0, 0))
    full = lambda a: pl.BlockSpec(a.shape, lambda b, i: (0, 0), pipeline_mode=pl.Buffered(1))
    band = [pl.BlockSpec((None, tm // dil, dil * A_SLAB), lambda b, i: (b, i, 0)) for _, dil in A_GROUPS]
    return pl.pallas_call(
        _final_kernel,
        grid=(bn, s // tm),
        in_specs=[tok(D_MODEL), vec, vec, vec, full(norm_g), *band,
                  tok(B_WIDTH), tok(C_WIDTH), full(wz), full(wbg), full(bbg), full(woa), full(wob), full(woc),
                  full(wout)],
        out_specs=tok(D_MODEL),
        out_shape=jax.ShapeDtypeStruct((bn, s, D_MODEL), F32),
        scratch_shapes=[pltpu.VMEM((A_HEADS, tm, LANE), F32) for _ in A_GROUPS[1:]],
        compiler_params=_cparams(2),
        name="final",
    )(x, shift, scale, gate, norm_g, *oa, yb, yc, wz, wbg, bbg, woa, wob, woc, wout)


B_Q_ORDER = (0, 3, 1, 4, 2, 5)


def _slab_layout(head_dim, n_heads, rot_groups):
    src = -np.ones((LANE,), np.int32)
    group = -np.ones((LANE,), np.int32)
    freq = np.zeros((LANE,), np.int32)
    sign = np.zeros((LANE,), np.float32)
    width = ROT_SHIFT // n_heads
    for hd in range(n_heads):
        low = [(start + i, g, i, -1.0) for g, (start, half) in enumerate(rot_groups) for i in range(half)]
        high = [(start + half + i, g, i, 1.0) for g, (start, half) in enumerate(rot_groups) for i in range(half)]
        rotary = {d for d, _, _, _ in low + high}
        rest = [(d, -1, 0, 0.0) for d in range(head_dim) if d not in rotary]
        n_low_rest = width - len(low)
        low, high = low + rest[:n_low_rest], high + rest[n_low_rest:]
        assert len(low) == width and len(high) <= width
        for base, items in ((hd * width, low), (hd * width + ROT_SHIFT, high)):
            for lane, (d, g, i, sg) in enumerate(items, start=base):
                src[lane], group[lane], freq[lane], sign[lane] = hd * head_dim + d, g, i, sg
    return src, group, freq, sign


LAYOUT_A = _slab_layout(A_HEAD_DIM, 1, ((0, A_ROT // 2),))
LAYOUT_B = _slab_layout(B_HEAD_DIM, 2, ((0, B_HEAD_DIM // 4), (B_HEAD_DIM // 2, B_HEAD_DIM // 4)))
LAYOUT_C = _slab_layout(C_HEAD_DIM, 2, ((0, C_ROT // 2),))


def _to_slabs(w, layout, dims_per_slab):
    src = layout[0]
    n_slabs = w.shape[-1] // dims_per_slab
    idx = np.concatenate([np.maximum(src, 0) + s * dims_per_slab for s in range(n_slabs)])
    keep = np.tile(src >= 0, n_slabs)
    return jnp.where(keep, jnp.take(w, idx, axis=-1), 0.0)


def _pad_heads(w, n_heads):
    lead = w.shape[:-1]
    w = w.reshape(*lead, n_heads, A_HEAD_DIM)
    w = jnp.pad(w, [(0, 0)] * len(lead) + [(0, 0), (0, LANE - A_HEAD_DIM)])
    return w.reshape(*lead, n_heads * LANE)


def _permute_heads(w, order, dim):
    lead = w.shape[:-1]
    w = w.reshape(*lead, len(order), dim)
    return w[..., jnp.array(order), :].reshape(*lead, len(order) * dim)


def _prep_layer(w_in, qn_a, kn_a, qn_b, kn_b, qn_c, kn_c, w_oa, w_ob, w_oc):
    offs = [0]
    for n in IN_SIZES:
        offs.append(offs[-1] + n)
    qa, ka, va, za, qb, kb, vb, zb, qc, kc, vc, zc = [w_in[:, offs[i]:offs[i + 1]] for i in range(len(IN_SIZES))]

    slab_a = functools.partial(_to_slabs, layout=LAYOUT_A, dims_per_slab=A_HEAD_DIM)
    slab_b = functools.partial(_to_slabs, layout=LAYOUT_B, dims_per_slab=2 * B_HEAD_DIM)
    slab_c = functools.partial(_to_slabs, layout=LAYOUT_C, dims_per_slab=2 * C_HEAD_DIM)
    qb = _permute_heads(qb, B_Q_ORDER, B_HEAD_DIM)

    w_a = jnp.concatenate([slab_a(qa), slab_a(ka), _pad_heads(va, N_A)], axis=-1).astype(BF16)
    gain_a = jnp.concatenate([slab_a(jnp.tile(qn_a, N_A)), slab_a(jnp.tile(kn_a, N_A)),
                              jnp.zeros((N_A * LANE,), F32)])[None, :]

    w_b = jnp.concatenate([slab_b(qb), slab_b(kb), vb], axis=-1).astype(BF16)
    gain_b = jnp.concatenate([slab_b(jnp.tile(qn_b, B_HEADS)), slab_b(jnp.tile(kn_b, B_KV_HEADS)),
                              jnp.zeros((B_KV,), F32)])[None, :]

    w_c = jnp.concatenate([slab_c(qc), slab_c(kc), vc], axis=-1).astype(BF16)
    gain_c = jnp.concatenate([slab_c(jnp.tile(qn_c, 2 * C_HEADS)), slab_c(jnp.tile(kn_c, 2 * C_HEADS)),
                              jnp.zeros((C_WIDTH,), F32)])[None, :]

    w_z = jnp.concatenate([_pad_heads(za, A_HEADS), _permute_heads(zb, B_Q_ORDER, B_HEAD_DIM), zc],
                          axis=-1).astype(BF16)
    w_oa_p = _pad_heads(w_oa.T, A_HEADS).T.astype(BF16)
    w_ob_p = _permute_heads(w_ob.T, B_Q_ORDER, B_HEAD_DIM).T.astype(BF16)
    return dict(w_a=w_a, gain_a=gain_a, w_b=w_b, gain_b=gain_b, w_c=w_c, gain_c=gain_c, w_z=w_z,
                w_oa=w_oa_p, w_ob=w_ob_p, w_oc=w_oc.astype(BF16))


def _inv_freq(dim, theta):
    return theta ** (-jnp.arange(0, dim, 2, dtype=F32) / dim)


def _rope_tables(layout, inv_freqs, streams):
    _, group, freq, sign = layout
    ang = 0.0
    for g, (inv, pos) in enumerate(zip(inv_freqs, streams)):
        ang = ang + pos[:, None] * (inv[freq] * (group == g))[None, :]
    return jnp.cos(ang), jnp.sin(ang) * sign[None, :]


def _tables(s):
    pos = jnp.arange(s)
    tok, row, col = [t.astype(F32) for t in (pos, pos // GRID_W, pos % GRID_W)]
    inv_b = _inv_freq(B_HEAD_DIM // 2, AXIAL_THETA)
    tab_a = _rope_tables(LAYOUT_A, [_inv_freq(A_ROT, ROPE_THETA)], [tok])
    tab_b = _rope_tables(LAYOUT_B, [inv_b, inv_b], [row, col])
    tab_c = _rope_tables(LAYOUT_C, [_inv_freq(C_ROT, ROPE_THETA)], [tok])
    return tab_a, tab_b, tab_c


def _layer(x, mod, l, lp, tabs, norm_g, lam_rows, subln, w_bg, b_bg, w_out):
    bn, s, _ = x.shape
    shift, scale, gate = [m.reshape(bn, 1, D_MODEL) for m in jnp.split(mod, 3, axis=-1)]
    tab_a, tab_b, tab_c = tabs
    lam_init = 0.8 - 0.6 * math.exp(-0.3 * l)

    a_outs = tuple((A_SLAB, kind, dil) for kind in ("q", "k", "v") for _, dil in A_GROUPS)
    pa = _project(x, shift, scale, norm_g, lp["w_a"], lp["gain_a"], tab_a, a_outs, two_heads=False,
                  head_dim=A_HEAD_DIM, q_scale=A_HEAD_DIM ** -0.5 * LOG2E, tm=min(PROJ_A_TM, s), name="proj_a")
    n_g = len(A_GROUPS)
    oa = [_band_attention(pa[g], pa[n_g + g], pa[2 * n_g + g], dil) for g, (_, dil) in enumerate(A_GROUPS)]

    qb, kb, vb = _project(x, shift, scale, norm_g, lp["w_b"], lp["gain_b"], tab_b,
                          ((B_WIDTH, "qT", 1), (B_KV, "k", 1), (B_KV, "vT", B_KV_HEADS)), two_heads=True,
                          head_dim=B_HEAD_DIM, q_scale=B_HEAD_DIM ** -0.5 * LOG2E, tm=min(PROJ_TM, s), name="proj_b")
    yb = _gqa_attention(qb, kb, vb)

    qc, kc, vc = _project(x, shift, scale, norm_g, lp["w_c"], lp["gain_c"], tab_c,
                          ((C_WIDTH, "qT", 1), (C_WIDTH, "k", 1), (C_WIDTH, "vT", 1)), two_heads=True,
                          head_dim=C_HEAD_DIM, q_scale=C_HEAD_DIM ** -0.5 * LOG2E, tm=min(PROJ_TM, s), name="proj_c")
    yc = _diff_attention(qc, kc, vc, lam_rows, subln, lam_init)

    return _final(x, shift, scale, gate, norm_g, oa, yb, yc, lp["w_z"], w_bg, b_bg, lp["w_oa"], lp["w_ob"],
                  lp["w_oc"], w_out, tm=min(FINAL_TM, s))


def kernel(x_prompt, x_sample, c_prompt, c_sample, norm_g, w_ada, b_ada, w_in, qn_a, kn_a, qn_b, kn_b, qn_c, kn_c,
           lam_q1, lam_k1, lam_q2, lam_k2, subln_c, w_oa, w_ob, w_oc, w_bg, b_bg, w_out):
    groups = ((x_prompt, c_prompt), (x_sample, c_sample))
    rows = [c.shape[0] for _, c in groups]
    pad = -sum(rows) % 8
    c_all = jnp.concatenate([c for _, c in groups] + [jnp.zeros((pad, D_MODEL), F32)], axis=0)
    mod_all = _ada_mod(c_all, w_ada.astype(BF16), b_ada)

    layers = [_prep_layer(w_in[l], qn_a[l], kn_a[l], qn_b[l], kn_b[l], qn_c[l], kn_c[l], w_oa[l], w_ob[l], w_oc[l])
              for l in range(DEPTH)]
    w_bg16, w_out16 = w_bg.astype(BF16), w_out.astype(BF16)

    outs = []
    row0 = 0
    for (x, c), n in zip(groups, rows):
        tabs = _tables(x.shape[1])
        for l in range(DEPTH):
            lam_rows = [p[l][None, :] for p in (lam_q1, lam_k1, lam_q2, lam_k2)]
            x = _layer(x, mod_all[l, row0:row0 + n], l, layers[l], tabs, norm_g[l][None, :], lam_rows,
                       subln_c[l][None, :], w_bg16[l], b_bg[l][None, :], w_out16[l])
        outs.append(x)
        row0 += n
    return tuple(outs)
```

```python
import functools
import math

import jax
import jax.numpy as jnp
import numpy as np
from jax import lax
from jax.experimental import pallas as pl
from jax.experimental.pallas import tpu as pltpu

D_MODEL = 1024
DEPTH = 2
GRID_W = 64
EPS = 1e-6
NEG = -1e30
ROPE_THETA = 500000.0
AXIAL_THETA = 10000.0

A_GROUPS = ((128, 1), (512, 4), (2048, 16))
A_HEADS = 4
A_HEAD_DIM = 96
A_ROT = A_HEAD_DIM // 4
A_BAND = 64
N_A = len(A_GROUPS) * A_HEADS

B_HEADS = 6
B_KV_HEADS = 2
B_HEAD_DIM = 64
C_HEADS = 4
C_HEAD_DIM = 64
C_ROT = C_HEAD_DIM // 4

A_QKV = N_A * A_HEAD_DIM
A_WIDTH = A_HEADS * A_HEAD_DIM
B_WIDTH = B_HEADS * B_HEAD_DIM
B_KV = B_KV_HEADS * B_HEAD_DIM
C_WIDTH = C_HEADS * 2 * C_HEAD_DIM
IN_SIZES = (A_QKV, A_QKV, A_QKV, A_WIDTH, B_WIDTH, B_KV, B_KV, B_WIDTH, C_WIDTH, C_WIDTH, C_WIDTH, C_WIDTH)

LANE = 128
HALF = LANE // 2
ROT_SHIFT = LANE // 2
A_SLAB = A_HEADS * LANE
BF16_SUBLANES = 16
ONES_ROWS = BF16_SUBLANES
FLASH_TK = 512
COL_GROUP = 256
SCORE_SLOTS = 2
FLASH_UNROLL = 4
PROJ_TM = 512
PROJ_A_TM = 256
FINAL_TM = 512
LOG2E = math.log2(math.e)
LN2 = math.log(2.0)
VMEM_LIMIT = 48 * 1024 * 1024

BF16 = jnp.bfloat16
F32 = jnp.float32


def _cparams(n_axes):
    return pltpu.CompilerParams(dimension_semantics=("arbitrary",) * n_axes, vmem_limit_bytes=VMEM_LIMIT)


def _dot(a, b):
    return jnp.dot(a, b, preferred_element_type=F32)


def _dot_nt(a, b):
    return lax.dot_general(a, b, (((1,), (1,)), ((), ())), preferred_element_type=F32)


def _silu(x):
    return x * jax.nn.sigmoid(x)


def _ada_kernel(c_ref, w_ref, b_ref, o_ref):
    c = c_ref[...]
    o_ref[...] = _dot(_silu(c).astype(BF16), w_ref[...]) + b_ref[...]


def _ada_mod(c_all, w_ada, b_ada):
    rows = c_all.shape[0]
    return pl.pallas_call(
        _ada_kernel,
        grid=(DEPTH, 3),
        in_specs=[
            pl.BlockSpec((rows, D_MODEL), lambda l, n: (0, 0)),
            pl.BlockSpec((None, D_MODEL, D_MODEL), lambda l, n: (l, 0, n)),
            pl.BlockSpec((None, 1, D_MODEL), lambda l, n: (l, 0, n)),
        ],
        out_specs=pl.BlockSpec((None, rows, D_MODEL), lambda l, n: (l, 0, n)),
        out_shape=jax.ShapeDtypeStruct((DEPTH, rows, 3 * D_MODEL), F32),
        compiler_params=_cparams(2),
        name="ada_mod",
    )(c_all, w_ada, b_ada.reshape(DEPTH, 1, 3 * D_MODEL))


def _modulated_norm(x, g, scale, shift):
    ms = jnp.mean(x * x, axis=-1, keepdims=True)
    h = x * lax.rsqrt(ms + EPS) * g
    return h * (1.0 + scale) + shift


def _first_head(index):
    return (index & (HALF - 1)) < HALF // 2


def _head_norm_rope(u, gain, cos, sin, two_heads, head_dim):
    sq = u * u
    if two_heads:
        first = _first_head(lax.broadcasted_iota(jnp.int32, (1, LANE), 1))
        s_first = jnp.sum(jnp.where(first, sq, 0.0), axis=-1, keepdims=True)
        s_all = jnp.sum(sq, axis=-1, keepdims=True)
        ms = jnp.where(first, s_first, s_all - s_first) * (1.0 / head_dim)
    else:
        ms = jnp.sum(sq, axis=-1, keepdims=True) * (1.0 / head_dim)
    y = u * lax.rsqrt(ms + EPS) * gain
    return y * cos + pltpu.roll(y, ROT_SHIFT, 1) * sin


def _proj_kernel(x_ref, shift_ref, scale_ref, g_ref, w_ref, gain_ref, cos_ref, sin_ref, *refs,
                 outs, two_heads, head_dim, q_scale):
    out_refs, stage_ref = refs[:len(outs)], refs[len(outs):]
    h = _modulated_norm(x_ref[...], g_ref[...], scale_ref[...], shift_ref[...]).astype(BF16)
    cos, sin = cos_ref[...], sin_ref[...]
    tm = x_ref.shape[0]
    col = 0
    for o_ref, (width, kind, dil) in zip(out_refs, outs):
        u = _dot(h, w_ref[:, col:col + width])
        if kind == "vT":
            dv = LANE // dil
            for s in range(width // LANE):
                ut = u[:, s * LANE:(s + 1) * LANE].T.astype(BF16)
                for hd in range(dil):
                    base = (s * dil + hd) * (dv + ONES_ROWS)
                    o_ref[base:base + dv, :] = ut[hd * dv:(hd + 1) * dv]
                    o_ref[base + dv:base + dv + ONES_ROWS, :] = jnp.ones((ONES_ROWS, tm), BF16)
        else:
            if kind != "v":
                slabs = []
                for s in range(width // LANE):
                    lo = s * LANE
                    y = _head_norm_rope(u[:, lo:lo + LANE], gain_ref[:, col + lo:col + lo + LANE], cos, sin,
                                        two_heads, head_dim)
                    slabs.append(y * q_scale if kind in ("q", "qT") and q_scale != 1.0 else y)
                u = jnp.concatenate(slabs, axis=-1)
            if kind == "qT":
                for s in range(width // LANE):
                    o_ref[s * LANE:(s + 1) * LANE, :] = u[:, s * LANE:(s + 1) * LANE].T.astype(BF16)
            elif dil == 1:
                o_ref[...] = u.astype(BF16)
            else:
                stage = stage_ref[0]
                for s in range(width // LANE):
                    stage[s] = u[:, s * LANE:(s + 1) * LANE]
                rows = lambda r: pl.ds(r, tm // dil, stride=dil)
                for r in range(dil):
                    for s in range(width // LANE):
                        lo = r * width + s * LANE
                        o_ref[:, lo:lo + LANE] = stage[s, rows(r), :].astype(BF16)
        col += width


def _vt_rows(v_heads):
    return LANE + v_heads * ONES_ROWS


def _proj_out(kind, width, dil, bn, s, tm):
    if kind == "qT":
        return (bn, width, s), pl.BlockSpec((None, width, tm), lambda b, i: (b, 0, i))
    if kind == "vT":
        per = FLASH_TK // tm
        rows = width // LANE * _vt_rows(dil)
        return ((bn, s // FLASH_TK, rows, FLASH_TK),
                pl.BlockSpec((None, None, rows, tm), lambda b, i: (b, i // per, 0, i % per)))
    return (bn, s // dil, dil * width), pl.BlockSpec((None, tm // dil, dil * width), lambda b, i: (b, i, 0))


def _project(x, shift, scale, norm_g, w, gain, tables, outs, *, two_heads, head_dim, q_scale, tm, name):
    bn, s, _ = x.shape
    wcols = w.shape[1]
    kern = functools.partial(_proj_kernel, outs=outs, two_heads=two_heads, head_dim=head_dim, q_scale=q_scale)
    vec = pl.BlockSpec((None, 1, D_MODEL), lambda b, i: (b, 0, 0))
    tab = pl.BlockSpec((tm, LANE), lambda b, i: (i, 0))
    shapes, specs = zip(*[_proj_out(kind, width, dil, bn, s, tm) for width, kind, dil in outs])
    n_slabs = max(width for width, _, _ in outs) // LANE
    dilated = any(d > 1 and kind != "vT" for _, kind, d in outs)
    stage = [pltpu.VMEM((n_slabs, tm, LANE), F32)] if dilated else []
    return pl.pallas_call(
        kern,
        grid=(bn, s // tm),
        in_specs=[
            pl.BlockSpec((None, tm, D_MODEL), lambda b, i: (b, i, 0)),
            vec, vec,
            pl.BlockSpec((1, D_MODEL), lambda b, i: (0, 0)),
            pl.BlockSpec((D_MODEL, wcols), lambda b, i: (0, 0)),
            pl.BlockSpec((1, wcols), lambda b, i: (0, 0)),
            tab, tab,
        ],
        out_specs=list(specs),
        out_shape=[jax.ShapeDtypeStruct(shape, BF16) for shape in shapes],
        scratch_shapes=stage,
        compiler_params=_cparams(2),
        name=name,
    )(x, shift, scale, norm_g, w, gain, *tables)


def _band_attn_kernel(q_ref, kp_ref, kc_ref, kn_ref, vp_ref, vc_ref, vn_ref, bias_ref, o_ref, *, tb, n_blocks):
    i = pl.program_id(2)
    nk = tb + 2 * A_BAND
    colk = lax.broadcasted_iota(jnp.int32, (1, nk), 1)
    first_col = jnp.where(i == 0, A_BAND, 0)
    end_col = jnp.where(i == n_blocks - 1, tb + A_BAND, nk)
    bias = bias_ref[...] + jnp.where((colk < first_col) | (colk >= end_col), NEG, 0.0)
    lane = lax.broadcasted_iota(jnp.int32, (1, LANE), 1)
    heads = [slice(j * LANE, (j + 1) * LANE) for j in range(A_HEADS)]
    band = lambda p_ref, c_ref, n_ref, sl: jnp.concatenate([p_ref[:, sl], c_ref[:, sl], n_ref[:, sl]], axis=0)
    scores = [_dot_nt(q_ref[:, sl], band(kp_ref, kc_ref, kn_ref, sl)) for sl in heads]
    stats = []
    for s in scores:
        s = s + bias
        m = jnp.max(s, axis=-1, keepdims=True)
        p = jnp.exp2(s - m)
        stats.append((m, jnp.sum(p, axis=-1, keepdims=True), p.astype(BF16)))
    for sl, (m, l, p) in zip(heads, stats):
        o = _dot(p, band(vp_ref, vc_ref, vn_ref, sl)) / l
        o_ref[:, sl] = jnp.where(lane >= A_HEAD_DIM, m * LN2 + jnp.log(l), o)


def _band_attention(q, k, v, dilation):
    bn, length, _ = q.shape
    tb = min(256, length)
    nb = length // tb
    per = tb // A_BAND
    cur = pl.BlockSpec((None, tb, A_SLAB), lambda b, r, i: (b, i, r))
    prv = pl.BlockSpec((None, A_BAND, A_SLAB), lambda b, r, i: (b, jnp.maximum(i * per - 1, 0), r))
    nxt = pl.BlockSpec((None, A_BAND, A_SLAB), lambda b, r, i: (b, jnp.minimum((i + 1) * per, nb * per - 1), r))
    nk = tb + 2 * A_BAND
    dist = np.arange(nk)[None, :] - A_BAND - np.arange(tb)[:, None]
    bias = np.where(np.abs(dist) <= A_BAND, 0.0, NEG).astype(np.float32)
    kern = functools.partial(_band_attn_kernel, tb=tb, n_blocks=nb)
    return pl.pallas_call(
        kern,
        grid=(bn, dilation, nb),
        in_specs=[cur, prv, cur, nxt, prv, cur, nxt, pl.BlockSpec((tb, nk), lambda b, r, i: (0, 0))],
        out_specs=cur,
        out_shape=jax.ShapeDtypeStruct((bn, length, dilation * A_SLAB), F32),
        compiler_params=_cparams(3),
        name=f"band_attn_d{dilation}",
    )(q, k, k, k, v, v, v, bias)


def _stacked_flash(qt_ref, k_ref, vt_ref, acc_sc, s_sc, *, tq, n_chunks, v_heads):
    dv = LANE // v_heads
    v_rows = [slice(hd * (dv + ONES_ROWS), (hd + 1) * (dv + ONES_ROWS)) for hd in range(v_heads)]
    first = _first_head(lax.broadcasted_iota(jnp.int32, (LANE, 1), 0))
    qt = qt_ref[...]
    zero = jnp.zeros_like(qt)
    q2t = jnp.concatenate([jnp.where(first, qt, zero), jnp.where(first, zero, qt)], axis=1)
    acc_sc[...] = jnp.zeros(acc_sc.shape, F32)

    wide = v_heads == 1 and n_chunks > FLASH_UNROLL
    width = min(2 * COL_GROUP if wide else COL_GROUP, tq)
    halves = [slice(lo, lo + width) for lo in range(0, 2 * tq, width)]

    def scores(c, slot, cols):
        start = pl.multiple_of(c * FLASH_TK, FLASH_TK)
        s_sc[slot, :, cols] = _dot(k_ref[pl.ds(start, FLASH_TK), :], q2t[:, cols])

    def step(c, slot, m_prev, c_next=None):
        m_next = jnp.maximum(m_prev, jnp.max(s_sc[slot], axis=0, keepdims=True))
        vc = vt_ref[c]
        alpha = jnp.exp2(m_prev - m_next)
        for cols in halves:
            if c_next is not None:
                scores(c_next, 1 - slot, cols)
            pt = jnp.exp2(s_sc[slot, :, cols] - m_next[:, cols]).astype(BF16)
            head = cols.start // tq
            acc_sc[:, cols] = alpha[:, cols] * acc_sc[:, cols] + _dot(vc[v_rows[head % v_heads]], pt)
        return m_next

    unroll = min(FLASH_UNROLL, n_chunks)

    def group(j, m):
        for u in range(unroll):
            c = unroll * j + u
            m = step(c, u % SCORE_SLOTS, m, c + 1)
        return m

    for cols in halves:
        scores(0, 0, cols)
    m = lax.fori_loop(0, n_chunks // unroll - 1, group, jnp.full((1, 2 * tq), -jnp.inf, F32))
    for c in range(n_chunks - unroll, n_chunks):
        m = step(c, c % SCORE_SLOTS, m, c + 1 if c + 1 < n_chunks else None)
    acc = acc_sc[...]
    return acc[:dv] / acc[dv:dv + 1]


def _gqa_kernel(qt_ref, k_ref, vt_ref, o_ref, acc_sc, s_sc, *, tq, n_chunks):
    o = _stacked_flash(qt_ref, k_ref, vt_ref, acc_sc, s_sc, tq=tq, n_chunks=n_chunks, v_heads=B_KV_HEADS)
    o_ref[...] = jnp.concatenate([o[:, :tq], o[:, tq:]], axis=0).T


def _diff_kernel(qt_ref, k_ref, vt_ref, lq1_ref, lk1_ref, lq2_ref, lk2_ref, sub_ref, o_ref, acc_sc,
                 s_sc, *, tq, n_chunks, lam_init):
    o = _stacked_flash(qt_ref, k_ref, vt_ref, acc_sc, s_sc, tq=tq, n_chunks=n_chunks, v_heads=1)
    lam = (jnp.exp(jnp.sum(lq1_ref[...] * lk1_ref[...], axis=-1, keepdims=True))
           - jnp.exp(jnp.sum(lq2_ref[...] * lk2_ref[...], axis=-1, keepdims=True)) + lam_init)
    oc = (o[:, :tq] - lam * o[:, tq:]).T
    ms = jnp.mean(oc * oc, axis=-1, keepdims=True)
    o_ref[...] = oc * lax.rsqrt(ms + EPS) * sub_ref[...] * (1.0 - lam_init)


def _flash_tq(s):
    return min(1024, s)


def _flash_scratch(tq, v_heads):
    return [pltpu.VMEM((LANE // v_heads + ONES_ROWS, 2 * tq), F32),
            pltpu.VMEM((SCORE_SLOTS, FLASH_TK, 2 * tq), F32)]


def _gqa_attention(qt, k, vt):
    bn, _, s = qt.shape
    tq, n_chunks = _flash_tq(s), s // FLASH_TK
    kern = functools.partial(_gqa_kernel, tq=tq, n_chunks=n_chunks)
    return pl.pallas_call(
        kern,
        grid=(bn, B_HEADS // 2, s // tq),
        in_specs=[pl.BlockSpec((None, LANE, tq), lambda b, p, i: (b, p, i)),
                  pl.BlockSpec((None, s, LANE), lambda b, p, i: (b, 0, 0)),
                  pl.BlockSpec((None, n_chunks, _vt_rows(B_KV_HEADS), FLASH_TK), lambda b, p, i: (b, 0, 0, 0))],
        out_specs=pl.BlockSpec((None, tq, LANE), lambda b, p, i: (b, i, p)),
        out_shape=jax.ShapeDtypeStruct((bn, s, B_WIDTH), F32),
        scratch_shapes=_flash_scratch(tq, B_KV_HEADS),
        compiler_params=_cparams(3),
        name="gqa_attn",
    )(qt, k, vt)


def _diff_attention(qt, k, vt, lam_rows, subln, lam_init):
    bn, _, s = qt.shape
    tq, n_chunks = _flash_tq(s), s // FLASH_TK
    kern = functools.partial(_diff_kernel, tq=tq, n_chunks=n_chunks, lam_init=lam_init)
    row = lambda n: pl.BlockSpec((1, n), lambda b, h, i: (0, 0))
    return pl.pallas_call(
        kern,
        grid=(bn, C_HEADS, s // tq),
        in_specs=[pl.BlockSpec((None, LANE, tq), lambda b, h, i: (b, h, i)),
                  pl.BlockSpec((None, s, LANE), lambda b, h, i: (b, 0, h)),
                  pl.BlockSpec((None, n_chunks, _vt_rows(1), FLASH_TK), lambda b, h, i: (b, 0, h, 0)),
                  row(C_HEAD_DIM), row(C_HEAD_DIM), row(C_HEAD_DIM), row(C_HEAD_DIM), row(LANE)],
        out_specs=pl.BlockSpec((None, tq, LANE), lambda b, h, i: (b, i, h)),
        out_shape=jax.ShapeDtypeStruct((bn, s, C_WIDTH), F32),
        scratch_shapes=_flash_scratch(tq, 1),
        compiler_params=_cparams(3),
        name="diff_attn",
    )(qt, k, vt, *lam_rows, subln)


def _final_kernel(x_ref, shift_ref, scale_ref, gate_ref, g_ref, oa0_ref, oa1_ref, oa2_ref, yb_ref, yc_ref,
                  wz_ref, wbg_ref, bbg_ref, woa_ref, wob_ref, woc_ref, wout_ref, o_ref, *stage_refs):
    x = x_ref[...]
    tm = x.shape[0]
    h = _modulated_norm(x, g_ref[...], scale_ref[...], shift_ref[...]).astype(BF16)
    lane = lax.broadcasted_iota(jnp.int32, (1, LANE), 1)

    for oa_ref, st_ref, (_, dil) in zip((oa1_ref, oa2_ref), stage_refs, A_GROUPS[1:]):
        for r in range(dil):
            for j in range(A_HEADS):
                lo = r * A_SLAB + j * LANE
                st_ref[j, pl.ds(r, tm // dil, stride=dil), :] = oa_ref[:, lo:lo + LANE]

    ya = []
    for j in range(A_HEADS):
        slabs = [oa0_ref[:, j * LANE:(j + 1) * LANE]] + [st_ref[j] for st_ref in stage_refs]
        lses = [t[:, A_HEAD_DIM:A_HEAD_DIM + 1] for t in slabs]
        top = jnp.maximum(jnp.maximum(lses[0], lses[1]), lses[2])
        ws = [jnp.exp(t - top) for t in lses]
        mix = (ws[0] * slabs[0] + ws[1] * slabs[1] + ws[2] * slabs[2]) / (ws[0] + ws[1] + ws[2])
        ya.append(jnp.where(lane < A_HEAD_DIM, mix, 0.0))
    ya = jnp.concatenate(ya, axis=-1)

    def branch(y, z_lo, z_hi, wo_ref, g_lo):
        z = _dot(h, wz_ref[:, z_lo:z_hi])
        p = _dot((y * _silu(z)).astype(BF16), wo_ref[...])
        gate = jax.nn.sigmoid(_dot(h, wbg_ref[:, g_lo:g_lo + D_MODEL]) + bbg_ref[:, g_lo:g_lo + D_MODEL])
        return gate * p

    za_hi = A_SLAB
    zb_hi = za_hi + B_WIDTH
    zc_hi = zb_hi + C_WIDTH
    merged = (branch(ya, 0, za_hi, woa_ref, 0)
              + branch(yb_ref[...], za_hi, zb_hi, wob_ref, D_MODEL)
              + branch(yc_ref[...], zb_hi, zc_hi, woc_ref, 2 * D_MODEL))
    out = _dot(merged.astype(BF16), wout_ref[...])
    o_ref[...] = x + gate_ref[...] * out


def _final(x, shift, scale, gate, norm_g, oa, yb, yc, wz, wbg, bbg, woa, wob, woc, wout, *, tm):
    bn, s, _ = x.shape
    tok = lambda w: pl.BlockSpec((None, tm, w), lambda b, i: (b, i, 0))
    vec = pl.BlockSpec((None, 1, D_MODEL), lambda b, i: (b, 0, 0))
    full = lambda a: pl.BlockSpec(a.shape, lambda b, i: (0, 0), pipeline_mode=pl.Buffered(1))
    band = [pl.BlockSpec((None, tm // dil, dil * A_SLAB), lambda b, i: (b, i, 0)) for _, dil in A_GROUPS]
    return pl.pallas_call(
        _final_kernel,
        grid=(bn, s // tm),
        in_specs=[tok(D_MODEL), vec, vec, vec, full(norm_g), *band,
                  tok(B_WIDTH), tok(C_WIDTH), full(wz), full(wbg), full(bbg), full(woa), full(wob), full(woc),
                  full(wout)],
        out_specs=tok(D_MODEL),
        out_shape=jax.ShapeDtypeStruct((bn, s, D_MODEL), F32),
        scratch_shapes=[pltpu.VMEM((A_HEADS, tm, LANE), F32) for _ in A_GROUPS[1:]],
        compiler_params=_cparams(2),
        name="final",
    )(x, shift, scale, gate, norm_g, *oa, yb, yc, wz, wbg, bbg, woa, wob, woc, wout)


B_Q_ORDER = (0, 3, 1, 4, 2, 5)


def _slab_layout(head_dim, n_heads, rot_groups):
    src = -np.ones((LANE,), np.int32)
    group = -np.ones((LANE,), np.int32)
    freq = np.zeros((LANE,), np.int32)
    sign = np.zeros((LANE,), np.float32)
    width = ROT_SHIFT // n_heads
    for hd in range(n_heads):
        low = [(start + i, g, i, -1.0) for g, (start, half) in enumerate(rot_groups) for i in range(half)]
        high = [(start + half + i, g, i, 1.0) for g, (start, half) in enumerate(rot_groups) for i in range(half)]
        rotary = {d for d, _, _, _ in low + high}
        rest = [(d, -1, 0, 0.0) for d in range(head_dim) if d not in rotary]
        n_low_rest = width - len(low)
        low, high = low + rest[:n_low_rest], high + rest[n_low_rest:]
        assert len(low) == width and len(high) <= width
        for base, items in ((hd * width, low), (hd * width + ROT_SHIFT, high)):
            for lane, (d, g, i, sg) in enumerate(items, start=base):
                src[lane], group[lane], freq[lane], sign[lane] = hd * head_dim + d, g, i, sg
    return src, group, freq, sign


LAYOUT_A = _slab_layout(A_HEAD_DIM, 1, ((0, A_ROT // 2),))
LAYOUT_B = _slab_layout(B_HEAD_DIM, 2, ((0, B_HEAD_DIM // 4), (B_HEAD_DIM // 2, B_HEAD_DIM // 4)))
LAYOUT_C = _slab_layout(C_HEAD_DIM, 2, ((0, C_ROT // 2),))


def _to_slabs(w, layout, dims_per_slab):
    src = layout[0]
    n_slabs = w.shape[-1] // dims_per_slab
    idx = np.concatenate([np.maximum(src, 0) + s * dims_per_slab for s in range(n_slabs)])
    keep = np.tile(src >= 0, n_slabs)
    return jnp.where(keep, jnp.take(w, idx, axis=-1), 0.0)


def _pad_heads(w, n_heads):
    lead = w.shape[:-1]
    w = w.reshape(*lead, n_heads, A_HEAD_DIM)
    w = jnp.pad(w, [(0, 0)] * len(lead) + [(0, 0), (0, LANE - A_HEAD_DIM)])
    return w.reshape(*lead, n_heads * LANE)


def _permute_heads(w, order, dim):
    lead = w.shape[:-1]
    w = w.reshape(*lead, len(order), dim)
    return w[..., jnp.array(order), :].reshape(*lead, len(order) * dim)


def _prep_layer(w_in, qn_a, kn_a, qn_b, kn_b, qn_c, kn_c, w_oa, w_ob, w_oc):
    offs = [0]
    for n in IN_SIZES:
        offs.append(offs[-1] + n)
    qa, ka, va, za, qb, kb, vb, zb, qc, kc, vc, zc = [w_in[:, offs[i]:offs[i + 1]] for i in range(len(IN_SIZES))]

    slab_a = functools.partial(_to_slabs, layout=LAYOUT_A, dims_per_slab=A_HEAD_DIM)
    slab_b = functools.partial(_to_slabs, layout=LAYOUT_B, dims_per_slab=2 * B_HEAD_DIM)
    slab_c = functools.partial(_to_slabs, layout=LAYOUT_C, dims_per_slab=2 * C_HEAD_DIM)
    qb = _permute_heads(qb, B_Q_ORDER, B_HEAD_DIM)

    w_a = jnp.concatenate([slab_a(qa), slab_a(ka), _pad_heads(va, N_A)], axis=-1).astype(BF16)
    gain_a = jnp.concatenate([slab_a(jnp.tile(qn_a, N_A)), slab_a(jnp.tile(kn_a, N_A)),
                              jnp.zeros((N_A * LANE,), F32)])[None, :]

    w_b = jnp.concatenate([slab_b(qb), slab_b(kb), vb], axis=-1).astype(BF16)
    gain_b = jnp.concatenate([slab_b(jnp.tile(qn_b, B_HEADS)), slab_b(jnp.tile(kn_b, B_KV_HEADS)),
                              jnp.zeros((B_KV,), F32)])[None, :]

    w_c = jnp.concatenate([slab_c(qc), slab_c(kc), vc], axis=-1).astype(BF16)
    gain_c = jnp.concatenate([slab_c(jnp.tile(qn_c, 2 * C_HEADS)), slab_c(jnp.tile(kn_c, 2 * C_HEADS)),
                              jnp.zeros((C_WIDTH,), F32)])[None, :]

    w_z = jnp.concatenate([_pad_heads(za, A_HEADS), _permute_heads(zb, B_Q_ORDER, B_HEAD_DIM), zc],
                          axis=-1).astype(BF16)
    w_oa_p = _pad_heads(w_oa.T, A_HEADS).T.astype(BF16)
    w_ob_p = _permute_heads(w_ob.T, B_Q_ORDER, B_HEAD_DIM).T.astype(BF16)
    return dict(w_a=w_a, gain_a=gain_a, w_b=w_b, gain_b=gain_b, w_c=w_c, gain_c=gain_c, w_z=w_z,
                w_oa=w_oa_p, w_ob=w_ob_p, w_oc=w_oc.astype(BF16))


def _inv_freq(dim, theta):
    return theta ** (-jnp.arange(0, dim, 2, dtype=F32) / dim)


def _rope_tables(layout, inv_freqs, streams):
    _, group, freq, sign = layout
    ang = 0.0
    for g, (inv, pos) in enumerate(zip(inv_freqs, streams)):
        ang = ang + pos[:, None] * (inv[freq] * (group == g))[None, :]
    return jnp.cos(ang), jnp.sin(ang) * sign[None, :]


def _tables(s):
    pos = jnp.arange(s)
    tok, row, col = [t.astype(F32) for t in (pos, pos // GRID_W, pos % GRID_W)]
    inv_b = _inv_freq(B_HEAD_DIM // 2, AXIAL_THETA)
    tab_a = _rope_tables(LAYOUT_A, [_inv_freq(A_ROT, ROPE_THETA)], [tok])
    tab_b = _rope_tables(LAYOUT_B, [inv_b, inv_b], [row, col])
    tab_c = _rope_tables(LAYOUT_C, [_inv_freq(C_ROT, ROPE_THETA)], [tok])
    return tab_a, tab_b, tab_c


def _layer(x, mod, l, lp, tabs, norm_g, lam_rows, subln, w_bg, b_bg, w_out):
    bn, s, _ = x.shape
    shift, scale, gate = [m.reshape(bn, 1, D_MODEL) for m in jnp.split(mod, 3, axis=-1)]
    tab_a, tab_b, tab_c = tabs
    lam_init = 0.8 - 0.6 * math.exp(-0.3 * l)

    a_outs = tuple((A_SLAB, kind, dil) for kind in ("q", "k", "v") for _, dil in A_GROUPS)
    pa = _project(x, shift, scale, norm_g, lp["w_a"], lp["gain_a"], tab_a, a_outs, two_heads=False,
                  head_dim=A_HEAD_DIM, q_scale=A_HEAD_DIM ** -0.5 * LOG2E, tm=min(PROJ_A_TM, s), name="proj_a")
    n_g = len(A_GROUPS)
    oa = [_band_attention(pa[g], pa[n_g + g], pa[2 * n_g + g], dil) for g, (_, dil) in enumerate(A_GROUPS)]

    qb, kb, vb = _project(x, shift, scale, norm_g, lp["w_b"], lp["gain_b"], tab_b,
                          ((B_WIDTH, "qT", 1), (B_KV, "k", 1), (B_KV, "vT", B_KV_HEADS)), two_heads=True,
                          head_dim=B_HEAD_DIM, q_scale=B_HEAD_DIM ** -0.5 * LOG2E, tm=min(PROJ_TM, s), name="proj_b")
    yb = _gqa_attention(qb, kb, vb)

    qc, kc, vc = _project(x, shift, scale, norm_g, lp["w_c"], lp["gain_c"], tab_c,
                          ((C_WIDTH, "qT", 1), (C_WIDTH, "k", 1), (C_WIDTH, "vT", 1)), two_heads=True,
                          head_dim=C_HEAD_DIM, q_scale=C_HEAD_DIM ** -0.5 * LOG2E, tm=min(PROJ_TM, s), name="proj_c")
    yc = _diff_attention(qc, kc, vc, lam_rows, subln, lam_init)

    return _final(x, shift, scale, gate, norm_g, oa, yb, yc, lp["w_z"], w_bg, b_bg, lp["w_oa"], lp["w_ob"],
                  lp["w_oc"], w_out, tm=min(FINAL_TM, s))


def kernel(x_prompt, x_sample, c_prompt, c_sample, norm_g, w_ada, b_ada, w_in, qn_a, kn_a, qn_b, kn_b, qn_c, kn_c,
           lam_q1, lam_k1, lam_q2, lam_k2, subln_c, w_oa, w_ob, w_oc, w_bg, b_bg, w_out):
    groups = ((x_prompt, c_prompt), (x_sample, c_sample))
    rows = [c.shape[0] for _, c in groups]
    pad = -sum(rows) % 8
    c_all = jnp.concatenate([c for _, c in groups] + [jnp.zeros((pad, D_MODEL), F32)], axis=0)
    mod_all = _ada_mod(c_all, w_ada.astype(BF16), b_ada)

    layers = [_prep_layer(w_in[l], qn_a[l], kn_a[l], qn_b[l], kn_b[l], qn_c[l], kn_c[l], w_oa[l], w_ob[l], w_oc[l])
              for l in range(DEPTH)]
    w_bg16, w_out16 = w_bg.astype(BF16), w_out.astype(BF16)

    outs = []
    row0 = 0
    for (x, c), n in zip(groups, rows):
        tabs = _tables(x.shape[1])
        for l in range(DEPTH):
            lam_rows = [p[l][None, :] for p in (lam_q1, lam_k1, lam_q2, lam_k2)]
            x = _layer(x, mod_all[l, row0:row0 + n], l, layers[l], tabs, norm_g[l][None, :], lam_rows,
                       subln_c[l][None, :], w_bg16[l], b_bg[l][None, :], w_out16[l])
        outs.append(x)
        row0 += n
    return tuple(outs)
```

```python
import functools
import math

import jax
import jax.numpy as jnp
import numpy as np
from jax import lax
from jax.experimental import pallas as pl
from jax.experimental.pallas import tpu as pltpu

D_MODEL = 1024
DEPTH = 2
GRID_W = 64
EPS = 1e-6
NEG = -1e30
ROPE_THETA = 500000.0
AXIAL_THETA = 10000.0

A_GROUPS = ((128, 1), (512, 4), (2048, 16))
A_HEADS = 4
A_HEAD_DIM = 96
A_ROT = A_HEAD_DIM // 4
A_BAND = 64
N_A = len(A_GROUPS) * A_HEADS

B_HEADS = 6
B_KV_HEADS = 2
B_HEAD_DIM = 64
C_HEADS = 4
C_HEAD_DIM = 64
C_ROT = C_HEAD_DIM // 4

A_QKV = N_A * A_HEAD_DIM
A_WIDTH = A_HEADS * A_HEAD_DIM
B_WIDTH = B_HEADS * B_HEAD_DIM
B_KV = B_KV_HEADS * B_HEAD_DIM
C_WIDTH = C_HEADS * 2 * C_HEAD_DIM
IN_SIZES = (A_QKV, A_QKV, A_QKV, A_WIDTH, B_WIDTH, B_KV, B_KV, B_WIDTH, C_WIDTH, C_WIDTH, C_WIDTH, C_WIDTH)

LANE = 128
HALF = LANE // 2
ROT_SHIFT = LANE // 2
A_SLAB = A_HEADS * LANE
BF16_SUBLANES = 16
ONES_ROWS = BF16_SUBLANES
FLASH_TK = 512
COL_GROUP = 256
SCORE_SLOTS = 2
FLASH_UNROLL = 4
PROJ_TM = 512
PROJ_A_TM = 256
FINAL_TM = 512
LOG2E = math.log2(math.e)
LN2 = math.log(2.0)
VMEM_LIMIT = 48 * 1024 * 1024

BF16 = jnp.bfloat16
F32 = jnp.float32


def _cparams(n_axes):
    return pltpu.CompilerParams(dimension_semantics=("arbitrary",) * n_axes, vmem_limit_bytes=VMEM_LIMIT)


def _dot(a, b):
    return jnp.dot(a, b, preferred_element_type=F32)


def _dot_nt(a, b):
    return lax.dot_general(a, b, (((1,), (1,)), ((), ())), preferred_element_type=F32)


def _silu(x):
    return x * jax.nn.sigmoid(x)


def _ada_kernel(c_ref, w_ref, b_ref, o_ref):
    c = c_ref[...]
    o_ref[...] = _dot(_silu(c).astype(BF16), w_ref[...]) + b_ref[...]


def _ada_mod(c_all, w_ada, b_ada):
    rows = c_all.shape[0]
    return pl.pallas_call(
        _ada_kernel,
        grid=(DEPTH, 3),
        in_specs=[
            pl.BlockSpec((rows, D_MODEL), lambda l, n: (0, 0)),
            pl.BlockSpec((None, D_MODEL, D_MODEL), lambda l, n: (l, 0, n)),
            pl.BlockSpec((None, 1, D_MODEL), lambda l, n: (l, 0, n)),
        ],
        out_specs=pl.BlockSpec((None, rows, D_MODEL), lambda l, n: (l, 0, n)),
        out_shape=jax.ShapeDtypeStruct((DEPTH, rows, 3 * D_MODEL), F32),
        compiler_params=_cparams(2),
        name="ada_mod",
    )(c_all, w_ada, b_ada.reshape(DEPTH, 1, 3 * D_MODEL))


def _modulated_norm(x, g, scale, shift):
    ms = jnp.mean(x * x, axis=-1, keepdims=True)
    h = x * lax.rsqrt(ms + EPS) * g
    return h * (1.0 + scale) + shift


def _first_head(index):
    return (index & (HALF - 1)) < HALF // 2


def _head_norm_rope(u, gain, cos, sin, two_heads, head_dim):
    sq = u * u
    if two_heads:
        first = _first_head(lax.broadcasted_iota(jnp.int32, (1, LANE), 1))
        s_first = jnp.sum(jnp.where(first, sq, 0.0), axis=-1, keepdims=True)
        s_all = jnp.sum(sq, axis=-1, keepdims=True)
        ms = jnp.where(first, s_first, s_all - s_first) * (1.0 / head_dim)
    else:
        ms = jnp.sum(sq, axis=-1, keepdims=True) * (1.0 / head_dim)
    y = u * lax.rsqrt(ms + EPS) * gain
    return y * cos + pltpu.roll(y, ROT_SHIFT, 1) * sin


def _proj_kernel(x_ref, shift_ref, scale_ref, g_ref, w_ref, gain_ref, cos_ref, sin_ref, *refs,
                 outs, two_heads, head_dim, q_scale):
    out_refs, stage_ref = refs[:len(outs)], refs[len(outs):]
    h = _modulated_norm(x_ref[...], g_ref[...], scale_ref[...], shift_ref[...]).astype(BF16)
    cos, sin = cos_ref[...], sin_ref[...]
    tm = x_ref.shape[0]
    col = 0
    for o_ref, (width, kind, dil) in zip(out_refs, outs):
        u = _dot(h, w_ref[:, col:col + width])
        if kind == "vT":
            dv = LANE // dil
            for s in range(width // LANE):
                ut = u[:, s * LANE:(s + 1) * LANE].T.astype(BF16)
                for hd in range(dil):
                    base = (s * dil + hd) * (dv + ONES_ROWS)
                    o_ref[base:base + dv, :] = ut[hd * dv:(hd + 1) * dv]
                    o_ref[base + dv:base + dv + ONES_ROWS, :] = jnp.ones((ONES_ROWS, tm), BF16)
        else:
            if kind != "v":
                slabs = []
                for s in range(width // LANE):
                    lo = s * LANE
                    y = _head_norm_rope(u[:, lo:lo + LANE], gain_ref[:, col + lo:col + lo + LANE], cos, sin,
                                        two_heads, head_dim)
                    slabs.append(y * q_scale if kind in ("q", "qT") and q_scale != 1.0 else y)
                u = jnp.concatenate(slabs, axis=-1)
            if kind == "qT":
                for s in range(width // LANE):
                    o_ref[s * LANE:(s + 1) * LANE, :] = u[:, s * LANE:(s + 1) * LANE].T.astype(BF16)
            elif dil == 1:
                o_ref[...] = u.astype(BF16)
            else:
                stage = stage_ref[0]
                for s in range(width // LANE):
                    stage[s] = u[:, s * LANE:(s + 1) * LANE]
                rows = lambda r: pl.ds(r, tm // dil, stride=dil)
                for r in range(dil):
                    for s in range(width // LANE):
                        lo = r * width + s * LANE
                        o_ref[:, lo:lo + LANE] = stage[s, rows(r), :].astype(BF16)
        col += width


def _vt_rows(v_heads):
    return LANE + v_heads * ONES_ROWS


def _proj_out(kind, width, dil, bn, s, tm):
    if kind == "qT":
        return (bn, width, s), pl.BlockSpec((None, width, tm), lambda b, i: (b, 0, i))
    if kind == "vT":
        per = FLASH_TK // tm
        rows = width // LANE * _vt_rows(dil)
        return ((bn, s // FLASH_TK, rows, FLASH_TK),
                pl.BlockSpec((None, None, rows, tm), lambda b, i: (b, i // per, 0, i % per)))
    return (bn, s // dil, dil * width), pl.BlockSpec((None, tm // dil, dil * width), lambda b, i: (b, i, 0))


def _project(x, shift, scale, norm_g, w, gain, tables, outs, *, two_heads, head_dim, q_scale, tm, name):
    bn, s, _ = x.shape
    wcols = w.shape[1]
    kern = functools.partial(_proj_kernel, outs=outs, two_heads=two_heads, head_dim=head_dim, q_scale=q_scale)
    vec = pl.BlockSpec((None, 1, D_MODEL), lambda b, i: (b, 0, 0))
    tab = pl.BlockSpec((tm, LANE), lambda b, i: (i, 0))
    shapes, specs = zip(*[_proj_out(kind, width, dil, bn, s, tm) for width, kind, dil in outs])
    n_slabs = max(width for width, _, _ in outs) // LANE
    dilated = any(d > 1 and kind != "vT" for _, kind, d in outs)
    stage = [pltpu.VMEM((n_slabs, tm, LANE), F32)] if dilated else []
    return pl.pallas_call(
        kern,
        grid=(bn, s // tm),
        in_specs=[
            pl.BlockSpec((None, tm, D_MODEL), lambda b, i: (b, i, 0)),
            vec, vec,
            pl.BlockSpec((1, D_MODEL), lambda b, i: (0, 0)),
            pl.BlockSpec((D_MODEL, wcols), lambda b, i: (0, 0)),
            pl.BlockSpec((1, wcols), lambda b, i: (0, 0)),
            tab, tab,
        ],
        out_specs=list(specs),
        out_shape=[jax.ShapeDtypeStruct(shape, BF16) for shape in shapes],
        scratch_shapes=stage,
        compiler_params=_cparams(2),
        name=name,
    )(x, shift, scale, norm_g, w, gain, *tables)


def _band_attn_kernel(q_ref, kp_ref, kc_ref, kn_ref, vp_ref, vc_ref, vn_ref, bias_ref, o_ref, *, tb, n_blocks):
    i = pl.program_id(2)
    nk = tb + 2 * A_BAND
    colk = lax.broadcasted_iota(jnp.int32, (1, nk), 1)
    first_col = jnp.where(i == 0, A_BAND, 0)
    end_col = jnp.where(i == n_blocks - 1, tb + A_BAND, nk)
    bias = bias_ref[...] + jnp.where((colk < first_col) | (colk >= end_col), NEG, 0.0)
    lane = lax.broadcasted_iota(jnp.int32, (1, LANE), 1)
    heads = [slice(j * LANE, (j + 1) * LANE) for j in range(A_HEADS)]
    band = lambda p_ref, c_ref, n_ref, sl: jnp.concatenate([p_ref[:, sl], c_ref[:, sl], n_ref[:, sl]], axis=0)
    scores = [_dot_nt(q_ref[:, sl], band(kp_ref, kc_ref, kn_ref, sl)) for sl in heads]
    stats = []
    for s in scores:
        s = s + bias
        m = jnp.max(s, axis=-1, keepdims=True)
        p = jnp.exp2(s - m)
        stats.append((m, jnp.sum(p, axis=-1, keepdims=True), p.astype(BF16)))
    for sl, (m, l, p) in zip(heads, stats):
        o = _dot(p, band(vp_ref, vc_ref, vn_ref, sl)) / l
        o_ref[:, sl] = jnp.where(lane >= A_HEAD_DIM, m * LN2 + jnp.log(l), o)


def _band_attention(q, k, v, dilation):
    bn, length, _ = q.shape
    tb = min(256, length)
    nb = length // tb
    per = tb // A_BAND
    cur = pl.BlockSpec((None, tb, A_SLAB), lambda b, r, i: (b, i, r))
    prv = pl.BlockSpec((None, A_BAND, A_SLAB), lambda b, r, i: (b, jnp.maximum(i * per - 1, 0), r))
    nxt = pl.BlockSpec((None, A_BAND, A_SLAB), lambda b, r, i: (b, jnp.minimum((i + 1) * per, nb * per - 1), r))
    nk = tb + 2 * A_BAND
    dist = np.arange(nk)[None, :] - A_BAND - np.arange(tb)[:, None]
    bias = np.where(np.abs(dist) <= A_BAND, 0.0, NEG).astype(np.float32)
    kern = functools.partial(_band_attn_kernel, tb=tb, n_blocks=nb)
    return pl.pallas_call(
        kern,
        grid=(bn, dilation, nb),
        in_specs=[cur, prv, cur, nxt, prv, cur, nxt, pl.BlockSpec((tb, nk), lambda b, r, i: (0, 0))],
        out_specs=cur,
        out_shape=jax.ShapeDtypeStruct((bn, length, dilation * A_SLAB), F32),
        compiler_params=_cparams(3),
        name=f"band_attn_d{dilation}",
    )(q, k, k, k, v, v, v, bias)


def _stacked_flash(qt_ref, k_ref, vt_ref, acc_sc, s_sc, *, tq, n_chunks, v_heads):
    dv = LANE // v_heads
    v_rows = [slice(hd * (dv + ONES_ROWS), (hd + 1) * (dv + ONES_ROWS)) for hd in range(v_heads)]
    first = _first_head(lax.broadcasted_iota(jnp.int32, (LANE, 1), 0))
    qt = qt_ref[...]
    zero = jnp.zeros_like(qt)
    q2t = jnp.concatenate([jnp.where(first, qt, zero), jnp.where(first, zero, qt)], axis=1)
    acc_sc[...] = jnp.zeros(acc_sc.shape, F32)

    wide = v_heads == 1 and n_chunks > FLASH_UNROLL
    width = min(2 * COL_GROUP if wide else COL_GROUP, tq)
    halves = [slice(lo, lo + width) for lo in range(0, 2 * tq, width)]

    def scores(c, slot, cols):
        start = pl.multiple_of(c * FLASH_TK, FLASH_TK)
        s_sc[slot, :, cols] = _dot(k_ref[pl.ds(start, FLASH_TK), :], q2t[:, cols])

    def step(c, slot, m_prev, c_next=None):
        m_next = jnp.maximum(m_prev, jnp.max(s_sc[slot], axis=0, keepdims=True))
        vc = vt_ref[c]
        alpha = jnp.exp2(m_prev - m_next)
        for cols in halves:
            if c_next is not None:
                scores(c_next, 1 - slot, cols)
            pt = jnp.exp2(s_sc[slot, :, cols] - m_next[:, cols]).astype(BF16)
            head = cols.start // tq
            acc_sc[:, cols] = alpha[:, cols] * acc_sc[:, cols] + _dot(vc[v_rows[head % v_heads]], pt)
        return m_next

    unroll = min(FLASH_UNROLL, n_chunks)

    def group(j, m):
        for u in range(unroll):
            c = unroll * j + u
            m = step(c, u % SCORE_SLOTS, m, c + 1)
        return m

    for cols in halves:
        scores(0, 0, cols)
    m = lax.fori_loop(0, n_chunks // unroll - 1, group, jnp.full((1, 2 * tq), -jnp.inf, F32))
    for c in range(n_chunks - unroll, n_chunks):
        m = step(c, c % SCORE_SLOTS, m, c + 1 if c + 1 < n_chunks else None)
    acc = acc_sc[...]
    return acc[:dv] / acc[dv:dv + 1]


def _gqa_kernel(qt_ref, k_ref, vt_ref, o_ref, acc_sc, s_sc, *, tq, n_chunks):
    o = _stacked_flash(qt_ref, k_ref, vt_ref, acc_sc, s_sc, tq=tq, n_chunks=n_chunks, v_heads=B_KV_HEADS)
    o_ref[...] = jnp.concatenate([o[:, :tq], o[:, tq:]], axis=0).T


def _diff_kernel(qt_ref, k_ref, vt_ref, lq1_ref, lk1_ref, lq2_ref, lk2_ref, sub_ref, o_ref, acc_sc,
                 s_sc, *, tq, n_chunks, lam_init):
    o = _stacked_flash(qt_ref, k_ref, vt_ref, acc_sc, s_sc, tq=tq, n_chunks=n_chunks, v_heads=1)
    lam = (jnp.exp(jnp.sum(lq1_ref[...] * lk1_ref[...], axis=-1, keepdims=True))
           - jnp.exp(jnp.sum(lq2_ref[...] * lk2_ref[...], axis=-1, keepdims=True)) + lam_init)
    oc = (o[:, :tq] - lam * o[:, tq:]).T
    ms = jnp.mean(oc * oc, axis=-1, keepdims=True)
    o_ref[...] = oc * lax.rsqrt(ms + EPS) * sub_ref[...] * (1.0 - lam_init)


def _flash_tq(s):
    return min(2048, s)


def _flash_scratch(tq, v_heads):
    return [pltpu.VMEM((LANE // v_heads + ONES_ROWS, 2 * tq), F32),
            pltpu.VMEM((SCORE_SLOTS, FLASH_TK, 2 * tq), F32)]


def _gqa_attention(qt, k, vt):
    bn, _, s = qt.shape
    tq, n_chunks = _flash_tq(s), s // FLASH_TK
    kern = functools.partial(_gqa_kernel, tq=tq, n_chunks=n_chunks)
    return pl.pallas_call(
        kern,
        grid=(bn, B_HEADS // 2, s // tq),
        in_specs=[pl.BlockSpec((None, LANE, tq), lambda b, p, i: (b, p, i)),
                  pl.BlockSpec((None, s, LANE), lambda b, p, i: (b, 0, 0)),
                  pl.BlockSpec((None, n_chunks, _vt_rows(B_KV_HEADS), FLASH_TK), lambda b, p, i: (b, 0, 0, 0))],
        out_specs=pl.BlockSpec((None, tq, LANE), lambda b, p, i: (b, i, p)),
        out_shape=jax.ShapeDtypeStruct((bn, s, B_WIDTH), F32),
        scratch_shapes=_flash_scratch(tq, B_KV_HEADS),
        compiler_params=_cparams(3),
        name="gqa_attn",
    )(qt, k, vt)


def _diff_attention(qt, k, vt, lam_rows, subln, lam_init):
    bn, _, s = qt.shape
    tq, n_chunks = _flash_tq(s), s // FLASH_TK
    kern = functools.partial(_diff_kernel, tq=tq, n_chunks=n_chunks, lam_init=lam_init)
    row = lambda n: pl.BlockSpec((1, n), lambda b, h, i: (0, 0))
    return pl.pallas_call(
        kern,
        grid=(bn, C_HEADS, s // tq),
        in_specs=[pl.BlockSpec((None, LANE, tq), lambda b, h, i: (b, h, i)),
                  pl.BlockSpec((None, s, LANE), lambda b, h, i: (b, 0, h)),
                  pl.BlockSpec((None, n_chunks, _vt_rows(1), FLASH_TK), lambda b, h, i: (b, 0, h, 0)),
                  row(C_HEAD_DIM), row(C_HEAD_DIM), row(C_HEAD_DIM), row(C_HEAD_DIM), row(LANE)],
        out_specs=pl.BlockSpec((None, tq, LANE), lambda b, h, i: (b, i, h)),
        out_shape=jax.ShapeDtypeStruct((bn, s, C_WIDTH), F32),
        scratch_shapes=_flash_scratch(tq, 1),
        compiler_params=_cparams(3),
        name="diff_attn",
    )(qt, k, vt, *lam_rows, subln)


def _final_kernel(x_ref, shift_ref, scale_ref, gate_ref, g_ref, oa0_ref, oa1_ref, oa2_ref, yb_ref, yc_ref,
                  wz_ref, wbg_ref, bbg_ref, woa_ref, wob_ref, woc_ref, wout_ref, o_ref, *stage_refs):
    x = x_ref[...]
    tm = x.shape[0]
    h = _modulated_norm(x, g_ref[...], scale_ref[...], shift_ref[...]).astype(BF16)
    lane = lax.broadcasted_iota(jnp.int32, (1, LANE), 1)

    for oa_ref, st_ref, (_, dil) in zip((oa1_ref, oa2_ref), stage_refs, A_GROUPS[1:]):
        for r in range(dil):
            for j in range(A_HEADS):
                lo = r * A_SLAB + j * LANE
                st_ref[j, pl.ds(r, tm // dil, stride=dil), :] = oa_ref[:, lo:lo + LANE]

    ya = []
    for j in range(A_HEADS):
        slabs = [oa0_ref[:, j * LANE:(j + 1) * LANE]] + [st_ref[j] for st_ref in stage_refs]
        lses = [t[:, A_HEAD_DIM:A_HEAD_DIM + 1] for t in slabs]
        top = jnp.maximum(jnp.maximum(lses[0], lses[1]), lses[2])
        ws = [jnp.exp(t - top) for t in lses]
        mix = (ws[0] * slabs[0] + ws[1] * slabs[1] + ws[2] * slabs[2]) / (ws[0] + ws[1] + ws[2])
        ya.append(jnp.where(lane < A_HEAD_DIM, mix, 0.0))
    ya = jnp.concatenate(ya, axis=-1)

    def branch(y, z_lo, z_hi, wo_ref, g_lo):
        z = _dot(h, wz_ref[:, z_lo:z_hi])
        p = _dot((y * _silu(z)).astype(BF16), wo_ref[...])
        gate = jax.nn.sigmoid(_dot(h, wbg_ref[:, g_lo:g_lo + D_MODEL]) + bbg_ref[:, g_lo:g_lo + D_MODEL])
        return gate * p

    za_hi = A_SLAB
    zb_hi = za_hi + B_WIDTH
    zc_hi = zb_hi + C_WIDTH
    merged = (branch(ya, 0, za_hi, woa_ref, 0)
              + branch(yb_ref[...], za_hi, zb_hi, wob_ref, D_MODEL)
              + branch(yc_ref[...], zb_hi, zc_hi, woc_ref, 2 * D_MODEL))
    out = _dot(merged.astype(BF16), wout_ref[...])
    o_ref[...] = x + gate_ref[...] * out


def _final(x, shift, scale, gate, norm_g, oa, yb, yc, wz, wbg, bbg, woa, wob, woc, wout, *, tm):
    bn, s, _ = x.shape
    tok = lambda w: pl.BlockSpec((None, tm, w), lambda b, i: (b, i, 0))
    vec = pl.BlockSpec((None, 1, D_MODEL), lambda b, i: (b, 0, 0))
    full = lambda a: pl.BlockSpec(a.shape, lambda b, i: (0, 0), pipeline_mode=pl.Buffered(1))
    band = [pl.BlockSpec((None, tm // dil, dil * A_SLAB), lambda b, i: (b, i, 0)) for _, dil in A_GROUPS]
    return pl.pallas_call(
        _final_kernel,
        grid=(bn, s // tm),
        in_specs=[tok(D_MODEL), vec, vec, vec, full(norm_g), *band,
                  tok(B_WIDTH), tok(C_WIDTH), full(wz), full(wbg), full(bbg), full(woa), full(wob), full(woc),
                  full(wout)],
        out_specs=tok(D_MODEL),
        out_shape=jax.ShapeDtypeStruct((bn, s, D_MODEL), F32),
        scratch_shapes=[pltpu.VMEM((A_HEADS, tm, LANE), F32) for _ in A_GROUPS[1:]],
        compiler_params=_cparams(2),
        name="final",
    )(x, shift, scale, gate, norm_g, *oa, yb, yc, wz, wbg, bbg, woa, wob, woc, wout)


B_Q_ORDER = (0, 3, 1, 4, 2, 5)


def _slab_layout(head_dim, n_heads, rot_groups):
    src = -np.ones((LANE,), np.int32)
    group = -np.ones((LANE,), np.int32)
    freq = np.zeros((LANE,), np.int32)
    sign = np.zeros((LANE,), np.float32)
    width = ROT_SHIFT // n_heads
    for hd in range(n_heads):
        low = [(start + i, g, i, -1.0) for g, (start, half) in enumerate(rot_groups) for i in range(half)]
        high = [(start + half + i, g, i, 1.0) for g, (start, half) in enumerate(rot_groups) for i in range(half)]
        rotary = {d for d, _, _, _ in low + high}
        rest = [(d, -1, 0, 0.0) for d in range(head_dim) if d not in rotary]
        n_low_rest = width - len(low)
        low, high = low + rest[:n_low_rest], high + rest[n_low_rest:]
        assert len(low) == width and len(high) <= width
        for base, items in ((hd * width, low), (hd * width + ROT_SHIFT, high)):
            for lane, (d, g, i, sg) in enumerate(items, start=base):
                src[lane], group[lane], freq[lane], sign[lane] = hd * head_dim + d, g, i, sg
    return src, group, freq, sign


LAYOUT_A = _slab_layout(A_HEAD_DIM, 1, ((0, A_ROT // 2),))
LAYOUT_B = _slab_layout(B_HEAD_DIM, 2, ((0, B_HEAD_DIM // 4), (B_HEAD_DIM // 2, B_HEAD_DIM // 4)))
LAYOUT_C = _slab_layout(C_HEAD_DIM, 2, ((0, C_ROT // 2),))


def _to_slabs(w, layout, dims_per_slab):
    src = layout[0]
    n_slabs = w.shape[-1] // dims_per_slab
    idx = np.concatenate([np.maximum(src, 0) + s * dims_per_slab for s in range(n_slabs)])
    keep = np.tile(src >= 0, n_slabs)
    return jnp.where(keep, jnp.take(w, idx, axis=-1), 0.0)


def _pad_heads(w, n_heads):
    lead = w.shape[:-1]
    w = w.reshape(*lead, n_heads, A_HEAD_DIM)
    w = jnp.pad(w, [(0, 0)] * len(lead) + [(0, 0), (0, LANE - A_HEAD_DIM)])
    return w.reshape(*lead, n_heads * LANE)


def _permute_heads(w, order, dim):
    lead = w.shape[:-1]
    w = w.reshape(*lead, len(order), dim)
    return w[..., jnp.array(order), :].reshape(*lead, len(order) * dim)


def _prep_layer(w_in, qn_a, kn_a, qn_b, kn_b, qn_c, kn_c, w_oa, w_ob, w_oc):
    offs = [0]
    for n in IN_SIZES:
        offs.append(offs[-1] + n)
    qa, ka, va, za, qb, kb, vb, zb, qc, kc, vc, zc = [w_in[:, offs[i]:offs[i + 1]] for i in range(len(IN_SIZES))]

    slab_a = functools.partial(_to_slabs, layout=LAYOUT_A, dims_per_slab=A_HEAD_DIM)
    slab_b = functools.partial(_to_slabs, layout=LAYOUT_B, dims_per_slab=2 * B_HEAD_DIM)
    slab_c = functools.partial(_to_slabs, layout=LAYOUT_C, dims_per_slab=2 * C_HEAD_DIM)
    qb = _permute_heads(qb, B_Q_ORDER, B_HEAD_DIM)

    w_a = jnp.concatenate([slab_a(qa), slab_a(ka), _pad_heads(va, N_A)], axis=-1).astype(BF16)
    gain_a = jnp.concatenate([slab_a(jnp.tile(qn_a, N_A)), slab_a(jnp.tile(kn_a, N_A)),
                              jnp.zeros((N_A * LANE,), F32)])[None, :]

    w_b = jnp.concatenate([slab_b(qb), slab_b(kb), vb], axis=-1).astype(BF16)
    gain_b = jnp.concatenate([slab_b(jnp.tile(qn_b, B_HEADS)), slab_b(jnp.tile(kn_b, B_KV_HEADS)),
                              jnp.zeros((B_KV,), F32)])[None, :]

    w_c = jnp.concatenate([slab_c(qc), slab_c(kc), vc], axis=-1).astype(BF16)
    gain_c = jnp.concatenate([slab_c(jnp.tile(qn_c, 2 * C_HEADS)), slab_c(jnp.tile(kn_c, 2 * C_HEADS)),
                              jnp.zeros((C_WIDTH,), F32)])[None, :]

    w_z = jnp.concatenate([_pad_heads(za, A_HEADS), _permute_heads(zb, B_Q_ORDER, B_HEAD_DIM), zc],
                          axis=-1).astype(BF16)
    w_oa_p = _pad_heads(w_oa.T, A_HEADS).T.astype(BF16)
    w_ob_p = _permute_heads(w_ob.T, B_Q_ORDER, B_HEAD_DIM).T.astype(BF16)
    return dict(w_a=w_a, gain_a=gain_a, w_b=w_b, gain_b=gain_b, w_c=w_c, gain_c=gain_c, w_z=w_z,
                w_oa=w_oa_p, w_ob=w_ob_p, w_oc=w_oc.astype(BF16))


def _inv_freq(dim, theta):
    return theta ** (-jnp.arange(0, dim, 2, dtype=F32) / dim)


def _rope_tables(layout, inv_freqs, streams):
    _, group, freq, sign = layout
    ang = 0.0
    for g, (inv, pos) in enumerate(zip(inv_freqs, streams)):
        ang = ang + pos[:, None] * (inv[freq] * (group == g))[None, :]
    return jnp.cos(ang), jnp.sin(ang) * sign[None, :]


def _tables(s):
    pos = jnp.arange(s)
    tok, row, col = [t.astype(F32) for t in (pos, pos // GRID_W, pos % GRID_W)]
    inv_b = _inv_freq(B_HEAD_DIM // 2, AXIAL_THETA)
    tab_a = _rope_tables(LAYOUT_A, [_inv_freq(A_ROT, ROPE_THETA)], [tok])
    tab_b = _rope_tables(LAYOUT_B, [inv_b, inv_b], [row, col])
    tab_c = _rope_tables(LAYOUT_C, [_inv_freq(C_ROT, ROPE_THETA)], [tok])
    return tab_a, tab_b, tab_c


def _layer(x, mod, l, lp, tabs, norm_g, lam_rows, subln, w_bg, b_bg, w_out):
    bn, s, _ = x.shape
    shift, scale, gate = [m.reshape(bn, 1, D_MODEL) for m in jnp.split(mod, 3, axis=-1)]
    tab_a, tab_b, tab_c = tabs
    lam_init = 0.8 - 0.6 * math.exp(-0.3 * l)

    a_outs = tuple((A_SLAB, kind, dil) for kind in ("q", "k", "v") for _, dil in A_GROUPS)
    pa = _project(x, shift, scale, norm_g, lp["w_a"], lp["gain_a"], tab_a, a_outs, two_heads=False,
                  head_dim=A_HEAD_DIM, q_scale=A_HEAD_DIM ** -0.5 * LOG2E, tm=min(PROJ_A_TM, s), name="proj_a")
    n_g = len(A_GROUPS)
    oa = [_band_attention(pa[g], pa[n_g + g], pa[2 * n_g + g], dil) for g, (_, dil) in enumerate(A_GROUPS)]

    qb, kb, vb = _project(x, shift, scale, norm_g, lp["w_b"], lp["gain_b"], tab_b,
                          ((B_WIDTH, "qT", 1), (B_KV, "k", 1), (B_KV, "vT", B_KV_HEADS)), two_heads=True,
                          head_dim=B_HEAD_DIM, q_scale=B_HEAD_DIM ** -0.5 * LOG2E, tm=min(PROJ_TM, s), name="proj_b")
    yb = _gqa_attention(qb, kb, vb)

    qc, kc, vc = _project(x, shift, scale, norm_g, lp["w_c"], lp["gain_c"], tab_c,
                          ((C_WIDTH, "qT", 1), (C_WIDTH, "k", 1), (C_WIDTH, "vT", 1)), two_heads=True,
                          head_dim=C_HEAD_DIM, q_scale=C_HEAD_DIM ** -0.5 * LOG2E, tm=min(PROJ_TM, s), name="proj_c")
    yc = _diff_attention(qc, kc, vc, lam_rows, subln, lam_init)

    return _final(x, shift, scale, gate, norm_g, oa, yb, yc, lp["w_z"], w_bg, b_bg, lp["w_oa"], lp["w_ob"],
                  lp["w_oc"], w_out, tm=min(FINAL_TM, s))


def kernel(x_prompt, x_sample, c_prompt, c_sample, norm_g, w_ada, b_ada, w_in, qn_a, kn_a, qn_b, kn_b, qn_c, kn_c,
           lam_q1, lam_k1, lam_q2, lam_k2, subln_c, w_oa, w_ob, w_oc, w_bg, b_bg, w_out):
    groups = ((x_prompt, c_prompt), (x_sample, c_sample))
    rows = [c.shape[0] for _, c in groups]
    pad = -sum(rows) % 8
    c_all = jnp.concatenate([c for _, c in groups] + [jnp.zeros((pad, D_MODEL), F32)], axis=0)
    mod_all = _ada_mod(c_all, w_ada.astype(BF16), b_ada)

    layers = [_prep_layer(w_in[l], qn_a[l], kn_a[l], qn_b[l], kn_b[l], qn_c[l], kn_c[l], w_oa[l], w_ob[l], w_oc[l])
              for l in range(DEPTH)]
    w_bg16, w_out16 = w_bg.astype(BF16), w_out.astype(BF16)

    outs = []
    row0 = 0
    for (x, c), n in zip(groups, rows):
        tabs = _tables(x.shape[1])
        for l in range(DEPTH):
            lam_rows = [p[l][None, :] for p in (lam_q1, lam_k1, lam_q2, lam_k2)]
            x = _layer(x, mod_all[l, row0:row0 + n], l, layers[l], tabs, norm_g[l][None, :], lam_rows,
                       subln_c[l][None, :], w_bg16[l], b_bg[l][None, :], w_out16[l])
        outs.append(x)
        row0 += n
    return tuple(outs)
```

```python
import functools
import math

import jax
import jax.numpy as jnp
import numpy as np
from jax import lax
from jax.experimental import pallas as pl
from jax.experimental.pallas import tpu as pltpu

D_MODEL = 1024
DEPTH = 2
GRID_W = 64
EPS = 1e-6
NEG = -1e30
ROPE_THETA = 500000.0
AXIAL_THETA = 10000.0

A_GROUPS = ((128, 1), (512, 4), (2048, 16))
A_HEADS = 4
A_HEAD_DIM = 96
A_ROT = A_HEAD_DIM // 4
A_BAND = 64
N_A = len(A_GROUPS) * A_HEADS

B_HEADS = 6
B_KV_HEADS = 2
B_HEAD_DIM = 64
C_HEADS = 4
C_HEAD_DIM = 64
C_ROT = C_HEAD_DIM // 4

A_QKV = N_A * A_HEAD_DIM
A_WIDTH = A_HEADS * A_HEAD_DIM
B_WIDTH = B_HEADS * B_HEAD_DIM
B_KV = B_KV_HEADS * B_HEAD_DIM
C_WIDTH = C_HEADS * 2 * C_HEAD_DIM
IN_SIZES = (A_QKV, A_QKV, A_QKV, A_WIDTH, B_WIDTH, B_KV, B_KV, B_WIDTH, C_WIDTH, C_WIDTH, C_WIDTH, C_WIDTH)

LANE = 128
HALF = LANE // 2
ROT_SHIFT = LANE // 2
A_SLAB = A_HEADS * LANE
BF16_SUBLANES = 16
ONES_ROWS = BF16_SUBLANES
FLASH_TK = 512
COL_GROUP = 256
SCORE_SLOTS = 2
FLASH_UNROLL = 4
PROJ_TM = 512
PROJ_A_TM = 256
FINAL_TM = 512
LOG2E = math.log2(math.e)
LN2 = math.log(2.0)
VMEM_LIMIT = 48 * 1024 * 1024

BF16 = jnp.bfloat16
F32 = jnp.float32


def _cparams(n_axes):
    return pltpu.CompilerParams(dimension_semantics=("arbitrary",) * n_axes, vmem_limit_bytes=VMEM_LIMIT)


def _dot(a, b):
    return jnp.dot(a, b, preferred_element_type=F32)


def _dot_nt(a, b):
    return lax.dot_general(a, b, (((1,), (1,)), ((), ())), preferred_element_type=F32)


def _silu(x):
    return x * jax.nn.sigmoid(x)


def _ada_kernel(c_ref, w_ref, b_ref, o_ref):
    c = c_ref[...]
    o_ref[...] = _dot(_silu(c).astype(BF16), w_ref[...]) + b_ref[...]


def _ada_mod(c_all, w_ada, b_ada):
    rows = c_all.shape[0]
    return pl.pallas_call(
        _ada_kernel,
        grid=(DEPTH, 3),
        in_specs=[
            pl.BlockSpec((rows, D_MODEL), lambda l, n: (0, 0)),
            pl.BlockSpec((None, D_MODEL, D_MODEL), lambda l, n: (l, 0, n)),
            pl.BlockSpec((None, 1, D_MODEL), lambda l, n: (l, 0, n)),
        ],
        out_specs=pl.BlockSpec((None, rows, D_MODEL), lambda l, n: (l, 0, n)),
        out_shape=jax.ShapeDtypeStruct((DEPTH, rows, 3 * D_MODEL), F32),
        compiler_params=_cparams(2),
        name="ada_mod",
    )(c_all, w_ada, b_ada.reshape(DEPTH, 1, 3 * D_MODEL))


def _modulated_norm(x, g, scale, shift):
    ms = jnp.mean(x * x, axis=-1, keepdims=True)
    h = x * lax.rsqrt(ms + EPS) * g
    return h * (1.0 + scale) + shift


def _first_head(index):
    return (index & (HALF - 1)) < HALF // 2


def _head_norm_rope(u, gain, cos, sin, two_heads, head_dim):
    sq = u * u
    if two_heads:
        first = _first_head(lax.broadcasted_iota(jnp.int32, (1, LANE), 1))
        s_first = jnp.sum(jnp.where(first, sq, 0.0), axis=-1, keepdims=True)
        s_all = jnp.sum(sq, axis=-1, keepdims=True)
        ms = jnp.where(first, s_first, s_all - s_first) * (1.0 / head_dim)
    else:
        ms = jnp.sum(sq, axis=-1, keepdims=True) * (1.0 / head_dim)
    y = u * lax.rsqrt(ms + EPS) * gain
    return y * cos + pltpu.roll(y, ROT_SHIFT, 1) * sin


def _proj_kernel(x_ref, shift_ref, scale_ref, g_ref, w_ref, gain_ref, cos_ref, sin_ref, *refs,
                 outs, two_heads, head_dim, q_scale):
    out_refs, stage_ref = refs[:len(outs)], refs[len(outs):]
    h = _modulated_norm(x_ref[...], g_ref[...], scale_ref[...], shift_ref[...]).astype(BF16)
    cos, sin = cos_ref[...], sin_ref[...]
    tm = x_ref.shape[0]
    col = 0
    for o_ref, (width, kind, dil) in zip(out_refs, outs):
        u = _dot(h, w_ref[:, col:col + width])
        if kind == "vT":
            dv = LANE // dil
            for s in range(width // LANE):
                ut = u[:, s * LANE:(s + 1) * LANE].T.astype(BF16)
                for hd in range(dil):
                    base = (s * dil + hd) * (dv + ONES_ROWS)
                    o_ref[base:base + dv, :] = ut[hd * dv:(hd + 1) * dv]
                    o_ref[base + dv:base + dv + ONES_ROWS, :] = jnp.ones((ONES_ROWS, tm), BF16)
        else:
            if kind != "v":
                slabs = []
                for s in range(width // LANE):
                    lo = s * LANE
                    y = _head_norm_rope(u[:, lo:lo + LANE], gain_ref[:, col + lo:col + lo + LANE], cos, sin,
                                        two_heads, head_dim)
                    slabs.append(y * q_scale if kind in ("q", "qT") and q_scale != 1.0 else y)
                u = jnp.concatenate(slabs, axis=-1)
            if kind == "qT":
                for s in range(width // LANE):
                    o_ref[s * LANE:(s + 1) * LANE, :] = u[:, s * LANE:(s + 1) * LANE].T.astype(BF16)
            elif dil == 1:
                o_ref[...] = u.astype(BF16)
            else:
                stage = stage_ref[0]
                for s in range(width // LANE):
                    stage[s] = u[:, s * LANE:(s + 1) * LANE]
                rows = lambda r: pl.ds(r, tm // dil, stride=dil)
                for r in range(dil):
                    for s in range(width // LANE):
                        lo = r * width + s * LANE
                        o_ref[:, lo:lo + LANE] = stage[s, rows(r), :].astype(BF16)
        col += width


def _vt_rows(v_heads):
    return LANE + v_heads * ONES_ROWS


def _proj_out(kind, width, dil, bn, s, tm):
    if kind == "qT":
        return (bn, width, s), pl.BlockSpec((None, width, tm), lambda b, i: (b, 0, i))
    if kind == "vT":
        per = FLASH_TK // tm
        rows = width // LANE * _vt_rows(dil)
        return ((bn, s // FLASH_TK, rows, FLASH_TK),
                pl.BlockSpec((None, None, rows, tm), lambda b, i: (b, i // per, 0, i % per)))
    return (bn, s // dil, dil * width), pl.BlockSpec((None, tm // dil, dil * width), lambda b, i: (b, i, 0))


def _project(x, shift, scale, norm_g, w, gain, tables, outs, *, two_heads, head_dim, q_scale, tm, name):
    bn, s, _ = x.shape
    wcols = w.shape[1]
    kern = functools.partial(_proj_kernel, outs=outs, two_heads=two_heads, head_dim=head_dim, q_scale=q_scale)
    vec = pl.BlockSpec((None, 1, D_MODEL), lambda b, i: (b, 0, 0))
    tab = pl.BlockSpec((tm, LANE), lambda b, i: (i, 0))
    shapes, specs = zip(*[_proj_out(kind, width, dil, bn, s, tm) for width, kind, dil in outs])
    n_slabs = max(width for width, _, _ in outs) // LANE
    dilated = any(d > 1 and kind != "vT" for _, kind, d in outs)
    stage = [pltpu.VMEM((n_slabs, tm, LANE), F32)] if dilated else []
    return pl.pallas_call(
        kern,
        grid=(bn, s // tm),
        in_specs=[
            pl.BlockSpec((None, tm, D_MODEL), lambda b, i: (b, i, 0)),
            vec, vec,
            pl.BlockSpec((1, D_MODEL), lambda b, i: (0, 0)),
            pl.BlockSpec((D_MODEL, wcols), lambda b, i: (0, 0)),
            pl.BlockSpec((1, wcols), lambda b, i: (0, 0)),
            tab, tab,
        ],
        out_specs=list(specs),
        out_shape=[jax.ShapeDtypeStruct(shape, BF16) for shape in shapes],
        scratch_shapes=stage,
        compiler_params=_cparams(2),
        name=name,
    )(x, shift, scale, norm_g, w, gain, *tables)


def _band_attn_kernel(q_ref, kp_ref, kc_ref, kn_ref, vp_ref, vc_ref, vn_ref, bias_ref, o_ref, *, tb, n_blocks):
    i = pl.program_id(2)
    nk = tb + 2 * A_BAND
    colk = lax.broadcasted_iota(jnp.int32, (1, nk), 1)
    first_col = jnp.where(i == 0, A_BAND, 0)
    end_col = jnp.where(i == n_blocks - 1, tb + A_BAND, nk)
    bias = bias_ref[...] + jnp.where((colk < first_col) | (colk >= end_col), NEG, 0.0)
    lane = lax.broadcasted_iota(jnp.int32, (1, LANE), 1)
    heads = [slice(j * LANE, (j + 1) * LANE) for j in range(A_HEADS)]
    band = lambda p_ref, c_ref, n_ref, sl: jnp.concatenate([p_ref[:, sl], c_ref[:, sl], n_ref[:, sl]], axis=0)
    scores = [_dot_nt(q_ref[:, sl], band(kp_ref, kc_ref, kn_ref, sl)) for sl in heads]
    stats = []
    for s in scores:
        s = s + bias
        m = jnp.max(s, axis=-1, keepdims=True)
        p = jnp.exp2(s - m)
        stats.append((m, jnp.sum(p, axis=-1, keepdims=True), p.astype(BF16)))
    for sl, (m, l, p) in zip(heads, stats):
        o = _dot(p, band(vp_ref, vc_ref, vn_ref, sl)) / l
        o_ref[:, sl] = jnp.where(lane >= A_HEAD_DIM, m * LN2 + jnp.log(l), o)


def _band_attention(q, k, v, dilation):
    bn, length, _ = q.shape
    tb = min(256, length)
    nb = length // tb
    per = tb // A_BAND
    cur = pl.BlockSpec((None, tb, A_SLAB), lambda b, r, i: (b, i, r))
    prv = pl.BlockSpec((None, A_BAND, A_SLAB), lambda b, r, i: (b, jnp.maximum(i * per - 1, 0), r))
    nxt = pl.BlockSpec((None, A_BAND, A_SLAB), lambda b, r, i: (b, jnp.minimum((i + 1) * per, nb * per - 1), r))
    nk = tb + 2 * A_BAND
    dist = np.arange(nk)[None, :] - A_BAND - np.arange(tb)[:, None]
    bias = np.where(np.abs(dist) <= A_BAND, 0.0, NEG).astype(np.float32)
    kern = functools.partial(_band_attn_kernel, tb=tb, n_blocks=nb)
    return pl.pallas_call(
        kern,
        grid=(bn, dilation, nb),
        in_specs=[cur, prv, cur, nxt, prv, cur, nxt, pl.BlockSpec((tb, nk), lambda b, r, i: (0, 0))],
        out_specs=cur,
        out_shape=jax.ShapeDtypeStruct((bn, length, dilation * A_SLAB), F32),
        compiler_params=_cparams(3),
        name=f"band_attn_d{dilation}",
    )(q, k, k, k, v, v, v, bias)


def _stacked_flash(qt_ref, k_ref, vt_ref, acc_sc, s_sc, *, tq, n_chunks, v_heads):
    dv = LANE // v_heads
    v_rows = [slice(hd * (dv + ONES_ROWS), (hd + 1) * (dv + ONES_ROWS)) for hd in range(v_heads)]
    first = _first_head(lax.broadcasted_iota(jnp.int32, (LANE, 1), 0))
    qt = qt_ref[...]
    zero = jnp.zeros_like(qt)
    q2t = jnp.concatenate([jnp.where(first, qt, zero), jnp.where(first, zero, qt)], axis=1)
    acc_sc[...] = jnp.zeros(acc_sc.shape, F32)

    wide = False
    width = min(2 * COL_GROUP if wide else COL_GROUP, tq)
    halves = [slice(lo, lo + width) for lo in range(0, 2 * tq, width)]

    def scores(c, slot, cols):
        start = pl.multiple_of(c * FLASH_TK, FLASH_TK)
        s_sc[slot, :, cols] = _dot(k_ref[pl.ds(start, FLASH_TK), :], q2t[:, cols])

    def step(c, slot, m_prev, c_next=None):
        m_next = jnp.maximum(m_prev, jnp.max(s_sc[slot], axis=0, keepdims=True))
        vc = vt_ref[c]
        alpha = jnp.exp2(m_prev - m_next)
        for cols in halves:
            if c_next is not None:
                scores(c_next, 1 - slot, cols)
            pt = jnp.exp2(s_sc[slot, :, cols] - m_next[:, cols]).astype(BF16)
            head = cols.start // tq
            acc_sc[:, cols] = alpha[:, cols] * acc_sc[:, cols] + _dot(vc[v_rows[head % v_heads]], pt)
        return m_next

    unroll = min(FLASH_UNROLL, n_chunks)

    def group(j, m):
        for u in range(unroll):
            c = unroll * j + u
            m = step(c, u % SCORE_SLOTS, m, c + 1)
        return m

    for cols in halves:
        scores(0, 0, cols)
    m = lax.fori_loop(0, n_chunks // unroll - 1, group, jnp.full((1, 2 * tq), -jnp.inf, F32))
    for c in range(n_chunks - unroll, n_chunks):
        m = step(c, c % SCORE_SLOTS, m, c + 1 if c + 1 < n_chunks else None)
    acc = acc_sc[...]
    return acc[:dv] / acc[dv:dv + 1]


def _gqa_kernel(qt_ref, k_ref, vt_ref, o_ref, acc_sc, s_sc, *, tq, n_chunks):
    o = _stacked_flash(qt_ref, k_ref, vt_ref, acc_sc, s_sc, tq=tq, n_chunks=n_chunks, v_heads=B_KV_HEADS)
    o_ref[...] = jnp.concatenate([o[:, :tq], o[:, tq:]], axis=0).T


def _diff_kernel(qt_ref, k_ref, vt_ref, lq1_ref, lk1_ref, lq2_ref, lk2_ref, sub_ref, o_ref, acc_sc,
                 s_sc, *, tq, n_chunks, lam_init):
    o = _stacked_flash(qt_ref, k_ref, vt_ref, acc_sc, s_sc, tq=tq, n_chunks=n_chunks, v_heads=1)
    lam = (jnp.exp(jnp.sum(lq1_ref[...] * lk1_ref[...], axis=-1, keepdims=True))
           - jnp.exp(jnp.sum(lq2_ref[...] * lk2_ref[...], axis=-1, keepdims=True)) + lam_init)
    oc = (o[:, :tq] - lam * o[:, tq:]).T
    ms = jnp.mean(oc * oc, axis=-1, keepdims=True)
    o_ref[...] = oc * lax.rsqrt(ms + EPS) * sub_ref[...] * (1.0 - lam_init)


def _flash_tq(s):
    return min(2048, s)


def _flash_scratch(tq, v_heads):
    return [pltpu.VMEM((LANE // v_heads + ONES_ROWS, 2 * tq), F32),
            pltpu.VMEM((SCORE_SLOTS, FLASH_TK, 2 * tq), F32)]


def _gqa_attention(qt, k, vt):
    bn, _, s = qt.shape
    tq, n_chunks = _flash_tq(s), s // FLASH_TK
    kern = functools.partial(_gqa_kernel, tq=tq, n_chunks=n_chunks)
    return pl.pallas_call(
        kern,
        grid=(bn, B_HEADS // 2, s // tq),
        in_specs=[pl.BlockSpec((None, LANE, tq), lambda b, p, i: (b, p, i)),
                  pl.BlockSpec((None, s, LANE), lambda b, p, i: (b, 0, 0)),
                  pl.BlockSpec((None, n_chunks, _vt_rows(B_KV_HEADS), FLASH_TK), lambda b, p, i: (b, 0, 0, 0))],
        out_specs=pl.BlockSpec((None, tq, LANE), lambda b, p, i: (b, i, p)),
        out_shape=jax.ShapeDtypeStruct((bn, s, B_WIDTH), F32),
        scratch_shapes=_flash_scratch(tq, B_KV_HEADS),
        compiler_params=_cparams(3),
        name="gqa_attn",
    )(qt, k, vt)


def _diff_attention(qt, k, vt, lam_rows, subln, lam_init):
    bn, _, s = qt.shape
    tq, n_chunks = _flash_tq(s), s // FLASH_TK
    kern = functools.partial(_diff_kernel, tq=tq, n_chunks=n_chunks, lam_init=lam_init)
    row = lambda n: pl.BlockSpec((1, n), lambda b, h, i: (0, 0))
    return pl.pallas_call(
        kern,
        grid=(bn, C_HEADS, s // tq),
        in_specs=[pl.BlockSpec((None, LANE, tq), lambda b, h, i: (b, h, i)),
                  pl.BlockSpec((None, s, LANE), lambda b, h, i: (b, 0, h)),
                  pl.BlockSpec((None, n_chunks, _vt_rows(1), FLASH_TK), lambda b, h, i: (b, 0, h, 0)),
                  row(C_HEAD_DIM), row(C_HEAD_DIM), row(C_HEAD_DIM), row(C_HEAD_DIM), row(LANE)],
        out_specs=pl.BlockSpec((None, tq, LANE), lambda b, h, i: (b, i, h)),
        out_shape=jax.ShapeDtypeStruct((bn, s, C_WIDTH), F32),
        scratch_shapes=_flash_scratch(tq, 1),
        compiler_params=_cparams(3),
        name="diff_attn",
    )(qt, k, vt, *lam_rows, subln)


def _final_kernel(x_ref, shift_ref, scale_ref, gate_ref, g_ref, oa0_ref, oa1_ref, oa2_ref, yb_ref, yc_ref,
                  wz_ref, wbg_ref, bbg_ref, woa_ref, wob_ref, woc_ref, wout_ref, o_ref, *stage_refs):
    x = x_ref[...]
    tm = x.shape[0]
    h = _modulated_norm(x, g_ref[...], scale_ref[...], shift_ref[...]).astype(BF16)
    lane = lax.broadcasted_iota(jnp.int32, (1, LANE), 1)

    for oa_ref, st_ref, (_, dil) in zip((oa1_ref, oa2_ref), stage_refs, A_GROUPS[1:]):
        for r in range(dil):
            for j in range(A_HEADS):
                lo = r * A_SLAB + j * LANE
                st_ref[j, pl.ds(r, tm // dil, stride=dil), :] = oa_ref[:, lo:lo + LANE]

    ya = []
    for j in range(A_HEADS):
        slabs = [oa0_ref[:, j * LANE:(j + 1) * LANE]] + [st_ref[j] for st_ref in stage_refs]
        lses = [t[:, A_HEAD_DIM:A_HEAD_DIM + 1] for t in slabs]
        top = jnp.maximum(jnp.maximum(lses[0], lses[1]), lses[2])
        ws = [jnp.exp(t - top) for t in lses]
        mix = (ws[0] * slabs[0] + ws[1] * slabs[1] + ws[2] * slabs[2]) / (ws[0] + ws[1] + ws[2])
        ya.append(jnp.where(lane < A_HEAD_DIM, mix, 0.0))
    ya = jnp.concatenate(ya, axis=-1)

    def branch(y, z_lo, z_hi, wo_ref, g_lo):
        z = _dot(h, wz_ref[:, z_lo:z_hi])
        p = _dot((y * _silu(z)).astype(BF16), wo_ref[...])
        gate = jax.nn.sigmoid(_dot(h, wbg_ref[:, g_lo:g_lo + D_MODEL]) + bbg_ref[:, g_lo:g_lo + D_MODEL])
        return gate * p

    za_hi = A_SLAB
    zb_hi = za_hi + B_WIDTH
    zc_hi = zb_hi + C_WIDTH
    merged = (branch(ya, 0, za_hi, woa_ref, 0)
              + branch(yb_ref[...], za_hi, zb_hi, wob_ref, D_MODEL)
              + branch(yc_ref[...], zb_hi, zc_hi, woc_ref, 2 * D_MODEL))
    out = _dot(merged.astype(BF16), wout_ref[...])
    o_ref[...] = x + gate_ref[...] * out


def _final(x, shift, scale, gate, norm_g, oa, yb, yc, wz, wbg, bbg, woa, wob, woc, wout, *, tm):
    bn, s, _ = x.shape
    tok = lambda w: pl.BlockSpec((None, tm, w), lambda b, i: (b, i, 0))
    vec = pl.BlockSpec((None, 1, D_MODEL), lambda b, i: (b, 0, 0))
    full = lambda a: pl.BlockSpec(a.shape, lambda b, i: (0, 0), pipeline_mode=pl.Buffered(1))
    band = [pl.BlockSpec((None, tm // dil, dil * A_SLAB), lambda b, i: (b, i, 0)) for _, dil in A_GROUPS]
    return pl.pallas_call(
        _final_kernel,
        grid=(bn, s // tm),
        in_specs=[tok(D_MODEL), vec, vec, vec, full(norm_g), *band,
                  tok(B_WIDTH), tok(C_WIDTH), full(wz), full(wbg), full(bbg), full(woa), full(wob), full(woc),
                  full(wout)],
        out_specs=tok(D_MODEL),
        out_shape=jax.ShapeDtypeStruct((bn, s, D_MODEL), F32),
        scratch_shapes=[pltpu.VMEM((A_HEADS, tm, LANE), F32) for _ in A_GROUPS[1:]],
        compiler_params=_cparams(2),
        name="final",
    )(x, shift, scale, gate, norm_g, *oa, yb, yc, wz, wbg, bbg, woa, wob, woc, wout)


B_Q_ORDER = (0, 3, 1, 4, 2, 5)


def _slab_layout(head_dim, n_heads, rot_groups):
    src = -np.ones((LANE,), np.int32)
    group = -np.ones((LANE,), np.int32)
    freq = np.zeros((LANE,), np.int32)
    sign = np.zeros((LANE,), np.float32)
    width = ROT_SHIFT // n_heads
    for hd in range(n_heads):
        low = [(start + i, g, i, -1.0) for g, (start, half) in enumerate(rot_groups) for i in range(half)]
        high = [(start + half + i, g, i, 1.0) for g, (start, half) in enumerate(rot_groups) for i in range(half)]
        rotary = {d for d, _, _, _ in low + high}
        rest = [(d, -1, 0, 0.0) for d in range(head_dim) if d not in rotary]
        n_low_rest = width - len(low)
        low, high = low + rest[:n_low_rest], high + rest[n_low_rest:]
        assert len(low) == width and len(high) <= width
        for base, items in ((hd * width, low), (hd * width + ROT_SHIFT, high)):
            for lane, (d, g, i, sg) in enumerate(items, start=base):
                src[lane], group[lane], freq[lane], sign[lane] = hd * head_dim + d, g, i, sg
    return src, group, freq, sign


LAYOUT_A = _slab_layout(A_HEAD_DIM, 1, ((0, A_ROT // 2),))
LAYOUT_B = _slab_layout(B_HEAD_DIM, 2, ((0, B_HEAD_DIM // 4), (B_HEAD_DIM // 2, B_HEAD_DIM // 4)))
LAYOUT_C = _slab_layout(C_HEAD_DIM, 2, ((0, C_ROT // 2),))


def _to_slabs(w, layout, dims_per_slab):
    src = layout[0]
    n_slabs = w.shape[-1] // dims_per_slab
    idx = np.concatenate([np.maximum(src, 0) + s * dims_per_slab for s in range(n_slabs)])
    keep = np.tile(src >= 0, n_slabs)
    return jnp.where(keep, jnp.take(w, idx, axis=-1), 0.0)


def _pad_heads(w, n_heads):
    lead = w.shape[:-1]
    w = w.reshape(*lead, n_heads, A_HEAD_DIM)
    w = jnp.pad(w, [(0, 0)] * len(lead) + [(0, 0), (0, LANE - A_HEAD_DIM)])
    return w.reshape(*lead, n_heads * LANE)


def _permute_heads(w, order, dim):
    lead = w.shape[:-1]
    w = w.reshape(*lead, len(order), dim)
    return w[..., jnp.array(order), :].reshape(*lead, len(order) * dim)


def _prep_layer(w_in, qn_a, kn_a, qn_b, kn_b, qn_c, kn_c, w_oa, w_ob, w_oc):
    offs = [0]
    for n in IN_SIZES:
        offs.append(offs[-1] + n)
    qa, ka, va, za, qb, kb, vb, zb, qc, kc, vc, zc = [w_in[:, offs[i]:offs[i + 1]] for i in range(len(IN_SIZES))]

    slab_a = functools.partial(_to_slabs, layout=LAYOUT_A, dims_per_slab=A_HEAD_DIM)
    slab_b = functools.partial(_to_slabs, layout=LAYOUT_B, dims_per_slab=2 * B_HEAD_DIM)
    slab_c = functools.partial(_to_slabs, layout=LAYOUT_C, dims_per_slab=2 * C_HEAD_DIM)
    qb = _permute_heads(qb, B_Q_ORDER, B_HEAD_DIM)

    w_a = jnp.concatenate([slab_a(qa), slab_a(ka), _pad_heads(va, N_A)], axis=-1).astype(BF16)
    gain_a = jnp.concatenate([slab_a(jnp.tile(qn_a, N_A)), slab_a(jnp.tile(kn_a, N_A)),
                              jnp.zeros((N_A * LANE,), F32)])[None, :]

    w_b = jnp.concatenate([slab_b(qb), slab_b(kb), vb], axis=-1).astype(BF16)
    gain_b = jnp.concatenate([slab_b(jnp.tile(qn_b, B_HEADS)), slab_b(jnp.tile(kn_b, B_KV_HEADS)),
                              jnp.zeros((B_KV,), F32)])[None, :]

    w_c = jnp.concatenate([slab_c(qc), slab_c(kc), vc], axis=-1).astype(BF16)
    gain_c = jnp.concatenate([slab_c(jnp.tile(qn_c, 2 * C_HEADS)), slab_c(jnp.tile(kn_c, 2 * C_HEADS)),
                              jnp.zeros((C_WIDTH,), F32)])[None, :]

    w_z = jnp.concatenate([_pad_heads(za, A_HEADS), _permute_heads(zb, B_Q_ORDER, B_HEAD_DIM), zc],
                          axis=-1).astype(BF16)
    w_oa_p = _pad_heads(w_oa.T, A_HEADS).T.astype(BF16)
    w_ob_p = _permute_heads(w_ob.T, B_Q_ORDER, B_HEAD_DIM).T.astype(BF16)
    return dict(w_a=w_a, gain_a=gain_a, w_b=w_b, gain_b=gain_b, w_c=w_c, gain_c=gain_c, w_z=w_z,
                w_oa=w_oa_p, w_ob=w_ob_p, w_oc=w_oc.astype(BF16))


def _inv_freq(dim, theta):
    return theta ** (-jnp.arange(0, dim, 2, dtype=F32) / dim)


def _rope_tables(layout, inv_freqs, streams):
    _, group, freq, sign = layout
    ang = 0.0
    for g, (inv, pos) in enumerate(zip(inv_freqs, streams)):
        ang = ang + pos[:, None] * (inv[freq] * (group == g))[None, :]
    return jnp.cos(ang), jnp.sin(ang) * sign[None, :]


def _tables(s):
    pos = jnp.arange(s)
    tok, row, col = [t.astype(F32) for t in (pos, pos // GRID_W, pos % GRID_W)]
    inv_b = _inv_freq(B_HEAD_DIM // 2, AXIAL_THETA)
    tab_a = _rope_tables(LAYOUT_A, [_inv_freq(A_ROT, ROPE_THETA)], [tok])
    tab_b = _rope_tables(LAYOUT_B, [inv_b, inv_b], [row, col])
    tab_c = _rope_tables(LAYOUT_C, [_inv_freq(C_ROT, ROPE_THETA)], [tok])
    return tab_a, tab_b, tab_c


def _layer(x, mod, l, lp, tabs, norm_g, lam_rows, subln, w_bg, b_bg, w_out):
    bn, s, _ = x.shape
    shift, scale, gate = [m.reshape(bn, 1, D_MODEL) for m in jnp.split(mod, 3, axis=-1)]
    tab_a, tab_b, tab_c = tabs
    lam_init = 0.8 - 0.6 * math.exp(-0.3 * l)

    a_outs = tuple((A_SLAB, kind, dil) for kind in ("q", "k", "v") for _, dil in A_GROUPS)
    pa = _project(x, shift, scale, norm_g, lp["w_a"], lp["gain_a"], tab_a, a_outs, two_heads=False,
                  head_dim=A_HEAD_DIM, q_scale=A_HEAD_DIM ** -0.5 * LOG2E, tm=min(PROJ_A_TM, s), name="proj_a")
    n_g = len(A_GROUPS)
    oa = [_band_attention(pa[g], pa[n_g + g], pa[2 * n_g + g], dil) for g, (_, dil) in enumerate(A_GROUPS)]

    qb, kb, vb = _project(x, shift, scale, norm_g, lp["w_b"], lp["gain_b"], tab_b,
                          ((B_WIDTH, "qT", 1), (B_KV, "k", 1), (B_KV, "vT", B_KV_HEADS)), two_heads=True,
                          head_dim=B_HEAD_DIM, q_scale=B_HEAD_DIM ** -0.5 * LOG2E, tm=min(PROJ_TM, s), name="proj_b")
    yb = _gqa_attention(qb, kb, vb)

    qc, kc, vc = _project(x, shift, scale, norm_g, lp["w_c"], lp["gain_c"], tab_c,
                          ((C_WIDTH, "qT", 1), (C_WIDTH, "k", 1), (C_WIDTH, "vT", 1)), two_heads=True,
                          head_dim=C_HEAD_DIM, q_scale=C_HEAD_DIM ** -0.5 * LOG2E, tm=min(PROJ_TM, s), name="proj_c")
    yc = _diff_attention(qc, kc, vc, lam_rows, subln, lam_init)

    return _final(x, shift, scale, gate, norm_g, oa, yb, yc, lp["w_z"], w_bg, b_bg, lp["w_oa"], lp["w_ob"],
                  lp["w_oc"], w_out, tm=min(FINAL_TM, s))


def kernel(x_prompt, x_sample, c_prompt, c_sample, norm_g, w_ada, b_ada, w_in, qn_a, kn_a, qn_b, kn_b, qn_c, kn_c,
           lam_q1, lam_k1, lam_q2, lam_k2, subln_c, w_oa, w_ob, w_oc, w_bg, b_bg, w_out):
    groups = ((x_prompt, c_prompt), (x_sample, c_sample))
    rows = [c.shape[0] for _, c in groups]
    pad = -sum(rows) % 8
    c_all = jnp.concatenate([c for _, c in groups] + [jnp.zeros((pad, D_MODEL), F32)], axis=0)
    mod_all = _ada_mod(c_all, w_ada.astype(BF16), b_ada)

    layers = [_prep_layer(w_in[l], qn_a[l], kn_a[l], qn_b[l], kn_b[l], qn_c[l], kn_c[l], w_oa[l], w_ob[l], w_oc[l])
              for l in range(DEPTH)]
    w_bg16, w_out16 = w_bg.astype(BF16), w_out.astype(BF16)

    outs = []
    row0 = 0
    for (x, c), n in zip(groups, rows):
        tabs = _tables(x.shape[1])
        for l in range(DEPTH):
            lam_rows = [p[l][None, :] for p in (lam_q1, lam_k1, lam_q2, lam_k2)]
            x = _layer(x, mod_all[l, row0:row0 + n], l, layers[l], tabs, norm_g[l][None, :], lam_rows,
                       subln_c[l][None, :], w_bg16[l], b_bg[l][None, :], w_out16[l])
        outs.append(x)
        row0 += n
    return tuple(outs)
```

```python
import functools
import math

import jax
import jax.numpy as jnp
import numpy as np
from jax import lax
from jax.experimental import pallas as pl
from jax.experimental.pallas import tpu as pltpu

D_MODEL = 1024
DEPTH = 2
GRID_W = 64
EPS = 1e-6
NEG = -1e30
ROPE_THETA = 500000.0
AXIAL_THETA = 10000.0

A_GROUPS = ((128, 1), (512, 4), (2048, 16))
A_HEADS = 4
A_HEAD_DIM = 96
A_ROT = A_HEAD_DIM // 4
A_BAND = 64
N_A = len(A_GROUPS) * A_HEADS

B_HEADS = 6
B_KV_HEADS = 2
B_HEAD_DIM = 64
C_HEADS = 4
C_HEAD_DIM = 64
C_ROT = C_HEAD_DIM // 4

A_QKV = N_A * A_HEAD_DIM
A_WIDTH = A_HEADS * A_HEAD_DIM
B_WIDTH = B_HEADS * B_HEAD_DIM
B_KV = B_KV_HEADS * B_HEAD_DIM
C_WIDTH = C_HEADS * 2 * C_HEAD_DIM
IN_SIZES = (A_QKV, A_QKV, A_QKV, A_WIDTH, B_WIDTH, B_KV, B_KV, B_WIDTH, C_WIDTH, C_WIDTH, C_WIDTH, C_WIDTH)

LANE = 128
HALF = LANE // 2
ROT_SHIFT = LANE // 2
A_SLAB = A_HEADS * LANE
BF16_SUBLANES = 16
ONES_ROWS = BF16_SUBLANES
FLASH_TK = 512
COL_GROUP = 256
SCORE_SLOTS = 2
FLASH_UNROLL = 4
PROJ_TM = 512
PROJ_A_TM = 256
FINAL_TM = 512
LOG2E = math.log2(math.e)
LN2 = math.log(2.0)
VMEM_LIMIT = 48 * 1024 * 1024

BF16 = jnp.bfloat16
F32 = jnp.float32


def _cparams(n_axes):
    return pltpu.CompilerParams(dimension_semantics=("arbitrary",) * n_axes, vmem_limit_bytes=VMEM_LIMIT)


def _dot(a, b):
    return jnp.dot(a, b, preferred_element_type=F32)


def _dot_nt(a, b):
    return lax.dot_general(a, b, (((1,), (1,)), ((), ())), preferred_element_type=F32)


def _silu(x):
    return x * jax.nn.sigmoid(x)


def _ada_kernel(c_ref, w_ref, b_ref, o_ref):
    c = c_ref[...]
    o_ref[...] = _dot(_silu(c).astype(BF16), w_ref[...]) + b_ref[...]


def _ada_mod(c_all, w_ada, b_ada):
    rows = c_all.shape[0]
    return pl.pallas_call(
        _ada_kernel,
        grid=(DEPTH, 3),
        in_specs=[
            pl.BlockSpec((rows, D_MODEL), lambda l, n: (0, 0)),
            pl.BlockSpec((None, D_MODEL, D_MODEL), lambda l, n: (l, 0, n)),
            pl.BlockSpec((None, 1, D_MODEL), lambda l, n: (l, 0, n)),
        ],
        out_specs=pl.BlockSpec((None, rows, D_MODEL), lambda l, n: (l, 0, n)),
        out_shape=jax.ShapeDtypeStruct((DEPTH, rows, 3 * D_MODEL), F32),
        compiler_params=_cparams(2),
        name="ada_mod",
    )(c_all, w_ada, b_ada.reshape(DEPTH, 1, 3 * D_MODEL))


def _modulated_norm(x, g, scale, shift):
    ms = jnp.mean(x * x, axis=-1, keepdims=True)
    h = x * lax.rsqrt(ms + EPS) * g
    return h * (1.0 + scale) + shift


def _first_head(index):
    return (index & (HALF - 1)) < HALF // 2


def _head_norm_rope(u, gain, cos, sin, two_heads, head_dim):
    sq = u * u
    if two_heads:
        first = _first_head(lax.broadcasted_iota(jnp.int32, (1, LANE), 1))
        s_first = jnp.sum(jnp.where(first, sq, 0.0), axis=-1, keepdims=True)
        s_all = jnp.sum(sq, axis=-1, keepdims=True)
        ms = jnp.where(first, s_first, s_all - s_first) * (1.0 / head_dim)
    else:
        ms = jnp.sum(sq, axis=-1, keepdims=True) * (1.0 / head_dim)
    y = u * lax.rsqrt(ms + EPS) * gain
    return y * cos + pltpu.roll(y, ROT_SHIFT, 1) * sin


def _proj_kernel(x_ref, shift_ref, scale_ref, g_ref, w_ref, gain_ref, cos_ref, sin_ref, *refs,
                 outs, two_heads, head_dim, q_scale):
    out_refs, stage_ref = refs[:len(outs)], refs[len(outs):]
    h = _modulated_norm(x_ref[...], g_ref[...], scale_ref[...], shift_ref[...]).astype(BF16)
    cos, sin = cos_ref[...], sin_ref[...]
    tm = x_ref.shape[0]
    col = 0
    for o_ref, (width, kind, dil) in zip(out_refs, outs):
        u = _dot(h, w_ref[:, col:col + width])
        if kind == "vT":
            dv = LANE // dil
            for s in range(width // LANE):
                ut = u[:, s * LANE:(s + 1) * LANE].T.astype(BF16)
                for hd in range(dil):
                    base = (s * dil + hd) * (dv + ONES_ROWS)
                    o_ref[base:base + dv, :] = ut[hd * dv:(hd + 1) * dv]
                    o_ref[base + dv:base + dv + ONES_ROWS, :] = jnp.ones((ONES_ROWS, tm), BF16)
        else:
            if kind != "v":
                slabs = []
                for s in range(width // LANE):
                    lo = s * LANE
                    y = _head_norm_rope(u[:, lo:lo + LANE], gain_ref[:, col + lo:col + lo + LANE], cos, sin,
                                        two_heads, head_dim)
                    slabs.append(y * q_scale if kind in ("q", "qT") and q_scale != 1.0 else y)
                u = jnp.concatenate(slabs, axis=-1)
            if kind == "qT":
                for s in range(width // LANE):
                    o_ref[s * LANE:(s + 1) * LANE, :] = u[:, s * LANE:(s + 1) * LANE].T.astype(BF16)
            elif dil == 1:
                o_ref[...] = u.astype(BF16)
            else:
                stage = stage_ref[0]
                for s in range(width // LANE):
                    stage[s] = u[:, s * LANE:(s + 1) * LANE]
                rows = lambda r: pl.ds(r, tm // dil, stride=dil)
                for r in range(dil):
                    for s in range(width // LANE):
                        lo = r * width + s * LANE
                        o_ref[:, lo:lo + LANE] = stage[s, rows(r), :].astype(BF16)
        col += width


def _vt_rows(v_heads):
    return LANE + v_heads * ONES_ROWS


def _proj_out(kind, width, dil, bn, s, tm):
    if kind == "qT":
        return (bn, width, s), pl.BlockSpec((None, width, tm), lambda b, i: (b, 0, i))
    if kind == "vT":
        per = FLASH_TK // tm
        rows = width // LANE * _vt_rows(dil)
        return ((bn, s // FLASH_TK, rows, FLASH_TK),
                pl.BlockSpec((None, None, rows, tm), lambda b, i: (b, i // per, 0, i % per)))
    return (bn, s // dil, dil * width), pl.BlockSpec((None, tm // dil, dil * width), lambda b, i: (b, i, 0))


def _project(x, shift, scale, norm_g, w, gain, tables, outs, *, two_heads, head_dim, q_scale, tm, name):
    bn, s, _ = x.shape
    wcols = w.shape[1]
    kern = functools.partial(_proj_kernel, outs=outs, two_heads=two_heads, head_dim=head_dim, q_scale=q_scale)
    vec = pl.BlockSpec((None, 1, D_MODEL), lambda b, i: (b, 0, 0))
    tab = pl.BlockSpec((tm, LANE), lambda b, i: (i, 0))
    shapes, specs = zip(*[_proj_out(kind, width, dil, bn, s, tm) for width, kind, dil in outs])
    n_slabs = max(width for width, _, _ in outs) // LANE
    dilated = any(d > 1 and kind != "vT" for _, kind, d in outs)
    stage = [pltpu.VMEM((n_slabs, tm, LANE), F32)] if dilated else []
    return pl.pallas_call(
        kern,
        grid=(bn, s // tm),
        in_specs=[
            pl.BlockSpec((None, tm, D_MODEL), lambda b, i: (b, i, 0)),
            vec, vec,
            pl.BlockSpec((1, D_MODEL), lambda b, i: (0, 0)),
            pl.BlockSpec((D_MODEL, wcols), lambda b, i: (0, 0)),
            pl.BlockSpec((1, wcols), lambda b, i: (0, 0)),
            tab, tab,
        ],
        out_specs=list(specs),
        out_shape=[jax.ShapeDtypeStruct(shape, BF16) for shape in shapes],
        scratch_shapes=stage,
        compiler_params=_cparams(2),
        name=name,
    )(x, shift, scale, norm_g, w, gain, *tables)


def _band_attn_kernel(q_ref, kp_ref, kc_ref, kn_ref, vp_ref, vc_ref, vn_ref, bias_ref, o_ref, *, tb, n_blocks):
    i = pl.program_id(2)
    nk = tb + 2 * A_BAND
    colk = lax.broadcasted_iota(jnp.int32, (1, nk), 1)
    first_col = jnp.where(i == 0, A_BAND, 0)
    end_col = jnp.where(i == n_blocks - 1, tb + A_BAND, nk)
    bias = bias_ref[...] + jnp.where((colk < first_col) | (colk >= end_col), NEG, 0.0)
    lane = lax.broadcasted_iota(jnp.int32, (1, LANE), 1)
    heads = [slice(j * LANE, (j + 1) * LANE) for j in range(A_HEADS)]
    band = lambda p_ref, c_ref, n_ref, sl: jnp.concatenate([p_ref[:, sl], c_ref[:, sl], n_ref[:, sl]], axis=0)
    scores = [_dot_nt(q_ref[:, sl], band(kp_ref, kc_ref, kn_ref, sl)) for sl in heads]
    stats = []
    for s in scores:
        s = s + bias
        m = jnp.max(s, axis=-1, keepdims=True)
        p = jnp.exp2(s - m)
        stats.append((m, jnp.sum(p, axis=-1, keepdims=True), p.astype(BF16)))
    for sl, (m, l, p) in zip(heads, stats):
        o = _dot(p, band(vp_ref, vc_ref, vn_ref, sl)) / l
        o_ref[:, sl] = jnp.where(lane >= A_HEAD_DIM, m * LN2 + jnp.log(l), o)


def _band_attention(q, k, v, dilation):
    bn, length, _ = q.shape
    tb = min(256, length)
    nb = length // tb
    per = tb // A_BAND
    cur = pl.BlockSpec((None, tb, A_SLAB), lambda b, r, i: (b, i, r))
    prv = pl.BlockSpec((None, A_BAND, A_SLAB), lambda b, r, i: (b, jnp.maximum(i * per - 1, 0), r))
    nxt = pl.BlockSpec((None, A_BAND, A_SLAB), lambda b, r, i: (b, jnp.minimum((i + 1) * per, nb * per - 1), r))
    nk = tb + 2 * A_BAND
    dist = np.arange(nk)[None, :] - A_BAND - np.arange(tb)[:, None]
    bias = np.where(np.abs(dist) <= A_BAND, 0.0, NEG).astype(np.float32)
    kern = functools.partial(_band_attn_kernel, tb=tb, n_blocks=nb)
    return pl.pallas_call(
        kern,
        grid=(bn, dilation, nb),
        in_specs=[cur, prv, cur, nxt, prv, cur, nxt, pl.BlockSpec((tb, nk), lambda b, r, i: (0, 0))],
        out_specs=cur,
        out_shape=jax.ShapeDtypeStruct((bn, length, dilation * A_SLAB), F32),
        compiler_params=_cparams(3),
        name=f"band_attn_d{dilation}",
    )(q, k, k, k, v, v, v, bias)


def _stacked_flash(qt_ref, k_ref, vt_ref, acc_sc, s_sc, *, tq, n_chunks, v_heads):
    dv = LANE // v_heads
    v_rows = [slice(hd * (dv + ONES_ROWS), (hd + 1) * (dv + ONES_ROWS)) for hd in range(v_heads)]
    first = _first_head(lax.broadcasted_iota(jnp.int32, (LANE, 1), 0))
    qt = qt_ref[...]
    zero = jnp.zeros_like(qt)
    q2t = jnp.concatenate([jnp.where(first, qt, zero), jnp.where(first, zero, qt)], axis=1)
    acc_sc[...] = jnp.zeros(acc_sc.shape, F32)

    width = min(COL_GROUP, tq)
    groups = [slice(lo, lo + width) for lo in range(0, 2 * tq, width)]

    def scores(c, slot, cols):
        start = pl.multiple_of(c * FLASH_TK, FLASH_TK)
        s_sc[slot, :, cols] = _dot(k_ref[pl.ds(start, FLASH_TK), :], q2t[:, cols])

    def step(c, slot, m_prev, c_next=None):
        m_next = jnp.maximum(m_prev, jnp.max(s_sc[slot], axis=0, keepdims=True))
        vc = vt_ref[c]
        alpha = jnp.exp2(m_prev - m_next)
        for cols in groups:
            if c_next is not None:
                scores(c_next, 1 - slot, cols)
            pt = jnp.exp2(s_sc[slot, :, cols] - m_next[:, cols]).astype(BF16)
            head = cols.start // tq
            acc_sc[:, cols] = alpha[:, cols] * acc_sc[:, cols] + _dot(vc[v_rows[head % v_heads]], pt)
        return m_next

    unroll = min(FLASH_UNROLL, n_chunks)

    def group(j, m):
        for u in range(unroll):
            c = unroll * j + u
            m = step(c, u % SCORE_SLOTS, m, c + 1)
        return m

    for cols in groups:
        scores(0, 0, cols)
    m = lax.fori_loop(0, n_chunks // unroll - 1, group, jnp.full((1, 2 * tq), -jnp.inf, F32))
    for c in range(n_chunks - unroll, n_chunks):
        m = step(c, c % SCORE_SLOTS, m, c + 1 if c + 1 < n_chunks else None)
    acc = acc_sc[...]
    return acc[:dv] / acc[dv:dv + 1]


def _gqa_kernel(qt_ref, k_ref, vt_ref, o_ref, acc_sc, s_sc, *, tq, n_chunks):
    o = _stacked_flash(qt_ref, k_ref, vt_ref, acc_sc, s_sc, tq=tq, n_chunks=n_chunks, v_heads=B_KV_HEADS)
    o_ref[...] = jnp.concatenate([o[:, :tq], o[:, tq:]], axis=0).T


def _diff_kernel(qt_ref, k_ref, vt_ref, lq1_ref, lk1_ref, lq2_ref, lk2_ref, sub_ref, o_ref, acc_sc,
                 s_sc, *, tq, n_chunks, lam_init):
    o = _stacked_flash(qt_ref, k_ref, vt_ref, acc_sc, s_sc, tq=tq, n_chunks=n_chunks, v_heads=1)
    lam = (jnp.exp(jnp.sum(lq1_ref[...] * lk1_ref[...], axis=-1, keepdims=True))
           - jnp.exp(jnp.sum(lq2_ref[...] * lk2_ref[...], axis=-1, keepdims=True)) + lam_init)
    oc = (o[:, :tq] - lam * o[:, tq:]).T
    ms = jnp.mean(oc * oc, axis=-1, keepdims=True)
    o_ref[...] = oc * lax.rsqrt(ms + EPS) * sub_ref[...] * (1.0 - lam_init)


def _flash_tq(s):
    return min(2048, s)


def _flash_scratch(tq, v_heads):
    return [pltpu.VMEM((LANE // v_heads + ONES_ROWS, 2 * tq), F32),
            pltpu.VMEM((SCORE_SLOTS, FLASH_TK, 2 * tq), F32)]


def _gqa_attention(qt, k, vt):
    bn, _, s = qt.shape
    tq, n_chunks = _flash_tq(s), s // FLASH_TK
    kern = functools.partial(_gqa_kernel, tq=tq, n_chunks=n_chunks)
    return pl.pallas_call(
        kern,
        grid=(bn, B_HEADS // 2, s // tq),
        in_specs=[pl.BlockSpec((None, LANE, tq), lambda b, p, i: (b, p, i)),
                  pl.BlockSpec((None, s, LANE), lambda b, p, i: (b, 0, 0)),
                  pl.BlockSpec((None, n_chunks, _vt_rows(B_KV_HEADS), FLASH_TK), lambda b, p, i: (b, 0, 0, 0))],
        out_specs=pl.BlockSpec((None, tq, LANE), lambda b, p, i: (b, i, p)),
        out_shape=jax.ShapeDtypeStruct((bn, s, B_WIDTH), F32),
        scratch_shapes=_flash_scratch(tq, B_KV_HEADS),
        compiler_params=_cparams(3),
        name="gqa_attn",
    )(qt, k, vt)


def _diff_attention(qt, k, vt, lam_rows, subln, lam_init):
    bn, _, s = qt.shape
    tq, n_chunks = _flash_tq(s), s // FLASH_TK
    kern = functools.partial(_diff_kernel, tq=tq, n_chunks=n_chunks, lam_init=lam_init)
    row = lambda n: pl.BlockSpec((1, n), lambda b, h, i: (0, 0))
    return pl.pallas_call(
        kern,
        grid=(bn, C_HEADS, s // tq),
        in_specs=[pl.BlockSpec((None, LANE, tq), lambda b, h, i: (b, h, i)),
                  pl.BlockSpec((None, s, LANE), lambda b, h, i: (b, 0, h)),
                  pl.BlockSpec((None, n_chunks, _vt_rows(1), FLASH_TK), lambda b, h, i: (b, 0, h, 0)),
                  row(C_HEAD_DIM), row(C_HEAD_DIM), row(C_HEAD_DIM), row(C_HEAD_DIM), row(LANE)],
        out_specs=pl.BlockSpec((None, tq, LANE), lambda b, h, i: (b, i, h)),
        out_shape=jax.ShapeDtypeStruct((bn, s, C_WIDTH), F32),
        scratch_shapes=_flash_scratch(tq, 1),
        compiler_params=_cparams(3),
        name="diff_attn",
    )(qt, k, vt, *lam_rows, subln)


def _final_kernel(x_ref, shift_ref, scale_ref, gate_ref, g_ref, oa0_ref, oa1_ref, oa2_ref, yb_ref, yc_ref,
                  wz_ref, wbg_ref, bbg_ref, woa_ref, wob_ref, woc_ref, wout_ref, o_ref, *stage_refs):
    x = x_ref[...]
    tm = x.shape[0]
    h = _modulated_norm(x, g_ref[...], scale_ref[...], shift_ref[...]).astype(BF16)
    lane = lax.broadcasted_iota(jnp.int32, (1, LANE), 1)

    for oa_ref, st_ref, (_, dil) in zip((oa1_ref, oa2_ref), stage_refs, A_GROUPS[1:]):
        for r in range(dil):
            for j in range(A_HEADS):
                lo = r * A_SLAB + j * LANE
                st_ref[j, pl.ds(r, tm // dil, stride=dil), :] = oa_ref[:, lo:lo + LANE]

    ya = []
    for j in range(A_HEADS):
        slabs = [oa0_ref[:, j * LANE:(j + 1) * LANE]] + [st_ref[j] for st_ref in stage_refs]
        lses = [t[:, A_HEAD_DIM:A_HEAD_DIM + 1] for t in slabs]
        top = jnp.maximum(jnp.maximum(lses[0], lses[1]), lses[2])
        ws = [jnp.exp(t - top) for t in lses]
        mix = (ws[0] * slabs[0] + ws[1] * slabs[1] + ws[2] * slabs[2]) / (ws[0] + ws[1] + ws[2])
        ya.append(jnp.where(lane < A_HEAD_DIM, mix, 0.0))
    ya = jnp.concatenate(ya, axis=-1)

    def branch(y, z_lo, z_hi, wo_ref, g_lo):
        z = _dot(h, wz_ref[:, z_lo:z_hi])
        p = _dot((y * _silu(z)).astype(BF16), wo_ref[...])
        gate = jax.nn.sigmoid(_dot(h, wbg_ref[:, g_lo:g_lo + D_MODEL]) + bbg_ref[:, g_lo:g_lo + D_MODEL])
        return gate * p

    za_hi = A_SLAB
    zb_hi = za_hi + B_WIDTH
    zc_hi = zb_hi + C_WIDTH
    merged = (branch(ya, 0, za_hi, woa_ref, 0)
              + branch(yb_ref[...], za_hi, zb_hi, wob_ref, D_MODEL)
              + branch(yc_ref[...], zb_hi, zc_hi, woc_ref, 2 * D_MODEL))
    out = _dot(merged.astype(BF16), wout_ref[...])
    o_ref[...] = x + gate_ref[...] * out


def _final(x, shift, scale, gate, norm_g, oa, yb, yc, wz, wbg, bbg, woa, wob, woc, wout, *, tm):
    bn, s, _ = x.shape
    tok = lambda w: pl.BlockSpec((None, tm, w), lambda b, i: (b, i, 0))
    vec = pl.BlockSpec((None, 1, D_MODEL), lambda b, i: (b, 0, 0))
    full = lambda a: pl.BlockSpec(a.shape, lambda b, i: (0, 0), pipeline_mode=pl.Buffered(1))
    band = [pl.BlockSpec((None, tm // dil, dil * A_SLAB), lambda b, i: (b, i, 0)) for _, dil in A_GROUPS]
    return pl.pallas_call(
        _final_kernel,
        grid=(bn, s // tm),
        in_specs=[tok(D_MODEL), vec, vec, vec, full(norm_g), *band,
                  tok(B_WIDTH), tok(C_WIDTH), full(wz), full(wbg), full(bbg), full(woa), full(wob), full(woc),
                  full(wout)],
        out_specs=tok(D_MODEL),
        out_shape=jax.ShapeDtypeStruct((bn, s, D_MODEL), F32),
        scratch_shapes=[pltpu.VMEM((A_HEADS, tm, LANE), F32) for _ in A_GROUPS[1:]],
        compiler_params=_cparams(2),
        name="final",
    )(x, shift, scale, gate, norm_g, *oa, yb, yc, wz, wbg, bbg, woa, wob, woc, wout)


B_Q_ORDER = (0, 3, 1, 4, 2, 5)


def _slab_layout(head_dim, n_heads, rot_groups):
    src = -np.ones((LANE,), np.int32)
    group = -np.ones((LANE,), np.int32)
    freq = np.zeros((LANE,), np.int32)
    sign = np.zeros((LANE,), np.float32)
    width = ROT_SHIFT // n_heads
    for hd in range(n_heads):
        low = [(start + i, g, i, -1.0) for g, (start, half) in enumerate(rot_groups) for i in range(half)]
        high = [(start + half + i, g, i, 1.0) for g, (start, half) in enumerate(rot_groups) for i in range(half)]
        rotary = {d for d, _, _, _ in low + high}
        rest = [(d, -1, 0, 0.0) for d in range(head_dim) if d not in rotary]
        n_low_rest = width - len(low)
        low, high = low + rest[:n_low_rest], high + rest[n_low_rest:]
        assert len(low) == width and len(high) <= width
        for base, items in ((hd * width, low), (hd * width + ROT_SHIFT, high)):
            for lane, (d, g, i, sg) in enumerate(items, start=base):
                src[lane], group[lane], freq[lane], sign[lane] = hd * head_dim + d, g, i, sg
    return src, group, freq, sign


LAYOUT_A = _slab_layout(A_HEAD_DIM, 1, ((0, A_ROT // 2),))
LAYOUT_B = _slab_layout(B_HEAD_DIM, 2, ((0, B_HEAD_DIM // 4), (B_HEAD_DIM // 2, B_HEAD_DIM // 4)))
LAYOUT_C = _slab_layout(C_HEAD_DIM, 2, ((0, C_ROT // 2),))


def _to_slabs(w, layout, dims_per_slab):
    src = layout[0]
    n_slabs = w.shape[-1] // dims_per_slab
    idx = np.concatenate([np.maximum(src, 0) + s * dims_per_slab for s in range(n_slabs)])
    keep = np.tile(src >= 0, n_slabs)
    return jnp.where(keep, jnp.take(w, idx, axis=-1), 0.0)


def _pad_heads(w, n_heads):
    lead = w.shape[:-1]
    w = w.reshape(*lead, n_heads, A_HEAD_DIM)
    w = jnp.pad(w, [(0, 0)] * len(lead) + [(0, 0), (0, LANE - A_HEAD_DIM)])
    return w.reshape(*lead, n_heads * LANE)


def _permute_heads(w, order, dim):
    lead = w.shape[:-1]
    w = w.reshape(*lead, len(order), dim)
    return w[..., jnp.array(order), :].reshape(*lead, len(order) * dim)


def _prep_layer(w_in, qn_a, kn_a, qn_b, kn_b, qn_c, kn_c, w_oa, w_ob, w_oc):
    offs = [0]
    for n in IN_SIZES:
        offs.append(offs[-1] + n)
    qa, ka, va, za, qb, kb, vb, zb, qc, kc, vc, zc = [w_in[:, offs[i]:offs[i + 1]] for i in range(len(IN_SIZES))]

    slab_a = functools.partial(_to_slabs, layout=LAYOUT_A, dims_per_slab=A_HEAD_DIM)
    slab_b = functools.partial(_to_slabs, layout=LAYOUT_B, dims_per_slab=2 * B_HEAD_DIM)
    slab_c = functools.partial(_to_slabs, layout=LAYOUT_C, dims_per_slab=2 * C_HEAD_DIM)
    qb = _permute_heads(qb, B_Q_ORDER, B_HEAD_DIM)

    w_a = jnp.concatenate([slab_a(qa), slab_a(ka), _pad_heads(va, N_A)], axis=-1).astype(BF16)
    gain_a = jnp.concatenate([slab_a(jnp.tile(qn_a, N_A)), slab_a(jnp.tile(kn_a, N_A)),
                              jnp.zeros((N_A * LANE,), F32)])[None, :]

    w_b = jnp.concatenate([slab_b(qb), slab_b(kb), vb], axis=-1).astype(BF16)
    gain_b = jnp.concatenate([slab_b(jnp.tile(qn_b, B_HEADS)), slab_b(jnp.tile(kn_b, B_KV_HEADS)),
                              jnp.zeros((B_KV,), F32)])[None, :]

    w_c = jnp.concatenate([slab_c(qc), slab_c(kc), vc], axis=-1).astype(BF16)
    gain_c = jnp.concatenate([slab_c(jnp.tile(qn_c, 2 * C_HEADS)), slab_c(jnp.tile(kn_c, 2 * C_HEADS)),
                              jnp.zeros((C_WIDTH,), F32)])[None, :]

    w_z = jnp.concatenate([_pad_heads(za, A_HEADS), _permute_heads(zb, B_Q_ORDER, B_HEAD_DIM), zc],
                          axis=-1).astype(BF16)
    w_oa_p = _pad_heads(w_oa.T, A_HEADS).T.astype(BF16)
    w_ob_p = _permute_heads(w_ob.T, B_Q_ORDER, B_HEAD_DIM).T.astype(BF16)
    return dict(w_a=w_a, gain_a=gain_a, w_b=w_b, gain_b=gain_b, w_c=w_c, gain_c=gain_c, w_z=w_z,
                w_oa=w_oa_p, w_ob=w_ob_p, w_oc=w_oc.astype(BF16))


def _inv_freq(dim, theta):
    return theta ** (-jnp.arange(0, dim, 2, dtype=F32) / dim)


def _rope_tables(layout, inv_freqs, streams):
    _, group, freq, sign = layout
    ang = 0.0
    for g, (inv, pos) in enumerate(zip(inv_freqs, streams)):
        ang = ang + pos[:, None] * (inv[freq] * (group == g))[None, :]
    return jnp.cos(ang), jnp.sin(ang) * sign[None, :]


def _tables(s):
    pos = jnp.arange(s)
    tok, row, col = [t.astype(F32) for t in (pos, pos // GRID_W, pos % GRID_W)]
    inv_b = _inv_freq(B_HEAD_DIM // 2, AXIAL_THETA)
    tab_a = _rope_tables(LAYOUT_A, [_inv_freq(A_ROT, ROPE_THETA)], [tok])
    tab_b = _rope_tables(LAYOUT_B, [inv_b, inv_b], [row, col])
    tab_c = _rope_tables(LAYOUT_C, [_inv_freq(C_ROT, ROPE_THETA)], [tok])
    return tab_a, tab_b, tab_c


def _layer(x, mod, l, lp, tabs, norm_g, lam_rows, subln, w_bg, b_bg, w_out):
    bn, s, _ = x.shape
    shift, scale, gate = [m.reshape(bn, 1, D_MODEL) for m in jnp.split(mod, 3, axis=-1)]
    tab_a, tab_b, tab_c = tabs
    lam_init = 0.8 - 0.6 * math.exp(-0.3 * l)

    a_outs = tuple((A_SLAB, kind, dil) for kind in ("q", "k", "v") for _, dil in A_GROUPS)
    pa = _project(x, shift, scale, norm_g, lp["w_a"], lp["gain_a"], tab_a, a_outs, two_heads=False,
                  head_dim=A_HEAD_DIM, q_scale=A_HEAD_DIM ** -0.5 * LOG2E, tm=min(PROJ_A_TM, s), name="proj_a")
    n_g = len(A_GROUPS)
    oa = [_band_attention(pa[g], pa[n_g + g], pa[2 * n_g + g], dil) for g, (_, dil) in enumerate(A_GROUPS)]

    qb, kb, vb = _project(x, shift, scale, norm_g, lp["w_b"], lp["gain_b"], tab_b,
                          ((B_WIDTH, "qT", 1), (B_KV, "k", 1), (B_KV, "vT", B_KV_HEADS)), two_heads=True,
                          head_dim=B_HEAD_DIM, q_scale=B_HEAD_DIM ** -0.5 * LOG2E, tm=min(PROJ_TM, s), name="proj_b")
    yb = _gqa_attention(qb, kb, vb)

    qc, kc, vc = _project(x, shift, scale, norm_g, lp["w_c"], lp["gain_c"], tab_c,
                          ((C_WIDTH, "qT", 1), (C_WIDTH, "k", 1), (C_WIDTH, "vT", 1)), two_heads=True,
                          head_dim=C_HEAD_DIM, q_scale=C_HEAD_DIM ** -0.5 * LOG2E, tm=min(PROJ_TM, s), name="proj_c")
    yc = _diff_attention(qc, kc, vc, lam_rows, subln, lam_init)

    return _final(x, shift, scale, gate, norm_g, oa, yb, yc, lp["w_z"], w_bg, b_bg, lp["w_oa"], lp["w_ob"],
                  lp["w_oc"], w_out, tm=min(FINAL_TM, s))


def kernel(x_prompt, x_sample, c_prompt, c_sample, norm_g, w_ada, b_ada, w_in, qn_a, kn_a, qn_b, kn_b, qn_c, kn_c,
           lam_q1, lam_k1, lam_q2, lam_k2, subln_c, w_oa, w_ob, w_oc, w_bg, b_bg, w_out):
    groups = ((x_prompt, c_prompt), (x_sample, c_sample))
    rows = [c.shape[0] for _, c in groups]
    pad = -sum(rows) % 8
    c_all = jnp.concatenate([c for _, c in groups] + [jnp.zeros((pad, D_MODEL), F32)], axis=0)
    mod_all = _ada_mod(c_all, w_ada.astype(BF16), b_ada)

    layers = [_prep_layer(w_in[l], qn_a[l], kn_a[l], qn_b[l], kn_b[l], qn_c[l], kn_c[l], w_oa[l], w_ob[l], w_oc[l])
              for l in range(DEPTH)]
    w_bg16, w_out16 = w_bg.astype(BF16), w_out.astype(BF16)

    outs = []
    row0 = 0
    for (x, c), n in zip(groups, rows):
        tabs = _tables(x.shape[1])
        for l in range(DEPTH):
            lam_rows = [p[l][None, :] for p in (lam_q1, lam_k1, lam_q2, lam_k2)]
            x = _layer(x, mod_all[l, row0:row0 + n], l, layers[l], tabs, norm_g[l][None, :], lam_rows,
                       subln_c[l][None, :], w_bg16[l], b_bg[l][None, :], w_out16[l])
        outs.append(x)
        row0 += n
    return tuple(outs)
```

```python
import functools
import math

import jax
import jax.numpy as jnp
import numpy as np
from jax import lax
from jax.experimental import pallas as pl
from jax.experimental.pallas import tpu as pltpu

D_MODEL = 1024
DEPTH = 2
GRID_W = 64
EPS = 1e-6
NEG = -1e30
ROPE_THETA = 500000.0
AXIAL_THETA = 10000.0

A_GROUPS = ((128, 1), (512, 4), (2048, 16))
A_HEADS = 4
A_HEAD_DIM = 96
A_ROT = A_HEAD_DIM // 4
A_BAND = 64
N_A = len(A_GROUPS) * A_HEADS

B_HEADS = 6
B_KV_HEADS = 2
B_HEAD_DIM = 64
C_HEADS = 4
C_HEAD_DIM = 64
C_ROT = C_HEAD_DIM // 4

A_QKV = N_A * A_HEAD_DIM
A_WIDTH = A_HEADS * A_HEAD_DIM
B_WIDTH = B_HEADS * B_HEAD_DIM
B_KV = B_KV_HEADS * B_HEAD_DIM
C_WIDTH = C_HEADS * 2 * C_HEAD_DIM
IN_SIZES = (A_QKV, A_QKV, A_QKV, A_WIDTH, B_WIDTH, B_KV, B_KV, B_WIDTH, C_WIDTH, C_WIDTH, C_WIDTH, C_WIDTH)

LANE = 128
HALF = LANE // 2
ROT_SHIFT = LANE // 2
A_SLAB = A_HEADS * LANE
BF16_SUBLANES = 16
ONES_ROWS = BF16_SUBLANES
FLASH_TK = 512
COL_GROUP = 256
SCORE_SLOTS = 2
FLASH_UNROLL = 4
PROJ_TM = 512
PROJ_A_TM = 256
FINAL_TM = 512
LOG2E = math.log2(math.e)
LN2 = math.log(2.0)
VMEM_LIMIT = 48 * 1024 * 1024

BF16 = jnp.bfloat16
F32 = jnp.float32


def _cparams(n_axes):
    return pltpu.CompilerParams(dimension_semantics=("arbitrary",) * n_axes, vmem_limit_bytes=VMEM_LIMIT)


def _dot(a, b):
    return jnp.dot(a, b, preferred_element_type=F32)


def _dot_nt(a, b):
    return lax.dot_general(a, b, (((1,), (1,)), ((), ())), preferred_element_type=F32)


def _silu(x):
    return x * jax.nn.sigmoid(x)


def _ada_kernel(c_ref, w_ref, b_ref, o_ref):
    c = c_ref[...]
    o_ref[...] = _dot(_silu(c).astype(BF16), w_ref[...]) + b_ref[...]


def _ada_mod(c_all, w_ada, b_ada):
    rows = c_all.shape[0]
    return pl.pallas_call(
        _ada_kernel,
        grid=(DEPTH, 3),
        in_specs=[
            pl.BlockSpec((rows, D_MODEL), lambda l, n: (0, 0)),
            pl.BlockSpec((None, D_MODEL, D_MODEL), lambda l, n: (l, 0, n)),
            pl.BlockSpec((None, 1, D_MODEL), lambda l, n: (l, 0, n)),
        ],
        out_specs=pl.BlockSpec((None, rows, D_MODEL), lambda l, n: (l, 0, n)),
        out_shape=jax.ShapeDtypeStruct((DEPTH, rows, 3 * D_MODEL), F32),
        compiler_params=_cparams(2),
        name="ada_mod",
    )(c_all, w_ada, b_ada.reshape(DEPTH, 1, 3 * D_MODEL))


def _modulated_norm(x, g, scale, shift):
    ms = jnp.mean(x * x, axis=-1, keepdims=True)
    h = x * lax.rsqrt(ms + EPS) * g
    return h * (1.0 + scale) + shift


def _first_head(index):
    return (index & (HALF - 1)) < HALF // 2


def _head_norm_rope(u, gain, cos, sin, two_heads, head_dim):
    sq = u * u
    if two_heads:
        first = _first_head(lax.broadcasted_iota(jnp.int32, (1, LANE), 1))
        s_first = jnp.sum(jnp.where(first, sq, 0.0), axis=-1, keepdims=True)
        s_all = jnp.sum(sq, axis=-1, keepdims=True)
        ms = jnp.where(first, s_first, s_all - s_first) * (1.0 / head_dim)
    else:
        ms = jnp.sum(sq, axis=-1, keepdims=True) * (1.0 / head_dim)
    y = u * lax.rsqrt(ms + EPS) * gain
    return y * cos + pltpu.roll(y, ROT_SHIFT, 1) * sin


def _proj_kernel(x_ref, shift_ref, scale_ref, g_ref, w_ref, gain_ref, *refs,
                 outs, n_tables, two_heads, head_dim, q_scale):
    tables = [(refs[2 * t][...], refs[2 * t + 1][...]) for t in range(n_tables)]
    refs = refs[2 * n_tables:]
    out_refs, stage_ref = refs[:len(outs)], refs[len(outs):]
    h = _modulated_norm(x_ref[...], g_ref[...], scale_ref[...], shift_ref[...]).astype(BF16)
    tm = x_ref.shape[0]
    col = 0
    for o_ref, (width, kind, dil, table) in zip(out_refs, outs):
        cos, sin = tables[table]
        u = _dot(h, w_ref[:, col:col + width])
        if kind == "vT":
            dv = LANE // dil
            for s in range(width // LANE):
                ut = u[:, s * LANE:(s + 1) * LANE].T.astype(BF16)
                for hd in range(dil):
                    base = (s * dil + hd) * (dv + ONES_ROWS)
                    o_ref[base:base + dv, :] = ut[hd * dv:(hd + 1) * dv]
                    o_ref[base + dv:base + dv + ONES_ROWS, :] = jnp.ones((ONES_ROWS, tm), BF16)
        else:
            if kind != "v":
                slabs = []
                for s in range(width // LANE):
                    lo = s * LANE
                    y = _head_norm_rope(u[:, lo:lo + LANE], gain_ref[:, col + lo:col + lo + LANE], cos, sin,
                                        two_heads, head_dim)
                    slabs.append(y * q_scale if kind in ("q", "qT") and q_scale != 1.0 else y)
                u = jnp.concatenate(slabs, axis=-1)
            if kind == "qT":
                for s in range(width // LANE):
                    o_ref[s * LANE:(s + 1) * LANE, :] = u[:, s * LANE:(s + 1) * LANE].T.astype(BF16)
            elif dil == 1:
                o_ref[...] = u.astype(BF16)
            else:
                stage = stage_ref[0]
                for s in range(width // LANE):
                    stage[s] = u[:, s * LANE:(s + 1) * LANE]
                rows = lambda r: pl.ds(r, tm // dil, stride=dil)
                for r in range(dil):
                    for s in range(width // LANE):
                        lo = r * width + s * LANE
                        o_ref[:, lo:lo + LANE] = stage[s, rows(r), :].astype(BF16)
        col += width


def _vt_rows(v_heads):
    return LANE + v_heads * ONES_ROWS


def _proj_out(kind, width, dil, bn, s, tm):
    if kind == "qT":
        return (bn, width, s), pl.BlockSpec((None, width, tm), lambda b, i: (b, 0, i))
    if kind == "vT":
        per = FLASH_TK // tm
        rows = width // LANE * _vt_rows(dil)
        return ((bn, s // FLASH_TK, rows, FLASH_TK),
                pl.BlockSpec((None, None, rows, tm), lambda b, i: (b, i // per, 0, i % per)))
    return (bn, s // dil, dil * width), pl.BlockSpec((None, tm // dil, dil * width), lambda b, i: (b, i, 0))


def _project(x, shift, scale, norm_g, w, gain, tables, outs, *, two_heads, head_dim, q_scale, tm, name):
    bn, s, _ = x.shape
    wcols = w.shape[1]
    kern = functools.partial(_proj_kernel, outs=outs, n_tables=len(tables) // 2, two_heads=two_heads,
                             head_dim=head_dim, q_scale=q_scale)
    vec = pl.BlockSpec((None, 1, D_MODEL), lambda b, i: (b, 0, 0))
    tab = pl.BlockSpec((tm, LANE), lambda b, i: (i, 0))
    shapes, specs = zip(*[_proj_out(kind, width, dil, bn, s, tm) for width, kind, dil, _ in outs])
    n_slabs = max(width for width, _, _, _ in outs) // LANE
    dilated = any(d > 1 and kind != "vT" for _, kind, d, _ in outs)
    stage = [pltpu.VMEM((n_slabs, tm, LANE), F32)] if dilated else []
    return pl.pallas_call(
        kern,
        grid=(bn, s // tm),
        in_specs=[
            pl.BlockSpec((None, tm, D_MODEL), lambda b, i: (b, i, 0)),
            vec, vec,
            pl.BlockSpec((1, D_MODEL), lambda b, i: (0, 0)),
            pl.BlockSpec((D_MODEL, wcols), lambda b, i: (0, 0)),
            pl.BlockSpec((1, wcols), lambda b, i: (0, 0)),
            *[tab] * len(tables),
        ],
        out_specs=list(specs),
        out_shape=[jax.ShapeDtypeStruct(shape, BF16) for shape in shapes],
        scratch_shapes=stage,
        compiler_params=_cparams(2),
        name=name,
    )(x, shift, scale, norm_g, w, gain, *tables)


def _band_attn_kernel(q_ref, kp_ref, kc_ref, kn_ref, vp_ref, vc_ref, vn_ref, bias_ref, o_ref, *, tb, n_blocks):
    i = pl.program_id(2)
    nk = tb + 2 * A_BAND
    colk = lax.broadcasted_iota(jnp.int32, (1, nk), 1)
    first_col = jnp.where(i == 0, A_BAND, 0)
    end_col = jnp.where(i == n_blocks - 1, tb + A_BAND, nk)
    bias = bias_ref[...] + jnp.where((colk < first_col) | (colk >= end_col), NEG, 0.0)
    lane = lax.broadcasted_iota(jnp.int32, (1, LANE), 1)
    heads = [slice(j * LANE, (j + 1) * LANE) for j in range(A_HEADS)]
    band = lambda p_ref, c_ref, n_ref, sl: jnp.concatenate([p_ref[:, sl], c_ref[:, sl], n_ref[:, sl]], axis=0)
    scores = [_dot_nt(q_ref[:, sl], band(kp_ref, kc_ref, kn_ref, sl)) for sl in heads]
    stats = []
    for s in scores:
        s = s + bias
        m = jnp.max(s, axis=-1, keepdims=True)
        p = jnp.exp2(s - m)
        stats.append((m, jnp.sum(p, axis=-1, keepdims=True), p.astype(BF16)))
    for sl, (m, l, p) in zip(heads, stats):
        o = _dot(p, band(vp_ref, vc_ref, vn_ref, sl)) / l
        o_ref[:, sl] = jnp.where(lane >= A_HEAD_DIM, m * LN2 + jnp.log(l), o)


def _band_attention(q, k, v, dilation):
    bn, length, _ = q.shape
    tb = min(256, length)
    nb = length // tb
    per = tb // A_BAND
    cur = pl.BlockSpec((None, tb, A_SLAB), lambda b, r, i: (b, i, r))
    prv = pl.BlockSpec((None, A_BAND, A_SLAB), lambda b, r, i: (b, jnp.maximum(i * per - 1, 0), r))
    nxt = pl.BlockSpec((None, A_BAND, A_SLAB), lambda b, r, i: (b, jnp.minimum((i + 1) * per, nb * per - 1), r))
    nk = tb + 2 * A_BAND
    dist = np.arange(nk)[None, :] - A_BAND - np.arange(tb)[:, None]
    bias = np.where(np.abs(dist) <= A_BAND, 0.0, NEG).astype(np.float32)
    kern = functools.partial(_band_attn_kernel, tb=tb, n_blocks=nb)
    return pl.pallas_call(
        kern,
        grid=(bn, dilation, nb),
        in_specs=[cur, prv, cur, nxt, prv, cur, nxt, pl.BlockSpec((tb, nk), lambda b, r, i: (0, 0))],
        out_specs=cur,
        out_shape=jax.ShapeDtypeStruct((bn, length, dilation * A_SLAB), F32),
        compiler_params=_cparams(3),
        name=f"band_attn_d{dilation}",
    )(q, k, k, k, v, v, v, bias)


def _stacked_flash(qt_ref, k_ref, vt_ref, acc_sc, s_sc, *, tq, n_chunks, v_heads):
    dv = LANE // v_heads
    v_rows = [slice(hd * (dv + ONES_ROWS), (hd + 1) * (dv + ONES_ROWS)) for hd in range(v_heads)]
    first = _first_head(lax.broadcasted_iota(jnp.int32, (LANE, 1), 0))
    qt = qt_ref[...]
    zero = jnp.zeros_like(qt)
    q2t = jnp.concatenate([jnp.where(first, qt, zero), jnp.where(first, zero, qt)], axis=1)
    acc_sc[...] = jnp.zeros(acc_sc.shape, F32)

    wide = v_heads == 1 and n_chunks > FLASH_UNROLL
    width = min(2 * COL_GROUP if wide else COL_GROUP, tq)
    halves = [slice(lo, lo + width) for lo in range(0, 2 * tq, width)]

    def scores(c, slot, cols):
        start = pl.multiple_of(c * FLASH_TK, FLASH_TK)
        s_sc[slot, :, cols] = _dot(k_ref[pl.ds(start, FLASH_TK), :], q2t[:, cols])

    def step(c, slot, m_prev, c_next=None):
        m_next = jnp.maximum(m_prev, jnp.max(s_sc[slot], axis=0, keepdims=True))
        vc = vt_ref[c]
        alpha = jnp.exp2(m_prev - m_next)
        for cols in halves:
            if c_next is not None:
                scores(c_next, 1 - slot, cols)
            pt = jnp.exp2(s_sc[slot, :, cols] - m_next[:, cols]).astype(BF16)
            head = cols.start // tq
            acc_sc[:, cols] = alpha[:, cols] * acc_sc[:, cols] + _dot(vc[v_rows[head % v_heads]], pt)
        return m_next

    unroll = min(FLASH_UNROLL, n_chunks)

    def group(j, m):
        for u in range(unroll):
            c = unroll * j + u
            m = step(c, u % SCORE_SLOTS, m, c + 1)
        return m

    for cols in halves:
        scores(0, 0, cols)
    m = lax.fori_loop(0, n_chunks // unroll - 1, group, jnp.full((1, 2 * tq), -jnp.inf, F32))
    for c in range(n_chunks - unroll, n_chunks):
        m = step(c, c % SCORE_SLOTS, m, c + 1 if c + 1 < n_chunks else None)
    acc = acc_sc[...]
    return acc[:dv] / acc[dv:dv + 1]


def _gqa_kernel(qt_ref, k_ref, vt_ref, o_ref, acc_sc, s_sc, *, tq, n_chunks):
    o = _stacked_flash(qt_ref, k_ref, vt_ref, acc_sc, s_sc, tq=tq, n_chunks=n_chunks, v_heads=B_KV_HEADS)
    o_ref[...] = jnp.concatenate([o[:, :tq], o[:, tq:]], axis=0).T


def _diff_kernel(qt_ref, k_ref, vt_ref, lq1_ref, lk1_ref, lq2_ref, lk2_ref, sub_ref, o_ref, acc_sc,
                 s_sc, *, tq, n_chunks, lam_init):
    o = _stacked_flash(qt_ref, k_ref, vt_ref, acc_sc, s_sc, tq=tq, n_chunks=n_chunks, v_heads=1)
    lam = (jnp.exp(jnp.sum(lq1_ref[...] * lk1_ref[...], axis=-1, keepdims=True))
           - jnp.exp(jnp.sum(lq2_ref[...] * lk2_ref[...], axis=-1, keepdims=True)) + lam_init)
    oc = (o[:, :tq] - lam * o[:, tq:]).T
    ms = jnp.mean(oc * oc, axis=-1, keepdims=True)
    o_ref[...] = oc * lax.rsqrt(ms + EPS) * sub_ref[...] * (1.0 - lam_init)


def _flash_tq(s):
    return min(2048, s)


def _flash_scratch(tq, v_heads):
    return [pltpu.VMEM((LANE // v_heads + ONES_ROWS, 2 * tq), F32),
            pltpu.VMEM((SCORE_SLOTS, FLASH_TK, 2 * tq), F32)]


def _gqa_attention(qt, k, vt):
    bn, _, s = qt.shape
    tq, n_chunks = _flash_tq(s), s // FLASH_TK
    kern = functools.partial(_gqa_kernel, tq=tq, n_chunks=n_chunks)
    return pl.pallas_call(
        kern,
        grid=(bn, B_HEADS // 2, s // tq),
        in_specs=[pl.BlockSpec((None, LANE, tq), lambda b, p, i: (b, p, i)),
                  pl.BlockSpec((None, s, LANE), lambda b, p, i: (b, 0, 0)),
                  pl.BlockSpec((None, n_chunks, _vt_rows(B_KV_HEADS), FLASH_TK), lambda b, p, i: (b, 0, 0, 0))],
        out_specs=pl.BlockSpec((None, tq, LANE), lambda b, p, i: (b, i, p)),
        out_shape=jax.ShapeDtypeStruct((bn, s, B_WIDTH), F32),
        scratch_shapes=_flash_scratch(tq, B_KV_HEADS),
        compiler_params=_cparams(3),
        name="gqa_attn",
    )(qt, k, vt)


def _diff_attention(qt, k, vt, lam_rows, subln, lam_init):
    bn, _, s = qt.shape
    tq, n_chunks = _flash_tq(s), s // FLASH_TK
    kern = functools.partial(_diff_kernel, tq=tq, n_chunks=n_chunks, lam_init=lam_init)
    row = lambda n: pl.BlockSpec((1, n), lambda b, h, i: (0, 0))
    return pl.pallas_call(
        kern,
        grid=(bn, C_HEADS, s // tq),
        in_specs=[pl.BlockSpec((None, LANE, tq), lambda b, h, i: (b, h, i)),
                  pl.BlockSpec((None, s, LANE), lambda b, h, i: (b, 0, h)),
                  pl.BlockSpec((None, n_chunks, _vt_rows(1), FLASH_TK), lambda b, h, i: (b, 0, h, 0)),
                  row(C_HEAD_DIM), row(C_HEAD_DIM), row(C_HEAD_DIM), row(C_HEAD_DIM), row(LANE)],
        out_specs=pl.BlockSpec((None, tq, LANE), lambda b, h, i: (b, i, h)),
        out_shape=jax.ShapeDtypeStruct((bn, s, C_WIDTH), F32),
        scratch_shapes=_flash_scratch(tq, 1),
        compiler_params=_cparams(3),
        name="diff_attn",
    )(qt, k, vt, *lam_rows, subln)


def _final_kernel(x_ref, shift_ref, scale_ref, gate_ref, g_ref, oa0_ref, oa1_ref, oa2_ref, yb_ref, yc_ref,
                  wz_ref, wbg_ref, bbg_ref, woa_ref, wob_ref, woc_ref, wout_ref, o_ref, *stage_refs):
    x = x_ref[...]
    tm = x.shape[0]
    h = _modulated_norm(x, g_ref[...], scale_ref[...], shift_ref[...]).astype(BF16)
    lane = lax.broadcasted_iota(jnp.int32, (1, LANE), 1)

    for oa_ref, st_ref, (_, dil) in zip((oa1_ref, oa2_ref), stage_refs, A_GROUPS[1:]):
        for r in range(dil):
            for j in range(A_HEADS):
                lo = r * A_SLAB + j * LANE
                st_ref[j, pl.ds(r, tm // dil, stride=dil), :] = oa_ref[:, lo:lo + LANE]

    ya = []
    for j in range(A_HEADS):
        slabs = [oa0_ref[:, j * LANE:(j + 1) * LANE]] + [st_ref[j] for st_ref in stage_refs]
        lses = [t[:, A_HEAD_DIM:A_HEAD_DIM + 1] for t in slabs]
        top = jnp.maximum(jnp.maximum(lses[0], lses[1]), lses[2])
        ws = [jnp.exp(t - top) for t in lses]
        mix = (ws[0] * slabs[0] + ws[1] * slabs[1] + ws[2] * slabs[2]) / (ws[0] + ws[1] + ws[2])
        ya.append(jnp.where(lane < A_HEAD_DIM, mix, 0.0))
    ya = jnp.concatenate(ya, axis=-1)

    def branch(y, z_lo, z_hi, wo_ref, g_lo):
        z = _dot(h, wz_ref[:, z_lo:z_hi])
        p = _dot((y * _silu(z)).astype(BF16), wo_ref[...])
        gate = jax.nn.sigmoid(_dot(h, wbg_ref[:, g_lo:g_lo + D_MODEL]) + bbg_ref[:, g_lo:g_lo + D_MODEL])
        return gate * p

    za_hi = A_SLAB
    zb_hi = za_hi + B_WIDTH
    zc_hi = zb_hi + C_WIDTH
    merged = (branch(ya, 0, za_hi, woa_ref, 0)
              + branch(yb_ref[...], za_hi, zb_hi, wob_ref, D_MODEL)
              + branch(yc_ref[...], zb_hi, zc_hi, woc_ref, 2 * D_MODEL))
    out = _dot(merged.astype(BF16), wout_ref[...])
    o_ref[...] = x + gate_ref[...] * out


def _final(x, shift, scale, gate, norm_g, oa, yb, yc, wz, wbg, bbg, woa, wob, woc, wout, *, tm):
    bn, s, _ = x.shape
    tok = lambda w: pl.BlockSpec((None, tm, w), lambda b, i: (b, i, 0))
    vec = pl.BlockSpec((None, 1, D_MODEL), lambda b, i: (b, 0, 0))
    full = lambda a: pl.BlockSpec(a.shape, lambda b, i: (0, 0), pipeline_mode=pl.Buffered(1))
    band = [pl.BlockSpec((None, tm // dil, dil * A_SLAB), lambda b, i: (b, i, 0)) for _, dil in A_GROUPS]
    return pl.pallas_call(
        _final_kernel,
        grid=(bn, s // tm),
        in_specs=[tok(D_MODEL), vec, vec, vec, full(norm_g), *band,
                  tok(B_WIDTH), tok(C_WIDTH), full(wz), full(wbg), full(bbg), full(woa), full(wob), full(woc),
                  full(wout)],
        out_specs=tok(D_MODEL),
        out_shape=jax.ShapeDtypeStruct((bn, s, D_MODEL), F32),
        scratch_shapes=[pltpu.VMEM((A_HEADS, tm, LANE), F32) for _ in A_GROUPS[1:]],
        compiler_params=_cparams(2),
        name="final",
    )(x, shift, scale, gate, norm_g, *oa, yb, yc, wz, wbg, bbg, woa, wob, woc, wout)


B_Q_ORDER = (0, 3, 1, 4, 2, 5)


def _slab_layout(head_dim, n_heads, rot_groups):
    src = -np.ones((LANE,), np.int32)
    group = -np.ones((LANE,), np.int32)
    freq = np.zeros((LANE,), np.int32)
    sign = np.zeros((LANE,), np.float32)
    width = ROT_SHIFT // n_heads
    for hd in range(n_heads):
        low = [(start + i, g, i, -1.0) for g, (start, half) in enumerate(rot_groups) for i in range(half)]
        high = [(start + half + i, g, i, 1.0) for g, (start, half) in enumerate(rot_groups) for i in range(half)]
        rotary = {d for d, _, _, _ in low + high}
        rest = [(d, -1, 0, 0.0) for d in range(head_dim) if d not in rotary]
        n_low_rest = width - len(low)
        low, high = low + rest[:n_low_rest], high + rest[n_low_rest:]
        assert len(low) == width and len(high) <= width
        for base, items in ((hd * width, low), (hd * width + ROT_SHIFT, high)):
            for lane, (d, g, i, sg) in enumerate(items, start=base):
                src[lane], group[lane], freq[lane], sign[lane] = hd * head_dim + d, g, i, sg
    return src, group, freq, sign


LAYOUT_A = _slab_layout(A_HEAD_DIM, 1, ((0, A_ROT // 2),))
LAYOUT_B = _slab_layout(B_HEAD_DIM, 2, ((0, B_HEAD_DIM // 4), (B_HEAD_DIM // 2, B_HEAD_DIM // 4)))
LAYOUT_C = _slab_layout(C_HEAD_DIM, 2, ((0, C_ROT // 2),))


def _to_slabs(w, layout, dims_per_slab):
    src = layout[0]
    n_slabs = w.shape[-1] // dims_per_slab
    idx = np.concatenate([np.maximum(src, 0) + s * dims_per_slab for s in range(n_slabs)])
    keep = np.tile(src >= 0, n_slabs)
    return jnp.where(keep, jnp.take(w, idx, axis=-1), 0.0)


def _pad_heads(w, n_heads):
    lead = w.shape[:-1]
    w = w.reshape(*lead, n_heads, A_HEAD_DIM)
    w = jnp.pad(w, [(0, 0)] * len(lead) + [(0, 0), (0, LANE - A_HEAD_DIM)])
    return w.reshape(*lead, n_heads * LANE)


def _permute_heads(w, order, dim):
    lead = w.shape[:-1]
    w = w.reshape(*lead, len(order), dim)
    return w[..., jnp.array(order), :].reshape(*lead, len(order) * dim)


def _prep_layer(w_in, qn_a, kn_a, qn_b, kn_b, qn_c, kn_c, w_oa, w_ob, w_oc):
    offs = [0]
    for n in IN_SIZES:
        offs.append(offs[-1] + n)
    qa, ka, va, za, qb, kb, vb, zb, qc, kc, vc, zc = [w_in[:, offs[i]:offs[i + 1]] for i in range(len(IN_SIZES))]

    slab_a = functools.partial(_to_slabs, layout=LAYOUT_A, dims_per_slab=A_HEAD_DIM)
    slab_b = functools.partial(_to_slabs, layout=LAYOUT_B, dims_per_slab=2 * B_HEAD_DIM)
    slab_c = functools.partial(_to_slabs, layout=LAYOUT_C, dims_per_slab=2 * C_HEAD_DIM)
    qb = _permute_heads(qb, B_Q_ORDER, B_HEAD_DIM)

    w_a = jnp.concatenate([slab_a(qa), slab_a(ka), _pad_heads(va, N_A)], axis=-1).astype(BF16)
    gain_a = jnp.concatenate([slab_a(jnp.tile(qn_a, N_A)), slab_a(jnp.tile(kn_a, N_A)),
                              jnp.zeros((N_A * LANE,), F32)])[None, :]

    w_b = jnp.concatenate([slab_b(qb), slab_b(kb), vb], axis=-1).astype(BF16)
    gain_b = jnp.concatenate([slab_b(jnp.tile(qn_b, B_HEADS)), slab_b(jnp.tile(kn_b, B_KV_HEADS)),
                              jnp.zeros((B_KV,), F32)])[None, :]

    w_c = jnp.concatenate([slab_c(qc), slab_c(kc), vc], axis=-1).astype(BF16)
    gain_c = jnp.concatenate([slab_c(jnp.tile(qn_c, 2 * C_HEADS)), slab_c(jnp.tile(kn_c, 2 * C_HEADS)),
                              jnp.zeros((C_WIDTH,), F32)])[None, :]

    w_z = jnp.concatenate([_pad_heads(za, A_HEADS), _permute_heads(zb, B_Q_ORDER, B_HEAD_DIM), zc],
                          axis=-1).astype(BF16)
    w_oa_p = _pad_heads(w_oa.T, A_HEADS).T.astype(BF16)
    w_ob_p = _permute_heads(w_ob.T, B_Q_ORDER, B_HEAD_DIM).T.astype(BF16)
    return dict(w_a=w_a, gain_a=gain_a, w_bc=jnp.concatenate([w_b, w_c], axis=-1),
                gain_bc=jnp.concatenate([gain_b, gain_c], axis=-1), w_z=w_z,
                w_oa=w_oa_p, w_ob=w_ob_p, w_oc=w_oc.astype(BF16))


def _inv_freq(dim, theta):
    return theta ** (-jnp.arange(0, dim, 2, dtype=F32) / dim)


def _rope_tables(layout, inv_freqs, streams):
    _, group, freq, sign = layout
    ang = 0.0
    for g, (inv, pos) in enumerate(zip(inv_freqs, streams)):
        ang = ang + pos[:, None] * (inv[freq] * (group == g))[None, :]
    return jnp.cos(ang), jnp.sin(ang) * sign[None, :]


def _tables(s):
    pos = jnp.arange(s)
    tok, row, col = [t.astype(F32) for t in (pos, pos // GRID_W, pos % GRID_W)]
    inv_b = _inv_freq(B_HEAD_DIM // 2, AXIAL_THETA)
    tab_a = _rope_tables(LAYOUT_A, [_inv_freq(A_ROT, ROPE_THETA)], [tok])
    tab_b = _rope_tables(LAYOUT_B, [inv_b, inv_b], [row, col])
    tab_c = _rope_tables(LAYOUT_C, [_inv_freq(C_ROT, ROPE_THETA)], [tok])
    return tab_a, tab_b, tab_c


def _layer(x, mod, l, lp, tabs, norm_g, lam_rows, subln, w_bg, b_bg, w_out):
    bn, s, _ = x.shape
    shift, scale, gate = [m.reshape(bn, 1, D_MODEL) for m in jnp.split(mod, 3, axis=-1)]
    tab_a, tab_b, tab_c = tabs
    lam_init = 0.8 - 0.6 * math.exp(-0.3 * l)

    a_outs = tuple((A_SLAB, kind, dil, 0) for kind in ("q", "k", "v") for _, dil in A_GROUPS)
    pa = _project(x, shift, scale, norm_g, lp["w_a"], lp["gain_a"], tab_a, a_outs, two_heads=False,
                  head_dim=A_HEAD_DIM, q_scale=A_HEAD_DIM ** -0.5 * LOG2E, tm=min(PROJ_A_TM, s), name="proj_a")
    n_g = len(A_GROUPS)
    oa = [_band_attention(pa[g], pa[n_g + g], pa[2 * n_g + g], dil) for g, (_, dil) in enumerate(A_GROUPS)]

    assert B_HEAD_DIM == C_HEAD_DIM
    bc_outs = ((B_WIDTH, "qT", 1, 0), (B_KV, "k", 1, 0), (B_KV, "vT", B_KV_HEADS, 0),
               (C_WIDTH, "qT", 1, 1), (C_WIDTH, "k", 1, 1), (C_WIDTH, "vT", 1, 1))
    qb, kb, vb, qc, kc, vc = _project(x, shift, scale, norm_g, lp["w_bc"], lp["gain_bc"], (*tab_b, *tab_c), bc_outs,
                                      two_heads=True, head_dim=B_HEAD_DIM, q_scale=B_HEAD_DIM ** -0.5 * LOG2E,
                                      tm=min(PROJ_TM, s), name="proj_bc")
    yb = _gqa_attention(qb, kb, vb)
    yc = _diff_attention(qc, kc, vc, lam_rows, subln, lam_init)

    return _final(x, shift, scale, gate, norm_g, oa, yb, yc, lp["w_z"], w_bg, b_bg, lp["w_oa"], lp["w_ob"],
                  lp["w_oc"], w_out, tm=min(FINAL_TM, s))


def kernel(x_prompt, x_sample, c_prompt, c_sample, norm_g, w_ada, b_ada, w_in, qn_a, kn_a, qn_b, kn_b, qn_c, kn_c,
           lam_q1, lam_k1, lam_q2, lam_k2, subln_c, w_oa, w_ob, w_oc, w_bg, b_bg, w_out):
    groups = ((x_prompt, c_prompt), (x_sample, c_sample))
    rows = [c.shape[0] for _, c in groups]
    pad = -sum(rows) % 8
    c_all = jnp.concatenate([c for _, c in groups] + [jnp.zeros((pad, D_MODEL), F32)], axis=0)
    mod_all = _ada_mod(c_all, w_ada.astype(BF16), b_ada)

    layers = [_prep_layer(w_in[l], qn_a[l], kn_a[l], qn_b[l], kn_b[l], qn_c[l], kn_c[l], w_oa[l], w_ob[l], w_oc[l])
              for l in range(DEPTH)]
    w_bg16, w_out16 = w_bg.astype(BF16), w_out.astype(BF16)

    outs = []
    row0 = 0
    for (x, c), n in zip(groups, rows):
        tabs = _tables(x.shape[1])
        for l in range(DEPTH):
            lam_rows = [p[l][None, :] for p in (lam_q1, lam_k1, lam_q2, lam_k2)]
            x = _layer(x, mod_all[l, row0:row0 + n], l, layers[l], tabs, norm_g[l][None, :], lam_rows,
                       subln_c[l][None, :], w_bg16[l], b_bg[l][None, :], w_out16[l])
        outs.append(x)
        row0 += n
    return tuple(outs)
```

```python
import functools
import math

import jax
import jax.numpy as jnp
import numpy as np
from jax import lax
from jax.experimental import pallas as pl
from jax.experimental.pallas import tpu as pltpu

D_MODEL = 1024
DEPTH = 2
GRID_W = 64
EPS = 1e-6
NEG = -1e30
ROPE_THETA = 500000.0
AXIAL_THETA = 10000.0

A_GROUPS = ((128, 1), (512, 4), (2048, 16))
A_HEADS = 4
A_HEAD_DIM = 96
A_ROT = A_HEAD_DIM // 4
A_BAND = 64
N_A = len(A_GROUPS) * A_HEADS

B_HEADS = 6
B_KV_HEADS = 2
B_HEAD_DIM = 64
C_HEADS = 4
C_HEAD_DIM = 64
C_ROT = C_HEAD_DIM // 4

A_QKV = N_A * A_HEAD_DIM
A_WIDTH = A_HEADS * A_HEAD_DIM
B_WIDTH = B_HEADS * B_HEAD_DIM
B_KV = B_KV_HEADS * B_HEAD_DIM
C_WIDTH = C_HEADS * 2 * C_HEAD_DIM
IN_SIZES = (A_QKV, A_QKV, A_QKV, A_WIDTH, B_WIDTH, B_KV, B_KV, B_WIDTH, C_WIDTH, C_WIDTH, C_WIDTH, C_WIDTH)

LANE = 128
HALF = LANE // 2
ROT_SHIFT = LANE // 2
A_SLAB = A_HEADS * LANE
BF16_SUBLANES = 16
ONES_ROWS = BF16_SUBLANES
FLASH_TK = 512
COL_GROUP = 256
SCORE_SLOTS = 2
FLASH_UNROLL = 2
PROJ_TM = 512
PROJ_A_TM = 256
FINAL_TM = 512
LOG2E = math.log2(math.e)
LN2 = math.log(2.0)
VMEM_LIMIT = 48 * 1024 * 1024

BF16 = jnp.bfloat16
F32 = jnp.float32


def _cparams(n_axes):
    return pltpu.CompilerParams(dimension_semantics=("arbitrary",) * n_axes, vmem_limit_bytes=VMEM_LIMIT)


def _dot(a, b):
    return jnp.dot(a, b, preferred_element_type=F32)


def _dot_nt(a, b):
    return lax.dot_general(a, b, (((1,), (1,)), ((), ())), preferred_element_type=F32)


def _silu(x):
    return x * jax.nn.sigmoid(x)


def _ada_kernel(c_ref, w_ref, b_ref, o_ref):
    c = c_ref[...]
    o_ref[...] = _dot(_silu(c).astype(BF16), w_ref[...]) + b_ref[...]


def _ada_mod(c_all, w_ada, b_ada):
    rows = c_all.shape[0]
    return pl.pallas_call(
        _ada_kernel,
        grid=(DEPTH, 3),
        in_specs=[
            pl.BlockSpec((rows, D_MODEL), lambda l, n: (0, 0)),
            pl.BlockSpec((None, D_MODEL, D_MODEL), lambda l, n: (l, 0, n)),
            pl.BlockSpec((None, 1, D_MODEL), lambda l, n: (l, 0, n)),
        ],
        out_specs=pl.BlockSpec((None, rows, D_MODEL), lambda l, n: (l, 0, n)),
        out_shape=jax.ShapeDtypeStruct((DEPTH, rows, 3 * D_MODEL), F32),
        compiler_params=_cparams(2),
        name="ada_mod",
    )(c_all, w_ada, b_ada.reshape(DEPTH, 1, 3 * D_MODEL))


def _modulated_norm(x, g, scale, shift):
    ms = jnp.mean(x * x, axis=-1, keepdims=True)
    h = x * lax.rsqrt(ms + EPS) * g
    return h * (1.0 + scale) + shift


def _first_head(index):
    return (index & (HALF - 1)) < HALF // 2


def _head_norm_rope(u, gain, cos, sin, two_heads, head_dim):
    sq = u * u
    if two_heads:
        first = _first_head(lax.broadcasted_iota(jnp.int32, (1, LANE), 1))
        s_first = jnp.sum(jnp.where(first, sq, 0.0), axis=-1, keepdims=True)
        s_all = jnp.sum(sq, axis=-1, keepdims=True)
        ms = jnp.where(first, s_first, s_all - s_first) * (1.0 / head_dim)
    else:
        ms = jnp.sum(sq, axis=-1, keepdims=True) * (1.0 / head_dim)
    y = u * lax.rsqrt(ms + EPS) * gain
    return y * cos + pltpu.roll(y, ROT_SHIFT, 1) * sin


def _proj_kernel(x_ref, shift_ref, scale_ref, g_ref, w_ref, gain_ref, *refs,
                 outs, n_tables, two_heads, head_dim, q_scale):
    tables = [(refs[2 * t][...], refs[2 * t + 1][...]) for t in range(n_tables)]
    refs = refs[2 * n_tables:]
    out_refs, stage_ref = refs[:len(outs)], refs[len(outs):]
    h = _modulated_norm(x_ref[...], g_ref[...], scale_ref[...], shift_ref[...]).astype(BF16)
    tm = x_ref.shape[0]
    col = 0
    for o_ref, (width, kind, dil, table) in zip(out_refs, outs):
        cos, sin = tables[table]
        u = _dot(h, w_ref[:, col:col + width])
        if kind == "vT":
            dv = LANE // dil
            for s in range(width // LANE):
                ut = u[:, s * LANE:(s + 1) * LANE].T.astype(BF16)
                for hd in range(dil):
                    base = (s * dil + hd) * (dv + ONES_ROWS)
                    o_ref[base:base + dv, :] = ut[hd * dv:(hd + 1) * dv]
                    o_ref[base + dv:base + dv + ONES_ROWS, :] = jnp.ones((ONES_ROWS, tm), BF16)
        else:
            if kind != "v":
                slabs = []
                for s in range(width // LANE):
                    lo = s * LANE
                    y = _head_norm_rope(u[:, lo:lo + LANE], gain_ref[:, col + lo:col + lo + LANE], cos, sin,
                                        two_heads, head_dim)
                    slabs.append(y * q_scale if kind in ("q", "qT") and q_scale != 1.0 else y)
                u = jnp.concatenate(slabs, axis=-1)
            if kind == "qT":
                for s in range(width // LANE):
                    o_ref[s * LANE:(s + 1) * LANE, :] = u[:, s * LANE:(s + 1) * LANE].T.astype(BF16)
            elif dil == 1:
                o_ref[...] = u.astype(BF16)
            else:
                stage = stage_ref[0]
                for s in range(width // LANE):
                    stage[s] = u[:, s * LANE:(s + 1) * LANE]
                rows = lambda r: pl.ds(r, tm // dil, stride=dil)
                for r in range(dil):
                    for s in range(width // LANE):
                        lo = r * width + s * LANE
                        o_ref[:, lo:lo + LANE] = stage[s, rows(r), :].astype(BF16)
        col += width


def _vt_rows(v_heads):
    return LANE + v_heads * ONES_ROWS


def _proj_out(kind, width, dil, bn, s, tm):
    if kind == "qT":
        return (bn, width, s), pl.BlockSpec((None, width, tm), lambda b, i: (b, 0, i))
    if kind == "vT":
        per = FLASH_TK // tm
        rows = width // LANE * _vt_rows(dil)
        return ((bn, s // FLASH_TK, rows, FLASH_TK),
                pl.BlockSpec((None, None, rows, tm), lambda b, i: (b, i // per, 0, i % per)))
    return (bn, s // dil, dil * width), pl.BlockSpec((None, tm // dil, dil * width), lambda b, i: (b, i, 0))


def _project(x, shift, scale, norm_g, w, gain, tables, outs, *, two_heads, head_dim, q_scale, tm, name):
    bn, s, _ = x.shape
    wcols = w.shape[1]
    kern = functools.partial(_proj_kernel, outs=outs, n_tables=len(tables) // 2, two_heads=two_heads,
                             head_dim=head_dim, q_scale=q_scale)
    vec = pl.BlockSpec((None, 1, D_MODEL), lambda b, i: (b, 0, 0))
    tab = pl.BlockSpec((tm, LANE), lambda b, i: (i, 0))
    shapes, specs = zip(*[_proj_out(kind, width, dil, bn, s, tm) for width, kind, dil, _ in outs])
    n_slabs = max(width for width, _, _, _ in outs) // LANE
    dilated = any(d > 1 and kind != "vT" for _, kind, d, _ in outs)
    stage = [pltpu.VMEM((n_slabs, tm, LANE), F32)] if dilated else []
    return pl.pallas_call(
        kern,
        grid=(bn, s // tm),
        in_specs=[
            pl.BlockSpec((None, tm, D_MODEL), lambda b, i: (b, i, 0)),
            vec, vec,
            pl.BlockSpec((1, D_MODEL), lambda b, i: (0, 0)),
            pl.BlockSpec((D_MODEL, wcols), lambda b, i: (0, 0)),
            pl.BlockSpec((1, wcols), lambda b, i: (0, 0)),
            *[tab] * len(tables),
        ],
        out_specs=list(specs),
        out_shape=[jax.ShapeDtypeStruct(shape, BF16) for shape in shapes],
        scratch_shapes=stage,
        compiler_params=_cparams(2),
        name=name,
    )(x, shift, scale, norm_g, w, gain, *tables)


def _band_attn_kernel(q_ref, kp_ref, kc_ref, kn_ref, vp_ref, vc_ref, vn_ref, bias_ref, o_ref, *, tb, n_blocks):
    i = pl.program_id(2)
    nk = tb + 2 * A_BAND
    colk = lax.broadcasted_iota(jnp.int32, (1, nk), 1)
    first_col = jnp.where(i == 0, A_BAND, 0)
    end_col = jnp.where(i == n_blocks - 1, tb + A_BAND, nk)
    bias = bias_ref[...] + jnp.where((colk < first_col) | (colk >= end_col), NEG, 0.0)
    lane = lax.broadcasted_iota(jnp.int32, (1, LANE), 1)
    heads = [slice(j * LANE, (j + 1) * LANE) for j in range(A_HEADS)]
    band = lambda p_ref, c_ref, n_ref, sl: jnp.concatenate([p_ref[:, sl], c_ref[:, sl], n_ref[:, sl]], axis=0)
    scores = [_dot_nt(q_ref[:, sl], band(kp_ref, kc_ref, kn_ref, sl)) for sl in heads]
    stats = []
    for s in scores:
        s = s + bias
        m = jnp.max(s, axis=-1, keepdims=True)
        p = jnp.exp2(s - m)
        stats.append((m, jnp.sum(p, axis=-1, keepdims=True), p.astype(BF16)))
    for sl, (m, l, p) in zip(heads, stats):
        o = _dot(p, band(vp_ref, vc_ref, vn_ref, sl)) / l
        o_ref[:, sl] = jnp.where(lane >= A_HEAD_DIM, m * LN2 + jnp.log(l), o)


def _band_attention(q, k, v, dilation):
    bn, length, _ = q.shape
    tb = min(256, length)
    nb = length // tb
    per = tb // A_BAND
    cur = pl.BlockSpec((None, tb, A_SLAB), lambda b, r, i: (b, i, r))
    prv = pl.BlockSpec((None, A_BAND, A_SLAB), lambda b, r, i: (b, jnp.maximum(i * per - 1, 0), r))
    nxt = pl.BlockSpec((None, A_BAND, A_SLAB), lambda b, r, i: (b, jnp.minimum((i + 1) * per, nb * per - 1), r))
    nk = tb + 2 * A_BAND
    dist = np.arange(nk)[None, :] - A_BAND - np.arange(tb)[:, None]
    bias = np.where(np.abs(dist) <= A_BAND, 0.0, NEG).astype(np.float32)
    kern = functools.partial(_band_attn_kernel, tb=tb, n_blocks=nb)
    return pl.pallas_call(
        kern,
        grid=(bn, dilation, nb),
        in_specs=[cur, prv, cur, nxt, prv, cur, nxt, pl.BlockSpec((tb, nk), lambda b, r, i: (0, 0))],
        out_specs=cur,
        out_shape=jax.ShapeDtypeStruct((bn, length, dilation * A_SLAB), F32),
        compiler_params=_cparams(3),
        name=f"band_attn_d{dilation}",
    )(q, k, k, k, v, v, v, bias)


def _stacked_flash(qt_ref, k_ref, vt_ref, acc_sc, s_sc, *, tq, n_chunks, v_heads):
    dv = LANE // v_heads
    v_rows = [slice(hd * (dv + ONES_ROWS), (hd + 1) * (dv + ONES_ROWS)) for hd in range(v_heads)]
    first = _first_head(lax.broadcasted_iota(jnp.int32, (LANE, 1), 0))
    qt = qt_ref[...]
    zero = jnp.zeros_like(qt)
    q2t = jnp.concatenate([jnp.where(first, qt, zero), jnp.where(first, zero, qt)], axis=1)
    acc_sc[...] = jnp.zeros(acc_sc.shape, F32)

    wide = v_heads == 1 and n_chunks > FLASH_UNROLL
    width = min(2 * COL_GROUP if wide else COL_GROUP, tq)
    halves = [slice(lo, lo + width) for lo in range(0, 2 * tq, width)]

    def scores(c, slot, cols):
        start = pl.multiple_of(c * FLASH_TK, FLASH_TK)
        s_sc[slot, :, cols] = _dot(k_ref[pl.ds(start, FLASH_TK), :], q2t[:, cols])

    def step(c, slot, m_prev, c_next=None):
        m_next = jnp.maximum(m_prev, jnp.max(s_sc[slot], axis=0, keepdims=True))
        vc = vt_ref[c]
        alpha = jnp.exp2(m_prev - m_next)
        for cols in halves:
            if c_next is not None:
                scores(c_next, 1 - slot, cols)
            pt = jnp.exp2(s_sc[slot, :, cols] - m_next[:, cols]).astype(BF16)
            head = cols.start // tq
            acc_sc[:, cols] = alpha[:, cols] * acc_sc[:, cols] + _dot(vc[v_rows[head % v_heads]], pt)
        return m_next

    unroll = min(FLASH_UNROLL, n_chunks)

    def group(j, m):
        for u in range(unroll):
            c = unroll * j + u
            m = step(c, u % SCORE_SLOTS, m, c + 1)
        return m

    for cols in halves:
        scores(0, 0, cols)
    m = lax.fori_loop(0, n_chunks // unroll - 1, group, jnp.full((1, 2 * tq), -jnp.inf, F32))
    for c in range(n_chunks - unroll, n_chunks):
        m = step(c, c % SCORE_SLOTS, m, c + 1 if c + 1 < n_chunks else None)
    acc = acc_sc[...]
    return acc[:dv] / acc[dv:dv + 1]


def _gqa_kernel(qt_ref, k_ref, vt_ref, o_ref, acc_sc, s_sc, *, tq, n_chunks):
    o = _stacked_flash(qt_ref, k_ref, vt_ref, acc_sc, s_sc, tq=tq, n_chunks=n_chunks, v_heads=B_KV_HEADS)
    o_ref[...] = jnp.concatenate([o[:, :tq], o[:, tq:]], axis=0).T


def _diff_kernel(qt_ref, k_ref, vt_ref, lq1_ref, lk1_ref, lq2_ref, lk2_ref, sub_ref, o_ref, acc_sc,
                 s_sc, *, tq, n_chunks, lam_init):
    o = _stacked_flash(qt_ref, k_ref, vt_ref, acc_sc, s_sc, tq=tq, n_chunks=n_chunks, v_heads=1)
    lam = (jnp.exp(jnp.sum(lq1_ref[...] * lk1_ref[...], axis=-1, keepdims=True))
           - jnp.exp(jnp.sum(lq2_ref[...] * lk2_ref[...], axis=-1, keepdims=True)) + lam_init)
    oc = (o[:, :tq] - lam * o[:, tq:]).T
    ms = jnp.mean(oc * oc, axis=-1, keepdims=True)
    o_ref[...] = oc * lax.rsqrt(ms + EPS) * sub_ref[...] * (1.0 - lam_init)


def _flash_tq(s):
    return min(2048, s)


def _flash_scratch(tq, v_heads):
    return [pltpu.VMEM((LANE // v_heads + ONES_ROWS, 2 * tq), F32),
            pltpu.VMEM((SCORE_SLOTS, FLASH_TK, 2 * tq), F32)]


def _gqa_attention(qt, k, vt):
    bn, _, s = qt.shape
    tq, n_chunks = _flash_tq(s), s // FLASH_TK
    kern = functools.partial(_gqa_kernel, tq=tq, n_chunks=n_chunks)
    return pl.pallas_call(
        kern,
        grid=(bn, B_HEADS // 2, s // tq),
        in_specs=[pl.BlockSpec((None, LANE, tq), lambda b, p, i: (b, p, i)),
                  pl.BlockSpec((None, s, LANE), lambda b, p, i: (b, 0, 0)),
                  pl.BlockSpec((None, n_chunks, _vt_rows(B_KV_HEADS), FLASH_TK), lambda b, p, i: (b, 0, 0, 0))],
        out_specs=pl.BlockSpec((None, tq, LANE), lambda b, p, i: (b, i, p)),
        out_shape=jax.ShapeDtypeStruct((bn, s, B_WIDTH), F32),
        scratch_shapes=_flash_scratch(tq, B_KV_HEADS),
        compiler_params=_cparams(3),
        name="gqa_attn",
    )(qt, k, vt)


def _diff_attention(qt, k, vt, lam_rows, subln, lam_init):
    bn, _, s = qt.shape
    tq, n_chunks = _flash_tq(s), s // FLASH_TK
    kern = functools.partial(_diff_kernel, tq=tq, n_chunks=n_chunks, lam_init=lam_init)
    row = lambda n: pl.BlockSpec((1, n), lambda b, h, i: (0, 0))
    return pl.pallas_call(
        kern,
        grid=(bn, C_HEADS, s // tq),
        in_specs=[pl.BlockSpec((None, LANE, tq), lambda b, h, i: (b, h, i)),
                  pl.BlockSpec((None, s, LANE), lambda b, h, i: (b, 0, h)),
                  pl.BlockSpec((None, n_chunks, _vt_rows(1), FLASH_TK), lambda b, h, i: (b, 0, h, 0)),
                  row(C_HEAD_DIM), row(C_HEAD_DIM), row(C_HEAD_DIM), row(C_HEAD_DIM), row(LANE)],
        out_specs=pl.BlockSpec((None, tq, LANE), lambda b, h, i: (b, i, h)),
        out_shape=jax.ShapeDtypeStruct((bn, s, C_WIDTH), F32),
        scratch_shapes=_flash_scratch(tq, 1),
        compiler_params=_cparams(3),
        name="diff_attn",
    )(qt, k, vt, *lam_rows, subln)


def _final_kernel(x_ref, shift_ref, scale_ref, gate_ref, g_ref, oa0_ref, oa1_ref, oa2_ref, yb_ref, yc_ref,
                  wz_ref, wbg_ref, bbg_ref, woa_ref, wob_ref, woc_ref, wout_ref, o_ref, *stage_refs):
    x = x_ref[...]
    tm = x.shape[0]
    h = _modulated_norm(x, g_ref[...], scale_ref[...], shift_ref[...]).astype(BF16)
    lane = lax.broadcasted_iota(jnp.int32, (1, LANE), 1)

    for oa_ref, st_ref, (_, dil) in zip((oa1_ref, oa2_ref), stage_refs, A_GROUPS[1:]):
        for r in range(dil):
            for j in range(A_HEADS):
                lo = r * A_SLAB + j * LANE
                st_ref[j, pl.ds(r, tm // dil, stride=dil), :] = oa_ref[:, lo:lo + LANE]

    ya = []
    for j in range(A_HEADS):
        slabs = [oa0_ref[:, j * LANE:(j + 1) * LANE]] + [st_ref[j] for st_ref in stage_refs]
        lses = [t[:, A_HEAD_DIM:A_HEAD_DIM + 1] for t in slabs]
        top = jnp.maximum(jnp.maximum(lses[0], lses[1]), lses[2])
        ws = [jnp.exp(t - top) for t in lses]
        mix = (ws[0] * slabs[0] + ws[1] * slabs[1] + ws[2] * slabs[2]) / (ws[0] + ws[1] + ws[2])
        ya.append(jnp.where(lane < A_HEAD_DIM, mix, 0.0))
    ya = jnp.concatenate(ya, axis=-1)

    def branch(y, z_lo, z_hi, wo_ref, g_lo):
        z = _dot(h, wz_ref[:, z_lo:z_hi])
        p = _dot((y * _silu(z)).astype(BF16), wo_ref[...])
        gate = jax.nn.sigmoid(_dot(h, wbg_ref[:, g_lo:g_lo + D_MODEL]) + bbg_ref[:, g_lo:g_lo + D_MODEL])
        return gate * p

    za_hi = A_SLAB
    zb_hi = za_hi + B_WIDTH
    zc_hi = zb_hi + C_WIDTH
    merged = (branch(ya, 0, za_hi, woa_ref, 0)
              + branch(yb_ref[...], za_hi, zb_hi, wob_ref, D_MODEL)
              + branch(yc_ref[...], zb_hi, zc_hi, woc_ref, 2 * D_MODEL))
    out = _dot(merged.astype(BF16), wout_ref[...])
    o_ref[...] = x + gate_ref[...] * out


def _final(x, shift, scale, gate, norm_g, oa, yb, yc, wz, wbg, bbg, woa, wob, woc, wout, *, tm):
    bn, s, _ = x.shape
    tok = lambda w: pl.BlockSpec((None, tm, w), lambda b, i: (b, i, 0))
    vec = pl.BlockSpec((None, 1, D_MODEL), lambda b, i: (b, 0, 0))
    full = lambda a: pl.BlockSpec(a.shape, lambda b, i: (0, 0), pipeline_mode=pl.Buffered(1))
    band = [pl.BlockSpec((None, tm // dil, dil * A_SLAB), lambda b, i: (b, i, 0)) for _, dil in A_GROUPS]
    return pl.pallas_call(
        _final_kernel,
        grid=(bn, s // tm),
        in_specs=[tok(D_MODEL), vec, vec, vec, full(norm_g), *band,
                  tok(B_WIDTH), tok(C_WIDTH), full(wz), full(wbg), full(bbg), full(woa), full(wob), full(woc),
                  full(wout)],
        out_specs=tok(D_MODEL),
        out_shape=jax.ShapeDtypeStruct((bn, s, D_MODEL), F32),
        scratch_shapes=[pltpu.VMEM((A_HEADS, tm, LANE), F32) for _ in A_GROUPS[1:]],
        compiler_params=_cparams(2),
        name="final",
    )(x, shift, scale, gate, norm_g, *oa, yb, yc, wz, wbg, bbg, woa, wob, woc, wout)


B_Q_ORDER = (0, 3, 1, 4, 2, 5)


def _slab_layout(head_dim, n_heads, rot_groups):
    src = -np.ones((LANE,), np.int32)
    group = -np.ones((LANE,), np.int32)
    freq = np.zeros((LANE,), np.int32)
    sign = np.zeros((LANE,), np.float32)
    width = ROT_SHIFT // n_heads
    for hd in range(n_heads):
        low = [(start + i, g, i, -1.0) for g, (start, half) in enumerate(rot_groups) for i in range(half)]
        high = [(start + half + i, g, i, 1.0) for g, (start, half) in enumerate(rot_groups) for i in range(half)]
        rotary = {d for d, _, _, _ in low + high}
        rest = [(d, -1, 0, 0.0) for d in range(head_dim) if d not in rotary]
        n_low_rest = width - len(low)
        low, high = low + rest[:n_low_rest], high + rest[n_low_rest:]
        assert len(low) == width and len(high) <= width
        for base, items in ((hd * width, low), (hd * width + ROT_SHIFT, high)):
            for lane, (d, g, i, sg) in enumerate(items, start=base):
                src[lane], group[lane], freq[lane], sign[lane] = hd * head_dim + d, g, i, sg
    return src, group, freq, sign


LAYOUT_A = _slab_layout(A_HEAD_DIM, 1, ((0, A_ROT // 2),))
LAYOUT_B = _slab_layout(B_HEAD_DIM, 2, ((0, B_HEAD_DIM // 4), (B_HEAD_DIM // 2, B_HEAD_DIM // 4)))
LAYOUT_C = _slab_layout(C_HEAD_DIM, 2, ((0, C_ROT // 2),))


def _to_slabs(w, layout, dims_per_slab):
    src = layout[0]
    n_slabs = w.shape[-1] // dims_per_slab
    idx = np.concatenate([np.maximum(src, 0) + s * dims_per_slab for s in range(n_slabs)])
    keep = np.tile(src >= 0, n_slabs)
    return jnp.where(keep, jnp.take(w, idx, axis=-1), 0.0)


def _pad_heads(w, n_heads):
    lead = w.shape[:-1]
    w = w.reshape(*lead, n_heads, A_HEAD_DIM)
    w = jnp.pad(w, [(0, 0)] * len(lead) + [(0, 0), (0, LANE - A_HEAD_DIM)])
    return w.reshape(*lead, n_heads * LANE)


def _permute_heads(w, order, dim):
    lead = w.shape[:-1]
    w = w.reshape(*lead, len(order), dim)
    return w[..., jnp.array(order), :].reshape(*lead, len(order) * dim)


def _prep_layer(w_in, qn_a, kn_a, qn_b, kn_b, qn_c, kn_c, w_oa, w_ob, w_oc):
    offs = [0]
    for n in IN_SIZES:
        offs.append(offs[-1] + n)
    qa, ka, va, za, qb, kb, vb, zb, qc, kc, vc, zc = [w_in[:, offs[i]:offs[i + 1]] for i in range(len(IN_SIZES))]

    slab_a = functools.partial(_to_slabs, layout=LAYOUT_A, dims_per_slab=A_HEAD_DIM)
    slab_b = functools.partial(_to_slabs, layout=LAYOUT_B, dims_per_slab=2 * B_HEAD_DIM)
    slab_c = functools.partial(_to_slabs, layout=LAYOUT_C, dims_per_slab=2 * C_HEAD_DIM)
    qb = _permute_heads(qb, B_Q_ORDER, B_HEAD_DIM)

    w_a = jnp.concatenate([slab_a(qa), slab_a(ka), _pad_heads(va, N_A)], axis=-1).astype(BF16)
    gain_a = jnp.concatenate([slab_a(jnp.tile(qn_a, N_A)), slab_a(jnp.tile(kn_a, N_A)),
                              jnp.zeros((N_A * LANE,), F32)])[None, :]

    w_b = jnp.concatenate([slab_b(qb), slab_b(kb), vb], axis=-1).astype(BF16)
    gain_b = jnp.concatenate([slab_b(jnp.tile(qn_b, B_HEADS)), slab_b(jnp.tile(kn_b, B_KV_HEADS)),
                              jnp.zeros((B_KV,), F32)])[None, :]

    w_c = jnp.concatenate([slab_c(qc), slab_c(kc), vc], axis=-1).astype(BF16)
    gain_c = jnp.concatenate([slab_c(jnp.tile(qn_c, 2 * C_HEADS)), slab_c(jnp.tile(kn_c, 2 * C_HEADS)),
                              jnp.zeros((C_WIDTH,), F32)])[None, :]

    w_z = jnp.concatenate([_pad_heads(za, A_HEADS), _permute_heads(zb, B_Q_ORDER, B_HEAD_DIM), zc],
                          axis=-1).astype(BF16)
    w_oa_p = _pad_heads(w_oa.T, A_HEADS).T.astype(BF16)
    w_ob_p = _permute_heads(w_ob.T, B_Q_ORDER, B_HEAD_DIM).T.astype(BF16)
    return dict(w_a=w_a, gain_a=gain_a, w_bc=jnp.concatenate([w_b, w_c], axis=-1),
                gain_bc=jnp.concatenate([gain_b, gain_c], axis=-1), w_z=w_z,
                w_oa=w_oa_p, w_ob=w_ob_p, w_oc=w_oc.astype(BF16))


def _inv_freq(dim, theta):
    return theta ** (-jnp.arange(0, dim, 2, dtype=F32) / dim)


def _rope_tables(layout, inv_freqs, streams):
    _, group, freq, sign = layout
    ang = 0.0
    for g, (inv, pos) in enumerate(zip(inv_freqs, streams)):
        ang = ang + pos[:, None] * (inv[freq] * (group == g))[None, :]
    return jnp.cos(ang), jnp.sin(ang) * sign[None, :]


def _tables(s):
    pos = jnp.arange(s)
    tok, row, col = [t.astype(F32) for t in (pos, pos // GRID_W, pos % GRID_W)]
    inv_b = _inv_freq(B_HEAD_DIM // 2, AXIAL_THETA)
    tab_a = _rope_tables(LAYOUT_A, [_inv_freq(A_ROT, ROPE_THETA)], [tok])
    tab_b = _rope_tables(LAYOUT_B, [inv_b, inv_b], [row, col])
    tab_c = _rope_tables(LAYOUT_C, [_inv_freq(C_ROT, ROPE_THETA)], [tok])
    return tab_a, tab_b, tab_c


def _layer(x, mod, l, lp, tabs, norm_g, lam_rows, subln, w_bg, b_bg, w_out):
    bn, s, _ = x.shape
    shift, scale, gate = [m.reshape(bn, 1, D_MODEL) for m in jnp.split(mod, 3, axis=-1)]
    tab_a, tab_b, tab_c = tabs
    lam_init = 0.8 - 0.6 * math.exp(-0.3 * l)

    a_outs = tuple((A_SLAB, kind, dil, 0) for kind in ("q", "k", "v") for _, dil in A_GROUPS)
    pa = _project(x, shift, scale, norm_g, lp["w_a"], lp["gain_a"], tab_a, a_outs, two_heads=False,
                  head_dim=A_HEAD_DIM, q_scale=A_HEAD_DIM ** -0.5 * LOG2E, tm=min(PROJ_A_TM, s), name="proj_a")
    n_g = len(A_GROUPS)
    oa = [_band_attention(pa[g], pa[n_g + g], pa[2 * n_g + g], dil) for g, (_, dil) in enumerate(A_GROUPS)]

    assert B_HEAD_DIM == C_HEAD_DIM
    bc_outs = ((B_WIDTH, "qT", 1, 0), (B_KV, "k", 1, 0), (B_KV, "vT", B_KV_HEADS, 0),
               (C_WIDTH, "qT", 1, 1), (C_WIDTH, "k", 1, 1), (C_WIDTH, "vT", 1, 1))
    qb, kb, vb, qc, kc, vc = _project(x, shift, scale, norm_g, lp["w_bc"], lp["gain_bc"], (*tab_b, *tab_c), bc_outs,
                                      two_heads=True, head_dim=B_HEAD_DIM, q_scale=B_HEAD_DIM ** -0.5 * LOG2E,
                                      tm=min(PROJ_TM, s), name="proj_bc")
    yb = _gqa_attention(qb, kb, vb)
    yc = _diff_attention(qc, kc, vc, lam_rows, subln, lam_init)

    return _final(x, shift, scale, gate, norm_g, oa, yb, yc, lp["w_z"], w_bg, b_bg, lp["w_oa"], lp["w_ob"],
                  lp["w_oc"], w_out, tm=min(FINAL_TM, s))


def kernel(x_prompt, x_sample, c_prompt, c_sample, norm_g, w_ada, b_ada, w_in, qn_a, kn_a, qn_b, kn_b, qn_c, kn_c,
           lam_q1, lam_k1, lam_q2, lam_k2, subln_c, w_oa, w_ob, w_oc, w_bg, b_bg, w_out):
    groups = ((x_prompt, c_prompt), (x_sample, c_sample))
    rows = [c.shape[0] for _, c in groups]
    pad = -sum(rows) % 8
    c_all = jnp.concatenate([c for _, c in groups] + [jnp.zeros((pad, D_MODEL), F32)], axis=0)
    mod_all = _ada_mod(c_all, w_ada.astype(BF16), b_ada)

    layers = [_prep_layer(w_in[l], qn_a[l], kn_a[l], qn_b[l], kn_b[l], qn_c[l], kn_c[l], w_oa[l], w_ob[l], w_oc[l])
              for l in range(DEPTH)]
    w_bg16, w_out16 = w_bg.astype(BF16), w_out.astype(BF16)

    outs = []
    row0 = 0
    for (x, c), n in zip(groups, rows):
        tabs = _tables(x.shape[1])
        for l in range(DEPTH):
            lam_rows = [p[l][None, :] for p in (lam_q1, lam_k1, lam_q2, lam_k2)]
            x = _layer(x, mod_all[l, row0:row0 + n], l, layers[l], tabs, norm_g[l][None, :], lam_rows,
                       subln_c[l][None, :], w_bg16[l], b_bg[l][None, :], w_out16[l])
        outs.append(x)
        row0 += n
    return tuple(outs)
```

```python
import functools
import math

import jax
import jax.numpy as jnp
import numpy as np
from jax import lax
from jax.experimental import pallas as pl
from jax.experimental.pallas import tpu as pltpu

D_MODEL = 1024
DEPTH = 2
GRID_W = 64
EPS = 1e-6
NEG = -1e30
ROPE_THETA = 500000.0
AXIAL_THETA = 10000.0

A_GROUPS = ((128, 1), (512, 4), (2048, 16))
A_HEADS = 4
A_HEAD_DIM = 96
A_ROT = A_HEAD_DIM // 4
A_BAND = 64
N_A = len(A_GROUPS) * A_HEADS

B_HEADS = 6
B_KV_HEADS = 2
B_HEAD_DIM = 64
C_HEADS = 4
C_HEAD_DIM = 64
C_ROT = C_HEAD_DIM // 4

A_QKV = N_A * A_HEAD_DIM
A_WIDTH = A_HEADS * A_HEAD_DIM
B_WIDTH = B_HEADS * B_HEAD_DIM
B_KV = B_KV_HEADS * B_HEAD_DIM
C_WIDTH = C_HEADS * 2 * C_HEAD_DIM
IN_SIZES = (A_QKV, A_QKV, A_QKV, A_WIDTH, B_WIDTH, B_KV, B_KV, B_WIDTH, C_WIDTH, C_WIDTH, C_WIDTH, C_WIDTH)

LANE = 128
HALF = LANE // 2
ROT_SHIFT = LANE // 2
A_SLAB = A_HEADS * LANE
BF16_SUBLANES = 16
ONES_ROWS = BF16_SUBLANES
FLASH_TK = 512
COL_GROUP = 256
SCORE_SLOTS = 2
FLASH_UNROLL = 4
PROJ_TM = 512
PROJ_A_TM = 256
FINAL_TM = 512
LOG2E = math.log2(math.e)
LN2 = math.log(2.0)
VMEM_LIMIT = 48 * 1024 * 1024

BF16 = jnp.bfloat16
F32 = jnp.float32


def _cparams(n_axes):
    return pltpu.CompilerParams(dimension_semantics=("arbitrary",) * n_axes, vmem_limit_bytes=VMEM_LIMIT)


def _dot(a, b):
    return jnp.dot(a, b, preferred_element_type=F32)


def _dot_nt(a, b):
    return lax.dot_general(a, b, (((1,), (1,)), ((), ())), preferred_element_type=F32)


def _silu(x):
    return x * jax.nn.sigmoid(x)


def _ada_kernel(c_ref, w_ref, b_ref, o_ref):
    c = c_ref[...]
    o_ref[...] = _dot(_silu(c).astype(BF16), w_ref[...]) + b_ref[...]


def _ada_mod(c_all, w_ada, b_ada):
    rows = c_all.shape[0]
    return pl.pallas_call(
        _ada_kernel,
        grid=(DEPTH, 3),
        in_specs=[
            pl.BlockSpec((rows, D_MODEL), lambda l, n: (0, 0)),
            pl.BlockSpec((None, D_MODEL, D_MODEL), lambda l, n: (l, 0, n)),
            pl.BlockSpec((None, 1, D_MODEL), lambda l, n: (l, 0, n)),
        ],
        out_specs=pl.BlockSpec((None, rows, D_MODEL), lambda l, n: (l, 0, n)),
        out_shape=jax.ShapeDtypeStruct((DEPTH, rows, 3 * D_MODEL), F32),
        compiler_params=_cparams(2),
        name="ada_mod",
    )(c_all, w_ada, b_ada.reshape(DEPTH, 1, 3 * D_MODEL))


def _modulated_norm(x, g, scale, shift):
    ms = jnp.mean(x * x, axis=-1, keepdims=True)
    h = x * lax.rsqrt(ms + EPS) * g
    return h * (1.0 + scale) + shift


def _first_head(index):
    return (index & (HALF - 1)) < HALF // 2


def _head_norm_rope(u, gain, cos, sin, two_heads, head_dim):
    sq = u * u
    if two_heads:
        first = _first_head(lax.broadcasted_iota(jnp.int32, (1, LANE), 1))
        s_first = jnp.sum(jnp.where(first, sq, 0.0), axis=-1, keepdims=True)
        s_all = jnp.sum(sq, axis=-1, keepdims=True)
        ms = jnp.where(first, s_first, s_all - s_first) * (1.0 / head_dim)
    else:
        ms = jnp.sum(sq, axis=-1, keepdims=True) * (1.0 / head_dim)
    y = u * lax.rsqrt(ms + EPS) * gain
    return y * cos + pltpu.roll(y, ROT_SHIFT, 1) * sin


def _proj_kernel(x_ref, shift_ref, scale_ref, g_ref, w_ref, gain_ref, *refs,
                 outs, n_tables, two_heads, head_dim, q_scale):
    tables = [(refs[2 * t][...], refs[2 * t + 1][...]) for t in range(n_tables)]
    refs = refs[2 * n_tables:]
    out_refs, stage_ref = refs[:len(outs)], refs[len(outs):]
    h = _modulated_norm(x_ref[...], g_ref[...], scale_ref[...], shift_ref[...]).astype(BF16)
    tm = x_ref.shape[0]
    col = 0
    for o_ref, (width, kind, dil, table) in zip(out_refs, outs):
        cos, sin = tables[table]
        u = _dot(h, w_ref[:, col:col + width])
        if kind == "vT":
            dv = LANE // dil
            for s in range(width // LANE):
                ut = u[:, s * LANE:(s + 1) * LANE].T.astype(BF16)
                for hd in range(dil):
                    base = (s * dil + hd) * (dv + ONES_ROWS)
                    o_ref[base:base + dv, :] = ut[hd * dv:(hd + 1) * dv]
                    o_ref[base + dv:base + dv + ONES_ROWS, :] = jnp.ones((ONES_ROWS, tm), BF16)
        else:
            if kind != "v":
                slabs = []
                for s in range(width // LANE):
                    lo = s * LANE
                    y = _head_norm_rope(u[:, lo:lo + LANE], gain_ref[:, col + lo:col + lo + LANE], cos, sin,
                                        two_heads, head_dim)
                    slabs.append(y * q_scale if kind in ("q", "qT") and q_scale != 1.0 else y)
                u = jnp.concatenate(slabs, axis=-1)
            if kind == "qT":
                for s in range(width // LANE):
                    o_ref[s * LANE:(s + 1) * LANE, :] = u[:, s * LANE:(s + 1) * LANE].T.astype(BF16)
            elif dil == 1:
                o_ref[...] = u.astype(BF16)
            else:
                stage = stage_ref[0]
                for s in range(width // LANE):
                    stage[s] = u[:, s * LANE:(s + 1) * LANE]
                rows = lambda r: pl.ds(r, tm // dil, stride=dil)
                for r in range(dil):
                    for s in range(width // LANE):
                        lo = r * width + s * LANE
                        o_ref[:, lo:lo + LANE] = stage[s, rows(r), :].astype(BF16)
        col += width


def _vt_rows(v_heads):
    return LANE + v_heads * ONES_ROWS


def _proj_out(kind, width, dil, bn, s, tm):
    if kind == "qT":
        return (bn, width, s), pl.BlockSpec((None, width, tm), lambda b, i: (b, 0, i))
    if kind == "vT":
        per = FLASH_TK // tm
        rows = width // LANE * _vt_rows(dil)
        return ((bn, s // FLASH_TK, rows, FLASH_TK),
                pl.BlockSpec((None, None, rows, tm), lambda b, i: (b, i // per, 0, i % per)))
    return (bn, s // dil, dil * width), pl.BlockSpec((None, tm // dil, dil * width), lambda b, i: (b, i, 0))


def _project(x, shift, scale, norm_g, w, gain, tables, outs, *, two_heads, head_dim, q_scale, tm, name):
    bn, s, _ = x.shape
    wcols = w.shape[1]
    kern = functools.partial(_proj_kernel, outs=outs, n_tables=len(tables) // 2, two_heads=two_heads,
                             head_dim=head_dim, q_scale=q_scale)
    vec = pl.BlockSpec((None, 1, D_MODEL), lambda b, i: (b, 0, 0))
    tab = pl.BlockSpec((tm, LANE), lambda b, i: (i, 0))
    shapes, specs = zip(*[_proj_out(kind, width, dil, bn, s, tm) for width, kind, dil, _ in outs])
    n_slabs = max(width for width, _, _, _ in outs) // LANE
    dilated = any(d > 1 and kind != "vT" for _, kind, d, _ in outs)
    stage = [pltpu.VMEM((n_slabs, tm, LANE), F32)] if dilated else []
    return pl.pallas_call(
        kern,
        grid=(bn, s // tm),
        in_specs=[
            pl.BlockSpec((None, tm, D_MODEL), lambda b, i: (b, i, 0)),
            vec, vec,
            pl.BlockSpec((1, D_MODEL), lambda b, i: (0, 0)),
            pl.BlockSpec((D_MODEL, wcols), lambda b, i: (0, 0)),
            pl.BlockSpec((1, wcols), lambda b, i: (0, 0)),
            *[tab] * len(tables),
        ],
        out_specs=list(specs),
        out_shape=[jax.ShapeDtypeStruct(shape, BF16) for shape in shapes],
        scratch_shapes=stage,
        compiler_params=_cparams(2),
        name=name,
    )(x, shift, scale, norm_g, w, gain, *tables)


def _band_attn_kernel(q_ref, kp_ref, kc_ref, kn_ref, vp_ref, vc_ref, vn_ref, bias_ref, o_ref, *, tb, n_blocks):
    i = pl.program_id(2)
    nk = tb + 2 * A_BAND
    colk = lax.broadcasted_iota(jnp.int32, (1, nk), 1)
    first_col = jnp.where(i == 0, A_BAND, 0)
    end_col = jnp.where(i == n_blocks - 1, tb + A_BAND, nk)
    bias = bias_ref[...] + jnp.where((colk < first_col) | (colk >= end_col), NEG, 0.0)
    lane = lax.broadcasted_iota(jnp.int32, (1, LANE), 1)
    heads = [slice(j * LANE, (j + 1) * LANE) for j in range(A_HEADS)]
    band = lambda p_ref, c_ref, n_ref, sl: jnp.concatenate([p_ref[:, sl], c_ref[:, sl], n_ref[:, sl]], axis=0)
    scores = [_dot_nt(q_ref[:, sl], band(kp_ref, kc_ref, kn_ref, sl)) for sl in heads]
    stats = []
    for s in scores:
        s = s + bias
        m = jnp.max(s, axis=-1, keepdims=True)
        p = jnp.exp2(s - m)
        stats.append((m, jnp.sum(p, axis=-1, keepdims=True), p.astype(BF16)))
    for sl, (m, l, p) in zip(heads, stats):
        o = _dot(p, band(vp_ref, vc_ref, vn_ref, sl)) / l
        o_ref[:, sl] = jnp.where(lane >= A_HEAD_DIM, m * LN2 + jnp.log(l), o)


def _band_attention(q, k, v, dilation):
    bn, length, _ = q.shape
    tb = min(256, length)
    nb = length // tb
    per = tb // A_BAND
    cur = pl.BlockSpec((None, tb, A_SLAB), lambda b, r, i: (b, i, r))
    prv = pl.BlockSpec((None, A_BAND, A_SLAB), lambda b, r, i: (b, jnp.maximum(i * per - 1, 0), r))
    nxt = pl.BlockSpec((None, A_BAND, A_SLAB), lambda b, r, i: (b, jnp.minimum((i + 1) * per, nb * per - 1), r))
    nk = tb + 2 * A_BAND
    dist = np.arange(nk)[None, :] - A_BAND - np.arange(tb)[:, None]
    bias = np.where(np.abs(dist) <= A_BAND, 0.0, NEG).astype(np.float32)
    kern = functools.partial(_band_attn_kernel, tb=tb, n_blocks=nb)
    return pl.pallas_call(
        kern,
        grid=(bn, dilation, nb),
        in_specs=[cur, prv, cur, nxt, prv, cur, nxt, pl.BlockSpec((tb, nk), lambda b, r, i: (0, 0))],
        out_specs=cur,
        out_shape=jax.ShapeDtypeStruct((bn, length, dilation * A_SLAB), F32),
        compiler_params=_cparams(3),
        name=f"band_attn_d{dilation}",
    )(q, k, k, k, v, v, v, bias)


def _stacked_flash(qt_ref, k_ref, vt_ref, acc_sc, s_sc, *, tq, n_chunks, v_heads):
    dv = LANE // v_heads
    v_rows = [slice(hd * (dv + ONES_ROWS), (hd + 1) * (dv + ONES_ROWS)) for hd in range(v_heads)]
    first = _first_head(lax.broadcasted_iota(jnp.int32, (LANE, 1), 0))
    qt = qt_ref[...]
    zero = jnp.zeros_like(qt)
    q2t = jnp.concatenate([jnp.where(first, qt, zero), jnp.where(first, zero, qt)], axis=1)
    acc_sc[...] = jnp.zeros(acc_sc.shape, F32)

    wide = v_heads == 1 and n_chunks > FLASH_UNROLL
    width = min(4 * COL_GROUP if wide else COL_GROUP, tq)
    halves = [slice(lo, lo + width) for lo in range(0, 2 * tq, width)]

    def scores(c, slot, cols):
        start = pl.multiple_of(c * FLASH_TK, FLASH_TK)
        s_sc[slot, :, cols] = _dot(k_ref[pl.ds(start, FLASH_TK), :], q2t[:, cols])

    def step(c, slot, m_prev, c_next=None):
        m_next = jnp.maximum(m_prev, jnp.max(s_sc[slot], axis=0, keepdims=True))
        vc = vt_ref[c]
        alpha = jnp.exp2(m_prev - m_next)
        for cols in halves:
            if c_next is not None:
                scores(c_next, 1 - slot, cols)
            pt = jnp.exp2(s_sc[slot, :, cols] - m_next[:, cols]).astype(BF16)
            head = cols.start // tq
            acc_sc[:, cols] = alpha[:, cols] * acc_sc[:, cols] + _dot(vc[v_rows[head % v_heads]], pt)
        return m_next

    unroll = min(FLASH_UNROLL, n_chunks)

    def group(j, m):
        for u in range(unroll):
            c = unroll * j + u
            m = step(c, u % SCORE_SLOTS, m, c + 1)
        return m

    for cols in halves:
        scores(0, 0, cols)
    m = lax.fori_loop(0, n_chunks // unroll - 1, group, jnp.full((1, 2 * tq), -jnp.inf, F32))
    for c in range(n_chunks - unroll, n_chunks):
        m = step(c, c % SCORE_SLOTS, m, c + 1 if c + 1 < n_chunks else None)
    acc = acc_sc[...]
    return acc[:dv] / acc[dv:dv + 1]


def _gqa_kernel(qt_ref, k_ref, vt_ref, o_ref, acc_sc, s_sc, *, tq, n_chunks):
    o = _stacked_flash(qt_ref, k_ref, vt_ref, acc_sc, s_sc, tq=tq, n_chunks=n_chunks, v_heads=B_KV_HEADS)
    o_ref[...] = jnp.concatenate([o[:, :tq], o[:, tq:]], axis=0).T


def _diff_kernel(qt_ref, k_ref, vt_ref, lq1_ref, lk1_ref, lq2_ref, lk2_ref, sub_ref, o_ref, acc_sc,
                 s_sc, *, tq, n_chunks, lam_init):
    o = _stacked_flash(qt_ref, k_ref, vt_ref, acc_sc, s_sc, tq=tq, n_chunks=n_chunks, v_heads=1)
    lam = (jnp.exp(jnp.sum(lq1_ref[...] * lk1_ref[...], axis=-1, keepdims=True))
           - jnp.exp(jnp.sum(lq2_ref[...] * lk2_ref[...], axis=-1, keepdims=True)) + lam_init)
    oc = (o[:, :tq] - lam * o[:, tq:]).T
    ms = jnp.mean(oc * oc, axis=-1, keepdims=True)
    o_ref[...] = oc * lax.rsqrt(ms + EPS) * sub_ref[...] * (1.0 - lam_init)


def _flash_tq(s):
    return min(2048, s)


def _flash_scratch(tq, v_heads):
    return [pltpu.VMEM((LANE // v_heads + ONES_ROWS, 2 * tq), F32),
            pltpu.VMEM((SCORE_SLOTS, FLASH_TK, 2 * tq), F32)]


def _gqa_attention(qt, k, vt):
    bn, _, s = qt.shape
    tq, n_chunks = _flash_tq(s), s // FLASH_TK
    kern = functools.partial(_gqa_kernel, tq=tq, n_chunks=n_chunks)
    return pl.pallas_call(
        kern,
        grid=(bn, B_HEADS // 2, s // tq),
        in_specs=[pl.BlockSpec((None, LANE, tq), lambda b, p, i: (b, p, i)),
                  pl.BlockSpec((None, s, LANE), lambda b, p, i: (b, 0, 0)),
                  pl.BlockSpec((None, n_chunks, _vt_rows(B_KV_HEADS), FLASH_TK), lambda b, p, i: (b, 0, 0, 0))],
        out_specs=pl.BlockSpec((None, tq, LANE), lambda b, p, i: (b, i, p)),
        out_shape=jax.ShapeDtypeStruct((bn, s, B_WIDTH), F32),
        scratch_shapes=_flash_scratch(tq, B_KV_HEADS),
        compiler_params=_cparams(3),
        name="gqa_attn",
    )(qt, k, vt)


def _diff_attention(qt, k, vt, lam_rows, subln, lam_init):
    bn, _, s = qt.shape
    tq, n_chunks = _flash_tq(s), s // FLASH_TK
    kern = functools.partial(_diff_kernel, tq=tq, n_chunks=n_chunks, lam_init=lam_init)
    row = lambda n: pl.BlockSpec((1, n), lambda b, h, i: (0, 0))
    return pl.pallas_call(
        kern,
        grid=(bn, C_HEADS, s // tq),
        in_specs=[pl.BlockSpec((None, LANE, tq), lambda b, h, i: (b, h, i)),
                  pl.BlockSpec((None, s, LANE), lambda b, h, i: (b, 0, h)),
                  pl.BlockSpec((None, n_chunks, _vt_rows(1), FLASH_TK), lambda b, h, i: (b, 0, h, 0)),
                  row(C_HEAD_DIM), row(C_HEAD_DIM), row(C_HEAD_DIM), row(C_HEAD_DIM), row(LANE)],
        out_specs=pl.BlockSpec((None, tq, LANE), lambda b, h, i: (b, i, h)),
        out_shape=jax.ShapeDtypeStruct((bn, s, C_WIDTH), F32),
        scratch_shapes=_flash_scratch(tq, 1),
        compiler_params=_cparams(3),
        name="diff_attn",
    )(qt, k, vt, *lam_rows, subln)


def _final_kernel(x_ref, shift_ref, scale_ref, gate_ref, g_ref, oa0_ref, oa1_ref, oa2_ref, yb_ref, yc_ref,
                  wz_ref, wbg_ref, bbg_ref, woa_ref, wob_ref, woc_ref, wout_ref, o_ref, *stage_refs):
    x = x_ref[...]
    tm = x.shape[0]
    h = _modulated_norm(x, g_ref[...], scale_ref[...], shift_ref[...]).astype(BF16)
    lane = lax.broadcasted_iota(jnp.int32, (1, LANE), 1)

    for oa_ref, st_ref, (_, dil) in zip((oa1_ref, oa2_ref), stage_refs, A_GROUPS[1:]):
        for r in range(dil):
            for j in range(A_HEADS):
                lo = r * A_SLAB + j * LANE
                st_ref[j, pl.ds(r, tm // dil, stride=dil), :] = oa_ref[:, lo:lo + LANE]

    ya = []
    for j in range(A_HEADS):
        slabs = [oa0_ref[:, j * LANE:(j + 1) * LANE]] + [st_ref[j] for st_ref in stage_refs]
        lses = [t[:, A_HEAD_DIM:A_HEAD_DIM + 1] for t in slabs]
        top = jnp.maximum(jnp.maximum(lses[0], lses[1]), lses[2])
        ws = [jnp.exp(t - top) for t in lses]
        mix = (ws[0] * slabs[0] + ws[1] * slabs[1] + ws[2] * slabs[2]) / (ws[0] + ws[1] + ws[2])
        ya.append(jnp.where(lane < A_HEAD_DIM, mix, 0.0))
    ya = jnp.concatenate(ya, axis=-1)

    def branch(y, z_lo, z_hi, wo_ref, g_lo):
        z = _dot(h, wz_ref[:, z_lo:z_hi])
        p = _dot((y * _silu(z)).astype(BF16), wo_ref[...])
        gate = jax.nn.sigmoid(_dot(h, wbg_ref[:, g_lo:g_lo + D_MODEL]) + bbg_ref[:, g_lo:g_lo + D_MODEL])
        return gate * p

    za_hi = A_SLAB
    zb_hi = za_hi + B_WIDTH
    zc_hi = zb_hi + C_WIDTH
    merged = (branch(ya, 0, za_hi, woa_ref, 0)
              + branch(yb_ref[...], za_hi, zb_hi, wob_ref, D_MODEL)
              + branch(yc_ref[...], zb_hi, zc_hi, woc_ref, 2 * D_MODEL))
    out = _dot(merged.astype(BF16), wout_ref[...])
    o_ref[...] = x + gate_ref[...] * out


def _final(x, shift, scale, gate, norm_g, oa, yb, yc, wz, wbg, bbg, woa, wob, woc, wout, *, tm):
    bn, s, _ = x.shape
    tok = lambda w: pl.BlockSpec((None, tm, w), lambda b, i: (b, i, 0))
    vec = pl.BlockSpec((None, 1, D_MODEL), lambda b, i: (b, 0, 0))
    full = lambda a: pl.BlockSpec(a.shape, lambda b, i: (0, 0), pipeline_mode=pl.Buffered(1))
    band = [pl.BlockSpec((None, tm // dil, dil * A_SLAB), lambda b, i: (b, i, 0)) for _, dil in A_GROUPS]
    return pl.pallas_call(
        _final_kernel,
        grid=(bn, s // tm),
        in_specs=[tok(D_MODEL), vec, vec, vec, full(norm_g), *band,
                  tok(B_WIDTH), tok(C_WIDTH), full(wz), full(wbg), full(bbg), full(woa), full(wob), full(woc),
                  full(wout)],
        out_specs=tok(D_MODEL),
        out_shape=jax.ShapeDtypeStruct((bn, s, D_MODEL), F32),
        scratch_shapes=[pltpu.VMEM((A_HEADS, tm, LANE), F32) for _ in A_GROUPS[1:]],
        compiler_params=_cparams(2),
        name="final",
    )(x, shift, scale, gate, norm_g, *oa, yb, yc, wz, wbg, bbg, woa, wob, woc, wout)


B_Q_ORDER = (0, 3, 1, 4, 2, 5)


def _slab_layout(head_dim, n_heads, rot_groups):
    src = -np.ones((LANE,), np.int32)
    group = -np.ones((LANE,), np.int32)
    freq = np.zeros((LANE,), np.int32)
    sign = np.zeros((LANE,), np.float32)
    width = ROT_SHIFT // n_heads
    for hd in range(n_heads):
        low = [(start + i, g, i, -1.0) for g, (start, half) in enumerate(rot_groups) for i in range(half)]
        high = [(start + half + i, g, i, 1.0) for g, (start, half) in enumerate(rot_groups) for i in range(half)]
        rotary = {d for d, _, _, _ in low + high}
        rest = [(d, -1, 0, 0.0) for d in range(head_dim) if d not in rotary]
        n_low_rest = width - len(low)
        low, high = low + rest[:n_low_rest], high + rest[n_low_rest:]
        assert len(low) == width and len(high) <= width
        for base, items in ((hd * width, low), (hd * width + ROT_SHIFT, high)):
            for lane, (d, g, i, sg) in enumerate(items, start=base):
                src[lane], group[lane], freq[lane], sign[lane] = hd * head_dim + d, g, i, sg
    return src, group, freq, sign


LAYOUT_A = _slab_layout(A_HEAD_DIM, 1, ((0, A_ROT // 2),))
LAYOUT_B = _slab_layout(B_HEAD_DIM, 2, ((0, B_HEAD_DIM // 4), (B_HEAD_DIM // 2, B_HEAD_DIM // 4)))
LAYOUT_C = _slab_layout(C_HEAD_DIM, 2, ((0, C_ROT // 2),))


def _to_slabs(w, layout, dims_per_slab):
    src = layout[0]
    n_slabs = w.shape[-1] // dims_per_slab
    idx = np.concatenate([np.maximum(src, 0) + s * dims_per_slab for s in range(n_slabs)])
    keep = np.tile(src >= 0, n_slabs)
    return jnp.where(keep, jnp.take(w, idx, axis=-1), 0.0)


def _pad_heads(w, n_heads):
    lead = w.shape[:-1]
    w = w.reshape(*lead, n_heads, A_HEAD_DIM)
    w = jnp.pad(w, [(0, 0)] * len(lead) + [(0, 0), (0, LANE - A_HEAD_DIM)])
    return w.reshape(*lead, n_heads * LANE)


def _permute_heads(w, order, dim):
    lead = w.shape[:-1]
    w = w.reshape(*lead, len(order), dim)
    return w[..., jnp.array(order), :].reshape(*lead, len(order) * dim)


def _prep_layer(w_in, qn_a, kn_a, qn_b, kn_b, qn_c, kn_c, w_oa, w_ob, w_oc):
    offs = [0]
    for n in IN_SIZES:
        offs.append(offs[-1] + n)
    qa, ka, va, za, qb, kb, vb, zb, qc, kc, vc, zc = [w_in[:, offs[i]:offs[i + 1]] for i in range(len(IN_SIZES))]

    slab_a = functools.partial(_to_slabs, layout=LAYOUT_A, dims_per_slab=A_HEAD_DIM)
    slab_b = functools.partial(_to_slabs, layout=LAYOUT_B, dims_per_slab=2 * B_HEAD_DIM)
    slab_c = functools.partial(_to_slabs, layout=LAYOUT_C, dims_per_slab=2 * C_HEAD_DIM)
    qb = _permute_heads(qb, B_Q_ORDER, B_HEAD_DIM)

    w_a = jnp.concatenate([slab_a(qa), slab_a(ka), _pad_heads(va, N_A)], axis=-1).astype(BF16)
    gain_a = jnp.concatenate([slab_a(jnp.tile(qn_a, N_A)), slab_a(jnp.tile(kn_a, N_A)),
                              jnp.zeros((N_A * LANE,), F32)])[None, :]

    w_b = jnp.concatenate([slab_b(qb), slab_b(kb), vb], axis=-1).astype(BF16)
    gain_b = jnp.concatenate([slab_b(jnp.tile(qn_b, B_HEADS)), slab_b(jnp.tile(kn_b, B_KV_HEADS)),
                              jnp.zeros((B_KV,), F32)])[None, :]

    w_c = jnp.concatenate([slab_c(qc), slab_c(kc), vc], axis=-1).astype(BF16)
    gain_c = jnp.concatenate([slab_c(jnp.tile(qn_c, 2 * C_HEADS)), slab_c(jnp.tile(kn_c, 2 * C_HEADS)),
                              jnp.zeros((C_WIDTH,), F32)])[None, :]

    w_z = jnp.concatenate([_pad_heads(za, A_HEADS), _permute_heads(zb, B_Q_ORDER, B_HEAD_DIM), zc],
                          axis=-1).astype(BF16)
    w_oa_p = _pad_heads(w_oa.T, A_HEADS).T.astype(BF16)
    w_ob_p = _permute_heads(w_ob.T, B_Q_ORDER, B_HEAD_DIM).T.astype(BF16)
    return dict(w_a=w_a, gain_a=gain_a, w_bc=jnp.concatenate([w_b, w_c], axis=-1),
                gain_bc=jnp.concatenate([gain_b, gain_c], axis=-1), w_z=w_z,
                w_oa=w_oa_p, w_ob=w_ob_p, w_oc=w_oc.astype(BF16))


def _inv_freq(dim, theta):
    return theta ** (-jnp.arange(0, dim, 2, dtype=F32) / dim)


def _rope_tables(layout, inv_freqs, streams):
    _, group, freq, sign = layout
    ang = 0.0
    for g, (inv, pos) in enumerate(zip(inv_freqs, streams)):
        ang = ang + pos[:, None] * (inv[freq] * (group == g))[None, :]
    return jnp.cos(ang), jnp.sin(ang) * sign[None, :]


def _tables(s):
    pos = jnp.arange(s)
    tok, row, col = [t.astype(F32) for t in (pos, pos // GRID_W, pos % GRID_W)]
    inv_b = _inv_freq(B_HEAD_DIM // 2, AXIAL_THETA)
    tab_a = _rope_tables(LAYOUT_A, [_inv_freq(A_ROT, ROPE_THETA)], [tok])
    tab_b = _rope_tables(LAYOUT_B, [inv_b, inv_b], [row, col])
    tab_c = _rope_tables(LAYOUT_C, [_inv_freq(C_ROT, ROPE_THETA)], [tok])
    return tab_a, tab_b, tab_c


def _layer(x, mod, l, lp, tabs, norm_g, lam_rows, subln, w_bg, b_bg, w_out):
    bn, s, _ = x.shape
    shift, scale, gate = [m.reshape(bn, 1, D_MODEL) for m in jnp.split(mod, 3, axis=-1)]
    tab_a, tab_b, tab_c = tabs
    lam_init = 0.8 - 0.6 * math.exp(-0.3 * l)

    a_outs = tuple((A_SLAB, kind, dil, 0) for kind in ("q", "k", "v") for _, dil in A_GROUPS)
    pa = _project(x, shift, scale, norm_g, lp["w_a"], lp["gain_a"], tab_a, a_outs, two_heads=False,
                  head_dim=A_HEAD_DIM, q_scale=A_HEAD_DIM ** -0.5 * LOG2E, tm=min(PROJ_A_TM, s), name="proj_a")
    n_g = len(A_GROUPS)
    oa = [_band_attention(pa[g], pa[n_g + g], pa[2 * n_g + g], dil) for g, (_, dil) in enumerate(A_GROUPS)]

    assert B_HEAD_DIM == C_HEAD_DIM
    bc_outs = ((B_WIDTH, "qT", 1, 0), (B_KV, "k", 1, 0), (B_KV, "vT", B_KV_HEADS, 0),
               (C_WIDTH, "qT", 1, 1), (C_WIDTH, "k", 1, 1), (C_WIDTH, "vT", 1, 1))
    qb, kb, vb, qc, kc, vc = _project(x, shift, scale, norm_g, lp["w_bc"], lp["gain_bc"], (*tab_b, *tab_c), bc_outs,
                                      two_heads=True, head_dim=B_HEAD_DIM, q_scale=B_HEAD_DIM ** -0.5 * LOG2E,
                                      tm=min(PROJ_TM, s), name="proj_bc")
    yb = _gqa_attention(qb, kb, vb)
    yc = _diff_attention(qc, kc, vc, lam_rows, subln, lam_init)

    return _final(x, shift, scale, gate, norm_g, oa, yb, yc, lp["w_z"], w_bg, b_bg, lp["w_oa"], lp["w_ob"],
                  lp["w_oc"], w_out, tm=min(FINAL_TM, s))


def kernel(x_prompt, x_sample, c_prompt, c_sample, norm_g, w_ada, b_ada, w_in, qn_a, kn_a, qn_b, kn_b, qn_c, kn_c,
           lam_q1, lam_k1, lam_q2, lam_k2, subln_c, w_oa, w_ob, w_oc, w_bg, b_bg, w_out):
    groups = ((x_prompt, c_prompt), (x_sample, c_sample))
    rows = [c.shape[0] for _, c in groups]
    pad = -sum(rows) % 8
    c_all = jnp.concatenate([c for _, c in groups] + [jnp.zeros((pad, D_MODEL), F32)], axis=0)
    mod_all = _ada_mod(c_all, w_ada.astype(BF16), b_ada)

    layers = [_prep_layer(w_in[l], qn_a[l], kn_a[l], qn_b[l], kn_b[l], qn_c[l], kn_c[l], w_oa[l], w_ob[l], w_oc[l])
              for l in range(DEPTH)]
    w_bg16, w_out16 = w_bg.astype(BF16), w_out.astype(BF16)

    outs = []
    row0 = 0
    for (x, c), n in zip(groups, rows):
        tabs = _tables(x.shape[1])
        for l in range(DEPTH):
            lam_rows = [p[l][None, :] for p in (lam_q1, lam_k1, lam_q2, lam_k2)]
            x = _layer(x, mod_all[l, row0:row0 + n], l, layers[l], tabs, norm_g[l][None, :], lam_rows,
                       subln_c[l][None, :], w_bg16[l], b_bg[l][None, :], w_out16[l])
        outs.append(x)
        row0 += n
    return tuple(outs)
```

```python
import functools
import math

import jax
import jax.numpy as jnp
import numpy as np
from jax import lax
from jax.experimental import pallas as pl
from jax.experimental.pallas import tpu as pltpu

D_MODEL = 1024
DEPTH = 2
GRID_W = 64
EPS = 1e-6
NEG = -1e30
ROPE_THETA = 500000.0
AXIAL_THETA = 10000.0

A_GROUPS = ((128, 1), (512, 4), (2048, 16))
A_HEADS = 4
A_HEAD_DIM = 96
A_ROT = A_HEAD_DIM // 4
A_BAND = 64
N_A = len(A_GROUPS) * A_HEADS

B_HEADS = 6
B_KV_HEADS = 2
B_HEAD_DIM = 64
C_HEADS = 4
C_HEAD_DIM = 64
C_ROT = C_HEAD_DIM // 4

A_QKV = N_A * A_HEAD_DIM
A_WIDTH = A_HEADS * A_HEAD_DIM
B_WIDTH = B_HEADS * B_HEAD_DIM
B_KV = B_KV_HEADS * B_HEAD_DIM
C_WIDTH = C_HEADS * 2 * C_HEAD_DIM
IN_SIZES = (A_QKV, A_QKV, A_QKV, A_WIDTH, B_WIDTH, B_KV, B_KV, B_WIDTH, C_WIDTH, C_WIDTH, C_WIDTH, C_WIDTH)

LANE = 128
HALF = LANE // 2
ROT_SHIFT = LANE // 2
A_SLAB = A_HEADS * LANE
BF16_SUBLANES = 16
ONES_ROWS = BF16_SUBLANES
FLASH_TK = 512
COL_GROUP = 256
SCORE_SLOTS = 2
FLASH_UNROLL = 4
PROJ_TM = 512
PROJ_A_TM = 256
FINAL_TM = 512
LOG2E = math.log2(math.e)
LN2 = math.log(2.0)
VMEM_LIMIT = 48 * 1024 * 1024

BF16 = jnp.bfloat16
F32 = jnp.float32


def _cparams(n_axes):
    return pltpu.CompilerParams(dimension_semantics=("arbitrary",) * n_axes, vmem_limit_bytes=VMEM_LIMIT)


def _dot(a, b):
    return jnp.dot(a, b, preferred_element_type=F32)


def _dot_nt(a, b):
    return lax.dot_general(a, b, (((1,), (1,)), ((), ())), preferred_element_type=F32)


def _silu(x):
    return x * jax.nn.sigmoid(x)


def _ada_kernel(c_ref, w_ref, b_ref, o_ref):
    c = c_ref[...]
    o_ref[...] = _dot(_silu(c).astype(BF16), w_ref[...]) + b_ref[...]


def _ada_mod(c_all, w_ada, b_ada):
    rows = c_all.shape[0]
    return pl.pallas_call(
        _ada_kernel,
        grid=(DEPTH, 3),
        in_specs=[
            pl.BlockSpec((rows, D_MODEL), lambda l, n: (0, 0)),
            pl.BlockSpec((None, D_MODEL, D_MODEL), lambda l, n: (l, 0, n)),
            pl.BlockSpec((None, 1, D_MODEL), lambda l, n: (l, 0, n)),
        ],
        out_specs=pl.BlockSpec((None, rows, D_MODEL), lambda l, n: (l, 0, n)),
        out_shape=jax.ShapeDtypeStruct((DEPTH, rows, 3 * D_MODEL), F32),
        compiler_params=_cparams(2),
        name="ada_mod",
    )(c_all, w_ada, b_ada.reshape(DEPTH, 1, 3 * D_MODEL))


def _modulated_norm(x, g, scale, shift):
    ms = jnp.mean(x * x, axis=-1, keepdims=True)
    h = x * lax.rsqrt(ms + EPS) * g
    return h * (1.0 + scale) + shift


def _first_head(index):
    return (index & (HALF - 1)) < HALF // 2


def _head_norm_rope(u, gain, cos, sin, two_heads, head_dim):
    sq = u * u
    if two_heads:
        first = _first_head(lax.broadcasted_iota(jnp.int32, (1, LANE), 1))
        s_first = jnp.sum(jnp.where(first, sq, 0.0), axis=-1, keepdims=True)
        s_all = jnp.sum(sq, axis=-1, keepdims=True)
        ms = jnp.where(first, s_first, s_all - s_first) * (1.0 / head_dim)
    else:
        ms = jnp.sum(sq, axis=-1, keepdims=True) * (1.0 / head_dim)
    y = u * lax.rsqrt(ms + EPS) * gain
    return y * cos + pltpu.roll(y, ROT_SHIFT, 1) * sin


def _proj_kernel(x_ref, shift_ref, scale_ref, g_ref, w_ref, gain_ref, *refs,
                 outs, n_tables, two_heads, head_dim, q_scale):
    tables = [(refs[2 * t][...], refs[2 * t + 1][...]) for t in range(n_tables)]
    refs = refs[2 * n_tables:]
    out_refs, stage_ref = refs[:len(outs)], refs[len(outs):]
    h = _modulated_norm(x_ref[...], g_ref[...], scale_ref[...], shift_ref[...]).astype(BF16)
    tm = x_ref.shape[0]
    col = 0
    for o_ref, (width, kind, dil, table) in zip(out_refs, outs):
        cos, sin = tables[table]
        u = _dot(h, w_ref[:, col:col + width])
        if kind == "vT":
            dv = LANE // dil
            for s in range(width // LANE):
                ut = u[:, s * LANE:(s + 1) * LANE].T.astype(BF16)
                for hd in range(dil):
                    base = (s * dil + hd) * (dv + ONES_ROWS)
                    o_ref[base:base + dv, :] = ut[hd * dv:(hd + 1) * dv]
                    o_ref[base + dv:base + dv + ONES_ROWS, :] = jnp.ones((ONES_ROWS, tm), BF16)
        else:
            if kind != "v":
                slabs = []
                for s in range(width // LANE):
                    lo = s * LANE
                    y = jnp.concatenate(
                        [_head_norm_rope(u[rs, lo:lo + LANE], gain_ref[:, col + lo:col + lo + LANE], cos[rs], sin[rs],
                                         two_heads, head_dim)
                         for rs in (slice(0, tm // 2), slice(tm // 2, tm))], axis=0)
                    slabs.append(y * q_scale if kind in ("q", "qT") and q_scale != 1.0 else y)
                u = jnp.concatenate(slabs, axis=-1)
            if kind == "qT":
                for s in range(width // LANE):
                    o_ref[s * LANE:(s + 1) * LANE, :] = u[:, s * LANE:(s + 1) * LANE].T.astype(BF16)
            elif dil == 1:
                o_ref[...] = u.astype(BF16)
            else:
                stage = stage_ref[0]
                for s in range(width // LANE):
                    stage[s] = u[:, s * LANE:(s + 1) * LANE]
                rows = lambda r: pl.ds(r, tm // dil, stride=dil)
                for r in range(dil):
                    for s in range(width // LANE):
                        lo = r * width + s * LANE
                        o_ref[:, lo:lo + LANE] = stage[s, rows(r), :].astype(BF16)
        col += width


def _vt_rows(v_heads):
    return LANE + v_heads * ONES_ROWS


def _proj_out(kind, width, dil, bn, s, tm):
    if kind == "qT":
        return (bn, width, s), pl.BlockSpec((None, width, tm), lambda b, i: (b, 0, i))
    if kind == "vT":
        per = FLASH_TK // tm
        rows = width // LANE * _vt_rows(dil)
        return ((bn, s // FLASH_TK, rows, FLASH_TK),
                pl.BlockSpec((None, None, rows, tm), lambda b, i: (b, i // per, 0, i % per)))
    return (bn, s // dil, dil * width), pl.BlockSpec((None, tm // dil, dil * width), lambda b, i: (b, i, 0))


def _project(x, shift, scale, norm_g, w, gain, tables, outs, *, two_heads, head_dim, q_scale, tm, name):
    bn, s, _ = x.shape
    wcols = w.shape[1]
    kern = functools.partial(_proj_kernel, outs=outs, n_tables=len(tables) // 2, two_heads=two_heads,
                             head_dim=head_dim, q_scale=q_scale)
    vec = pl.BlockSpec((None, 1, D_MODEL), lambda b, i: (b, 0, 0))
    tab = pl.BlockSpec((tm, LANE), lambda b, i: (i, 0))
    shapes, specs = zip(*[_proj_out(kind, width, dil, bn, s, tm) for width, kind, dil, _ in outs])
    n_slabs = max(width for width, _, _, _ in outs) // LANE
    dilated = any(d > 1 and kind != "vT" for _, kind, d, _ in outs)
    stage = [pltpu.VMEM((n_slabs, tm, LANE), F32)] if dilated else []
    return pl.pallas_call(
        kern,
        grid=(bn, s // tm),
        in_specs=[
            pl.BlockSpec((None, tm, D_MODEL), lambda b, i: (b, i, 0)),
            vec, vec,
            pl.BlockSpec((1, D_MODEL), lambda b, i: (0, 0)),
            pl.BlockSpec((D_MODEL, wcols), lambda b, i: (0, 0)),
            pl.BlockSpec((1, wcols), lambda b, i: (0, 0)),
            *[tab] * len(tables),
        ],
        out_specs=list(specs),
        out_shape=[jax.ShapeDtypeStruct(shape, BF16) for shape in shapes],
        scratch_shapes=stage,
        compiler_params=_cparams(2),
        name=name,
    )(x, shift, scale, norm_g, w, gain, *tables)


def _band_attn_kernel(q_ref, kp_ref, kc_ref, kn_ref, vp_ref, vc_ref, vn_ref, bias_ref, o_ref, *, tb, n_blocks):
    i = pl.program_id(2)
    nk = tb + 2 * A_BAND
    colk = lax.broadcasted_iota(jnp.int32, (1, nk), 1)
    first_col = jnp.where(i == 0, A_BAND, 0)
    end_col = jnp.where(i == n_blocks - 1, tb + A_BAND, nk)
    bias = bias_ref[...] + jnp.where((colk < first_col) | (colk >= end_col), NEG, 0.0)
    lane = lax.broadcasted_iota(jnp.int32, (1, LANE), 1)
    heads = [slice(j * LANE, (j + 1) * LANE) for j in range(A_HEADS)]
    band = lambda p_ref, c_ref, n_ref, sl: jnp.concatenate([p_ref[:, sl], c_ref[:, sl], n_ref[:, sl]], axis=0)
    scores = [_dot_nt(q_ref[:, sl], band(kp_ref, kc_ref, kn_ref, sl)) for sl in heads]
    stats = []
    for s in scores:
        s = s + bias
        m = jnp.max(s, axis=-1, keepdims=True)
        p = jnp.exp2(s - m)
        stats.append((m, jnp.sum(p, axis=-1, keepdims=True), p.astype(BF16)))
    for sl, (m, l, p) in zip(heads, stats):
        o = _dot(p, band(vp_ref, vc_ref, vn_ref, sl)) / l
        o_ref[:, sl] = jnp.where(lane >= A_HEAD_DIM, m * LN2 + jnp.log(l), o)


def _band_attention(q, k, v, dilation):
    bn, length, _ = q.shape
    tb = min(256, length)
    nb = length // tb
    per = tb // A_BAND
    cur = pl.BlockSpec((None, tb, A_SLAB), lambda b, r, i: (b, i, r))
    prv = pl.BlockSpec((None, A_BAND, A_SLAB), lambda b, r, i: (b, jnp.maximum(i * per - 1, 0), r))
    nxt = pl.BlockSpec((None, A_BAND, A_SLAB), lambda b, r, i: (b, jnp.minimum((i + 1) * per, nb * per - 1), r))
    nk = tb + 2 * A_BAND
    dist = np.arange(nk)[None, :] - A_BAND - np.arange(tb)[:, None]
    bias = np.where(np.abs(dist) <= A_BAND, 0.0, NEG).astype(np.float32)
    kern = functools.partial(_band_attn_kernel, tb=tb, n_blocks=nb)
    return pl.pallas_call(
        kern,
        grid=(bn, dilation, nb),
        in_specs=[cur, prv, cur, nxt, prv, cur, nxt, pl.BlockSpec((tb, nk), lambda b, r, i: (0, 0))],
        out_specs=cur,
        out_shape=jax.ShapeDtypeStruct((bn, length, dilation * A_SLAB), F32),
        compiler_params=_cparams(3),
        name=f"band_attn_d{dilation}",
    )(q, k, k, k, v, v, v, bias)


def _stacked_flash(qt_ref, k_ref, vt_ref, acc_sc, s_sc, *, tq, n_chunks, v_heads):
    dv = LANE // v_heads
    v_rows = [slice(hd * (dv + ONES_ROWS), (hd + 1) * (dv + ONES_ROWS)) for hd in range(v_heads)]
    first = _first_head(lax.broadcasted_iota(jnp.int32, (LANE, 1), 0))
    qt = qt_ref[...]
    zero = jnp.zeros_like(qt)
    q2t = jnp.concatenate([jnp.where(first, qt, zero), jnp.where(first, zero, qt)], axis=1)
    acc_sc[...] = jnp.zeros(acc_sc.shape, F32)

    wide = v_heads == 1 and n_chunks > FLASH_UNROLL
    width = min(2 * COL_GROUP if wide else COL_GROUP, tq)
    halves = [slice(lo, lo + width) for lo in range(0, 2 * tq, width)]

    def scores(c, slot, cols):
        start = pl.multiple_of(c * FLASH_TK, FLASH_TK)
        s_sc[slot, :, cols] = _dot(k_ref[pl.ds(start, FLASH_TK), :], q2t[:, cols])

    def step(c, slot, m_prev, c_next=None):
        m_next = jnp.maximum(m_prev, jnp.max(s_sc[slot], axis=0, keepdims=True))
        vc = vt_ref[c]
        alpha = jnp.exp2(m_prev - m_next)
        for cols in halves:
            if c_next is not None:
                scores(c_next, 1 - slot, cols)
            pt = jnp.exp2(s_sc[slot, :, cols] - m_next[:, cols]).astype(BF16)
            head = cols.start // tq
            acc_sc[:, cols] = alpha[:, cols] * acc_sc[:, cols] + _dot(vc[v_rows[head % v_heads]], pt)
        return m_next

    unroll = min(FLASH_UNROLL, n_chunks)

    def group(j, m):
        for u in range(unroll):
            c = unroll * j + u
            m = step(c, u % SCORE_SLOTS, m, c + 1)
        return m

    for cols in halves:
        scores(0, 0, cols)
    m = lax.fori_loop(0, n_chunks // unroll - 1, group, jnp.full((1, 2 * tq), -jnp.inf, F32))
    for c in range(n_chunks - unroll, n_chunks):
        m = step(c, c % SCORE_SLOTS, m, c + 1 if c + 1 < n_chunks else None)
    acc = acc_sc[...]
    return acc[:dv] / acc[dv:dv + 1]


def _gqa_kernel(qt_ref, k_ref, vt_ref, o_ref, acc_sc, s_sc, *, tq, n_chunks):
    o = _stacked_flash(qt_ref, k_ref, vt_ref, acc_sc, s_sc, tq=tq, n_chunks=n_chunks, v_heads=B_KV_HEADS)
    o_ref[...] = jnp.concatenate([o[:, :tq], o[:, tq:]], axis=0).T


def _diff_kernel(qt_ref, k_ref, vt_ref, lq1_ref, lk1_ref, lq2_ref, lk2_ref, sub_ref, o_ref, acc_sc,
                 s_sc, *, tq, n_chunks, lam_init):
    o = _stacked_flash(qt_ref, k_ref, vt_ref, acc_sc, s_sc, tq=tq, n_chunks=n_chunks, v_heads=1)
    lam = (jnp.exp(jnp.sum(lq1_ref[...] * lk1_ref[...], axis=-1, keepdims=True))
           - jnp.exp(jnp.sum(lq2_ref[...] * lk2_ref[...], axis=-1, keepdims=True)) + lam_init)
    oc = (o[:, :tq] - lam * o[:, tq:]).T
    ms = jnp.mean(oc * oc, axis=-1, keepdims=True)
    o_ref[...] = oc * lax.rsqrt(ms + EPS) * sub_ref[...] * (1.0 - lam_init)


def _flash_tq(s):
    return min(2048, s)


def _flash_scratch(tq, v_heads):
    return [pltpu.VMEM((LANE // v_heads + ONES_ROWS, 2 * tq), F32),
            pltpu.VMEM((SCORE_SLOTS, FLASH_TK, 2 * tq), F32)]


def _gqa_attention(qt, k, vt):
    bn, _, s = qt.shape
    tq, n_chunks = _flash_tq(s), s // FLASH_TK
    kern = functools.partial(_gqa_kernel, tq=tq, n_chunks=n_chunks)
    return pl.pallas_call(
        kern,
        grid=(bn, B_HEADS // 2, s // tq),
        in_specs=[pl.BlockSpec((None, LANE, tq), lambda b, p, i: (b, p, i)),
                  pl.BlockSpec((None, s, LANE), lambda b, p, i: (b, 0, 0)),
                  pl.BlockSpec((None, n_chunks, _vt_rows(B_KV_HEADS), FLASH_TK), lambda b, p, i: (b, 0, 0, 0))],
        out_specs=pl.BlockSpec((None, tq, LANE), lambda b, p, i: (b, i, p)),
        out_shape=jax.ShapeDtypeStruct((bn, s, B_WIDTH), F32),
        scratch_shapes=_flash_scratch(tq, B_KV_HEADS),
        compiler_params=_cparams(3),
        name="gqa_attn",
    )(qt, k, vt)


def _diff_attention(qt, k, vt, lam_rows, subln, lam_init):
    bn, _, s = qt.shape
    tq, n_chunks = _flash_tq(s), s // FLASH_TK
    kern = functools.partial(_diff_kernel, tq=tq, n_chunks=n_chunks, lam_init=lam_init)
    row = lambda n: pl.BlockSpec((1, n), lambda b, h, i: (0, 0))
    return pl.pallas_call(
        kern,
        grid=(bn, C_HEADS, s // tq),
        in_specs=[pl.BlockSpec((None, LANE, tq), lambda b, h, i: (b, h, i)),
                  pl.BlockSpec((None, s, LANE), lambda b, h, i: (b, 0, h)),
                  pl.BlockSpec((None, n_chunks, _vt_rows(1), FLASH_TK), lambda b, h, i: (b, 0, h, 0)),
                  row(C_HEAD_DIM), row(C_HEAD_DIM), row(C_HEAD_DIM), row(C_HEAD_DIM), row(LANE)],
        out_specs=pl.BlockSpec((None, tq, LANE), lambda b, h, i: (b, i, h)),
        out_shape=jax.ShapeDtypeStruct((bn, s, C_WIDTH), F32),
        scratch_shapes=_flash_scratch(tq, 1),
        compiler_params=_cparams(3),
        name="diff_attn",
    )(qt, k, vt, *lam_rows, subln)


def _final_kernel(x_ref, shift_ref, scale_ref, gate_ref, g_ref, oa0_ref, oa1_ref, oa2_ref, yb_ref, yc_ref,
                  wz_ref, wbg_ref, bbg_ref, woa_ref, wob_ref, woc_ref, wout_ref, o_ref, *stage_refs):
    x = x_ref[...]
    tm = x.shape[0]
    h = _modulated_norm(x, g_ref[...], scale_ref[...], shift_ref[...]).astype(BF16)
    lane = lax.broadcasted_iota(jnp.int32, (1, LANE), 1)

    for oa_ref, st_ref, (_, dil) in zip((oa1_ref, oa2_ref), stage_refs, A_GROUPS[1:]):
        for r in range(dil):
            for j in range(A_HEADS):
                lo = r * A_SLAB + j * LANE
                st_ref[j, pl.ds(r, tm // dil, stride=dil), :] = oa_ref[:, lo:lo + LANE]

    ya = []
    for j in range(A_HEADS):
        slabs = [oa0_ref[:, j * LANE:(j + 1) * LANE]] + [st_ref[j] for st_ref in stage_refs]
        lses = [t[:, A_HEAD_DIM:A_HEAD_DIM + 1] for t in slabs]
        top = jnp.maximum(jnp.maximum(lses[0], lses[1]), lses[2])
        ws = [jnp.exp(t - top) for t in lses]
        mix = (ws[0] * slabs[0] + ws[1] * slabs[1] + ws[2] * slabs[2]) / (ws[0] + ws[1] + ws[2])
        ya.append(jnp.where(lane < A_HEAD_DIM, mix, 0.0))
    ya = jnp.concatenate(ya, axis=-1)

    def branch(y, z_lo, z_hi, wo_ref, g_lo):
        z = _dot(h, wz_ref[:, z_lo:z_hi])
        p = _dot((y * _silu(z)).astype(BF16), wo_ref[...])
        gate = jax.nn.sigmoid(_dot(h, wbg_ref[:, g_lo:g_lo + D_MODEL]) + bbg_ref[:, g_lo:g_lo + D_MODEL])
        return gate * p

    za_hi = A_SLAB
    zb_hi = za_hi + B_WIDTH
    zc_hi = zb_hi + C_WIDTH
    merged = (branch(ya, 0, za_hi, woa_ref, 0)
              + branch(yb_ref[...], za_hi, zb_hi, wob_ref, D_MODEL)
              + branch(yc_ref[...], zb_hi, zc_hi, woc_ref, 2 * D_MODEL))
    out = _dot(merged.astype(BF16), wout_ref[...])
    o_ref[...] = x + gate_ref[...] * out


def _final(x, shift, scale, gate, norm_g, oa, yb, yc, wz, wbg, bbg, woa, wob, woc, wout, *, tm):
    bn, s, _ = x.shape
    tok = lambda w: pl.BlockSpec((None, tm, w), lambda b, i: (b, i, 0))
    vec = pl.BlockSpec((None, 1, D_MODEL), lambda b, i: (b, 0, 0))
    full = lambda a: pl.BlockSpec(a.shape, lambda b, i: (0, 0), pipeline_mode=pl.Buffered(1))
    band = [pl.BlockSpec((None, tm // dil, dil * A_SLAB), lambda b, i: (b, i, 0)) for _, dil in A_GROUPS]
    return pl.pallas_call(
        _final_kernel,
        grid=(bn, s // tm),
        in_specs=[tok(D_MODEL), vec, vec, vec, full(norm_g), *band,
                  tok(B_WIDTH), tok(C_WIDTH), full(wz), full(wbg), full(bbg), full(woa), full(wob), full(woc),
                  full(wout)],
        out_specs=tok(D_MODEL),
        out_shape=jax.ShapeDtypeStruct((bn, s, D_MODEL), F32),
        scratch_shapes=[pltpu.VMEM((A_HEADS, tm, LANE), F32) for _ in A_GROUPS[1:]],
        compiler_params=_cparams(2),
        name="final",
    )(x, shift, scale, gate, norm_g, *oa, yb, yc, wz, wbg, bbg, woa, wob, woc, wout)


B_Q_ORDER = (0, 3, 1, 4, 2, 5)


def _slab_layout(head_dim, n_heads, rot_groups):
    src = -np.ones((LANE,), np.int32)
    group = -np.ones((LANE,), np.int32)
    freq = np.zeros((LANE,), np.int32)
    sign = np.zeros((LANE,), np.float32)
    width = ROT_SHIFT // n_heads
    for hd in range(n_heads):
        low = [(start + i, g, i, -1.0) for g, (start, half) in enumerate(rot_groups) for i in range(half)]
        high = [(start + half + i, g, i, 1.0) for g, (start, half) in enumerate(rot_groups) for i in range(half)]
        rotary = {d for d, _, _, _ in low + high}
        rest = [(d, -1, 0, 0.0) for d in range(head_dim) if d not in rotary]
        n_low_rest = width - len(low)
        low, high = low + rest[:n_low_rest], high + rest[n_low_rest:]
        assert len(low) == width and len(high) <= width
        for base, items in ((hd * width, low), (hd * width + ROT_SHIFT, high)):
            for lane, (d, g, i, sg) in enumerate(items, start=base):
                src[lane], group[lane], freq[lane], sign[lane] = hd * head_dim + d, g, i, sg
    return src, group, freq, sign


LAYOUT_A = _slab_layout(A_HEAD_DIM, 1, ((0, A_ROT // 2),))
LAYOUT_B = _slab_layout(B_HEAD_DIM, 2, ((0, B_HEAD_DIM // 4), (B_HEAD_DIM // 2, B_HEAD_DIM // 4)))
LAYOUT_C = _slab_layout(C_HEAD_DIM, 2, ((0, C_ROT // 2),))


def _to_slabs(w, layout, dims_per_slab):
    src = layout[0]
    n_slabs = w.shape[-1] // dims_per_slab
    idx = np.concatenate([np.maximum(src, 0) + s * dims_per_slab for s in range(n_slabs)])
    keep = np.tile(src >= 0, n_slabs)
    return jnp.where(keep, jnp.take(w, idx, axis=-1), 0.0)


def _pad_heads(w, n_heads):
    lead = w.shape[:-1]
    w = w.reshape(*lead, n_heads, A_HEAD_DIM)
    w = jnp.pad(w, [(0, 0)] * len(lead) + [(0, 0), (0, LANE - A_HEAD_DIM)])
    return w.reshape(*lead, n_heads * LANE)


def _permute_heads(w, order, dim):
    lead = w.shape[:-1]
    w = w.reshape(*lead, len(order), dim)
    return w[..., jnp.array(order), :].reshape(*lead, len(order) * dim)


def _prep_layer(w_in, qn_a, kn_a, qn_b, kn_b, qn_c, kn_c, w_oa, w_ob, w_oc):
    offs = [0]
    for n in IN_SIZES:
        offs.append(offs[-1] + n)
    qa, ka, va, za, qb, kb, vb, zb, qc, kc, vc, zc = [w_in[:, offs[i]:offs[i + 1]] for i in range(len(IN_SIZES))]

    slab_a = functools.partial(_to_slabs, layout=LAYOUT_A, dims_per_slab=A_HEAD_DIM)
    slab_b = functools.partial(_to_slabs, layout=LAYOUT_B, dims_per_slab=2 * B_HEAD_DIM)
    slab_c = functools.partial(_to_slabs, layout=LAYOUT_C, dims_per_slab=2 * C_HEAD_DIM)
    qb = _permute_heads(qb, B_Q_ORDER, B_HEAD_DIM)

    w_a = jnp.concatenate([slab_a(qa), slab_a(ka), _pad_heads(va, N_A)], axis=-1).astype(BF16)
    gain_a = jnp.concatenate([slab_a(jnp.tile(qn_a, N_A)), slab_a(jnp.tile(kn_a, N_A)),
                              jnp.zeros((N_A * LANE,), F32)])[None, :]

    w_b = jnp.concatenate([slab_b(qb), slab_b(kb), vb], axis=-1).astype(BF16)
    gain_b = jnp.concatenate([slab_b(jnp.tile(qn_b, B_HEADS)), slab_b(jnp.tile(kn_b, B_KV_HEADS)),
                              jnp.zeros((B_KV,), F32)])[None, :]

    w_c = jnp.concatenate([slab_c(qc), slab_c(kc), vc], axis=-1).astype(BF16)
    gain_c = jnp.concatenate([slab_c(jnp.tile(qn_c, 2 * C_HEADS)), slab_c(jnp.tile(kn_c, 2 * C_HEADS)),
                              jnp.zeros((C_WIDTH,), F32)])[None, :]

    w_z = jnp.concatenate([_pad_heads(za, A_HEADS), _permute_heads(zb, B_Q_ORDER, B_HEAD_DIM), zc],
                          axis=-1).astype(BF16)
    w_oa_p = _pad_heads(w_oa.T, A_HEADS).T.astype(BF16)
    w_ob_p = _permute_heads(w_ob.T, B_Q_ORDER, B_HEAD_DIM).T.astype(BF16)
    return dict(w_a=w_a, gain_a=gain_a, w_bc=jnp.concatenate([w_b, w_c], axis=-1),
                gain_bc=jnp.concatenate([gain_b, gain_c], axis=-1), w_z=w_z,
                w_oa=w_oa_p, w_ob=w_ob_p, w_oc=w_oc.astype(BF16))


def _inv_freq(dim, theta):
    return theta ** (-jnp.arange(0, dim, 2, dtype=F32) / dim)


def _rope_tables(layout, inv_freqs, streams):
    _, group, freq, sign = layout
    ang = 0.0
    for g, (inv, pos) in enumerate(zip(inv_freqs, streams)):
        ang = ang + pos[:, None] * (inv[freq] * (group == g))[None, :]
    return jnp.cos(ang), jnp.sin(ang) * sign[None, :]


def _tables(s):
    pos = jnp.arange(s)
    tok, row, col = [t.astype(F32) for t in (pos, pos // GRID_W, pos % GRID_W)]
    inv_b = _inv_freq(B_HEAD_DIM // 2, AXIAL_THETA)
    tab_a = _rope_tables(LAYOUT_A, [_inv_freq(A_ROT, ROPE_THETA)], [tok])
    tab_b = _rope_tables(LAYOUT_B, [inv_b, inv_b], [row, col])
    tab_c = _rope_tables(LAYOUT_C, [_inv_freq(C_ROT, ROPE_THETA)], [tok])
    return tab_a, tab_b, tab_c


def _layer(x, mod, l, lp, tabs, norm_g, lam_rows, subln, w_bg, b_bg, w_out):
    bn, s, _ = x.shape
    shift, scale, gate = [m.reshape(bn, 1, D_MODEL) for m in jnp.split(mod, 3, axis=-1)]
    tab_a, tab_b, tab_c = tabs
    lam_init = 0.8 - 0.6 * math.exp(-0.3 * l)

    a_outs = tuple((A_SLAB, kind, dil, 0) for kind in ("q", "k", "v") for _, dil in A_GROUPS)
    pa = _project(x, shift, scale, norm_g, lp["w_a"], lp["gain_a"], tab_a, a_outs, two_heads=False,
                  head_dim=A_HEAD_DIM, q_scale=A_HEAD_DIM ** -0.5 * LOG2E, tm=min(PROJ_A_TM, s), name="proj_a")
    n_g = len(A_GROUPS)
    oa = [_band_attention(pa[g], pa[n_g + g], pa[2 * n_g + g], dil) for g, (_, dil) in enumerate(A_GROUPS)]

    assert B_HEAD_DIM == C_HEAD_DIM
    bc_outs = ((B_WIDTH, "qT", 1, 0), (B_KV, "k", 1, 0), (B_KV, "vT", B_KV_HEADS, 0),
               (C_WIDTH, "qT", 1, 1), (C_WIDTH, "k", 1, 1), (C_WIDTH, "vT", 1, 1))
    qb, kb, vb, qc, kc, vc = _project(x, shift, scale, norm_g, lp["w_bc"], lp["gain_bc"], (*tab_b, *tab_c), bc_outs,
                                      two_heads=True, head_dim=B_HEAD_DIM, q_scale=B_HEAD_DIM ** -0.5 * LOG2E,
                                      tm=min(PROJ_TM, s), name="proj_bc")
    yb = _gqa_attention(qb, kb, vb)
    yc = _diff_attention(qc, kc, vc, lam_rows, subln, lam_init)

    return _final(x, shift, scale, gate, norm_g, oa, yb, yc, lp["w_z"], w_bg, b_bg, lp["w_oa"], lp["w_ob"],
                  lp["w_oc"], w_out, tm=min(FINAL_TM, s))


def kernel(x_prompt, x_sample, c_prompt, c_sample, norm_g, w_ada, b_ada, w_in, qn_a, kn_a, qn_b, kn_b, qn_c, kn_c,
           lam_q1, lam_k1, lam_q2, lam_k2, subln_c, w_oa, w_ob, w_oc, w_bg, b_bg, w_out):
    groups = ((x_prompt, c_prompt), (x_sample, c_sample))
    rows = [c.shape[0] for _, c in groups]
    pad = -sum(rows) % 8
    c_all = jnp.concatenate([c for _, c in groups] + [jnp.zeros((pad, D_MODEL), F32)], axis=0)
    mod_all = _ada_mod(c_all, w_ada.astype(BF16), b_ada)

    layers = [_prep_layer(w_in[l], qn_a[l], kn_a[l], qn_b[l], kn_b[l], qn_c[l], kn_c[l], w_oa[l], w_ob[l], w_oc[l])
              for l in range(DEPTH)]
    w_bg16, w_out16 = w_bg.astype(BF16), w_out.astype(BF16)

    outs = []
    row0 = 0
    for (x, c), n in zip(groups, rows):
        tabs = _tables(x.shape[1])
        for l in range(DEPTH):
            lam_rows = [p[l][None, :] for p in (lam_q1, lam_k1, lam_q2, lam_k2)]
            x = _layer(x, mod_all[l, row0:row0 + n], l, layers[l], tabs, norm_g[l][None, :], lam_rows,
                       subln_c[l][None, :], w_bg16[l], b_bg[l][None, :], w_out16[l])
        outs.append(x)
        row0 += n
    return tuple(outs)
```
